```python
import jax, jax.numpy as jnp
from jax import lax
import numpy as np

D_MODEL = 1024
BATCH = 4
SEQ = 4096
DEPTH = 1

ML_HEADS = 4
ML_DQK = 128
ML_DV = 256
CONV_K = 4
RET_HEADS = 4
RET_DQK = 128
RET_DV = 256
ROPE_BASE = 10000.0
CHUNK = 128
N_EXPERTS = 32
TOP_K = 4
D_EXPERT = D_MODEL
SWIGLU_LIMIT = 7.0
SWIGLU_ALPHA = 1.702
EXPERT_BLOCK = 256
EPS = 1e-5

ML_QK_W = ML_HEADS * ML_DQK
ML_V_W = ML_HEADS * ML_DV
RET_QK_W = RET_HEADS * RET_DQK
RET_V_W = RET_HEADS * RET_DV
IN_SIZES = (2 * ML_QK_W, ML_V_W, ML_V_W, 2 * ML_HEADS, RET_QK_W, RET_QK_W, RET_V_W, RET_V_W, D_MODEL, D_MODEL)
D_IN = sum(IN_SIZES)

kernel_name = "hybrid_mlstm_retention_moe_adaln"


def rmsnorm(x, g):
    xf = x.astype(jnp.float32)
    y = xf * lax.rsqrt(jnp.mean(xf * xf, axis=-1, keepdims=True) + EPS)
    return (y * g.astype(jnp.float32)).astype(x.dtype)


def head_layernorm(h, n_heads, g):
    B, S, W = h.shape
    hf = h.astype(jnp.float32).reshape(B, S, n_heads, W // n_heads)
    mu = jnp.mean(hf, axis=-1, keepdims=True)
    var = jnp.mean(jnp.square(hf - mu), axis=-1, keepdims=True)
    y = ((hf - mu) * lax.rsqrt(var + EPS)).reshape(B, S, W)
    return (y * g.astype(jnp.float32)).astype(h.dtype)


def causal_dwconv(x, w, b):
    C = x.shape[-1]
    y = lax.conv_general_dilated(x, w[:, None, :].astype(x.dtype), window_strides=(1,), padding=((CONV_K - 1, 0),),
                                 dimension_numbers=('NWC', 'WIO', 'NWC'), feature_group_count=C)
    return y + b.astype(x.dtype)


def rotary(x, positions):
    half = x.shape[-1] // 2
    inv = ROPE_BASE ** (-jnp.arange(half, dtype=jnp.float32) / half)
    ang = positions.astype(jnp.float32)[..., None] * inv
    cos = jnp.cos(ang)[:, :, None, :]
    sin = jnp.sin(ang)[:, :, None, :]
    xf = x.astype(jnp.float32)
    x1, x2 = xf[..., :half], xf[..., half:]
    return jnp.concatenate([x1 * cos - x2 * sin, x2 * cos + x1 * sin], axis=-1).astype(x.dtype)


def to_chunks(t, n_heads):
    B, S, _ = t.shape
    return t.reshape(B, S // CHUNK, CHUNK, n_heads, -1).transpose(0, 3, 1, 2, 4)


def gate_chunks(t):
    B, S, H = t.shape
    return t.reshape(B, S // CHUNK, CHUNK, H).transpose(0, 3, 1, 2)


def from_chunks(t):
    B, H, NC, L, d = t.shape
    return t.transpose(0, 2, 3, 1, 4).reshape(B, NC * L, H * d)


def mlstm_chunkwise(q, k, v, i_pre, f_pre):
    L = q.shape[-2]
    dk = q.shape[-1]
    q = q.astype(jnp.float32) * (dk ** -0.5)
    k = k.astype(jnp.float32)
    v = v.astype(jnp.float32)
    a = jnp.cumsum(jax.nn.log_sigmoid(f_pre), axis=-1)
    A = a[..., -1]
    causal = jnp.tril(jnp.ones((L, L), dtype=bool))
    d_log = jnp.where(causal, a[..., :, None] - a[..., None, :] + i_pre[..., None, :], -jnp.inf)
    m_intra = jnp.max(d_log, axis=-1)
    w_state = A[..., None] - a + i_pre
    m_loc = jnp.max(w_state, axis=-1)
    ws = jnp.exp(w_state - m_loc[..., None])
    c_loc = jnp.einsum('bhcl,bhclk,bhclv->bhckv', ws, k, v)
    n_loc = jnp.einsum('bhcl,bhclk->bhck', ws, k)

    def step(carry, inp):
        c_st, n_st, m_st = carry
        a_c, m_l, c_l, n_l = inp
        m_new = jnp.maximum(a_c + m_st, m_l)
        s_prev = jnp.exp(a_c + m_st - m_new)
        s_loc = jnp.exp(m_l - m_new)
        c_new = s_prev[..., None, None] * c_st + s_loc[..., None, None] * c_l
        n_new = s_prev[..., None] * n_st + s_loc[..., None] * n_l
        return (c_new, n_new, m_new), (c_st, n_st, m_st)

    B, H = q.shape[0], q.shape[1]
    init = (jnp.zeros((B, H, dk, v.shape[-1]), jnp.float32), jnp.zeros((B, H, dk), jnp.float32), jnp.zeros((B, H), jnp.float32))
    xs = (jnp.moveaxis(A, 2, 0), jnp.moveaxis(m_loc, 2, 0), jnp.moveaxis(c_loc, 2, 0), jnp.moveaxis(n_loc, 2, 0))
    _, (c_prev, n_prev, m_prev) = lax.scan(step, init, xs)
    c_prev = jnp.moveaxis(c_prev, 0, 2)
    n_prev = jnp.moveaxis(n_prev, 0, 2)
    m_prev = jnp.moveaxis(m_prev, 0, 2)

    inter_log = a + m_prev[..., None]
    m_t = jnp.maximum(inter_log, m_intra)
    p = jnp.exp(d_log - m_t[..., None]) * jnp.einsum('bhcjk,bhcsk->bhcjs', q, k)
    inter_scale = jnp.exp(inter_log - m_t)
    num = jnp.einsum('bhcjs,bhcsv->bhcjv', p, v) + inter_scale[..., None] * jnp.einsum('bhcjk,bhckv->bhcjv', q, c_prev)
    den = jnp.sum(p, axis=-1) + inter_scale * jnp.einsum('bhcjk,bhck->bhcj', q, n_prev)
    return num / jnp.maximum(jnp.abs(den), jnp.exp(-m_t))[..., None]


def retention_chunkwise(q, k, v):
    H, L = q.shape[1], q.shape[-2]
    log_gamma = jnp.log(1.0 - 2.0 ** (-5.0 - jnp.arange(H, dtype=jnp.float32)))
    q = q.astype(jnp.float32)
    k = k.astype(jnp.float32) * (k.shape[-1] ** -0.5)
    v = v.astype(jnp.float32)
    pos = jnp.arange(L, dtype=jnp.float32)
    rel = pos[:, None] - pos[None, :]
    decay = jnp.where(rel >= 0, jnp.exp(log_gamma[:, None, None] * jnp.maximum(rel, 0.0)), 0.0)
    scores = jnp.einsum('bhcjk,bhcsk->bhcjs', q, k) * decay[None, :, None]
    intra = jnp.einsum('bhcjs,bhcsv->bhcjv', scores, v)
    k_decay = jnp.exp(log_gamma[:, None] * (L - 1 - pos))
    r_loc = jnp.einsum('bhcsk,hs,bhcsv->bhckv', k, k_decay, v)
    g_chunk = jnp.exp(log_gamma * L)[None, :, None, None]

    def step(r, r_l):
        return g_chunk * r + r_l, r

    init = jnp.zeros((q.shape[0], H, q.shape[-1], v.shape[-1]), jnp.float32)
    _, r_prev = lax.scan(step, init, jnp.moveaxis(r_loc, 2, 0))
    r_prev = jnp.moveaxis(r_prev, 0, 2)
    q_decay = jnp.exp(log_gamma[:, None] * (pos + 1.0))
    inter = jnp.einsum('bhcjk,bhckv->bhcjv', q, r_prev) * q_decay[None, :, None, :, None]
    return intra + inter


def mixer(xm, positions, w_in, conv_w, conv_b, b_if, ml_norm_g, ret_norm_g, w_branch_ml, w_branch_ret, w_out):
    B, S, _ = xm.shape
    split_idx = np.cumsum(IN_SIZES)[:-1].tolist()
    proj = xm @ w_in
    ml_qk, ml_v, ml_o, ml_if, ret_q, ret_k, ret_v, ret_g, gate_ml, gate_ret = jnp.split(proj, split_idx, axis=-1)

    ml_qk = jax.nn.silu(causal_dwconv(ml_qk, conv_w, conv_b))
    ml_q, ml_k = ml_qk[..., :ML_QK_W], ml_qk[..., ML_QK_W:]
    if_pre = (ml_if + b_if).astype(jnp.float32)
    h_ml = mlstm_chunkwise(to_chunks(ml_q, ML_HEADS), to_chunks(ml_k, ML_HEADS), to_chunks(ml_v, ML_HEADS),
                           gate_chunks(if_pre[..., :ML_HEADS]), gate_chunks(if_pre[..., ML_HEADS:]))
    h_ml = from_chunks(h_ml).astype(xm.dtype)
    h_ml = jax.nn.sigmoid(ml_o) * head_layernorm(h_ml, ML_HEADS, ml_norm_g)

    ret_q = rotary(ret_q.reshape(B, S, RET_HEADS, RET_DQK), positions).reshape(B, S, RET_QK_W)
    ret_k = rotary(ret_k.reshape(B, S, RET_HEADS, RET_DQK), positions).reshape(B, S, RET_QK_W)
    h_ret = retention_chunkwise(to_chunks(ret_q, RET_HEADS), to_chunks(ret_k, RET_HEADS), to_chunks(ret_v, RET_HEADS))
    h_ret = from_chunks(h_ret).astype(xm.dtype)
    h_ret = jax.nn.silu(ret_g) * head_layernorm(h_ret, RET_HEADS, ret_norm_g)

    y = jax.nn.sigmoid(gate_ml) * (h_ml @ w_branch_ml) + jax.nn.sigmoid(gate_ret) * (h_ret @ w_branch_ret)
    return y @ w_out


def clamped_swiglu(gu):
    x_glu = jnp.minimum(gu[..., ::2], SWIGLU_LIMIT)
    x_lin = jnp.clip(gu[..., 1::2], -SWIGLU_LIMIT, SWIGLU_LIMIT)
    return x_glu * jax.nn.sigmoid(SWIGLU_ALPHA * x_glu) * (x_lin + 1.0)


def moe(xm, w_router, b_router, w_gate_up, b_gate_up, w_down, b_down):
    B, S, D = xm.shape
    T = B * S
    xt = xm.reshape(T, D)
    logits = (xt @ w_router + b_router).astype(jnp.float32)
    top_val, top_idx = lax.top_k(logits, TOP_K)
    top_w = jax.nn.softmax(top_val, axis=-1)
    n_assign = T * TOP_K
    n_blocks = -(-n_assign // EXPERT_BLOCK) + N_EXPERTS
    n_slots = n_blocks * EXPERT_BLOCK
    e_flat = top_idx.reshape(n_assign)
    tok_flat = jnp.arange(n_assign, dtype=jnp.int32) // TOP_K
    w_flat = top_w.reshape(n_assign)
    counts = jnp.bincount(e_flat, length=N_EXPERTS)
    group_start = jnp.cumsum(counts) - counts
    blocks_per_e = (counts + EXPERT_BLOCK - 1) // EXPERT_BLOCK
    block_end = jnp.cumsum(blocks_per_e)
    pad_start = (block_end - blocks_per_e) * EXPERT_BLOCK
    order = jnp.argsort(e_flat)
    e_sorted = e_flat[order]
    rank = jnp.arange(n_assign, dtype=jnp.int32) - group_start[e_sorted]
    dest = pad_start[e_sorted] + rank
    slot_tok = jnp.full((n_slots,), T, jnp.int32).at[dest].set(tok_flat[order])
    slot_w = jnp.zeros((n_slots,), jnp.float32).at[dest].set(w_flat[order])
    block_expert = jnp.minimum(jnp.searchsorted(block_end, jnp.arange(n_blocks), side='right'), N_EXPERTS - 1)
    x_pad = jnp.concatenate([xt, jnp.zeros((1, D), xt.dtype)], axis=0)
    x_blocks = x_pad[slot_tok].reshape(n_blocks, EXPERT_BLOCK, D)

    def expert_block(args):
        xb, e = args
        gu = xb @ w_gate_up[e] + b_gate_up[e]
        return clamped_swiglu(gu) @ w_down[e] + b_down[e]

    y_blocks = lax.map(expert_block, (x_blocks, block_expert))
    y_slots = y_blocks.reshape(n_slots, D) * slot_w[:, None].astype(y_blocks.dtype)
    y = jax.ops.segment_sum(y_slots, slot_tok, num_segments=T + 1)[:T]
    return y.reshape(B, S, D)


def setup_inputs(seed: int = 0) -> dict:
    key = jax.random.key(seed)
    ks = jax.random.split(key, 24)
    D, L, E, F = D_MODEL, DEPTH, N_EXPERTS, D_EXPERT
    nrm = lambda k, shape, s: jax.random.normal(k, shape, jnp.float32) * s
    x = nrm(ks[0], (BATCH, SEQ, D), 1.0)
    c = nrm(ks[1], (BATCH, D), 1.0)
    offsets = jax.random.randint(ks[2], (BATCH, 1), 0, 1024, dtype=jnp.int32)
    positions = offsets + jnp.arange(SEQ, dtype=jnp.int32)[None, :]
    b_i = -1.0 + nrm(ks[8], (L, ML_HEADS), 0.1)
    b_f = jnp.linspace(3.0, 6.0, ML_HEADS, dtype=jnp.float32)[None, :] + nrm(ks[9], (L, ML_HEADS), 0.1)
    return {
        "x": x,
        "c": c,
        "positions": positions,
        "w_ada": nrm(ks[3], (L, D, 6 * D), 0.5 * D ** -0.5),
        "b_ada": nrm(ks[4], (L, 6 * D), 0.02),
        "norm_mix_g": 1.0 + nrm(ks[5], (L, D), 0.02),
        "w_in": nrm(ks[6], (L, D, D_IN), D ** -0.5),
        "conv_w": nrm(ks[7], (L, CONV_K, 2 * ML_QK_W), CONV_K ** -0.5),
        "conv_b": nrm(ks[10], (L, 2 * ML_QK_W), 0.02),
        "b_if": jnp.concatenate([b_i, b_f], axis=-1),
        "ml_norm_g": 1.0 + nrm(ks[11], (L, ML_V_W), 0.02),
        "ret_norm_g": 1.0 + nrm(ks[12], (L, RET_V_W), 0.02),
        "w_branch_ml": nrm(ks[13], (L, ML_V_W, D), ML_V_W ** -0.5),
        "w_branch_ret": nrm(ks[14], (L, RET_V_W, D), RET_V_W ** -0.5),
        "w_out": nrm(ks[15], (L, D, D), D ** -0.5),
        "norm_ffn_g": 1.0 + nrm(ks[16], (L, D), 0.02),
        "w_router": nrm(ks[17], (L, D, E), D ** -0.5),
        "b_router": nrm(ks[18], (L, E), 0.01),
        "w_gate_up": nrm(ks[19], (L, E, D, 2 * F), D ** -0.5),
        "b_gate_up": nrm(ks[20], (L, E, 2 * F), 0.02),
        "w_down": nrm(ks[21], (L, E, F, D), F ** -0.5),
        "b_down": nrm(ks[22], (L, E, D), 0.02),
        "norm_final_g": 1.0 + nrm(ks[23], (D,), 0.02),
    }


def reference(x, c, positions, w_ada, b_ada, norm_mix_g, w_in, conv_w, conv_b, b_if, ml_norm_g, ret_norm_g,
              w_branch_ml, w_branch_ret, w_out, norm_ffn_g, w_router, b_router, w_gate_up, b_gate_up, w_down, b_down,
              norm_final_g):
    c_act = jax.nn.silu(c)
    for l in range(DEPTH):
        mod = c_act @ w_ada[l] + b_ada[l]
        shift_m, scale_m, gate_m, shift_f, scale_f, gate_f = jnp.split(mod[:, None, :], 6, axis=-1)
        h = rmsnorm(x, norm_mix_g[l]) * (1.0 + scale_m) + shift_m
        x = x + gate_m * mixer(h, positions, w_in[l], conv_w[l], conv_b[l], b_if[l], ml_norm_g[l], ret_norm_g[l],
                               w_branch_ml[l], w_branch_ret[l], w_out[l])
        h = rmsnorm(x, norm_ffn_g[l]) * (1.0 + scale_f) + shift_f
        x = x + gate_f * moe(h, w_router[l], b_router[l], w_gate_up[l], b_gate_up[l], w_down[l], b_down[l])
    return rmsnorm(x, norm_final_g)
```

```python
import functools

import numpy as np
import jax
import jax.numpy as jnp
from jax import lax
from jax.experimental import pallas as pl
from jax.experimental.pallas import tpu as pltpu

D_MODEL = 1024
N_HEADS = 4
D_QK = 128
D_V = 256
CONV_K = 4
ROPE_BASE = 10000.0
CHUNK = 128
N_EXPERTS = 32
TOP_K = 4
SWIGLU_LIMIT = 7.0
SWIGLU_ALPHA = 1.702
EXPERT_BLOCK = 256
EPS = 1e-5

QK_W = N_HEADS * D_QK
V_W = N_HEADS * D_V
LANES = 128
GATE_W = 2 * LANES
V_EXT = D_V + LANES

C_MLQ, C_MLK, C_MLV, C_MLO = 0, 512, 1024, 2048
C_RQ, C_RK, C_RV, C_RG = 3072, 3584, 4096, 5120
C_GML, C_GRET = 6144, 7168
MIX_W = 6144
PROJ_W = 8192

BF16 = jnp.bfloat16
F32 = jnp.float32
VMEM_LIMIT = 56 * 1024 * 1024


def _dot(a, b):
    return jnp.dot(a, b, preferred_element_type=F32)


def _dot_nt(a, b):
    return lax.dot_general(a, b, (((1,), (1,)), ((), ())), preferred_element_type=F32)


def _split(a):
    hi = a.astype(BF16)
    lo = (a - hi.astype(F32)).astype(BF16)
    return hi, lo


def _dot3(a, b_hi, b_lo):
    a_hi, a_lo = _split(a)
    return _dot(a_hi, b_hi) + (_dot(a_lo, b_hi) + _dot(a_hi, b_lo))


def _sigmoid(x):
    return 1.0 / (1.0 + jnp.exp(-x))


def _rms_mod(x, g, scale, shift):
    ms = jnp.mean(x * x, axis=-1, keepdims=True)
    return (x * lax.rsqrt(ms + EPS) * g) * (1.0 + scale) + shift


def _ada_kernel(c_ref, whi_ref, wlo_ref, b_ref, o_ref):
    c = c_ref[...]
    ca = c * _sigmoid(c)
    o_ref[...] = _dot3(ca, whi_ref[...], wlo_ref[...]) + b_ref[...]


def _ada(c8, w_hi, w_lo, b):
    n = w_hi.shape[1]
    tn = 1024
    return pl.pallas_call(
        _ada_kernel,
        grid=(n // tn,),
        in_specs=[
            pl.BlockSpec((8, D_MODEL), lambda j: (0, 0)),
            pl.BlockSpec((D_MODEL, tn), lambda j: (0, j)),
            pl.BlockSpec((D_MODEL, tn), lambda j: (0, j)),
            pl.BlockSpec((1, tn), lambda j: (0, j)),
        ],
        out_specs=pl.BlockSpec((8, tn), lambda j: (0, j)),
        out_shape=jax.ShapeDtypeStruct((8, n), F32),
        name="ada",
    )(c8, w_hi, w_lo, b)


def _inproj_kernel(x_ref, mod_ref, g_ref, w_ref, wif_hi_ref, wif_lo_ref, proj_ref, gates_ref, hb_ref):
    j = pl.program_id(1)

    @pl.when(j == 0)
    def _():
        h = _rms_mod(x_ref[...], g_ref[...], mod_ref[0, 1:2, :], mod_ref[0, 0:1, :])
        hb_ref[...] = h.astype(BF16)
        gates_ref[...] = _dot3(h, wif_hi_ref[...], wif_lo_ref[...])

    proj_ref[...] = _dot(hb_ref[...], w_ref[...]).astype(BF16)


def _inproj(x2, mod6, g, w_main, wif_hi, wif_lo, seq):
    t = x2.shape[0]
    tm, tn = 512, 2048
    per_b = seq // tm
    return pl.pallas_call(
        _inproj_kernel,
        grid=(t // tm, PROJ_W // tn),
        in_specs=[
            pl.BlockSpec((tm, D_MODEL), lambda i, j: (i, 0)),
            pl.BlockSpec((1, 6, D_MODEL), lambda i, j: (i // per_b, 0, 0)),
            pl.BlockSpec((1, D_MODEL), lambda i, j: (0, 0)),
            pl.BlockSpec((D_MODEL, tn), lambda i, j: (0, j)),
            pl.BlockSpec((D_MODEL, GATE_W), lambda i, j: (0, 0)),
            pl.BlockSpec((D_MODEL, GATE_W), lambda i, j: (0, 0)),
        ],
        out_specs=[
            pl.BlockSpec((tm, tn), lambda i, j: (i, j)),
            pl.BlockSpec((tm, GATE_W), lambda i, j: (i, 0)),
        ],
        out_shape=[
            jax.ShapeDtypeStruct((t, PROJ_W), BF16),
            jax.ShapeDtypeStruct((t, GATE_W), F32),
        ],
        scratch_shapes=[pltpu.VMEM((tm, D_MODEL), BF16)],
        compiler_params=pltpu.CompilerParams(
            dimension_semantics=("arbitrary", "arbitrary"), vmem_limit_bytes=VMEM_LIMIT),
        name="inproj",
    )(x2, mod6, g, w_main, wif_hi, wif_lo)


def _head_norm(h, g):
    mu = jnp.mean(h, axis=-1, keepdims=True)
    d = h - mu
    var = jnp.mean(d * d, axis=-1, keepdims=True)
    return d * lax.rsqrt(var + EPS) * g


def _mix_kernel(proj_ref, gates_ref, cos_ref, sin_ref, convw_ref, convb_ref, bif_ref, mlg_ref, retg_ref,
                decay_ref, kdec_ref, qdec_ref, gch_ref,
                hm_ref, hr_ref,
                prev_ref, c_st, m_st, r_st):
    L = CHUNK

    @pl.when(pl.program_id(1) == 0)
    def _():
        prev_ref[...] = jnp.zeros_like(prev_ref)
        c_st[...] = jnp.zeros_like(c_st)
        m_st[...] = jnp.zeros_like(m_st)
        r_st[...] = jnp.zeros_like(r_st)

    rows = lax.broadcasted_iota(jnp.int32, (L, L), 0)
    cols = lax.broadcasted_iota(jnp.int32, (L, L), 1)
    causal = rows >= cols
    tril = jnp.where(causal, 1.0, 0.0).astype(BF16)

    cur = proj_ref[:, C_MLQ:C_MLQ + 2 * QK_W]
    xx = jnp.concatenate([prev_ref[...], cur], axis=0)
    r2 = lax.broadcasted_iota(jnp.int32, (L, 2 * L), 0)
    c2 = lax.broadcasted_iota(jnp.int32, (L, 2 * L), 1)
    acc = convb_ref[...] + cur.astype(F32) * convw_ref[CONV_K - 1:CONV_K, :]
    for d in range(1, CONV_K):
        shift = jnp.where(c2 == r2 + (L - d), 1.0, 0.0).astype(BF16)
        acc = acc + _dot(shift, xx) * convw_ref[CONV_K - 1 - d:CONV_K - d, :]
    prev_ref[...] = cur
    qk = acc * _sigmoid(acc)

    g = gates_ref[...] + bif_ref[...]
    gi = g[:, :LANES]
    gf = g[:, LANES:]
    lf = jnp.minimum(gf, 0.0) - jnp.log(1.0 + jnp.exp(-jnp.abs(gf)))
    lf_hi, lf_lo = _split(lf)
    a_all = _dot(tril, lf_hi) + _dot(tril, lf_lo)
    a_last = a_all[L - 1:L, :]
    bm = gi - a_all
    bm_t = bm.T
    w_state = a_last + bm
    m_loc = jnp.max(w_state, axis=0, keepdims=True)
    m_prev = m_st[...]
    inter_log = a_all + m_prev
    m_new = jnp.maximum(a_last + m_prev, m_loc)
    s_prev = jnp.exp(a_last + m_prev - m_new)
    s_loc = jnp.exp(m_loc - m_new)
    ws_all = jnp.exp(w_state - m_loc)
    m_st[...] = m_new

    ones_blk = jnp.ones((L, LANES), BF16)
    q_scale = D_QK ** -0.5

    for h in range(N_HEADS):
        q = (qk[:, h * D_QK:(h + 1) * D_QK] * q_scale).astype(BF16)
        k_f = qk[:, QK_W + h * D_QK:QK_W + (h + 1) * D_QK]
        k = k_f.astype(BF16)
        v_ext = jnp.concatenate([proj_ref[:, C_MLV + h * D_V:C_MLV + (h + 1) * D_V], ones_blk], axis=1)
        dlog = jnp.where(causal, a_all[:, h:h + 1] + bm_t[h:h + 1, :], -jnp.inf)
        m_intra = jnp.max(dlog, axis=-1, keepdims=True)
        il = inter_log[:, h:h + 1]
        m_t = jnp.maximum(il, m_intra)
        p = jnp.exp(dlog - m_t) * _dot_nt(q, k)
        isc = jnp.exp(il - m_t)
        tot = _dot(p.astype(BF16), v_ext) + isc * _dot(q, c_st[h].astype(BF16))
        den = tot[:, D_V:D_V + 1]
        hout = tot[:, :D_V] / jnp.maximum(jnp.abs(den), jnp.exp(-m_t))
        kw_t = (k_f * ws_all[:, h:h + 1]).T.astype(BF16)
        c_st[h] = s_prev[:, h:h + 1] * c_st[h] + s_loc[:, h:h + 1] * _dot(kw_t, v_ext)
        y = _head_norm(hout, mlg_ref[:, h * D_V:(h + 1) * D_V])
        o = proj_ref[:, C_MLO + h * D_V:C_MLO + (h + 1) * D_V].astype(F32)
        hm_ref[:, h * D_V:(h + 1) * D_V] = (_sigmoid(o) * y).astype(BF16)

    cos2 = cos_ref[...]
    sin2 = sin_ref[...]
    k_scale = D_QK ** -0.5
    for h in range(N_HEADS):
        q_raw = proj_ref[:, C_RQ + h * D_QK:C_RQ + (h + 1) * D_QK].astype(F32)
        k_raw = proj_ref[:, C_RK + h * D_QK:C_RK + (h + 1) * D_QK].astype(F32)
        q = (q_raw * cos2 + pltpu.roll(q_raw, D_QK // 2, 1) * sin2).astype(BF16)
        k_f = (k_raw * cos2 + pltpu.roll(k_raw, D_QK // 2, 1) * sin2) * k_scale
        v = proj_ref[:, C_RV + h * D_V:C_RV + (h + 1) * D_V]
        sc = _dot_nt(q, k_f.astype(BF16)) * decay_ref[h]
        hret = _dot(sc.astype(BF16), v) + _dot(q, r_st[h].astype(BF16)) * qdec_ref[h]
        kd_t = (k_f * kdec_ref[h]).T.astype(BF16)
        r_st[h] = gch_ref[h][:, 0:1] * r_st[h] + _dot(kd_t, v)
        y = _head_norm(hret, retg_ref[:, h * D_V:(h + 1) * D_V])
        gt = proj_ref[:, C_RG + h * D_V:C_RG + (h + 1) * D_V].astype(F32)
        hr_ref[:, h * D_V:(h + 1) * D_V] = (gt * _sigmoid(gt) * y).astype(BF16)


def _mix(proj, gates, cos2, sin2, conv_w, conv_b, b_if2, ml_g, ret_g, decay, kdec, qdec, gch, batch, seq):
    t = proj.shape[0]
    nc = seq // CHUNK
    L = CHUNK
    full = lambda shape: pl.BlockSpec(shape, lambda b, c: (0,) * len(shape))
    return pl.pallas_call(
        _mix_kernel,
        grid=(batch, nc),
        in_specs=[
            pl.BlockSpec((L, MIX_W), lambda b, c: (b * nc + c, 0)),
            pl.BlockSpec((L, GATE_W), lambda b, c: (b * nc + c, 0)),
            pl.BlockSpec((L, LANES), lambda b, c: (b * nc + c, 0)),
            pl.BlockSpec((L, LANES), lambda b, c: (b * nc + c, 0)),
            full((CONV_K, 2 * QK_W)),
            full((1, 2 * QK_W)),
            full((1, GATE_W)),
            full((1, V_W)),
            full((1, V_W)),
            full((N_HEADS, L, L)),
            full((N_HEADS, L, 1)),
            full((N_HEADS, L, 1)),
            full((N_HEADS, 1, LANES)),
        ],
        out_specs=[
            pl.BlockSpec((L, V_W), lambda b, c: (b * nc + c, 0)),
            pl.BlockSpec((L, V_W), lambda b, c: (b * nc + c, 0)),
        ],
        out_shape=[jax.ShapeDtypeStruct((t, V_W), BF16), jax.ShapeDtypeStruct((t, V_W), BF16)],
        scratch_shapes=[
            pltpu.VMEM((L, 2 * QK_W), BF16),
            pltpu.VMEM((N_HEADS, D_QK, V_EXT), F32),
            pltpu.VMEM((1, LANES), F32),
            pltpu.VMEM((N_HEADS, D_QK, D_V), F32),
        ],
        compiler_params=pltpu.CompilerParams(
            dimension_semantics=("arbitrary", "arbitrary"), vmem_limit_bytes=VMEM_LIMIT),
        name="mix",
    )(proj, gates, cos2, sin2, conv_w, conv_b, b_if2, ml_g, ret_g, decay, kdec, qdec, gch)


def _post_kernel(hm_ref, hr_ref, gm_ref, gr_ref, x_ref, mod_ref, wbm_ref, wbr_ref, wout_ref, g_ref,
                 wr_hi_ref, wr_lo_ref, br_ref, x1_ref, hf_ref, logit_ref):
    ym = _dot(hm_ref[...], wbm_ref[...])
    yr = _dot(hr_ref[...], wbr_ref[...])
    y = _sigmoid(gm_ref[...].astype(F32)) * ym + _sigmoid(gr_ref[...].astype(F32)) * yr
    o = _dot(y.astype(BF16), wout_ref[...])
    x1 = x_ref[...] + mod_ref[0, 2:3, :] * o
    x1_ref[...] = x1
    hf = _rms_mod(x1, g_ref[...], mod_ref[0, 4:5, :], mod_ref[0, 3:4, :])
    hf_ref[...] = hf.astype(BF16)
    logit_ref[...] = _dot3(hf, wr_hi_ref[...], wr_lo_ref[...]) + br_ref[...]


def _post(hm, hr, proj, x2, mod6, wbm, wbr, wout, g, wr_hi, wr_lo, br, seq):
    t = x2.shape[0]
    tm = 512
    per_b = seq // tm
    d = D_MODEL
    const = lambda shape: pl.BlockSpec(shape, lambda i: (0,) * len(shape))
    return pl.pallas_call(
        _post_kernel,
        grid=(t // tm,),
        in_specs=[
            pl.BlockSpec((tm, d), lambda i: (i, 0)),
            pl.BlockSpec((tm, d), lambda i: (i, 0)),
            pl.BlockSpec((tm, d), lambda i: (i, C_GML // d)),
            pl.BlockSpec((tm, d), lambda i: (i, C_GRET // d)),
            pl.BlockSpec((tm, d), lambda i: (i, 0)),
            pl.BlockSpec((1, 6, d), lambda i: (i // per_b, 0, 0)),
            const((d, d)), const((d, d)), const((d, d)),
            const((1, d)),
            const((d, LANES)), const((d, LANES)), const((1, LANES)),
        ],
        out_specs=[
            pl.BlockSpec((tm, d), lambda i: (i, 0)),
            pl.BlockSpec((tm, d), lambda i: (i, 0)),
            pl.BlockSpec((tm, LANES), lambda i: (i, 0)),
        ],
        out_shape=[
            jax.ShapeDtypeStruct((t, d), F32),
            jax.ShapeDtypeStruct((t, d), BF16),
            jax.ShapeDtypeStruct((t, LANES), F32),
        ],
        compiler_params=pltpu.CompilerParams(
            dimension_semantics=("arbitrary",), vmem_limit_bytes=VMEM_LIMIT),
        name="post",
    )(hm, hr, proj, proj, x2, mod6, wbm, wbr, wout, g, wr_hi, wr_lo, br)


def _expert_kernel(be_ref, xs_ref, wgu_ref, bgu_ref, wd_ref, bd_ref, sw_ref, ys_ref, wgu_bf, wd_bf):
    b = pl.program_id(0)
    prev = be_ref[jnp.maximum(b - 1, 0)]
    first = jnp.logical_or(b == 0, be_ref[b] != prev)
    two = 2 * LANES

    @pl.when(first)
    def _():
        r = lax.broadcasted_iota(jnp.int32, (two, two), 0)
        c = lax.broadcasted_iota(jnp.int32, (two, two), 1)
        src = jnp.where(c < LANES, 2 * c, 2 * (c - LANES) + 1)
        perm = jnp.where(r == src, 1.0, 0.0).astype(BF16)
        for blk in range(wgu_bf.shape[1] // two):
            wb = wgu_ref[0, :, blk * two:(blk + 1) * two].astype(BF16)
            wgu_bf[:, blk * two:(blk + 1) * two] = _dot(wb, perm).astype(BF16)
        wd_bf[...] = wd_ref[0].astype(BF16)

    gu = _dot(xs_ref[...], wgu_bf[...]) + bgu_ref[0]
    n_blk = gu.shape[1] // two
    x_glu = jnp.concatenate([gu[:, i * two:i * two + LANES] for i in range(n_blk)], axis=1)
    x_lin = jnp.concatenate([gu[:, i * two + LANES:(i + 1) * two] for i in range(n_blk)], axis=1)
    x_glu = jnp.minimum(x_glu, SWIGLU_LIMIT)
    x_lin = jnp.clip(x_lin, -SWIGLU_LIMIT, SWIGLU_LIMIT)
    act = x_glu * _sigmoid(SWIGLU_ALPHA * x_glu) * (x_lin + 1.0)
    y = _dot(act.astype(BF16), wd_bf[...]) + bd_ref[0]
    ys_ref[...] = y * sw_ref[...]


def _experts(block_expert, xs, w_gate_up, b_gu_perm, w_down, b_down, slot_w):
    n_slots = xs.shape[0]
    n_blocks = n_slots // EXPERT_BLOCK
    d = D_MODEL
    f2 = w_gate_up.shape[-1]
    grid_spec = pltpu.PrefetchScalarGridSpec(
        num_scalar_prefetch=1,
        grid=(n_blocks,),
        in_specs=[
            pl.BlockSpec((EXPERT_BLOCK, d), lambda b, be: (b, 0)),
            pl.BlockSpec((1, d, f2), lambda b, be: (be[b], 0, 0)),
            pl.BlockSpec((1, 1, f2), lambda b, be: (be[b], 0, 0)),
            pl.BlockSpec((1, f2 // 2, d), lambda b, be: (be[b], 0, 0)),
            pl.BlockSpec((1, 1, d), lambda b, be: (be[b], 0, 0)),
            pl.BlockSpec((EXPERT_BLOCK, 1), lambda b, be: (b, 0)),
        ],
        out_specs=pl.BlockSpec((EXPERT_BLOCK, d), lambda b, be: (b, 0)),
        scratch_shapes=[pltpu.VMEM((d, f2), BF16), pltpu.VMEM((f2 // 2, d), BF16)],
    )
    return pl.pallas_call(
        _expert_kernel,
        grid_spec=grid_spec,
        out_shape=jax.ShapeDtypeStruct((n_slots, d), F32),
        compiler_params=pltpu.CompilerParams(
            dimension_semantics=("arbitrary",), vmem_limit_bytes=VMEM_LIMIT),
        name="expert",
    )(block_expert, xs, w_gate_up, b_gu_perm, w_down, b_down, slot_w)


def _final_kernel(x1_ref, y_ref, mod_ref, g_ref, o_ref):
    x = x1_ref[...] + mod_ref[0, 5:6, :] * y_ref[...]
    ms = jnp.mean(x * x, axis=-1, keepdims=True)
    o_ref[...] = x * lax.rsqrt(ms + EPS) * g_ref[...]


def _final(x1, y, mod6, g, seq):
    t = x1.shape[0]
    tm = 1024
    per_b = seq // tm
    d = D_MODEL
    return pl.pallas_call(
        _final_kernel,
        grid=(t // tm,),
        in_specs=[
            pl.BlockSpec((tm, d), lambda i: (i, 0)),
            pl.BlockSpec((tm, d), lambda i: (i, 0)),
            pl.BlockSpec((1, 6, d), lambda i: (i // per_b, 0, 0)),
            pl.BlockSpec((1, d), lambda i: (0, 0)),
        ],
        out_specs=pl.BlockSpec((tm, d), lambda i: (i, 0)),
        out_shape=jax.ShapeDtypeStruct((t, d), F32),
        compiler_params=pltpu.CompilerParams(
            dimension_semantics=("arbitrary",), vmem_limit_bytes=VMEM_LIMIT),
        name="final",
    )(x1, y, mod6, g)


def _split_w(w):
    hi = w.astype(BF16)
    lo = (w - hi.astype(F32)).astype(BF16)
    return hi, lo


def _route(logits, t):
    top_val, top_idx = lax.top_k(logits, TOP_K)
    top_w = jax.nn.softmax(top_val, axis=-1)
    n_assign = t * TOP_K
    n_blocks = -(-n_assign // EXPERT_BLOCK) + N_EXPERTS
    n_slots = n_blocks * EXPERT_BLOCK
    e_flat = top_idx.reshape(n_assign)
    tok_flat = jnp.arange(n_assign, dtype=jnp.int32) // TOP_K
    w_flat = top_w.reshape(n_assign)
    counts = jnp.bincount(e_flat, length=N_EXPERTS)
    group_start = jnp.cumsum(counts) - counts
    blocks_per_e = (counts + EXPERT_BLOCK - 1) // EXPERT_BLOCK
    block_end = jnp.cumsum(blocks_per_e)
    pad_start = (block_end - blocks_per_e) * EXPERT_BLOCK
    order = jnp.argsort(e_flat)
    e_sorted = e_flat[order]
    rank = jnp.arange(n_assign, dtype=jnp.int32) - group_start[e_sorted]
    dest = pad_start[e_sorted] + rank
    slot_tok = jnp.full((n_slots,), t, jnp.int32).at[dest].set(tok_flat[order])
    slot_w = jnp.zeros((n_slots,), F32).at[dest].set(w_flat[order])
    block_expert = jnp.minimum(jnp.searchsorted(block_end, jnp.arange(n_blocks), side='right'), N_EXPERTS - 1)
    return slot_tok, slot_w, block_expert.astype(jnp.int32)


def kernel(x, c, positions, w_ada, b_ada, norm_mix_g, w_in, conv_w, conv_b, b_if, ml_norm_g, ret_norm_g,
           w_branch_ml, w_branch_ret, w_out, norm_ffn_g, w_router, b_router, w_gate_up, b_gate_up, w_down,
           b_down, norm_final_g):
    batch, seq, d = x.shape
    t = batch * seq
    x2 = x.reshape(t, d)
    depth = w_ada.shape[0]
    c8 = jnp.concatenate([c, jnp.zeros((8 - batch, d), c.dtype)], axis=0)

    half = D_QK // 2
    inv = ROPE_BASE ** (-jnp.arange(half, dtype=F32) / half)
    ang = positions.astype(F32)[..., None] * inv
    cos = jnp.cos(ang).reshape(t, half)
    sin = jnp.sin(ang).reshape(t, half)
    cos2 = jnp.concatenate([cos, cos], axis=-1)
    sin2 = jnp.concatenate([-sin, sin], axis=-1)
    L = CHUNK
    log_gamma = jnp.log(1.0 - 2.0 ** (-5.0 - jnp.arange(N_HEADS, dtype=F32)))
    pos = jnp.arange(L, dtype=F32)
    rel = pos[:, None] - pos[None, :]
    decay = jnp.where(rel >= 0, jnp.exp(log_gamma[:, None, None] * jnp.maximum(rel, 0.0)), 0.0)
    kdec = jnp.exp(log_gamma[:, None] * (L - 1 - pos))[:, :, None]
    qdec = jnp.exp(log_gamma[:, None] * (pos + 1.0))[:, :, None]
    gch = jnp.broadcast_to(jnp.exp(log_gamma * L)[:, None, None], (N_HEADS, 1, LANES))

    assert depth == 1, "the final norm is fused after the single layer"
    l = 0
    wa_hi, wa_lo = _split_w(w_ada[l])
    mod = _ada(c8, wa_hi, wa_lo, b_ada[l][None, :])
    mod6 = mod[:batch].reshape(batch, 6, d)

    w = w_in[l]
    w_main = jnp.concatenate(
        [w[:, 0:3072], w[:, 3080:8200]], axis=1).astype(BF16)
    w_if = w[:, 3072:3080]
    zpad = jnp.zeros((d, LANES - N_HEADS), F32)
    wif = jnp.concatenate([w_if[:, :N_HEADS], zpad, w_if[:, N_HEADS:], zpad], axis=1)
    wif_hi, wif_lo = _split_w(wif)
    zb = jnp.zeros((LANES - N_HEADS,), F32)
    b_if2 = jnp.concatenate([b_if[l][:N_HEADS], zb, b_if[l][N_HEADS:], zb])[None, :]

    proj, gates = _inproj(x2, mod6, norm_mix_g[l][None, :], w_main, wif_hi, wif_lo, seq)
    hm, hr = _mix(proj, gates, cos2, sin2, conv_w[l], conv_b[l][None, :], b_if2,
                  ml_norm_g[l][None, :], ret_norm_g[l][None, :], decay, kdec, qdec, gch, batch, seq)

    wr = jnp.concatenate([w_router[l], jnp.zeros((d, LANES - N_EXPERTS), F32)], axis=1)
    wr_hi, wr_lo = _split_w(wr)
    br = jnp.concatenate([b_router[l], jnp.full((LANES - N_EXPERTS,), -1e30, F32)])[None, :]
    x1, hf, logits = _post(hm, hr, proj, x2, mod6, w_branch_ml[l].astype(BF16), w_branch_ret[l].astype(BF16),
                           w_out[l].astype(BF16), norm_ffn_g[l][None, :], wr_hi, wr_lo, br, seq)

    slot_tok, slot_w, block_expert = _route(logits[:, :N_EXPERTS], t)
    hf_pad = jnp.concatenate([hf, jnp.zeros((1, d), hf.dtype)], axis=0)
    xs = hf_pad[slot_tok]
    f2 = w_gate_up.shape[-1]
    bgu = b_gate_up[l].reshape(N_EXPERTS, f2 // (2 * LANES), LANES, 2)
    bgu = jnp.swapaxes(bgu, -1, -2).reshape(N_EXPERTS, 1, f2)
    ys = _experts(block_expert, xs, w_gate_up[l], bgu, w_down[l], b_down[l][:, None, :], slot_w[:, None])
    y = jax.ops.segment_sum(ys, slot_tok, num_segments=t + 1)[:t]
    out = _final(x1, y, mod6, norm_final_g[None, :], seq)
    return out.reshape(batch, seq, d)
```

```python
import functools

import numpy as np
import jax
import jax.numpy as jnp
from jax import lax
from jax.experimental import pallas as pl
from jax.experimental.pallas import tpu as pltpu

D_MODEL = 1024
N_HEADS = 4
D_QK = 128
D_V = 256
CONV_K = 4
ROPE_BASE = 10000.0
CHUNK = 128
N_EXPERTS = 32
TOP_K = 4
SWIGLU_LIMIT = 7.0
SWIGLU_ALPHA = 1.702
EXPERT_BLOCK = 256
EPS = 1e-5

QK_W = N_HEADS * D_QK
V_W = N_HEADS * D_V
LANES = 128
GATE_W = 2 * LANES
V_EXT = D_V + LANES
SLAB = D_MODEL // LANES

C_MLQ, C_MLK, C_MLV, C_MLO = 0, 512, 1024, 2048
C_RQ, C_RK, C_RV, C_RG = 3072, 3584, 4096, 5120
C_GML, C_GRET = 6144, 7168
MIX_W = 6144
PROJ_W = 8192

BF16 = jnp.bfloat16
F32 = jnp.float32
VMEM_LIMIT = 56 * 1024 * 1024


def _dot(a, b):
    return jnp.dot(a, b, preferred_element_type=F32)


def _dot_nt(a, b):
    return lax.dot_general(a, b, (((1,), (1,)), ((), ())), preferred_element_type=F32)


def _split(a):
    hi = a.astype(BF16)
    lo = (a - hi.astype(F32)).astype(BF16)
    return hi, lo


def _dot3(a, b_hi, b_lo):
    a_hi, a_lo = _split(a)
    return _dot(a_hi, b_hi) + (_dot(a_lo, b_hi) + _dot(a_hi, b_lo))


def _sigmoid(x):
    return 1.0 / (1.0 + jnp.exp(-x))


def _rms_mod(x, g, scale, shift):
    ms = jnp.mean(x * x, axis=-1, keepdims=True)
    return (x * lax.rsqrt(ms + EPS) * g) * (1.0 + scale) + shift


def _ada_kernel(c_ref, whi_ref, wlo_ref, b_ref, o_ref):
    c = c_ref[...]
    ca = c * _sigmoid(c)
    o_ref[...] = _dot3(ca, whi_ref[...], wlo_ref[...]) + b_ref[...]


def _ada(c8, w_hi, w_lo, b):
    n = w_hi.shape[1]
    tn = 1024
    return pl.pallas_call(
        _ada_kernel,
        grid=(n // tn,),
        in_specs=[
            pl.BlockSpec((8, D_MODEL), lambda j: (0, 0)),
            pl.BlockSpec((D_MODEL, tn), lambda j: (0, j)),
            pl.BlockSpec((D_MODEL, tn), lambda j: (0, j)),
            pl.BlockSpec((1, tn), lambda j: (0, j)),
        ],
        out_specs=pl.BlockSpec((8, tn), lambda j: (0, j)),
        out_shape=jax.ShapeDtypeStruct((8, n), F32),
        name="ada",
    )(c8, w_hi, w_lo, b)


def _inproj_kernel(x_ref, mod_ref, g_ref, w_ref, wif_hi_ref, wif_lo_ref, proj_ref, gates_ref, hb_ref):
    j = pl.program_id(1)

    @pl.when(j == 0)
    def _():
        h = _rms_mod(x_ref[...], g_ref[...], mod_ref[0, 1:2, :], mod_ref[0, 0:1, :])
        hb_ref[...] = h.astype(BF16)
        gates_ref[...] = _dot3(h, wif_hi_ref[...], wif_lo_ref[...])

    proj_ref[...] = _dot(hb_ref[...], w_ref[...]).astype(BF16)


def _inproj(x2, mod6, g, w_main, wif_hi, wif_lo, seq):
    t = x2.shape[0]
    tm, tn = 512, 2048
    per_b = seq // tm
    return pl.pallas_call(
        _inproj_kernel,
        grid=(t // tm, PROJ_W // tn),
        in_specs=[
            pl.BlockSpec((tm, D_MODEL), lambda i, j: (i, 0)),
            pl.BlockSpec((1, 6, D_MODEL), lambda i, j: (i // per_b, 0, 0)),
            pl.BlockSpec((1, D_MODEL), lambda i, j: (0, 0)),
            pl.BlockSpec((D_MODEL, tn), lambda i, j: (0, j)),
            pl.BlockSpec((D_MODEL, GATE_W), lambda i, j: (0, 0)),
            pl.BlockSpec((D_MODEL, GATE_W), lambda i, j: (0, 0)),
        ],
        out_specs=[
            pl.BlockSpec((tm, tn), lambda i, j: (i, j)),
            pl.BlockSpec((tm, GATE_W), lambda i, j: (i, 0)),
        ],
        out_shape=[
            jax.ShapeDtypeStruct((t, PROJ_W), BF16),
            jax.ShapeDtypeStruct((t, GATE_W), F32),
        ],
        scratch_shapes=[pltpu.VMEM((tm, D_MODEL), BF16)],
        compiler_params=pltpu.CompilerParams(
            dimension_semantics=("arbitrary", "arbitrary"), vmem_limit_bytes=VMEM_LIMIT),
        name="inproj",
    )(x2, mod6, g, w_main, wif_hi, wif_lo)


def _head_norm(h, g):
    mu = jnp.mean(h, axis=-1, keepdims=True)
    d = h - mu
    var = jnp.mean(d * d, axis=-1, keepdims=True)
    return d * lax.rsqrt(var + EPS) * g


def _mix_kernel(proj_ref, gates_ref, cos_ref, sin_ref, convw_ref, convb_ref, bif_ref, mlg_ref, retg_ref,
                decay_ref, kdec_ref, qdec_ref, gch_ref,
                hm_ref, hr_ref,
                prev_ref, c_st, m_st, r_st):
    L = CHUNK

    @pl.when(pl.program_id(1) == 0)
    def _():
        prev_ref[...] = jnp.zeros_like(prev_ref)
        c_st[...] = jnp.zeros_like(c_st)
        m_st[...] = jnp.zeros_like(m_st)
        r_st[...] = jnp.zeros_like(r_st)

    rows = lax.broadcasted_iota(jnp.int32, (L, L), 0)
    cols = lax.broadcasted_iota(jnp.int32, (L, L), 1)
    causal = rows >= cols
    tril = jnp.where(causal, 1.0, 0.0).astype(BF16)

    cur = proj_ref[:, C_MLQ:C_MLQ + 2 * QK_W]
    xx = jnp.concatenate([prev_ref[...], cur], axis=0)
    r2 = lax.broadcasted_iota(jnp.int32, (L, 2 * L), 0)
    c2 = lax.broadcasted_iota(jnp.int32, (L, 2 * L), 1)
    acc = convb_ref[...] + cur.astype(F32) * convw_ref[CONV_K - 1:CONV_K, :]
    for d in range(1, CONV_K):
        shift = jnp.where(c2 == r2 + (L - d), 1.0, 0.0).astype(BF16)
        acc = acc + _dot(shift, xx) * convw_ref[CONV_K - 1 - d:CONV_K - d, :]
    prev_ref[...] = cur
    qk = acc * _sigmoid(acc)

    g = gates_ref[...] + bif_ref[...]
    gi = g[:, :LANES]
    gf = g[:, LANES:]
    lf = jnp.minimum(gf, 0.0) - jnp.log(1.0 + jnp.exp(-jnp.abs(gf)))
    lf_hi, lf_lo = _split(lf)
    a_all = _dot(tril, lf_hi) + _dot(tril, lf_lo)
    a_last = a_all[L - 1:L, :]
    bm = gi - a_all
    bm_t = bm.T
    w_state = a_last + bm
    m_loc = jnp.max(w_state, axis=0, keepdims=True)
    m_prev = m_st[...]
    inter_log = a_all + m_prev
    m_new = jnp.maximum(a_last + m_prev, m_loc)
    s_prev = jnp.exp(a_last + m_prev - m_new)
    s_loc = jnp.exp(m_loc - m_new)
    ws_all = jnp.exp(w_state - m_loc)
    m_st[...] = m_new

    ones_blk = jnp.ones((L, LANES), BF16)
    q_scale = D_QK ** -0.5

    for h in range(N_HEADS):
        q = (qk[:, h * D_QK:(h + 1) * D_QK] * q_scale).astype(BF16)
        k_f = qk[:, QK_W + h * D_QK:QK_W + (h + 1) * D_QK]
        k = k_f.astype(BF16)
        v_ext = jnp.concatenate([proj_ref[:, C_MLV + h * D_V:C_MLV + (h + 1) * D_V], ones_blk], axis=1)
        dlog = jnp.where(causal, a_all[:, h:h + 1] + bm_t[h:h + 1, :], -jnp.inf)
        m_intra = jnp.max(dlog, axis=-1, keepdims=True)
        il = inter_log[:, h:h + 1]
        m_t = jnp.maximum(il, m_intra)
        p = jnp.exp(dlog - m_t) * _dot_nt(q, k)
        isc = jnp.exp(il - m_t)
        tot = _dot(p.astype(BF16), v_ext) + isc * _dot(q, c_st[h].astype(BF16))
        den = tot[:, D_V:D_V + 1]
        hout = tot[:, :D_V] / jnp.maximum(jnp.abs(den), jnp.exp(-m_t))
        kw_t = (k_f * ws_all[:, h:h + 1]).T.astype(BF16)
        c_st[h] = s_prev[:, h:h + 1] * c_st[h] + s_loc[:, h:h + 1] * _dot(kw_t, v_ext)
        y = _head_norm(hout, mlg_ref[:, h * D_V:(h + 1) * D_V])
        o = proj_ref[:, C_MLO + h * D_V:C_MLO + (h + 1) * D_V].astype(F32)
        hm_ref[:, h * D_V:(h + 1) * D_V] = (_sigmoid(o) * y).astype(BF16)

    cos2 = cos_ref[...]
    sin2 = sin_ref[...]
    k_scale = D_QK ** -0.5
    for h in range(N_HEADS):
        q_raw = proj_ref[:, C_RQ + h * D_QK:C_RQ + (h + 1) * D_QK].astype(F32)
        k_raw = proj_ref[:, C_RK + h * D_QK:C_RK + (h + 1) * D_QK].astype(F32)
        q = (q_raw * cos2 + pltpu.roll(q_raw, D_QK // 2, 1) * sin2).astype(BF16)
        k_f = (k_raw * cos2 + pltpu.roll(k_raw, D_QK // 2, 1) * sin2) * k_scale
        v = proj_ref[:, C_RV + h * D_V:C_RV + (h + 1) * D_V]
        sc = _dot_nt(q, k_f.astype(BF16)) * decay_ref[h]
        hret = _dot(sc.astype(BF16), v) + _dot(q, r_st[h].astype(BF16)) * qdec_ref[h]
        kd_t = (k_f * kdec_ref[h]).T.astype(BF16)
        r_st[h] = gch_ref[h][:, 0:1] * r_st[h] + _dot(kd_t, v)
        y = _head_norm(hret, retg_ref[:, h * D_V:(h + 1) * D_V])
        gt = proj_ref[:, C_RG + h * D_V:C_RG + (h + 1) * D_V].astype(F32)
        hr_ref[:, h * D_V:(h + 1) * D_V] = (gt * _sigmoid(gt) * y).astype(BF16)


def _mix(proj, gates, cos2, sin2, conv_w, conv_b, b_if2, ml_g, ret_g, decay, kdec, qdec, gch, batch, seq):
    t = proj.shape[0]
    nc = seq // CHUNK
    L = CHUNK
    full = lambda shape: pl.BlockSpec(shape, lambda b, c: (0,) * len(shape))
    return pl.pallas_call(
        _mix_kernel,
        grid=(batch, nc),
        in_specs=[
            pl.BlockSpec((L, MIX_W), lambda b, c: (b * nc + c, 0)),
            pl.BlockSpec((L, GATE_W), lambda b, c: (b * nc + c, 0)),
            pl.BlockSpec((L, LANES), lambda b, c: (b * nc + c, 0)),
            pl.BlockSpec((L, LANES), lambda b, c: (b * nc + c, 0)),
            full((CONV_K, 2 * QK_W)),
            full((1, 2 * QK_W)),
            full((1, GATE_W)),
            full((1, V_W)),
            full((1, V_W)),
            full((N_HEADS, L, L)),
            full((N_HEADS, L, 1)),
            full((N_HEADS, L, 1)),
            full((N_HEADS, 1, LANES)),
        ],
        out_specs=[
            pl.BlockSpec((L, V_W), lambda b, c: (b * nc + c, 0)),
            pl.BlockSpec((L, V_W), lambda b, c: (b * nc + c, 0)),
        ],
        out_shape=[jax.ShapeDtypeStruct((t, V_W), BF16), jax.ShapeDtypeStruct((t, V_W), BF16)],
        scratch_shapes=[
            pltpu.VMEM((L, 2 * QK_W), BF16),
            pltpu.VMEM((N_HEADS, D_QK, V_EXT), F32),
            pltpu.VMEM((1, LANES), F32),
            pltpu.VMEM((N_HEADS, D_QK, D_V), F32),
        ],
        compiler_params=pltpu.CompilerParams(
            dimension_semantics=("arbitrary", "arbitrary"), vmem_limit_bytes=VMEM_LIMIT),
        name="mix",
    )(proj, gates, cos2, sin2, conv_w, conv_b, b_if2, ml_g, ret_g, decay, kdec, qdec, gch)


def _to_slabs(ref, val):
    rows = val.shape[0]
    for s in range(SLAB):
        ref[pl.ds(s, rows, stride=SLAB), :] = val[:, s * LANES:(s + 1) * LANES]


def _from_slabs(ref, first_slab, rows):
    return jnp.concatenate(
        [ref[pl.ds(first_slab * SLAB + s, rows, stride=SLAB), :] for s in range(SLAB)], axis=1)


def _post_kernel(hm_ref, hr_ref, gm_ref, gr_ref, x_ref, mod_ref, wbm_ref, wbr_ref, wout_ref, g_ref,
                 wr_hi_ref, wr_lo_ref, br_ref, x1_ref, hf_ref, logit_ref):
    ym = _dot(hm_ref[...], wbm_ref[...])
    yr = _dot(hr_ref[...], wbr_ref[...])
    y = _sigmoid(gm_ref[...].astype(F32)) * ym + _sigmoid(gr_ref[...].astype(F32)) * yr
    o = _dot(y.astype(BF16), wout_ref[...])
    x1 = x_ref[...] + mod_ref[0, 2:3, :] * o
    x1_ref[...] = x1
    hf = _rms_mod(x1, g_ref[...], mod_ref[0, 4:5, :], mod_ref[0, 3:4, :])
    _to_slabs(hf_ref, hf)
    logit_ref[...] = _dot3(hf, wr_hi_ref[...], wr_lo_ref[...]) + br_ref[...]


def _post(hm, hr, proj, x2, mod6, wbm, wbr, wout, g, wr_hi, wr_lo, br, seq):
    t = x2.shape[0]
    tm = 512
    per_b = seq // tm
    d = D_MODEL
    const = lambda shape: pl.BlockSpec(shape, lambda i: (0,) * len(shape))
    return pl.pallas_call(
        _post_kernel,
        grid=(t // tm,),
        in_specs=[
            pl.BlockSpec((tm, d), lambda i: (i, 0)),
            pl.BlockSpec((tm, d), lambda i: (i, 0)),
            pl.BlockSpec((tm, d), lambda i: (i, C_GML // d)),
            pl.BlockSpec((tm, d), lambda i: (i, C_GRET // d)),
            pl.BlockSpec((tm, d), lambda i: (i, 0)),
            pl.BlockSpec((1, 6, d), lambda i: (i // per_b, 0, 0)),
            const((d, d)), const((d, d)), const((d, d)),
            const((1, d)),
            const((d, LANES)), const((d, LANES)), const((1, LANES)),
        ],
        out_specs=[
            pl.BlockSpec((tm, d), lambda i: (i, 0)),
            pl.BlockSpec((tm * SLAB, LANES), lambda i: (i, 0)),
            pl.BlockSpec((tm, LANES), lambda i: (i, 0)),
        ],
        out_shape=[
            jax.ShapeDtypeStruct((t, d), F32),
            jax.ShapeDtypeStruct((t * SLAB, LANES), F32),
            jax.ShapeDtypeStruct((t, LANES), F32),
        ],
        compiler_params=pltpu.CompilerParams(
            dimension_semantics=("arbitrary",), vmem_limit_bytes=VMEM_LIMIT),
        name="post",
    )(hm, hr, proj, proj, x2, mod6, wbm, wbr, wout, g, wr_hi, wr_lo, br)


def _route_kernel(logit_ref, dest_ref, w_ref, bend_ref, cnt_st, pad_st):
    ph = pl.program_id(0)
    i = pl.program_id(1)
    tm = logit_ref.shape[0]
    lane = lax.broadcasted_iota(jnp.int32, (tm, LANES), 1)
    l = logit_ref[...]
    onehots, vals = [], []
    for _ in range(TOP_K):
        m = jnp.max(l, axis=-1, keepdims=True)
        idx = jnp.min(jnp.where(l == m, lane, LANES), axis=-1, keepdims=True)
        oh = lane == idx
        onehots.append(oh)
        vals.append(m)
        l = jnp.where(oh, -jnp.inf, l)
    sel = jnp.where(onehots[0] | onehots[1] | onehots[2] | onehots[3], 1.0, 0.0)
    tile_cnt = jnp.sum(sel, axis=0, keepdims=True)

    @pl.when(jnp.logical_and(ph == 0, i == 0))
    def _():
        cnt_st[...] = jnp.zeros_like(cnt_st)

    @pl.when(ph == 0)
    def _():
        cnt_st[...] = cnt_st[...] + tile_cnt

    @pl.when(jnp.logical_and(ph == 1, i == 0))
    def _():
        blocks = jnp.floor((cnt_st[...] + (EXPERT_BLOCK - 1)) * (1.0 / EXPERT_BLOCK))
        r = lax.broadcasted_iota(jnp.int32, (LANES, LANES), 0)
        c = lax.broadcasted_iota(jnp.int32, (LANES, LANES), 1)
        upper = jnp.where(r < c, 1.0, 0.0).astype(BF16)
        blocks8 = jnp.broadcast_to(blocks, (8, LANES))
        excl = _dot(blocks8.astype(BF16), upper)
        pad_st[...] = excl[0:1, :] * EXPERT_BLOCK
        bend_ref[...] = excl + blocks8
        cnt_st[...] = jnp.zeros_like(cnt_st)

    @pl.when(ph == 1)
    def _():
        carry = cnt_st[...]
        r = lax.broadcasted_iota(jnp.int32, (tm, tm), 0)
        c = lax.broadcasted_iota(jnp.int32, (tm, tm), 1)
        lower = jnp.where(r > c, 1.0, 0.0).astype(BF16)
        base = pad_st[...] + carry + _dot(lower, sel.astype(BF16))
        dest = jnp.zeros((tm, LANES), F32)
        wts = jnp.zeros((tm, LANES), F32)
        ex = [jnp.exp(v - vals[0]) for v in vals]
        den = ex[0] + ex[1] + ex[2] + ex[3]
        for k in range(TOP_K):
            dk = jnp.sum(jnp.where(onehots[k], base, 0.0), axis=-1, keepdims=True)
            dest = jnp.where(lane == k, dk, dest)
            wts = jnp.where(lane == k, ex[k] / den, wts)
        dest_ref[...] = dest.astype(jnp.int32)
        w_ref[...] = wts
        cnt_st[...] = carry + tile_cnt


def _route(logits):
    t = logits.shape[0]
    tm = 512
    return pl.pallas_call(
        _route_kernel,
        grid=(2, t // tm),
        in_specs=[pl.BlockSpec((tm, LANES), lambda ph, i: (i, 0))],
        out_specs=[
            pl.BlockSpec((tm, LANES), lambda ph, i: (i * ph, 0)),
            pl.BlockSpec((tm, LANES), lambda ph, i: (i * ph, 0)),
            pl.BlockSpec((8, LANES), lambda ph, i: (0, 0)),
        ],
        out_shape=[
            jax.ShapeDtypeStruct((t, LANES), jnp.int32),
            jax.ShapeDtypeStruct((t, LANES), F32),
            jax.ShapeDtypeStruct((8, LANES), F32),
        ],
        scratch_shapes=[pltpu.VMEM((1, LANES), F32), pltpu.VMEM((1, LANES), F32)],
        compiler_params=pltpu.CompilerParams(
            dimension_semantics=("arbitrary", "arbitrary"), vmem_limit_bytes=VMEM_LIMIT),
        name="route",
    )(logits)


DMA_UNROLL = 8


def _row_copy(src_ref, src_slab, dst_ref, dst_slab, sem):
    src = pl.multiple_of(src_slab * SLAB, SLAB)
    dst = pl.multiple_of(dst_slab * SLAB, SLAB)
    return pltpu.make_async_copy(src_ref.at[pl.ds(src, SLAB), :], dst_ref.at[pl.ds(dst, SLAB), :], sem)


def _dispatch_kernel(dest_ref, hf_hbm, xs_in, xs_out, sem):
    del xs_in
    n_assign = dest_ref.shape[0]
    base_tok = pl.program_id(0) * (n_assign // TOP_K)

    def body(g, carry):
        for u in range(DMA_UNROLL):
            tok = base_tok + g * (DMA_UNROLL // TOP_K) + u // TOP_K
            _row_copy(hf_hbm, tok, xs_out, dest_ref[g * DMA_UNROLL + u], sem).start()
        return carry

    lax.fori_loop(0, n_assign // DMA_UNROLL, body, 0)
    rows = n_assign * SLAB
    pltpu.make_async_copy(hf_hbm.at[pl.ds(0, rows), :], xs_out.at[pl.ds(0, rows), :], sem).wait()


def _dispatch(dest_flat, hf_slab, n_slots):
    n_assign = dest_flat.shape[0]
    step = 8192
    xs0 = jnp.zeros((n_slots * SLAB, LANES), F32)
    return pl.pallas_call(
        _dispatch_kernel,
        grid=(n_assign // step,),
        in_specs=[
            pl.BlockSpec((step,), lambda i: (i,), memory_space=pltpu.SMEM),
            pl.BlockSpec(memory_space=pl.ANY),
            pl.BlockSpec(memory_space=pl.ANY),
        ],
        out_specs=pl.BlockSpec(memory_space=pl.ANY),
        out_shape=jax.ShapeDtypeStruct((n_slots * SLAB, LANES), F32),
        scratch_shapes=[pltpu.SemaphoreType.DMA],
        input_output_aliases={2: 0},
        compiler_params=pltpu.CompilerParams(dimension_semantics=("arbitrary",)),
        name="dispatch",
    )(dest_flat, hf_slab, xs0)


def _expert_kernel(be_ref, xs_ref, wgu_ref, bgu_ref, wd_ref, bd_ref, ys_ref, wgu_bf, wd_bf):
    b = pl.program_id(0)
    prev = be_ref[jnp.maximum(b - 1, 0)]
    first = jnp.logical_or(b == 0, be_ref[b] != prev)
    two = 2 * LANES

    @pl.when(first)
    def _():
        r = lax.broadcasted_iota(jnp.int32, (two, two), 0)
        c = lax.broadcasted_iota(jnp.int32, (two, two), 1)
        src = jnp.where(c < LANES, 2 * c, 2 * (c - LANES) + 1)
        perm = jnp.where(r == src, 1.0, 0.0).astype(BF16)
        for blk in range(wgu_bf.shape[1] // two):
            wb = wgu_ref[0, :, blk * two:(blk + 1) * two].astype(BF16)
            wgu_bf[:, blk * two:(blk + 1) * two] = _dot(wb, perm).astype(BF16)
        wd_bf[...] = wd_ref[0].astype(BF16)

    x = _from_slabs(xs_ref, 0, EXPERT_BLOCK).astype(BF16)
    gu = _dot(x, wgu_bf[...]) + bgu_ref[0]
    n_blk = gu.shape[1] // two
    x_glu = jnp.concatenate([gu[:, i * two:i * two + LANES] for i in range(n_blk)], axis=1)
    x_lin = jnp.concatenate([gu[:, i * two + LANES:(i + 1) * two] for i in range(n_blk)], axis=1)
    x_glu = jnp.minimum(x_glu, SWIGLU_LIMIT)
    x_lin = jnp.clip(x_lin, -SWIGLU_LIMIT, SWIGLU_LIMIT)
    act = x_glu * _sigmoid(SWIGLU_ALPHA * x_glu) * (x_lin + 1.0)
    y = _dot(act.astype(BF16), wd_bf[...]) + bd_ref[0]
    _to_slabs(ys_ref, y)


def _experts(block_expert, xs, w_gate_up, b_gu_perm, w_down, b_down):
    n_blocks = block_expert.shape[0]
    d = D_MODEL
    f2 = w_gate_up.shape[-1]
    blk_rows = EXPERT_BLOCK * SLAB
    grid_spec = pltpu.PrefetchScalarGridSpec(
        num_scalar_prefetch=1,
        grid=(n_blocks,),
        in_specs=[
            pl.BlockSpec((blk_rows, LANES), lambda b, be: (b, 0)),
            pl.BlockSpec((1, d, f2), lambda b, be: (be[b], 0, 0)),
            pl.BlockSpec((1, 1, f2), lambda b, be: (be[b], 0, 0)),
            pl.BlockSpec((1, f2 // 2, d), lambda b, be: (be[b], 0, 0)),
            pl.BlockSpec((1, 1, d), lambda b, be: (be[b], 0, 0)),
        ],
        out_specs=pl.BlockSpec((blk_rows, LANES), lambda b, be: (b, 0)),
        scratch_shapes=[pltpu.VMEM((d, f2), BF16), pltpu.VMEM((f2 // 2, d), BF16)],
    )
    return pl.pallas_call(
        _expert_kernel,
        grid_spec=grid_spec,
        out_shape=jax.ShapeDtypeStruct((n_blocks * blk_rows, LANES), F32),
        compiler_params=pltpu.CompilerParams(
            dimension_semantics=("arbitrary",), vmem_limit_bytes=VMEM_LIMIT),
        name="expert",
    )(block_expert, xs, w_gate_up, b_gu_perm, w_down, b_down)


def _combine_kernel(dest_ref, ys_hbm, w_ref, x1_ref, mod_ref, g_ref, o_ref, buf, sem):
    tm = x1_ref.shape[0]
    n_assign = tm * TOP_K

    def body(g, carry):
        for u in range(DMA_UNROLL):
            t_local = g * (DMA_UNROLL // TOP_K) + u // TOP_K
            k = u % TOP_K
            _row_copy(ys_hbm, dest_ref[g * DMA_UNROLL + u], buf, k * tm + t_local, sem).start()
        return carry

    lax.fori_loop(0, n_assign // DMA_UNROLL, body, 0)
    pltpu.make_async_copy(ys_hbm.at[pl.ds(0, n_assign * SLAB), :], buf, sem).wait()

    w = w_ref[...]
    y = w[:, 0:1] * _from_slabs(buf, 0, tm)
    for k in range(1, TOP_K):
        y = y + w[:, k:k + 1] * _from_slabs(buf, k * tm, tm)
    x = x1_ref[...] + mod_ref[0, 5:6, :] * y
    ms = jnp.mean(x * x, axis=-1, keepdims=True)
    o_ref[...] = x * lax.rsqrt(ms + EPS) * g_ref[...]


def _combine(dest_flat, ys, wts, x1, mod6, g, seq):
    t = x1.shape[0]
    tm = 256
    per_b = seq // tm
    d = D_MODEL
    return pl.pallas_call(
        _combine_kernel,
        grid=(t // tm,),
        in_specs=[
            pl.BlockSpec((tm * TOP_K,), lambda i: (i,), memory_space=pltpu.SMEM),
            pl.BlockSpec(memory_space=pl.ANY),
            pl.BlockSpec((tm, LANES), lambda i: (i, 0)),
            pl.BlockSpec((tm, d), lambda i: (i, 0)),
            pl.BlockSpec((1, 6, d), lambda i: (i // per_b, 0, 0)),
            pl.BlockSpec((1, d), lambda i: (0, 0)),
        ],
        out_specs=pl.BlockSpec((tm, d), lambda i: (i, 0)),
        out_shape=jax.ShapeDtypeStruct((t, d), F32),
        scratch_shapes=[pltpu.VMEM((tm * TOP_K * SLAB, LANES), F32), pltpu.SemaphoreType.DMA],
        compiler_params=pltpu.CompilerParams(
            dimension_semantics=("arbitrary",), vmem_limit_bytes=VMEM_LIMIT),
        name="combine",
    )(dest_flat, ys, wts, x1, mod6, g)


def _split_w(w):
    hi = w.astype(BF16)
    lo = (w - hi.astype(F32)).astype(BF16)
    return hi, lo


def kernel(x, c, positions, w_ada, b_ada, norm_mix_g, w_in, conv_w, conv_b, b_if, ml_norm_g, ret_norm_g,
           w_branch_ml, w_branch_ret, w_out, norm_ffn_g, w_router, b_router, w_gate_up, b_gate_up, w_down,
           b_down, norm_final_g):
    batch, seq, d = x.shape
    t = batch * seq
    x2 = x.reshape(t, d)
    depth = w_ada.shape[0]
    c8 = jnp.concatenate([c, jnp.zeros((8 - batch, d), c.dtype)], axis=0)

    half = D_QK // 2
    inv = ROPE_BASE ** (-jnp.arange(half, dtype=F32) / half)
    ang = positions.astype(F32)[..., None] * inv
    cos = jnp.cos(ang).reshape(t, half)
    sin = jnp.sin(ang).reshape(t, half)
    cos2 = jnp.concatenate([cos, cos], axis=-1)
    sin2 = jnp.concatenate([-sin, sin], axis=-1)
    L = CHUNK
    log_gamma = jnp.log(1.0 - 2.0 ** (-5.0 - jnp.arange(N_HEADS, dtype=F32)))
    pos = jnp.arange(L, dtype=F32)
    rel = pos[:, None] - pos[None, :]
    decay = jnp.where(rel >= 0, jnp.exp(log_gamma[:, None, None] * jnp.maximum(rel, 0.0)), 0.0)
    kdec = jnp.exp(log_gamma[:, None] * (L - 1 - pos))[:, :, None]
    qdec = jnp.exp(log_gamma[:, None] * (pos + 1.0))[:, :, None]
    gch = jnp.broadcast_to(jnp.exp(log_gamma * L)[:, None, None], (N_HEADS, 1, LANES))

    assert depth == 1, "the final norm is fused after the single layer"
    l = 0
    wa_hi, wa_lo = _split_w(w_ada[l])
    mod = _ada(c8, wa_hi, wa_lo, b_ada[l][None, :])
    mod6 = mod[:batch].reshape(batch, 6, d)

    w = w_in[l]
    w_main = jnp.concatenate(
        [w[:, 0:3072], w[:, 3080:8200]], axis=1).astype(BF16)
    w_if = w[:, 3072:3080]
    zpad = jnp.zeros((d, LANES - N_HEADS), F32)
    wif = jnp.concatenate([w_if[:, :N_HEADS], zpad, w_if[:, N_HEADS:], zpad], axis=1)
    wif_hi, wif_lo = _split_w(wif)
    zb = jnp.zeros((LANES - N_HEADS,), F32)
    b_if2 = jnp.concatenate([b_if[l][:N_HEADS], zb, b_if[l][N_HEADS:], zb])[None, :]

    proj, gates = _inproj(x2, mod6, norm_mix_g[l][None, :], w_main, wif_hi, wif_lo, seq)
    hm, hr = _mix(proj, gates, cos2, sin2, conv_w[l], conv_b[l][None, :], b_if2,
                  ml_norm_g[l][None, :], ret_norm_g[l][None, :], decay, kdec, qdec, gch, batch, seq)

    wr = jnp.concatenate([w_router[l], jnp.zeros((d, LANES - N_EXPERTS), F32)], axis=1)
    wr_hi, wr_lo = _split_w(wr)
    br = jnp.concatenate([b_router[l], jnp.full((LANES - N_EXPERTS,), -1e30, F32)])[None, :]
    x1, hf, logits = _post(hm, hr, proj, x2, mod6, w_branch_ml[l].astype(BF16), w_branch_ret[l].astype(BF16),
                           w_out[l].astype(BF16), norm_ffn_g[l][None, :], wr_hi, wr_lo, br, seq)

    n_assign = t * TOP_K
    n_blocks = -(-n_assign // EXPERT_BLOCK) + N_EXPERTS
    dest, wts, block_end = _route(logits)
    dest_flat = dest[:, :TOP_K].reshape(n_assign)
    bend = block_end[0, :N_EXPERTS].astype(jnp.int32)
    block_expert = jnp.minimum(
        jnp.sum((bend[None, :] <= jnp.arange(n_blocks, dtype=jnp.int32)[:, None]).astype(jnp.int32), axis=1),
        N_EXPERTS - 1)
    xs = _dispatch(dest_flat, hf, n_blocks * EXPERT_BLOCK)
    f2 = w_gate_up.shape[-1]
    bgu = b_gate_up[l].reshape(N_EXPERTS, f2 // (2 * LANES), LANES, 2)
    bgu = jnp.swapaxes(bgu, -1, -2).reshape(N_EXPERTS, 1, f2)
    ys = _experts(block_expert, xs, w_gate_up[l], bgu, w_down[l], b_down[l][:, None, :])
    out = _combine(dest_flat, ys, wts, x1, mod6, norm_final_g[None, :], seq)
    return out.reshape(batch, seq, d)
```

```python
import functools

import numpy as np
import jax
import jax.numpy as jnp
from jax import lax
from jax.experimental import pallas as pl
from jax.experimental.pallas import tpu as pltpu

D_MODEL = 1024
N_HEADS = 4
D_QK = 128
D_V = 256
CONV_K = 4
ROPE_BASE = 10000.0
CHUNK = 128
N_EXPERTS = 32
TOP_K = 4
SWIGLU_LIMIT = 7.0
SWIGLU_ALPHA = 1.702
EXPERT_BLOCK = 256
EPS = 1e-5

QK_W = N_HEADS * D_QK
V_W = N_HEADS * D_V
LANES = 128
GATE_W = 2 * LANES
V_EXT = D_V + LANES
SLAB = D_MODEL // LANES

C_MLQ, C_MLK, C_MLV, C_MLO = 0, 512, 1024, 2048
C_RQ, C_RK, C_RV, C_RG = 3072, 3584, 4096, 5120
C_GML, C_GRET = 6144, 7168
MIX_W = 6144
PROJ_W = 8192

BF16 = jnp.bfloat16
F32 = jnp.float32
VMEM_LIMIT = 56 * 1024 * 1024


def _dot(a, b):
    return jnp.dot(a, b, preferred_element_type=F32)


def _dot_nt(a, b):
    return lax.dot_general(a, b, (((1,), (1,)), ((), ())), preferred_element_type=F32)


def _split(a):
    hi = a.astype(BF16)
    lo = (a - hi.astype(F32)).astype(BF16)
    return hi, lo


def _dot3(a, b_hi, b_lo):
    a_hi, a_lo = _split(a)
    return _dot(a_hi, b_hi) + (_dot(a_lo, b_hi) + _dot(a_hi, b_lo))


def _sigmoid(x):
    return 1.0 / (1.0 + jnp.exp(-x))


def _rms_mod(x, g, scale, shift):
    ms = jnp.mean(x * x, axis=-1, keepdims=True)
    return (x * lax.rsqrt(ms + EPS) * g) * (1.0 + scale) + shift


def _ada_kernel(c_ref, whi_ref, wlo_ref, b_ref, o_ref):
    c = c_ref[...]
    ca = c * _sigmoid(c)
    o_ref[...] = _dot3(ca, whi_ref[...], wlo_ref[...]) + b_ref[...]


def _ada(c8, w_hi, w_lo, b):
    n = w_hi.shape[1]
    tn = 1024
    return pl.pallas_call(
        _ada_kernel,
        grid=(n // tn,),
        in_specs=[
            pl.BlockSpec((8, D_MODEL), lambda j: (0, 0)),
            pl.BlockSpec((D_MODEL, tn), lambda j: (0, j)),
            pl.BlockSpec((D_MODEL, tn), lambda j: (0, j)),
            pl.BlockSpec((1, tn), lambda j: (0, j)),
        ],
        out_specs=pl.BlockSpec((8, tn), lambda j: (0, j)),
        out_shape=jax.ShapeDtypeStruct((8, n), F32),
        name="ada",
    )(c8, w_hi, w_lo, b)


def _inproj_kernel(x_ref, mod_ref, g_ref, w_ref, wif_hi_ref, wif_lo_ref, proj_ref, gates_ref, hb_ref):
    j = pl.program_id(1)

    @pl.when(j == 0)
    def _():
        h = _rms_mod(x_ref[...], g_ref[...], mod_ref[0, 1:2, :], mod_ref[0, 0:1, :])
        hb_ref[...] = h.astype(BF16)
        gates_ref[...] = _dot3(h, wif_hi_ref[...], wif_lo_ref[...])

    proj_ref[...] = _dot(hb_ref[...], w_ref[...]).astype(BF16)


def _inproj(x2, mod6, g, w_main, wif_hi, wif_lo, seq):
    t = x2.shape[0]
    tm, tn = 512, 2048
    per_b = seq // tm
    return pl.pallas_call(
        _inproj_kernel,
        grid=(t // tm, PROJ_W // tn),
        in_specs=[
            pl.BlockSpec((tm, D_MODEL), lambda i, j: (i, 0)),
            pl.BlockSpec((1, 6, D_MODEL), lambda i, j: (i // per_b, 0, 0)),
            pl.BlockSpec((1, D_MODEL), lambda i, j: (0, 0)),
            pl.BlockSpec((D_MODEL, tn), lambda i, j: (0, j)),
            pl.BlockSpec((D_MODEL, GATE_W), lambda i, j: (0, 0)),
            pl.BlockSpec((D_MODEL, GATE_W), lambda i, j: (0, 0)),
        ],
        out_specs=[
            pl.BlockSpec((tm, tn), lambda i, j: (i, j)),
            pl.BlockSpec((tm, GATE_W), lambda i, j: (i, 0)),
        ],
        out_shape=[
            jax.ShapeDtypeStruct((t, PROJ_W), BF16),
            jax.ShapeDtypeStruct((t, GATE_W), F32),
        ],
        scratch_shapes=[pltpu.VMEM((tm, D_MODEL), BF16)],
        compiler_params=pltpu.CompilerParams(
            dimension_semantics=("arbitrary", "arbitrary"), vmem_limit_bytes=VMEM_LIMIT),
        name="inproj",
    )(x2, mod6, g, w_main, wif_hi, wif_lo)


def _head_norm(h, g):
    mu = jnp.mean(h, axis=-1, keepdims=True)
    d = h - mu
    var = jnp.mean(d * d, axis=-1, keepdims=True)
    return d * lax.rsqrt(var + EPS) * g


def _mix_kernel(proj_ref, gates_ref, cos_ref, sin_ref, convw_ref, convb_ref, bif_ref, mlg_ref, retg_ref,
                decay_ref, kdec_ref, qdec_ref, gch_ref,
                hm_ref, hr_ref,
                prev_ref, c_st, m_st, r_st):
    L = CHUNK

    @pl.when(pl.program_id(1) == 0)
    def _():
        prev_ref[...] = jnp.zeros_like(prev_ref)
        c_st[...] = jnp.zeros_like(c_st)
        m_st[...] = jnp.zeros_like(m_st)
        r_st[...] = jnp.zeros_like(r_st)

    rows = lax.broadcasted_iota(jnp.int32, (L, L), 0)
    cols = lax.broadcasted_iota(jnp.int32, (L, L), 1)
    causal = rows >= cols
    tril = jnp.where(causal, 1.0, 0.0).astype(BF16)

    cur = proj_ref[:, C_MLQ:C_MLQ + 2 * QK_W]
    xx = jnp.concatenate([prev_ref[...], cur], axis=0)
    r2 = lax.broadcasted_iota(jnp.int32, (L, 2 * L), 0)
    c2 = lax.broadcasted_iota(jnp.int32, (L, 2 * L), 1)
    acc = convb_ref[...] + cur.astype(F32) * convw_ref[CONV_K - 1:CONV_K, :]
    for d in range(1, CONV_K):
        shift = jnp.where(c2 == r2 + (L - d), 1.0, 0.0).astype(BF16)
        acc = acc + _dot(shift, xx) * convw_ref[CONV_K - 1 - d:CONV_K - d, :]
    prev_ref[...] = cur
    qk = acc * _sigmoid(acc)

    g = gates_ref[...] + bif_ref[...]
    gi = g[:, :LANES]
    gf = g[:, LANES:]
    lf = jnp.minimum(gf, 0.0) - jnp.log(1.0 + jnp.exp(-jnp.abs(gf)))
    lf_hi, lf_lo = _split(lf)
    a_all = _dot(tril, lf_hi) + _dot(tril, lf_lo)
    a_last = a_all[L - 1:L, :]
    bm = gi - a_all
    bm_t = bm.T
    w_state = a_last + bm
    m_loc = jnp.max(w_state, axis=0, keepdims=True)
    m_prev = m_st[...]
    inter_log = a_all + m_prev
    m_new = jnp.maximum(a_last + m_prev, m_loc)
    s_prev = jnp.exp(a_last + m_prev - m_new)
    s_loc = jnp.exp(m_loc - m_new)
    ws_all = jnp.exp(w_state - m_loc)
    m_st[...] = m_new

    ones_blk = jnp.ones((L, LANES), BF16)
    q_scale = D_QK ** -0.5

    for h in range(N_HEADS):
        q = (qk[:, h * D_QK:(h + 1) * D_QK] * q_scale).astype(BF16)
        k_f = qk[:, QK_W + h * D_QK:QK_W + (h + 1) * D_QK]
        k = k_f.astype(BF16)
        v_ext = jnp.concatenate([proj_ref[:, C_MLV + h * D_V:C_MLV + (h + 1) * D_V], ones_blk], axis=1)
        dlog = jnp.where(causal, a_all[:, h:h + 1] + bm_t[h:h + 1, :], -jnp.inf)
        m_intra = jnp.max(dlog, axis=-1, keepdims=True)
        il = inter_log[:, h:h + 1]
        m_t = jnp.maximum(il, m_intra)
        p = jnp.exp(dlog - m_t) * _dot_nt(q, k)
        isc = jnp.exp(il - m_t)
        tot = _dot(p.astype(BF16), v_ext) + isc * _dot(q, c_st[h].astype(BF16))
        den = tot[:, D_V:D_V + 1]
        hout = tot[:, :D_V] / jnp.maximum(jnp.abs(den), jnp.exp(-m_t))
        kw_t = (k_f * ws_all[:, h:h + 1]).T.astype(BF16)
        c_st[h] = s_prev[:, h:h + 1] * c_st[h] + s_loc[:, h:h + 1] * _dot(kw_t, v_ext)
        y = _head_norm(hout, mlg_ref[:, h * D_V:(h + 1) * D_V])
        o = proj_ref[:, C_MLO + h * D_V:C_MLO + (h + 1) * D_V].astype(F32)
        hm_ref[:, h * D_V:(h + 1) * D_V] = (_sigmoid(o) * y).astype(BF16)

    cos2 = cos_ref[...]
    sin2 = sin_ref[...]
    k_scale = D_QK ** -0.5
    for h in range(N_HEADS):
        q_raw = proj_ref[:, C_RQ + h * D_QK:C_RQ + (h + 1) * D_QK].astype(F32)
        k_raw = proj_ref[:, C_RK + h * D_QK:C_RK + (h + 1) * D_QK].astype(F32)
        q = (q_raw * cos2 + pltpu.roll(q_raw, D_QK // 2, 1) * sin2).astype(BF16)
        k_f = (k_raw * cos2 + pltpu.roll(k_raw, D_QK // 2, 1) * sin2) * k_scale
        v = proj_ref[:, C_RV + h * D_V:C_RV + (h + 1) * D_V]
        sc = _dot_nt(q, k_f.astype(BF16)) * decay_ref[h]
        hret = _dot(sc.astype(BF16), v) + _dot(q, r_st[h].astype(BF16)) * qdec_ref[h]
        kd_t = (k_f * kdec_ref[h]).T.astype(BF16)
        r_st[h] = gch_ref[h][:, 0:1] * r_st[h] + _dot(kd_t, v)
        y = _head_norm(hret, retg_ref[:, h * D_V:(h + 1) * D_V])
        gt = proj_ref[:, C_RG + h * D_V:C_RG + (h + 1) * D_V].astype(F32)
        hr_ref[:, h * D_V:(h + 1) * D_V] = (gt * _sigmoid(gt) * y).astype(BF16)


def _mix(proj, gates, cos2, sin2, conv_w, conv_b, b_if2, ml_g, ret_g, decay, kdec, qdec, gch, batch, seq):
    t = proj.shape[0]
    nc = seq // CHUNK
    L = CHUNK
    full = lambda shape: pl.BlockSpec(shape, lambda b, c: (0,) * len(shape))
    return pl.pallas_call(
        _mix_kernel,
        grid=(batch, nc),
        in_specs=[
            pl.BlockSpec((L, MIX_W), lambda b, c: (b * nc + c, 0)),
            pl.BlockSpec((L, GATE_W), lambda b, c: (b * nc + c, 0)),
            pl.BlockSpec((L, LANES), lambda b, c: (b * nc + c, 0)),
            pl.BlockSpec((L, LANES), lambda b, c: (b * nc + c, 0)),
            full((CONV_K, 2 * QK_W)),
            full((1, 2 * QK_W)),
            full((1, GATE_W)),
            full((1, V_W)),
            full((1, V_W)),
            full((N_HEADS, L, L)),
            full((N_HEADS, L, 1)),
            full((N_HEADS, L, 1)),
            full((N_HEADS, 1, LANES)),
        ],
        out_specs=[
            pl.BlockSpec((L, V_W), lambda b, c: (b * nc + c, 0)),
            pl.BlockSpec((L, V_W), lambda b, c: (b * nc + c, 0)),
        ],
        out_shape=[jax.ShapeDtypeStruct((t, V_W), BF16), jax.ShapeDtypeStruct((t, V_W), BF16)],
        scratch_shapes=[
            pltpu.VMEM((L, 2 * QK_W), BF16),
            pltpu.VMEM((N_HEADS, D_QK, V_EXT), F32),
            pltpu.VMEM((1, LANES), F32),
            pltpu.VMEM((N_HEADS, D_QK, D_V), F32),
        ],
        compiler_params=pltpu.CompilerParams(
            dimension_semantics=("arbitrary", "arbitrary"), vmem_limit_bytes=VMEM_LIMIT),
        name="mix",
    )(proj, gates, cos2, sin2, conv_w, conv_b, b_if2, ml_g, ret_g, decay, kdec, qdec, gch)


def _to_slabs(ref, val):
    rows = val.shape[0]
    for s in range(SLAB):
        ref[pl.ds(s, rows, stride=SLAB), :] = val[:, s * LANES:(s + 1) * LANES]


def _from_slabs(ref, first_slab, rows):
    return jnp.concatenate(
        [ref[pl.ds(first_slab * SLAB + s, rows, stride=SLAB), :] for s in range(SLAB)], axis=1)


def _post_kernel(hm_ref, hr_ref, gm_ref, gr_ref, x_ref, mod_ref, wbm_ref, wbr_ref, wout_ref, g_ref,
                 wr_hi_ref, wr_lo_ref, br_ref, x1_ref, hf_ref, logit_ref):
    ym = _dot(hm_ref[...], wbm_ref[...])
    yr = _dot(hr_ref[...], wbr_ref[...])
    y = _sigmoid(gm_ref[...].astype(F32)) * ym + _sigmoid(gr_ref[...].astype(F32)) * yr
    o = _dot(y.astype(BF16), wout_ref[...])
    x1 = x_ref[...] + mod_ref[0, 2:3, :] * o
    x1_ref[...] = x1
    hf = _rms_mod(x1, g_ref[...], mod_ref[0, 4:5, :], mod_ref[0, 3:4, :])
    _to_slabs(hf_ref, hf)
    logit_ref[...] = _dot3(hf, wr_hi_ref[...], wr_lo_ref[...]) + br_ref[...]


def _post(hm, hr, proj, x2, mod6, wbm, wbr, wout, g, wr_hi, wr_lo, br, seq):
    t = x2.shape[0]
    tm = 512
    per_b = seq // tm
    d = D_MODEL
    const = lambda shape: pl.BlockSpec(shape, lambda i: (0,) * len(shape))
    return pl.pallas_call(
        _post_kernel,
        grid=(t // tm,),
        in_specs=[
            pl.BlockSpec((tm, d), lambda i: (i, 0)),
            pl.BlockSpec((tm, d), lambda i: (i, 0)),
            pl.BlockSpec((tm, d), lambda i: (i, C_GML // d)),
            pl.BlockSpec((tm, d), lambda i: (i, C_GRET // d)),
            pl.BlockSpec((tm, d), lambda i: (i, 0)),
            pl.BlockSpec((1, 6, d), lambda i: (i // per_b, 0, 0)),
            const((d, d)), const((d, d)), const((d, d)),
            const((1, d)),
            const((d, LANES)), const((d, LANES)), const((1, LANES)),
        ],
        out_specs=[
            pl.BlockSpec((tm, d), lambda i: (i, 0)),
            pl.BlockSpec((tm * SLAB, LANES), lambda i: (i, 0)),
            pl.BlockSpec((tm, LANES), lambda i: (i, 0)),
        ],
        out_shape=[
            jax.ShapeDtypeStruct((t, d), F32),
            jax.ShapeDtypeStruct((t * SLAB, LANES), F32),
            jax.ShapeDtypeStruct((t, LANES), F32),
        ],
        compiler_params=pltpu.CompilerParams(
            dimension_semantics=("arbitrary",), vmem_limit_bytes=VMEM_LIMIT),
        name="post",
    )(hm, hr, proj, proj, x2, mod6, wbm, wbr, wout, g, wr_hi, wr_lo, br)


def _route_kernel(logit_ref, dest_ref, w_ref, bend_ref, cnt_st, pad_st):
    ph = pl.program_id(0)
    i = pl.program_id(1)
    tm = logit_ref.shape[0]
    lane = lax.broadcasted_iota(jnp.int32, (tm, LANES), 1)
    l = logit_ref[...]
    onehots, vals = [], []
    for _ in range(TOP_K):
        m = jnp.max(l, axis=-1, keepdims=True)
        idx = jnp.min(jnp.where(l == m, lane, LANES), axis=-1, keepdims=True)
        oh = lane == idx
        onehots.append(oh)
        vals.append(m)
        l = jnp.where(oh, -jnp.inf, l)
    sel = jnp.where(onehots[0] | onehots[1] | onehots[2] | onehots[3], 1.0, 0.0)
    tile_cnt = jnp.sum(sel, axis=0, keepdims=True)

    @pl.when(jnp.logical_and(ph == 0, i == 0))
    def _():
        cnt_st[...] = jnp.zeros_like(cnt_st)

    @pl.when(ph == 0)
    def _():
        cnt_st[...] = cnt_st[...] + tile_cnt

    @pl.when(jnp.logical_and(ph == 1, i == 0))
    def _():
        blocks = jnp.floor((cnt_st[...] + (EXPERT_BLOCK - 1)) * (1.0 / EXPERT_BLOCK))
        r = lax.broadcasted_iota(jnp.int32, (LANES, LANES), 0)
        c = lax.broadcasted_iota(jnp.int32, (LANES, LANES), 1)
        upper = jnp.where(r < c, 1.0, 0.0).astype(BF16)
        blocks8 = jnp.broadcast_to(blocks, (8, LANES))
        excl = _dot(blocks8.astype(BF16), upper)
        pad_st[...] = excl[0:1, :] * EXPERT_BLOCK
        bend_ref[...] = excl + blocks8
        cnt_st[...] = jnp.zeros_like(cnt_st)

    @pl.when(ph == 1)
    def _():
        carry = cnt_st[...]
        r = lax.broadcasted_iota(jnp.int32, (tm, tm), 0)
        c = lax.broadcasted_iota(jnp.int32, (tm, tm), 1)
        lower = jnp.where(r > c, 1.0, 0.0).astype(BF16)
        base = pad_st[...] + carry + _dot(lower, sel.astype(BF16))
        dest = jnp.zeros((tm, LANES), F32)
        wts = jnp.zeros((tm, LANES), F32)
        ex = [jnp.exp(v - vals[0]) for v in vals]
        den = ex[0] + ex[1] + ex[2] + ex[3]
        for k in range(TOP_K):
            dk = jnp.sum(jnp.where(onehots[k], base, 0.0), axis=-1, keepdims=True)
            dest = jnp.where(lane == k, dk, dest)
            wts = jnp.where(lane == k, ex[k] / den, wts)
        dest_ref[...] = dest.astype(jnp.int32)
        w_ref[...] = wts
        cnt_st[...] = carry + tile_cnt


def _route(logits):
    t = logits.shape[0]
    tm = 512
    return pl.pallas_call(
        _route_kernel,
        grid=(2, t // tm),
        in_specs=[pl.BlockSpec((tm, LANES), lambda ph, i: (i, 0))],
        out_specs=[
            pl.BlockSpec((tm, LANES), lambda ph, i: (i * ph, 0)),
            pl.BlockSpec((tm, LANES), lambda ph, i: (i * ph, 0)),
            pl.BlockSpec((8, LANES), lambda ph, i: (0, 0)),
        ],
        out_shape=[
            jax.ShapeDtypeStruct((t, LANES), jnp.int32),
            jax.ShapeDtypeStruct((t, LANES), F32),
            jax.ShapeDtypeStruct((8, LANES), F32),
        ],
        scratch_shapes=[pltpu.VMEM((1, LANES), F32), pltpu.VMEM((1, LANES), F32)],
        compiler_params=pltpu.CompilerParams(
            dimension_semantics=("arbitrary", "arbitrary"), vmem_limit_bytes=VMEM_LIMIT),
        name="route",
    )(logits)


DMA_UNROLL = 8


def _row_copy(src_ref, src_slab, dst_ref, dst_slab, sem):
    src = pl.multiple_of(src_slab * SLAB, SLAB)
    dst = pl.multiple_of(dst_slab * SLAB, SLAB)
    return pltpu.make_async_copy(src_ref.at[pl.ds(src, SLAB), :], dst_ref.at[pl.ds(dst, SLAB), :], sem)


def _dispatch_kernel(dest_ref, hf_ref, xs_in, xs_out, sem):
    del xs_in
    n_assign = dest_ref.shape[0]

    def body(g, carry):
        for u in range(DMA_UNROLL):
            t_local = g * (DMA_UNROLL // TOP_K) + u // TOP_K
            _row_copy(hf_ref, t_local, xs_out, dest_ref[g * DMA_UNROLL + u], sem).start()
        return carry

    lax.fori_loop(0, n_assign // DMA_UNROLL, body, 0)
    for _ in range(TOP_K):
        pltpu.make_async_copy(hf_ref, xs_out.at[pl.ds(0, hf_ref.shape[0]), :], sem).wait()


def _dispatch(dest_flat, hf_slab, n_slots):
    n_assign = dest_flat.shape[0]
    tm = 512
    xs0 = jnp.zeros((n_slots * SLAB, LANES), F32)
    return pl.pallas_call(
        _dispatch_kernel,
        grid=(n_assign // (tm * TOP_K),),
        in_specs=[
            pl.BlockSpec((tm * TOP_K,), lambda i: (i,), memory_space=pltpu.SMEM),
            pl.BlockSpec((tm * SLAB, LANES), lambda i: (i, 0)),
            pl.BlockSpec(memory_space=pl.ANY),
        ],
        out_specs=pl.BlockSpec(memory_space=pl.ANY),
        out_shape=jax.ShapeDtypeStruct((n_slots * SLAB, LANES), F32),
        scratch_shapes=[pltpu.SemaphoreType.DMA],
        input_output_aliases={2: 0},
        compiler_params=pltpu.CompilerParams(dimension_semantics=("arbitrary",)),
        name="dispatch",
    )(dest_flat, hf_slab, xs0)


def _expert_kernel(be_ref, xs_ref, wgu_ref, bgu_ref, wd_ref, bd_ref, ys_ref, wgu_bf, wd_bf):
    b = pl.program_id(0)
    prev = be_ref[jnp.maximum(b - 1, 0)]
    first = jnp.logical_or(b == 0, be_ref[b] != prev)
    two = 2 * LANES

    @pl.when(first)
    def _():
        r = lax.broadcasted_iota(jnp.int32, (two, two), 0)
        c = lax.broadcasted_iota(jnp.int32, (two, two), 1)
        src = jnp.where(c < LANES, 2 * c, 2 * (c - LANES) + 1)
        perm = jnp.where(r == src, 1.0, 0.0).astype(BF16)
        for blk in range(wgu_bf.shape[1] // two):
            wb = wgu_ref[0, :, blk * two:(blk + 1) * two].astype(BF16)
            wgu_bf[:, blk * two:(blk + 1) * two] = _dot(wb, perm).astype(BF16)
        wd_bf[...] = wd_ref[0].astype(BF16)

    x = _from_slabs(xs_ref, 0, EXPERT_BLOCK).astype(BF16)
    gu = _dot(x, wgu_bf[...]) + bgu_ref[0]
    n_blk = gu.shape[1] // two
    x_glu = jnp.concatenate([gu[:, i * two:i * two + LANES] for i in range(n_blk)], axis=1)
    x_lin = jnp.concatenate([gu[:, i * two + LANES:(i + 1) * two] for i in range(n_blk)], axis=1)
    x_glu = jnp.minimum(x_glu, SWIGLU_LIMIT)
    x_lin = jnp.clip(x_lin, -SWIGLU_LIMIT, SWIGLU_LIMIT)
    act = x_glu * _sigmoid(SWIGLU_ALPHA * x_glu) * (x_lin + 1.0)
    y = _dot(act.astype(BF16), wd_bf[...]) + bd_ref[0]
    _to_slabs(ys_ref, y)


def _experts(block_expert, xs, w_gate_up, b_gu_perm, w_down, b_down):
    n_blocks = block_expert.shape[0]
    d = D_MODEL
    f2 = w_gate_up.shape[-1]
    blk_rows = EXPERT_BLOCK * SLAB
    grid_spec = pltpu.PrefetchScalarGridSpec(
        num_scalar_prefetch=1,
        grid=(n_blocks,),
        in_specs=[
            pl.BlockSpec((blk_rows, LANES), lambda b, be: (b, 0)),
            pl.BlockSpec((1, d, f2), lambda b, be: (be[b], 0, 0)),
            pl.BlockSpec((1, 1, f2), lambda b, be: (be[b], 0, 0)),
            pl.BlockSpec((1, f2 // 2, d), lambda b, be: (be[b], 0, 0)),
            pl.BlockSpec((1, 1, d), lambda b, be: (be[b], 0, 0)),
        ],
        out_specs=pl.BlockSpec((blk_rows, LANES), lambda b, be: (b, 0)),
        scratch_shapes=[pltpu.VMEM((d, f2), BF16), pltpu.VMEM((f2 // 2, d), BF16)],
    )
    return pl.pallas_call(
        _expert_kernel,
        grid_spec=grid_spec,
        out_shape=jax.ShapeDtypeStruct((n_blocks * blk_rows, LANES), F32),
        compiler_params=pltpu.CompilerParams(
            dimension_semantics=("arbitrary",), vmem_limit_bytes=VMEM_LIMIT),
        name="expert",
    )(block_expert, xs, w_gate_up, b_gu_perm, w_down, b_down)


def _combine_kernel(dest_ref, ys_hbm, w_ref, x1_ref, mod_ref, g_ref, o_ref, buf, sem):
    tm = x1_ref.shape[0]
    n_assign = tm * TOP_K

    def body(g, carry):
        for u in range(DMA_UNROLL):
            t_local = g * (DMA_UNROLL // TOP_K) + u // TOP_K
            k = u % TOP_K
            _row_copy(ys_hbm, dest_ref[g * DMA_UNROLL + u], buf, k * tm + t_local, sem).start()
        return carry

    lax.fori_loop(0, n_assign // DMA_UNROLL, body, 0)
    pltpu.make_async_copy(ys_hbm.at[pl.ds(0, n_assign * SLAB), :], buf, sem).wait()

    w = w_ref[...]
    y = w[:, 0:1] * _from_slabs(buf, 0, tm)
    for k in range(1, TOP_K):
        y = y + w[:, k:k + 1] * _from_slabs(buf, k * tm, tm)
    x = x1_ref[...] + mod_ref[0, 5:6, :] * y
    ms = jnp.mean(x * x, axis=-1, keepdims=True)
    o_ref[...] = x * lax.rsqrt(ms + EPS) * g_ref[...]


def _combine(dest_flat, ys, wts, x1, mod6, g, seq):
    t = x1.shape[0]
    tm = 256
    per_b = seq // tm
    d = D_MODEL
    return pl.pallas_call(
        _combine_kernel,
        grid=(t // tm,),
        in_specs=[
            pl.BlockSpec((tm * TOP_K,), lambda i: (i,), memory_space=pltpu.SMEM),
            pl.BlockSpec(memory_space=pl.ANY),
            pl.BlockSpec((tm, LANES), lambda i: (i, 0)),
            pl.BlockSpec((tm, d), lambda i: (i, 0)),
            pl.BlockSpec((1, 6, d), lambda i: (i // per_b, 0, 0)),
            pl.BlockSpec((1, d), lambda i: (0, 0)),
        ],
        out_specs=pl.BlockSpec((tm, d), lambda i: (i, 0)),
        out_shape=jax.ShapeDtypeStruct((t, d), F32),
        scratch_shapes=[pltpu.VMEM((tm * TOP_K * SLAB, LANES), F32), pltpu.SemaphoreType.DMA],
        compiler_params=pltpu.CompilerParams(
            dimension_semantics=("arbitrary",), vmem_limit_bytes=VMEM_LIMIT),
        name="combine",
    )(dest_flat, ys, wts, x1, mod6, g)


def _split_w(w):
    hi = w.astype(BF16)
    lo = (w - hi.astype(F32)).astype(BF16)
    return hi, lo


def kernel(x, c, positions, w_ada, b_ada, norm_mix_g, w_in, conv_w, conv_b, b_if, ml_norm_g, ret_norm_g,
           w_branch_ml, w_branch_ret, w_out, norm_ffn_g, w_router, b_router, w_gate_up, b_gate_up, w_down,
           b_down, norm_final_g):
    batch, seq, d = x.shape
    t = batch * seq
    x2 = x.reshape(t, d)
    depth = w_ada.shape[0]
    c8 = jnp.concatenate([c, jnp.zeros((8 - batch, d), c.dtype)], axis=0)

    half = D_QK // 2
    inv = ROPE_BASE ** (-jnp.arange(half, dtype=F32) / half)
    ang = positions.astype(F32)[..., None] * inv
    cos = jnp.cos(ang).reshape(t, half)
    sin = jnp.sin(ang).reshape(t, half)
    cos2 = jnp.concatenate([cos, cos], axis=-1)
    sin2 = jnp.concatenate([-sin, sin], axis=-1)
    L = CHUNK
    log_gamma = jnp.log(1.0 - 2.0 ** (-5.0 - jnp.arange(N_HEADS, dtype=F32)))
    pos = jnp.arange(L, dtype=F32)
    rel = pos[:, None] - pos[None, :]
    decay = jnp.where(rel >= 0, jnp.exp(log_gamma[:, None, None] * jnp.maximum(rel, 0.0)), 0.0)
    kdec = jnp.exp(log_gamma[:, None] * (L - 1 - pos))[:, :, None]
    qdec = jnp.exp(log_gamma[:, None] * (pos + 1.0))[:, :, None]
    gch = jnp.broadcast_to(jnp.exp(log_gamma * L)[:, None, None], (N_HEADS, 1, LANES))

    assert depth == 1, "the final norm is fused after the single layer"
    l = 0
    wa_hi, wa_lo = _split_w(w_ada[l])
    mod = _ada(c8, wa_hi, wa_lo, b_ada[l][None, :])
    mod6 = mod[:batch].reshape(batch, 6, d)

    w = w_in[l]
    w_main = jnp.concatenate(
        [w[:, 0:3072], w[:, 3080:8200]], axis=1).astype(BF16)
    w_if = w[:, 3072:3080]
    zpad = jnp.zeros((d, LANES - N_HEADS), F32)
    wif = jnp.concatenate([w_if[:, :N_HEADS], zpad, w_if[:, N_HEADS:], zpad], axis=1)
    wif_hi, wif_lo = _split_w(wif)
    zb = jnp.zeros((LANES - N_HEADS,), F32)
    b_if2 = jnp.concatenate([b_if[l][:N_HEADS], zb, b_if[l][N_HEADS:], zb])[None, :]

    proj, gates = _inproj(x2, mod6, norm_mix_g[l][None, :], w_main, wif_hi, wif_lo, seq)
    hm, hr = _mix(proj, gates, cos2, sin2, conv_w[l], conv_b[l][None, :], b_if2,
                  ml_norm_g[l][None, :], ret_norm_g[l][None, :], decay, kdec, qdec, gch, batch, seq)

    wr = jnp.concatenate([w_router[l], jnp.zeros((d, LANES - N_EXPERTS), F32)], axis=1)
    wr_hi, wr_lo = _split_w(wr)
    br = jnp.concatenate([b_router[l], jnp.full((LANES - N_EXPERTS,), -1e30, F32)])[None, :]
    x1, hf, logits = _post(hm, hr, proj, x2, mod6, w_branch_ml[l].astype(BF16), w_branch_ret[l].astype(BF16),
                           w_out[l].astype(BF16), norm_ffn_g[l][None, :], wr_hi, wr_lo, br, seq)

    n_assign = t * TOP_K
    n_blocks = -(-n_assign // EXPERT_BLOCK) + N_EXPERTS
    dest, wts, block_end = _route(logits)
    dest_flat = dest[:, :TOP_K].reshape(n_assign)
    bend = block_end[0, :N_EXPERTS].astype(jnp.int32)
    block_expert = jnp.minimum(
        jnp.sum((bend[None, :] <= jnp.arange(n_blocks, dtype=jnp.int32)[:, None]).astype(jnp.int32), axis=1),
        N_EXPERTS - 1)
    xs = _dispatch(dest_flat, hf, n_blocks * EXPERT_BLOCK)
    f2 = w_gate_up.shape[-1]
    bgu = b_gate_up[l].reshape(N_EXPERTS, f2 // (2 * LANES), LANES, 2)
    bgu = jnp.swapaxes(bgu, -1, -2).reshape(N_EXPERTS, 1, f2)
    ys = _experts(block_expert, xs, w_gate_up[l], bgu, w_down[l], b_down[l][:, None, :])
    out = _combine(dest_flat, ys, wts, x1, mod6, norm_final_g[None, :], seq)
    return out.reshape(batch, seq, d)
```

```python
import functools

import numpy as np
import jax
import jax.numpy as jnp
from jax import lax
from jax.experimental import pallas as pl
from jax.experimental.pallas import tpu as pltpu

D_MODEL = 1024
N_HEADS = 4
D_QK = 128
D_V = 256
CONV_K = 4
ROPE_BASE = 10000.0
CHUNK = 128
N_EXPERTS = 32
TOP_K = 4
SWIGLU_LIMIT = 7.0
SWIGLU_ALPHA = 1.702
EXPERT_BLOCK = 512
EPS = 1e-5

QK_W = N_HEADS * D_QK
V_W = N_HEADS * D_V
LANES = 128
GATE_W = 2 * LANES
V_EXT = D_V + LANES
SLAB = D_MODEL // LANES

C_MLQ, C_MLK, C_MLV, C_MLO = 0, 512, 1024, 2048
C_RQ, C_RK, C_RV, C_RG = 3072, 3584, 4096, 5120
C_GML, C_GRET = 6144, 7168
MIX_W = 6144
PROJ_W = 8192

BF16 = jnp.bfloat16
F32 = jnp.float32
VMEM_LIMIT = 56 * 1024 * 1024


def _dot(a, b):
    return jnp.dot(a, b, preferred_element_type=F32)


def _dot_nt(a, b):
    return lax.dot_general(a, b, (((1,), (1,)), ((), ())), preferred_element_type=F32)


def _split(a):
    hi = a.astype(BF16)
    lo = (a - hi.astype(F32)).astype(BF16)
    return hi, lo


def _dot3(a, b_hi, b_lo):
    a_hi, a_lo = _split(a)
    return _dot(a_hi, b_hi) + (_dot(a_lo, b_hi) + _dot(a_hi, b_lo))


def _sigmoid(x):
    return 1.0 / (1.0 + jnp.exp(-x))


def _rms_mod(x, g, scale, shift):
    ms = jnp.mean(x * x, axis=-1, keepdims=True)
    return (x * lax.rsqrt(ms + EPS) * g) * (1.0 + scale) + shift


def _ada_kernel(c_ref, whi_ref, wlo_ref, b_ref, o_ref):
    c = c_ref[...]
    ca = c * _sigmoid(c)
    o_ref[...] = _dot3(ca, whi_ref[...], wlo_ref[...]) + b_ref[...]


def _ada(c8, w_hi, w_lo, b):
    n = w_hi.shape[1]
    tn = 1024
    return pl.pallas_call(
        _ada_kernel,
        grid=(n // tn,),
        in_specs=[
            pl.BlockSpec((8, D_MODEL), lambda j: (0, 0)),
            pl.BlockSpec((D_MODEL, tn), lambda j: (0, j)),
            pl.BlockSpec((D_MODEL, tn), lambda j: (0, j)),
            pl.BlockSpec((1, tn), lambda j: (0, j)),
        ],
        out_specs=pl.BlockSpec((8, tn), lambda j: (0, j)),
        out_shape=jax.ShapeDtypeStruct((8, n), F32),
        name="ada",
    )(c8, w_hi, w_lo, b)


def _inproj_kernel(x_ref, mod_ref, g_ref, w_ref, wif_hi_ref, wif_lo_ref, proj_ref, gates_ref, hb_ref):
    j = pl.program_id(1)

    @pl.when(j == 0)
    def _():
        h = _rms_mod(x_ref[...], g_ref[...], mod_ref[0, 1:2, :], mod_ref[0, 0:1, :])
        hb_ref[...] = h.astype(BF16)
        gates_ref[...] = _dot3(h, wif_hi_ref[...], wif_lo_ref[...])

    proj_ref[...] = _dot(hb_ref[...], w_ref[...]).astype(BF16)


def _inproj(x2, mod6, g, w_main, wif_hi, wif_lo, seq):
    t = x2.shape[0]
    tm, tn = 512, 2048
    per_b = seq // tm
    return pl.pallas_call(
        _inproj_kernel,
        grid=(t // tm, PROJ_W // tn),
        in_specs=[
            pl.BlockSpec((tm, D_MODEL), lambda i, j: (i, 0)),
            pl.BlockSpec((1, 6, D_MODEL), lambda i, j: (i // per_b, 0, 0)),
            pl.BlockSpec((1, D_MODEL), lambda i, j: (0, 0)),
            pl.BlockSpec((D_MODEL, tn), lambda i, j: (0, j)),
            pl.BlockSpec((D_MODEL, GATE_W), lambda i, j: (0, 0)),
            pl.BlockSpec((D_MODEL, GATE_W), lambda i, j: (0, 0)),
        ],
        out_specs=[
            pl.BlockSpec((tm, tn), lambda i, j: (i, j)),
            pl.BlockSpec((tm, GATE_W), lambda i, j: (i, 0)),
        ],
        out_shape=[
            jax.ShapeDtypeStruct((t, PROJ_W), BF16),
            jax.ShapeDtypeStruct((t, GATE_W), F32),
        ],
        scratch_shapes=[pltpu.VMEM((tm, D_MODEL), BF16)],
        compiler_params=pltpu.CompilerParams(
            dimension_semantics=("arbitrary", "arbitrary"), vmem_limit_bytes=VMEM_LIMIT),
        name="inproj",
    )(x2, mod6, g, w_main, wif_hi, wif_lo)


def _head_norm(h, g):
    mu = jnp.mean(h, axis=-1, keepdims=True)
    d = h - mu
    var = jnp.mean(d * d, axis=-1, keepdims=True)
    return d * lax.rsqrt(var + EPS) * g


def _mix_kernel(proj_ref, gates_ref, cos_ref, sin_ref, convw_ref, convb_ref, bif_ref, mlg_ref, retg_ref,
                decay_ref, kdec_ref, qdec_ref, gch_ref,
                hm_ref, hr_ref,
                prev_ref, c_st, m_st, r_st):
    L = CHUNK

    @pl.when(pl.program_id(1) == 0)
    def _():
        prev_ref[...] = jnp.zeros_like(prev_ref)
        c_st[...] = jnp.zeros_like(c_st)
        m_st[...] = jnp.zeros_like(m_st)
        r_st[...] = jnp.zeros_like(r_st)

    rows = lax.broadcasted_iota(jnp.int32, (L, L), 0)
    cols = lax.broadcasted_iota(jnp.int32, (L, L), 1)
    causal = rows >= cols
    tril = jnp.where(causal, 1.0, 0.0).astype(BF16)

    cur = proj_ref[:, C_MLQ:C_MLQ + 2 * QK_W]
    xx = jnp.concatenate([prev_ref[...], cur], axis=0)
    r2 = lax.broadcasted_iota(jnp.int32, (L, 2 * L), 0)
    c2 = lax.broadcasted_iota(jnp.int32, (L, 2 * L), 1)
    acc = convb_ref[...] + cur.astype(F32) * convw_ref[CONV_K - 1:CONV_K, :]
    for d in range(1, CONV_K):
        shift = jnp.where(c2 == r2 + (L - d), 1.0, 0.0).astype(BF16)
        acc = acc + _dot(shift, xx) * convw_ref[CONV_K - 1 - d:CONV_K - d, :]
    prev_ref[...] = cur
    qk = acc * _sigmoid(acc)

    g = gates_ref[...] + bif_ref[...]
    gi = g[:, :LANES]
    gf = g[:, LANES:]
    lf = jnp.minimum(gf, 0.0) - jnp.log(1.0 + jnp.exp(-jnp.abs(gf)))
    lf_hi, lf_lo = _split(lf)
    a_all = _dot(tril, lf_hi) + _dot(tril, lf_lo)
    a_last = a_all[L - 1:L, :]
    bm = gi - a_all
    bm_t = bm.T
    w_state = a_last + bm
    m_loc = jnp.max(w_state, axis=0, keepdims=True)
    m_prev = m_st[...]
    inter_log = a_all + m_prev
    m_new = jnp.maximum(a_last + m_prev, m_loc)
    s_prev = jnp.exp(a_last + m_prev - m_new)
    s_loc = jnp.exp(m_loc - m_new)
    ws_all = jnp.exp(w_state - m_loc)
    m_st[...] = m_new

    ones_blk = jnp.ones((L, LANES), BF16)
    q_scale = D_QK ** -0.5

    for h in range(N_HEADS):
        q = (qk[:, h * D_QK:(h + 1) * D_QK] * q_scale).astype(BF16)
        k_f = qk[:, QK_W + h * D_QK:QK_W + (h + 1) * D_QK]
        k = k_f.astype(BF16)
        v_ext = jnp.concatenate([proj_ref[:, C_MLV + h * D_V:C_MLV + (h + 1) * D_V], ones_blk], axis=1)
        dlog = jnp.where(causal, a_all[:, h:h + 1] + bm_t[h:h + 1, :], -jnp.inf)
        m_intra = jnp.max(dlog, axis=-1, keepdims=True)
        il = inter_log[:, h:h + 1]
        m_t = jnp.maximum(il, m_intra)
        p = jnp.exp(dlog - m_t) * _dot_nt(q, k)
        isc = jnp.exp(il - m_t)
        tot = _dot(p.astype(BF16), v_ext) + isc * _dot(q, c_st[h].astype(BF16))
        den = tot[:, D_V:D_V + 1]
        hout = tot[:, :D_V] / jnp.maximum(jnp.abs(den), jnp.exp(-m_t))
        kw_t = (k_f * ws_all[:, h:h + 1]).T.astype(BF16)
        c_st[h] = s_prev[:, h:h + 1] * c_st[h] + s_loc[:, h:h + 1] * _dot(kw_t, v_ext)
        y = _head_norm(hout, mlg_ref[:, h * D_V:(h + 1) * D_V])
        o = proj_ref[:, C_MLO + h * D_V:C_MLO + (h + 1) * D_V].astype(F32)
        hm_ref[:, h * D_V:(h + 1) * D_V] = (_sigmoid(o) * y).astype(BF16)

    cos2 = cos_ref[...]
    sin2 = sin_ref[...]
    k_scale = D_QK ** -0.5
    for h in range(N_HEADS):
        q_raw = proj_ref[:, C_RQ + h * D_QK:C_RQ + (h + 1) * D_QK].astype(F32)
        k_raw = proj_ref[:, C_RK + h * D_QK:C_RK + (h + 1) * D_QK].astype(F32)
        q = (q_raw * cos2 + pltpu.roll(q_raw, D_QK // 2, 1) * sin2).astype(BF16)
        k_f = (k_raw * cos2 + pltpu.roll(k_raw, D_QK // 2, 1) * sin2) * k_scale
        v = proj_ref[:, C_RV + h * D_V:C_RV + (h + 1) * D_V]
        sc = _dot_nt(q, k_f.astype(BF16)) * decay_ref[h]
        hret = _dot(sc.astype(BF16), v) + _dot(q, r_st[h].astype(BF16)) * qdec_ref[h]
        kd_t = (k_f * kdec_ref[h]).T.astype(BF16)
        r_st[h] = gch_ref[h][:, 0:1] * r_st[h] + _dot(kd_t, v)
        y = _head_norm(hret, retg_ref[:, h * D_V:(h + 1) * D_V])
        gt = proj_ref[:, C_RG + h * D_V:C_RG + (h + 1) * D_V].astype(F32)
        hr_ref[:, h * D_V:(h + 1) * D_V] = (gt * _sigmoid(gt) * y).astype(BF16)


def _mix(proj, gates, cos2, sin2, conv_w, conv_b, b_if2, ml_g, ret_g, decay, kdec, qdec, gch, batch, seq):
    t = proj.shape[0]
    nc = seq // CHUNK
    L = CHUNK
    full = lambda shape: pl.BlockSpec(shape, lambda b, c: (0,) * len(shape))
    return pl.pallas_call(
        _mix_kernel,
        grid=(batch, nc),
        in_specs=[
            pl.BlockSpec((L, MIX_W), lambda b, c: (b * nc + c, 0)),
            pl.BlockSpec((L, GATE_W), lambda b, c: (b * nc + c, 0)),
            pl.BlockSpec((L, LANES), lambda b, c: (b * nc + c, 0)),
            pl.BlockSpec((L, LANES), lambda b, c: (b * nc + c, 0)),
            full((CONV_K, 2 * QK_W)),
            full((1, 2 * QK_W)),
            full((1, GATE_W)),
            full((1, V_W)),
            full((1, V_W)),
            full((N_HEADS, L, L)),
            full((N_HEADS, L, 1)),
            full((N_HEADS, L, 1)),
            full((N_HEADS, 1, LANES)),
        ],
        out_specs=[
            pl.BlockSpec((L, V_W), lambda b, c: (b * nc + c, 0)),
            pl.BlockSpec((L, V_W), lambda b, c: (b * nc + c, 0)),
        ],
        out_shape=[jax.ShapeDtypeStruct((t, V_W), BF16), jax.ShapeDtypeStruct((t, V_W), BF16)],
        scratch_shapes=[
            pltpu.VMEM((L, 2 * QK_W), BF16),
            pltpu.VMEM((N_HEADS, D_QK, V_EXT), F32),
            pltpu.VMEM((1, LANES), F32),
            pltpu.VMEM((N_HEADS, D_QK, D_V), F32),
        ],
        compiler_params=pltpu.CompilerParams(
            dimension_semantics=("arbitrary", "arbitrary"), vmem_limit_bytes=VMEM_LIMIT),
        name="mix",
    )(proj, gates, cos2, sin2, conv_w, conv_b, b_if2, ml_g, ret_g, decay, kdec, qdec, gch)


def _to_slabs(ref, val):
    rows = val.shape[0]
    for s in range(SLAB):
        ref[pl.ds(s, rows, stride=SLAB), :] = val[:, s * LANES:(s + 1) * LANES]


def _from_slabs(ref, first_slab, rows):
    return jnp.concatenate(
        [ref[pl.ds(first_slab * SLAB + s, rows, stride=SLAB), :] for s in range(SLAB)], axis=1)


def _post_kernel(hm_ref, hr_ref, gm_ref, gr_ref, x_ref, mod_ref, wbm_ref, wbr_ref, wout_ref, g_ref,
                 wr_hi_ref, wr_lo_ref, br_ref, x1_ref, hf_ref, logit_ref):
    ym = _dot(hm_ref[...], wbm_ref[...])
    yr = _dot(hr_ref[...], wbr_ref[...])
    y = _sigmoid(gm_ref[...].astype(F32)) * ym + _sigmoid(gr_ref[...].astype(F32)) * yr
    o = _dot(y.astype(BF16), wout_ref[...])
    x1 = x_ref[...] + mod_ref[0, 2:3, :] * o
    x1_ref[...] = x1
    hf = _rms_mod(x1, g_ref[...], mod_ref[0, 4:5, :], mod_ref[0, 3:4, :])
    _to_slabs(hf_ref, hf)
    logit_ref[...] = _dot3(hf, wr_hi_ref[...], wr_lo_ref[...]) + br_ref[...]


def _post(hm, hr, proj, x2, mod6, wbm, wbr, wout, g, wr_hi, wr_lo, br, seq):
    t = x2.shape[0]
    tm = 512
    per_b = seq // tm
    d = D_MODEL
    const = lambda shape: pl.BlockSpec(shape, lambda i: (0,) * len(shape))
    return pl.pallas_call(
        _post_kernel,
        grid=(t // tm,),
        in_specs=[
            pl.BlockSpec((tm, d), lambda i: (i, 0)),
            pl.BlockSpec((tm, d), lambda i: (i, 0)),
            pl.BlockSpec((tm, d), lambda i: (i, C_GML // d)),
            pl.BlockSpec((tm, d), lambda i: (i, C_GRET // d)),
            pl.BlockSpec((tm, d), lambda i: (i, 0)),
            pl.BlockSpec((1, 6, d), lambda i: (i // per_b, 0, 0)),
            const((d, d)), const((d, d)), const((d, d)),
            const((1, d)),
            const((d, LANES)), const((d, LANES)), const((1, LANES)),
        ],
        out_specs=[
            pl.BlockSpec((tm, d), lambda i: (i, 0)),
            pl.BlockSpec((tm * SLAB, LANES), lambda i: (i, 0)),
            pl.BlockSpec((tm, LANES), lambda i: (i, 0)),
        ],
        out_shape=[
            jax.ShapeDtypeStruct((t, d), F32),
            jax.ShapeDtypeStruct((t * SLAB, LANES), F32),
            jax.ShapeDtypeStruct((t, LANES), F32),
        ],
        compiler_params=pltpu.CompilerParams(
            dimension_semantics=("arbitrary",), vmem_limit_bytes=VMEM_LIMIT),
        name="post",
    )(hm, hr, proj, proj, x2, mod6, wbm, wbr, wout, g, wr_hi, wr_lo, br)


def _route_kernel(logit_ref, dest_ref, w_ref, bend_ref, cnt_st, pad_st):
    ph = pl.program_id(0)
    i = pl.program_id(1)
    tm = logit_ref.shape[0]
    lane = lax.broadcasted_iota(jnp.int32, (tm, LANES), 1)
    l = logit_ref[...]
    onehots, vals = [], []
    for _ in range(TOP_K):
        m = jnp.max(l, axis=-1, keepdims=True)
        idx = jnp.min(jnp.where(l == m, lane, LANES), axis=-1, keepdims=True)
        oh = lane == idx
        onehots.append(oh)
        vals.append(m)
        l = jnp.where(oh, -jnp.inf, l)
    sel = jnp.where(onehots[0] | onehots[1] | onehots[2] | onehots[3], 1.0, 0.0)
    tile_cnt = jnp.sum(sel, axis=0, keepdims=True)

    @pl.when(jnp.logical_and(ph == 0, i == 0))
    def _():
        cnt_st[...] = jnp.zeros_like(cnt_st)

    @pl.when(ph == 0)
    def _():
        cnt_st[...] = cnt_st[...] + tile_cnt

    @pl.when(jnp.logical_and(ph == 1, i == 0))
    def _():
        blocks = jnp.floor((cnt_st[...] + (EXPERT_BLOCK - 1)) * (1.0 / EXPERT_BLOCK))
        r = lax.broadcasted_iota(jnp.int32, (LANES, LANES), 0)
        c = lax.broadcasted_iota(jnp.int32, (LANES, LANES), 1)
        upper = jnp.where(r < c, 1.0, 0.0).astype(BF16)
        blocks8 = jnp.broadcast_to(blocks, (8, LANES))
        excl = _dot(blocks8.astype(BF16), upper)
        pad_st[...] = excl[0:1, :] * EXPERT_BLOCK
        row = lax.broadcasted_iota(jnp.int32, (8, LANES), 0)
        bend_ref[...] = jnp.where(row == 0, excl + blocks8,
                                  jnp.where(row == 1, excl * EXPERT_BLOCK + cnt_st[...], 0.0))
        cnt_st[...] = jnp.zeros_like(cnt_st)

    @pl.when(ph == 1)
    def _():
        carry = cnt_st[...]
        r = lax.broadcasted_iota(jnp.int32, (tm, tm), 0)
        c = lax.broadcasted_iota(jnp.int32, (tm, tm), 1)
        lower = jnp.where(r > c, 1.0, 0.0).astype(BF16)
        base = pad_st[...] + carry + _dot(lower, sel.astype(BF16))
        dest = jnp.zeros((tm, LANES), F32)
        wts = jnp.zeros((tm, LANES), F32)
        ex = [jnp.exp(v - vals[0]) for v in vals]
        den = ex[0] + ex[1] + ex[2] + ex[3]
        for k in range(TOP_K):
            dk = jnp.sum(jnp.where(onehots[k], base, 0.0), axis=-1, keepdims=True)
            dest = jnp.where(lane == k, dk, dest)
            wts = jnp.where(lane == k, ex[k] / den, wts)
        dest_ref[...] = dest.astype(jnp.int32)
        w_ref[...] = wts
        cnt_st[...] = carry + tile_cnt


def _route(logits):
    t = logits.shape[0]
    tm = 512
    return pl.pallas_call(
        _route_kernel,
        grid=(2, t // tm),
        in_specs=[pl.BlockSpec((tm, LANES), lambda ph, i: (i, 0))],
        out_specs=[
            pl.BlockSpec((tm, LANES), lambda ph, i: (i * ph, 0)),
            pl.BlockSpec((tm, LANES), lambda ph, i: (i * ph, 0)),
            pl.BlockSpec((8, LANES), lambda ph, i: (0, 0)),
        ],
        out_shape=[
            jax.ShapeDtypeStruct((t, LANES), jnp.int32),
            jax.ShapeDtypeStruct((t, LANES), F32),
            jax.ShapeDtypeStruct((8, LANES), F32),
        ],
        scratch_shapes=[pltpu.VMEM((1, LANES), F32), pltpu.VMEM((1, LANES), F32)],
        compiler_params=pltpu.CompilerParams(
            dimension_semantics=("arbitrary", "arbitrary"), vmem_limit_bytes=VMEM_LIMIT),
        name="route",
    )(logits)


DMA_UNROLL = 8


def _row_copy(src_ref, src_slab, dst_ref, dst_slab, sem):
    src = pl.multiple_of(src_slab * SLAB, SLAB)
    dst = pl.multiple_of(dst_slab * SLAB, SLAB)
    return pltpu.make_async_copy(src_ref.at[pl.ds(src, SLAB), :], dst_ref.at[pl.ds(dst, SLAB), :], sem)


def _dispatch_kernel(dest_ref, padfirst_ref, hf_ref, xs_out, zbuf, sem, zsem):
    n_assign = dest_ref.shape[0]
    blk_rows = zbuf.shape[0]

    @pl.when(pl.program_id(0) == 0)
    def _():
        zbuf[...] = jnp.zeros_like(zbuf)
        for e in range(N_EXPERTS):
            start = pl.multiple_of(padfirst_ref[e] * SLAB, SLAB)
            pltpu.make_async_copy(zbuf, xs_out.at[pl.ds(start, blk_rows), :], zsem).start()
        for e in range(N_EXPERTS):
            pltpu.make_async_copy(zbuf, xs_out.at[pl.ds(0, blk_rows), :], zsem).wait()

        n_used = padfirst_ref[N_EXPERTS]
        n_total = xs_out.shape[0] // blk_rows

        def zero_start(b, carry):
            start = pl.multiple_of(b * blk_rows, blk_rows)
            pltpu.make_async_copy(zbuf, xs_out.at[pl.ds(start, blk_rows), :], zsem).start()
            return carry

        def zero_wait(b, carry):
            pltpu.make_async_copy(zbuf, xs_out.at[pl.ds(0, blk_rows), :], zsem).wait()
            return carry

        lax.fori_loop(n_used, n_total, zero_start, 0)
        lax.fori_loop(n_used, n_total, zero_wait, 0)

    def body(g, carry):
        for u in range(DMA_UNROLL):
            t_local = g * (DMA_UNROLL // TOP_K) + u // TOP_K
            _row_copy(hf_ref, t_local, xs_out, dest_ref[g * DMA_UNROLL + u], sem).start(priority=u % 2)
        return carry

    lax.fori_loop(0, n_assign // DMA_UNROLL, body, 0)
    for _ in range(TOP_K):
        pltpu.make_async_copy(hf_ref, xs_out.at[pl.ds(0, hf_ref.shape[0]), :], sem).wait()


def _dispatch(dest_flat, pad_first, hf_slab, n_blocks):
    n_assign = dest_flat.shape[0]
    tm = 512
    blk_rows = EXPERT_BLOCK * SLAB
    return pl.pallas_call(
        _dispatch_kernel,
        grid=(n_assign // (tm * TOP_K),),
        in_specs=[
            pl.BlockSpec((tm * TOP_K,), lambda i: (i,), memory_space=pltpu.SMEM),
            pl.BlockSpec(memory_space=pltpu.SMEM),
            pl.BlockSpec((tm * SLAB, LANES), lambda i: (i, 0)),
        ],
        out_specs=pl.BlockSpec(memory_space=pl.ANY),
        out_shape=jax.ShapeDtypeStruct(((n_blocks + 1) * blk_rows, LANES), F32),
        scratch_shapes=[pltpu.VMEM((blk_rows, LANES), F32), pltpu.SemaphoreType.DMA, pltpu.SemaphoreType.DMA],
        compiler_params=pltpu.CompilerParams(dimension_semantics=("arbitrary",), vmem_limit_bytes=VMEM_LIMIT),
        name="dispatch",
    )(dest_flat, pad_first, hf_slab)


def _expert_kernel(be_ref, nused_ref, xs_ref, wgu_ref, bgu_ref, wd_ref, bd_ref, ys_ref, wgu_bf, wd_bf):
    b = pl.program_id(0)

    @pl.when(b < nused_ref[0])
    def _():
        _expert_block(b, be_ref, xs_ref, wgu_ref, bgu_ref, wd_ref, bd_ref, ys_ref, wgu_bf, wd_bf)

    @pl.when(b >= nused_ref[0])
    def _():
        ys_ref[...] = jnp.zeros_like(ys_ref)


def _expert_block(b, be_ref, xs_ref, wgu_ref, bgu_ref, wd_ref, bd_ref, ys_ref, wgu_bf, wd_bf):
    prev = be_ref[jnp.maximum(b - 1, 0)]
    first = jnp.logical_or(b == 0, be_ref[b] != prev)
    two = 2 * LANES

    @pl.when(first)
    def _():
        r = lax.broadcasted_iota(jnp.int32, (two, two), 0)
        c = lax.broadcasted_iota(jnp.int32, (two, two), 1)
        src = jnp.where(c < LANES, 2 * c, 2 * (c - LANES) + 1)
        perm = jnp.where(r == src, 1.0, 0.0).astype(BF16)
        for blk in range(wgu_bf.shape[1] // two):
            wb = wgu_ref[0, :, blk * two:(blk + 1) * two].astype(BF16)
            wgu_bf[:, blk * two:(blk + 1) * two] = _dot(wb, perm).astype(BF16)
        wd_bf[...] = wd_ref[0].astype(BF16)

    x = _from_slabs(xs_ref, 0, EXPERT_BLOCK).astype(BF16)
    gu = _dot(x, wgu_bf[...]) + bgu_ref[0]
    n_blk = gu.shape[1] // two
    x_glu = jnp.concatenate([gu[:, i * two:i * two + LANES] for i in range(n_blk)], axis=1)
    x_lin = jnp.concatenate([gu[:, i * two + LANES:(i + 1) * two] for i in range(n_blk)], axis=1)
    x_glu = jnp.minimum(x_glu, SWIGLU_LIMIT)
    x_lin = jnp.clip(x_lin, -SWIGLU_LIMIT, SWIGLU_LIMIT)
    act = x_glu * _sigmoid(SWIGLU_ALPHA * x_glu) * (x_lin + 1.0)
    y = _dot(act.astype(BF16), wd_bf[...]) + bd_ref[0]
    _to_slabs(ys_ref, y)


def _experts(block_expert, n_used, xs, w_gate_up, b_gu_perm, w_down, b_down):
    n_blocks = block_expert.shape[0]
    d = D_MODEL
    f2 = w_gate_up.shape[-1]
    blk_rows = EXPERT_BLOCK * SLAB
    grid_spec = pltpu.PrefetchScalarGridSpec(
        num_scalar_prefetch=2,
        grid=(n_blocks,),
        in_specs=[
            pl.BlockSpec((blk_rows, LANES), lambda b, be, nu: (b, 0)),
            pl.BlockSpec((1, d, f2), lambda b, be, nu: (be[b], 0, 0)),
            pl.BlockSpec((1, 1, f2), lambda b, be, nu: (be[b], 0, 0)),
            pl.BlockSpec((1, f2 // 2, d), lambda b, be, nu: (be[b], 0, 0)),
            pl.BlockSpec((1, 1, d), lambda b, be, nu: (be[b], 0, 0)),
        ],
        out_specs=pl.BlockSpec((blk_rows, LANES), lambda b, be, nu: (b, 0)),
        scratch_shapes=[pltpu.VMEM((d, f2), BF16), pltpu.VMEM((f2 // 2, d), BF16)],
    )
    return pl.pallas_call(
        _expert_kernel,
        grid_spec=grid_spec,
        out_shape=jax.ShapeDtypeStruct((n_blocks * blk_rows, LANES), F32),
        compiler_params=pltpu.CompilerParams(
            dimension_semantics=("arbitrary",), vmem_limit_bytes=VMEM_LIMIT),
        name="expert",
    )(block_expert, n_used, xs, w_gate_up, b_gu_perm, w_down, b_down)


def _combine_kernel(dest_ref, ys_hbm, w_ref, x1_ref, mod_ref, g_ref, o_ref, buf, sem):
    tm = x1_ref.shape[0]
    n_assign = tm * TOP_K

    def body(g, carry):
        for u in range(DMA_UNROLL):
            t_local = g * (DMA_UNROLL // TOP_K) + u // TOP_K
            k = u % TOP_K
            _row_copy(ys_hbm, dest_ref[g * DMA_UNROLL + u], buf, k * tm + t_local, sem).start(priority=u % 2)
        return carry

    lax.fori_loop(0, n_assign // DMA_UNROLL, body, 0)
    pltpu.make_async_copy(ys_hbm.at[pl.ds(0, n_assign * SLAB), :], buf, sem).wait()

    w = w_ref[...]
    y = w[:, 0:1] * _from_slabs(buf, 0, tm)
    for k in range(1, TOP_K):
        y = y + w[:, k:k + 1] * _from_slabs(buf, k * tm, tm)
    x = x1_ref[...] + mod_ref[0, 5:6, :] * y
    ms = jnp.mean(x * x, axis=-1, keepdims=True)
    o_ref[...] = x * lax.rsqrt(ms + EPS) * g_ref[...]


def _combine(dest_flat, ys, wts, x1, mod6, g, seq):
    t = x1.shape[0]
    tm = 256
    per_b = seq // tm
    d = D_MODEL
    return pl.pallas_call(
        _combine_kernel,
        grid=(t // tm,),
        in_specs=[
            pl.BlockSpec((tm * TOP_K,), lambda i: (i,), memory_space=pltpu.SMEM),
            pl.BlockSpec(memory_space=pl.ANY),
            pl.BlockSpec((tm, LANES), lambda i: (i, 0)),
            pl.BlockSpec((tm, d), lambda i: (i, 0)),
            pl.BlockSpec((1, 6, d), lambda i: (i // per_b, 0, 0)),
            pl.BlockSpec((1, d), lambda i: (0, 0)),
        ],
        out_specs=pl.BlockSpec((tm, d), lambda i: (i, 0)),
        out_shape=jax.ShapeDtypeStruct((t, d), F32),
        scratch_shapes=[pltpu.VMEM((tm * TOP_K * SLAB, LANES), F32), pltpu.SemaphoreType.DMA],
        compiler_params=pltpu.CompilerParams(
            dimension_semantics=("arbitrary",), vmem_limit_bytes=VMEM_LIMIT),
        name="combine",
    )(dest_flat, ys, wts, x1, mod6, g)


def _split_w(w):
    hi = w.astype(BF16)
    lo = (w - hi.astype(F32)).astype(BF16)
    return hi, lo


def kernel(x, c, positions, w_ada, b_ada, norm_mix_g, w_in, conv_w, conv_b, b_if, ml_norm_g, ret_norm_g,
           w_branch_ml, w_branch_ret, w_out, norm_ffn_g, w_router, b_router, w_gate_up, b_gate_up, w_down,
           b_down, norm_final_g):
    batch, seq, d = x.shape
    t = batch * seq
    x2 = x.reshape(t, d)
    depth = w_ada.shape[0]
    c8 = jnp.concatenate([c, jnp.zeros((8 - batch, d), c.dtype)], axis=0)

    half = D_QK // 2
    inv = ROPE_BASE ** (-jnp.arange(half, dtype=F32) / half)
    ang = positions.astype(F32)[..., None] * inv
    cos = jnp.cos(ang).reshape(t, half)
    sin = jnp.sin(ang).reshape(t, half)
    cos2 = jnp.concatenate([cos, cos], axis=-1)
    sin2 = jnp.concatenate([-sin, sin], axis=-1)
    L = CHUNK
    log_gamma = jnp.log(1.0 - 2.0 ** (-5.0 - jnp.arange(N_HEADS, dtype=F32)))
    pos = jnp.arange(L, dtype=F32)
    rel = pos[:, None] - pos[None, :]
    decay = jnp.where(rel >= 0, jnp.exp(log_gamma[:, None, None] * jnp.maximum(rel, 0.0)), 0.0)
    kdec = jnp.exp(log_gamma[:, None] * (L - 1 - pos))[:, :, None]
    qdec = jnp.exp(log_gamma[:, None] * (pos + 1.0))[:, :, None]
    gch = jnp.broadcast_to(jnp.exp(log_gamma * L)[:, None, None], (N_HEADS, 1, LANES))

    assert depth == 1, "the final norm is fused after the single layer"
    l = 0
    wa_hi, wa_lo = _split_w(w_ada[l])
    mod = _ada(c8, wa_hi, wa_lo, b_ada[l][None, :])
    mod6 = mod[:batch].reshape(batch, 6, d)

    w = w_in[l]
    w_main = jnp.concatenate(
        [w[:, 0:3072], w[:, 3080:8200]], axis=1).astype(BF16)
    w_if = w[:, 3072:3080]
    zpad = jnp.zeros((d, LANES - N_HEADS), F32)
    wif = jnp.concatenate([w_if[:, :N_HEADS], zpad, w_if[:, N_HEADS:], zpad], axis=1)
    wif_hi, wif_lo = _split_w(wif)
    zb = jnp.zeros((LANES - N_HEADS,), F32)
    b_if2 = jnp.concatenate([b_if[l][:N_HEADS], zb, b_if[l][N_HEADS:], zb])[None, :]

    proj, gates = _inproj(x2, mod6, norm_mix_g[l][None, :], w_main, wif_hi, wif_lo, seq)
    hm, hr = _mix(proj, gates, cos2, sin2, conv_w[l], conv_b[l][None, :], b_if2,
                  ml_norm_g[l][None, :], ret_norm_g[l][None, :], decay, kdec, qdec, gch, batch, seq)

    wr = jnp.concatenate([w_router[l], jnp.zeros((d, LANES - N_EXPERTS), F32)], axis=1)
    wr_hi, wr_lo = _split_w(wr)
    br = jnp.concatenate([b_router[l], jnp.full((LANES - N_EXPERTS,), -1e30, F32)])[None, :]
    x1, hf, logits = _post(hm, hr, proj, x2, mod6, w_branch_ml[l].astype(BF16), w_branch_ret[l].astype(BF16),
                           w_out[l].astype(BF16), norm_ffn_g[l][None, :], wr_hi, wr_lo, br, seq)

    n_assign = t * TOP_K
    n_blocks = -(-n_assign // EXPERT_BLOCK) + N_EXPERTS
    dest, wts, tables = _route(logits)
    dest_flat = dest[:, :TOP_K].reshape(n_assign)
    bend = tables[0, :N_EXPERTS].astype(jnp.int32)
    pad_first = tables[1, :N_EXPERTS].astype(jnp.int32)
    n_used = bend[N_EXPERTS - 1:]
    blk = jnp.minimum(jnp.arange(n_blocks, dtype=jnp.int32), n_used - 1)
    block_expert = jnp.minimum(
        jnp.sum((bend[None, :] <= blk[:, None]).astype(jnp.int32), axis=1), N_EXPERTS - 1)
    xs = _dispatch(dest_flat, jnp.concatenate([pad_first, n_used]), hf, n_blocks)
    f2 = w_gate_up.shape[-1]
    bgu = b_gate_up[l].reshape(N_EXPERTS, f2 // (2 * LANES), LANES, 2)
    bgu = jnp.swapaxes(bgu, -1, -2).reshape(N_EXPERTS, 1, f2)
    ys = _experts(block_expert, n_used, xs, w_gate_up[l], bgu, w_down[l], b_down[l][:, None, :])
    out = _combine(dest_flat, ys, wts, x1, mod6, norm_final_g[None, :], seq)
    return out.reshape(batch, seq, d)
```

```python
import functools

import numpy as np
import jax
import jax.numpy as jnp
from jax import lax
from jax.experimental import pallas as pl
from jax.experimental.pallas import tpu as pltpu

D_MODEL = 1024
N_HEADS = 4
D_QK = 128
D_V = 256
CONV_K = 4
ROPE_BASE = 10000.0
CHUNK = 128
N_EXPERTS = 32
TOP_K = 4
SWIGLU_LIMIT = 7.0
SWIGLU_ALPHA = 1.702
EXPERT_BLOCK = 512
EPS = 1e-5

QK_W = N_HEADS * D_QK
V_W = N_HEADS * D_V
LANES = 128
GATE_W = 2 * LANES
V_EXT = D_V + LANES
MIX_GROUP = 4
SLAB = D_MODEL // LANES

C_MLQ, C_MLK, C_MLV, C_MLO = 0, 512, 1024, 2048
C_RQ, C_RK, C_RV, C_RG = 3072, 3584, 4096, 5120
C_GML, C_GRET = 6144, 7168
MIX_W = 6144
PROJ_W = 8192

BF16 = jnp.bfloat16
F32 = jnp.float32
VMEM_LIMIT = 56 * 1024 * 1024


def _dot(a, b):
    return jnp.dot(a, b, preferred_element_type=F32)


def _dot_nt(a, b):
    return lax.dot_general(a, b, (((1,), (1,)), ((), ())), preferred_element_type=F32)


def _split(a):
    hi = a.astype(BF16)
    lo = (a - hi.astype(F32)).astype(BF16)
    return hi, lo


def _dot3(a, b_hi, b_lo):
    a_hi, a_lo = _split(a)
    return _dot(a_hi, b_hi) + (_dot(a_lo, b_hi) + _dot(a_hi, b_lo))


def _sigmoid(x):
    return 0.5 * jnp.tanh(0.5 * x) + 0.5


def _sigmoid_mul(x, y):
    hy = 0.5 * y
    return hy * jnp.tanh(0.5 * x) + hy


def _silu(x):
    hx = 0.5 * x
    return hx * jnp.tanh(hx) + hx


def _rms_mod(x, g, scale, shift):
    ms = jnp.mean(x * x, axis=-1, keepdims=True)
    return (x * lax.rsqrt(ms + EPS) * g) * (1.0 + scale) + shift


def _ada_kernel(c_ref, whi_ref, wlo_ref, b_ref, o_ref):
    c = c_ref[...]
    ca = _silu(c)
    o_ref[...] = _dot3(ca, whi_ref[...], wlo_ref[...]) + b_ref[...]


def _ada(c8, w_hi, w_lo, b):
    n = w_hi.shape[1]
    tn = 1024
    return pl.pallas_call(
        _ada_kernel,
        grid=(n // tn,),
        in_specs=[
            pl.BlockSpec((8, D_MODEL), lambda j: (0, 0)),
            pl.BlockSpec((D_MODEL, tn), lambda j: (0, j)),
            pl.BlockSpec((D_MODEL, tn), lambda j: (0, j)),
            pl.BlockSpec((1, tn), lambda j: (0, j)),
        ],
        out_specs=pl.BlockSpec((8, tn), lambda j: (0, j)),
        out_shape=jax.ShapeDtypeStruct((8, n), F32),
        name="ada",
    )(c8, w_hi, w_lo, b)


def _inproj_kernel(x_ref, mod_ref, g_ref, w_ref, wif_hi_ref, wif_lo_ref, proj_ref, gates_ref, hb_ref):
    j = pl.program_id(1)

    @pl.when(j == 0)
    def _():
        h = _rms_mod(x_ref[...], g_ref[...], mod_ref[0, 1:2, :], mod_ref[0, 0:1, :])
        hb_ref[...] = h.astype(BF16)
        gates_ref[...] = _dot3(h, wif_hi_ref[...], wif_lo_ref[...])

    proj_ref[...] = _dot(hb_ref[...], w_ref[...]).astype(BF16)


def _inproj(x2, mod6, g, w_main, wif_hi, wif_lo, seq):
    t = x2.shape[0]
    tm, tn = 512, 2048
    per_b = seq // tm
    return pl.pallas_call(
        _inproj_kernel,
        grid=(t // tm, PROJ_W // tn),
        in_specs=[
            pl.BlockSpec((tm, D_MODEL), lambda i, j: (i, 0)),
            pl.BlockSpec((1, 6, D_MODEL), lambda i, j: (i // per_b, 0, 0)),
            pl.BlockSpec((1, D_MODEL), lambda i, j: (0, 0)),
            pl.BlockSpec((D_MODEL, tn), lambda i, j: (0, j)),
            pl.BlockSpec((D_MODEL, GATE_W), lambda i, j: (0, 0)),
            pl.BlockSpec((D_MODEL, GATE_W), lambda i, j: (0, 0)),
        ],
        out_specs=[
            pl.BlockSpec((tm, tn), lambda i, j: (i, j)),
            pl.BlockSpec((tm, GATE_W), lambda i, j: (i, 0)),
        ],
        out_shape=[
            jax.ShapeDtypeStruct((t, PROJ_W), BF16),
            jax.ShapeDtypeStruct((t, GATE_W), F32),
        ],
        scratch_shapes=[pltpu.VMEM((tm, D_MODEL), BF16)],
        compiler_params=pltpu.CompilerParams(
            dimension_semantics=("arbitrary", "arbitrary"), vmem_limit_bytes=VMEM_LIMIT),
        name="inproj",
    )(x2, mod6, g, w_main, wif_hi, wif_lo)


def _row_mean(a):
    inv_n = jnp.full((a.shape[1], LANES), 1.0 / a.shape[1], BF16)
    m = _dot(a.astype(BF16), inv_n)
    return jnp.concatenate([m] * (a.shape[1] // LANES), axis=1)


def _head_norm(h, g):
    d = h - _row_mean(h)
    return d * lax.rsqrt(_row_mean(d * d) + EPS) * g


def _mix_kernel(proj_ref, gates_ref, cos_ref, sin_ref, convw_ref, convb_ref, bif_ref, mlg_ref, retg_ref,
                decay_ref, kdec_ref, qdec_ref, gch_ref,
                hm_ref, hr_ref,
                prev_ref, c_st, m_st, r_st):
    group = proj_ref.shape[0]

    @pl.when(pl.program_id(1) == 0)
    def _():
        def zero_state(i, carry):
            c_st[i] = jnp.zeros(c_st.shape[1:], F32)
            r_st[i] = jnp.zeros(r_st.shape[1:], F32)
            return carry

        lax.fori_loop(0, group * N_HEADS, zero_state, 0)
        prev_ref[...] = jnp.zeros_like(prev_ref)
        m_st[...] = jnp.zeros_like(m_st)

    steps = []
    for gb in range(group):
        states = slice(gb * N_HEADS, (gb + 1) * N_HEADS)
        steps.append(_mlstm_steps(proj_ref.at[gb], gates_ref.at[gb], convw_ref, convb_ref, bif_ref, mlg_ref,
                                  hm_ref.at[gb], prev_ref.at[gb], c_st.at[states], m_st.at[gb]))
        steps.append(_retention_steps(proj_ref.at[gb], cos_ref.at[gb], sin_ref.at[gb], retg_ref, decay_ref,
                                      kdec_ref, qdec_ref, gch_ref, hr_ref.at[gb], r_st.at[states]))
    while steps:
        for st in list(steps):
            if next(st, True):
                steps.remove(st)


def _mlstm_steps(proj_ref, gates_ref, convw_ref, convb_ref, bif_ref, mlg_ref, hm_ref, prev_ref, c_st, m_st):
    L = CHUNK
    rows = lax.broadcasted_iota(jnp.int32, (L, L), 0)
    cols = lax.broadcasted_iota(jnp.int32, (L, L), 1)
    causal = rows >= cols
    tril = jnp.where(causal, 1.0, 0.0).astype(BF16)

    cur = proj_ref[:, C_MLQ:C_MLQ + 2 * QK_W]
    xx = jnp.concatenate([prev_ref[...], cur], axis=0)
    r2 = lax.broadcasted_iota(jnp.int32, (L, 2 * L), 0)
    c2 = lax.broadcasted_iota(jnp.int32, (L, 2 * L), 1)
    acc = convb_ref[...] + cur.astype(F32) * convw_ref[CONV_K - 1:CONV_K, :]
    for d in range(1, CONV_K):
        shift = jnp.where(c2 == r2 + (L - d), 1.0, 0.0).astype(BF16)
        acc = acc + _dot(shift, xx) * convw_ref[CONV_K - 1 - d:CONV_K - d, :]
        yield False
    prev_ref[...] = cur
    qk = _silu(acc)
    yield False

    g = gates_ref[...] + bif_ref[...]
    gi = g[:, :LANES]
    gf = g[:, LANES:]
    lf = jnp.minimum(gf, 0.0) - jnp.log(1.0 + jnp.exp(-jnp.abs(gf)))
    lf_hi, lf_lo = _split(lf)
    a_all = _dot(tril, lf_hi) + _dot(tril, lf_lo)
    yield False
    a_last = a_all[L - 1:L, :]
    bm = gi - a_all
    bm_t = bm.T
    w_state = a_last + bm
    m_loc = jnp.max(w_state, axis=0, keepdims=True)
    m_prev = m_st[...]
    inter_log = a_all + m_prev
    m_new = jnp.maximum(a_last + m_prev, m_loc)
    s_prev = jnp.exp(a_last + m_prev - m_new)
    s_loc = jnp.exp(m_loc - m_new)
    ws_all = jnp.exp(w_state - m_loc) * s_loc
    m_st[...] = m_new
    yield False

    ones_blk = jnp.ones((L, LANES), BF16)
    q_scale = D_QK ** -0.5

    for h in range(N_HEADS):
        q = (qk[:, h * D_QK:(h + 1) * D_QK] * q_scale).astype(BF16)
        k_f = qk[:, QK_W + h * D_QK:QK_W + (h + 1) * D_QK]
        k = k_f.astype(BF16)
        v_ext = jnp.concatenate([proj_ref[:, C_MLV + h * D_V:C_MLV + (h + 1) * D_V], ones_blk], axis=1)
        dlog = jnp.where(causal, a_all[:, h:h + 1] + bm_t[h:h + 1, :], -jnp.inf)
        m_intra = jnp.max(dlog, axis=-1, keepdims=True)
        s = _dot_nt(q, k)
        qc = _dot(q, c_st[h].astype(BF16))
        yield False
        il = inter_log[:, h:h + 1]
        m_t = jnp.maximum(il, m_intra)
        p = jnp.exp(dlog - m_t) * s
        isc = jnp.exp(il - m_t)
        tot = _dot(p.astype(BF16), v_ext) + isc * qc
        yield False
        den = tot[:, D_V:D_V + 1]
        hout = tot[:, :D_V] / jnp.maximum(jnp.abs(den), jnp.exp(-m_t))
        kw_t = (k_f * ws_all[:, h:h + 1]).T.astype(BF16)
        c_st[h] = s_prev[:, h:h + 1] * c_st[h] + _dot(kw_t, v_ext)
        yield False
        y = _head_norm(hout, mlg_ref[:, h * D_V:(h + 1) * D_V])
        o = proj_ref[:, C_MLO + h * D_V:C_MLO + (h + 1) * D_V].astype(F32)
        hm_ref[:, h * D_V:(h + 1) * D_V] = _sigmoid_mul(o, y).astype(BF16)
        yield False


def _retention_steps(proj_ref, cos_ref, sin_ref, retg_ref, decay_ref, kdec_ref, qdec_ref, gch_ref, hr_ref, r_st):
    cos2 = cos_ref[...]
    sin2 = sin_ref[...]
    k_scale = D_QK ** -0.5
    for h in range(N_HEADS):
        q_raw = proj_ref[:, C_RQ + h * D_QK:C_RQ + (h + 1) * D_QK].astype(F32)
        k_raw = proj_ref[:, C_RK + h * D_QK:C_RK + (h + 1) * D_QK].astype(F32)
        q = (q_raw * cos2 + pltpu.roll(q_raw, D_QK // 2, 1) * sin2).astype(BF16)
        k_f = (k_raw * cos2 + pltpu.roll(k_raw, D_QK // 2, 1) * sin2) * k_scale
        v = proj_ref[:, C_RV + h * D_V:C_RV + (h + 1) * D_V]
        yield False
        sc = _dot_nt(q, k_f.astype(BF16)) * decay_ref[h]
        hret = _dot(sc.astype(BF16), v) + _dot(q, r_st[h].astype(BF16)) * qdec_ref[h]
        yield False
        kd_t = (k_f * kdec_ref[h]).T.astype(BF16)
        r_st[h] = gch_ref[h][:, 0:1] * r_st[h] + _dot(kd_t, v)
        yield False
        y = _head_norm(hret, retg_ref[:, h * D_V:(h + 1) * D_V])
        gt = proj_ref[:, C_RG + h * D_V:C_RG + (h + 1) * D_V].astype(F32)
        hr_ref[:, h * D_V:(h + 1) * D_V] = (_silu(gt) * y).astype(BF16)
        yield False


def _mix(proj, gates, cos2, sin2, conv_w, conv_b, b_if2, ml_g, ret_g, decay, kdec, qdec, gch, batch, seq):
    t = proj.shape[0]
    nc = seq // CHUNK
    L = CHUNK
    G = MIX_GROUP
    proj = proj.reshape(batch, seq, PROJ_W)
    gates = gates.reshape(batch, seq, GATE_W)
    cos2 = cos2.reshape(batch, seq, LANES)
    sin2 = sin2.reshape(batch, seq, LANES)
    full = lambda shape: pl.BlockSpec(shape, lambda b, c: (0,) * len(shape))
    hm, hr = pl.pallas_call(
        _mix_kernel,
        grid=(batch // G, nc),
        in_specs=[
            pl.BlockSpec((G, L, MIX_W), lambda b, c: (b, c, 0)),
            pl.BlockSpec((G, L, GATE_W), lambda b, c: (b, c, 0)),
            pl.BlockSpec((G, L, LANES), lambda b, c: (b, c, 0)),
            pl.BlockSpec((G, L, LANES), lambda b, c: (b, c, 0)),
            full((CONV_K, 2 * QK_W)),
            full((1, 2 * QK_W)),
            full((1, GATE_W)),
            full((1, V_W)),
            full((1, V_W)),
            full((N_HEADS, L, L)),
            full((N_HEADS, L, 1)),
            full((N_HEADS, L, 1)),
            full((N_HEADS, 1, LANES)),
        ],
        out_specs=[
            pl.BlockSpec((G, L, V_W), lambda b, c: (b, c, 0)),
            pl.BlockSpec((G, L, V_W), lambda b, c: (b, c, 0)),
        ],
        out_shape=[jax.ShapeDtypeStruct((batch, seq, V_W), BF16), jax.ShapeDtypeStruct((batch, seq, V_W), BF16)],
        scratch_shapes=[
            pltpu.VMEM((G, L, 2 * QK_W), BF16),
            pltpu.VMEM((G * N_HEADS, D_QK, V_EXT), F32),
            pltpu.VMEM((G, 1, LANES), F32),
            pltpu.VMEM((G * N_HEADS, D_QK, D_V), F32),
        ],
        compiler_params=pltpu.CompilerParams(
            dimension_semantics=("arbitrary", "arbitrary"), vmem_limit_bytes=VMEM_LIMIT),
        name="mix",
    )(proj, gates, cos2, sin2, conv_w, conv_b, b_if2, ml_g, ret_g, decay, kdec, qdec, gch)
    return hm.reshape(t, V_W), hr.reshape(t, V_W)


def _to_slabs(ref, val):
    rows = val.shape[0]
    for s in range(SLAB):
        ref[pl.ds(s, rows, stride=SLAB), :] = val[:, s * LANES:(s + 1) * LANES]


def _from_slabs(ref, first_slab, rows):
    return jnp.concatenate(
        [ref[pl.ds(first_slab * SLAB + s, rows, stride=SLAB), :] for s in range(SLAB)], axis=1)


def _post_kernel(hm_ref, hr_ref, gm_ref, gr_ref, x_ref, mod_ref, wbm_ref, wbr_ref, wout_ref, g_ref,
                 wr_hi_ref, wr_lo_ref, br_ref, x1_ref, hf_ref, logit_ref):
    ym = _dot(hm_ref[...], wbm_ref[...])
    yr = _dot(hr_ref[...], wbr_ref[...])
    y = _sigmoid_mul(gm_ref[...].astype(F32), ym) + _sigmoid_mul(gr_ref[...].astype(F32), yr)
    o = _dot(y.astype(BF16), wout_ref[...])
    x1 = x_ref[...] + mod_ref[0, 2:3, :] * o
    x1_ref[...] = x1
    hf = _rms_mod(x1, g_ref[...], mod_ref[0, 4:5, :], mod_ref[0, 3:4, :])
    _to_slabs(hf_ref, hf)
    logit_ref[...] = _dot3(hf, wr_hi_ref[...], wr_lo_ref[...]) + br_ref[...]


def _post(hm, hr, proj, x2, mod6, wbm, wbr, wout, g, wr_hi, wr_lo, br, seq):
    t = x2.shape[0]
    tm = 512
    per_b = seq // tm
    d = D_MODEL
    const = lambda shape: pl.BlockSpec(shape, lambda i: (0,) * len(shape))
    return pl.pallas_call(
        _post_kernel,
        grid=(t // tm,),
        in_specs=[
            pl.BlockSpec((tm, d), lambda i: (i, 0)),
            pl.BlockSpec((tm, d), lambda i: (i, 0)),
            pl.BlockSpec((tm, d), lambda i: (i, C_GML // d)),
            pl.BlockSpec((tm, d), lambda i: (i, C_GRET // d)),
            pl.BlockSpec((tm, d), lambda i: (i, 0)),
            pl.BlockSpec((1, 6, d), lambda i: (i // per_b, 0, 0)),
            const((d, d)), const((d, d)), const((d, d)),
            const((1, d)),
            const((d, LANES)), const((d, LANES)), const((1, LANES)),
        ],
        out_specs=[
            pl.BlockSpec((tm, d), lambda i: (i, 0)),
            pl.BlockSpec((tm * SLAB, LANES), lambda i: (i, 0)),
            pl.BlockSpec((tm, LANES), lambda i: (i, 0)),
        ],
        out_shape=[
            jax.ShapeDtypeStruct((t, d), F32),
            jax.ShapeDtypeStruct((t * SLAB, LANES), F32),
            jax.ShapeDtypeStruct((t, LANES), F32),
        ],
        compiler_params=pltpu.CompilerParams(
            dimension_semantics=("arbitrary",), vmem_limit_bytes=VMEM_LIMIT),
        name="post",
    )(hm, hr, proj, proj, x2, mod6, wbm, wbr, wout, g, wr_hi, wr_lo, br)


def _route_kernel(logit_ref, dest_ref, w_ref, bend_ref, cnt_st, pad_st):
    ph = pl.program_id(0)
    i = pl.program_id(1)
    tm = logit_ref.shape[0]
    lane = lax.broadcasted_iota(jnp.int32, (tm, LANES), 1)
    l = logit_ref[...]
    onehots, vals = [], []
    for _ in range(TOP_K):
        m = jnp.max(l, axis=-1, keepdims=True)
        idx = jnp.min(jnp.where(l == m, lane, LANES), axis=-1, keepdims=True)
        oh = lane == idx
        onehots.append(oh)
        vals.append(m)
        l = jnp.where(oh, -jnp.inf, l)
    sel = jnp.where(onehots[0] | onehots[1] | onehots[2] | onehots[3], 1.0, 0.0)
    tile_cnt = jnp.sum(sel, axis=0, keepdims=True)

    @pl.when(jnp.logical_and(ph == 0, i == 0))
    def _():
        cnt_st[...] = jnp.zeros_like(cnt_st)

    @pl.when(ph == 0)
    def _():
        cnt_st[...] = cnt_st[...] + tile_cnt

    @pl.when(jnp.logical_and(ph == 1, i == 0))
    def _():
        blocks = jnp.floor((cnt_st[...] + (EXPERT_BLOCK - 1)) * (1.0 / EXPERT_BLOCK))
        r = lax.broadcasted_iota(jnp.int32, (LANES, LANES), 0)
        c = lax.broadcasted_iota(jnp.int32, (LANES, LANES), 1)
        upper = jnp.where(r < c, 1.0, 0.0).astype(BF16)
        blocks8 = jnp.broadcast_to(blocks, (8, LANES))
        excl = _dot(blocks8.astype(BF16), upper)
        pad_st[...] = excl[0:1, :] * EXPERT_BLOCK
        row = lax.broadcasted_iota(jnp.int32, (8, LANES), 0)
        bend_ref[...] = jnp.where(row == 0, excl + blocks8,
                                  jnp.where(row == 1, excl * EXPERT_BLOCK + cnt_st[...], 0.0))
        cnt_st[...] = jnp.zeros_like(cnt_st)

    @pl.when(ph == 1)
    def _():
        carry = cnt_st[...]
        r = lax.broadcasted_iota(jnp.int32, (tm, tm), 0)
        c = lax.broadcasted_iota(jnp.int32, (tm, tm), 1)
        lower = jnp.where(r > c, 1.0, 0.0).astype(BF16)
        base = pad_st[...] + carry + _dot(lower, sel.astype(BF16))
        dest = jnp.zeros((tm, LANES), F32)
        wts = jnp.zeros((tm, LANES), F32)
        ex = [jnp.exp(v - vals[0]) for v in vals]
        den = ex[0] + ex[1] + ex[2] + ex[3]
        for k in range(TOP_K):
            dk = jnp.sum(jnp.where(onehots[k], base, 0.0), axis=-1, keepdims=True)
            dest = jnp.where(lane == k, dk, dest)
            wts = jnp.where(lane == k, ex[k] / den, wts)
        dest_ref[...] = dest.astype(jnp.int32)
        w_ref[...] = wts
        cnt_st[...] = carry + tile_cnt


def _route(logits):
    t = logits.shape[0]
    tm = 512
    return pl.pallas_call(
        _route_kernel,
        grid=(2, t // tm),
        in_specs=[pl.BlockSpec((tm, LANES), lambda ph, i: (i, 0))],
        out_specs=[
            pl.BlockSpec((tm, LANES), lambda ph, i: (i * ph, 0)),
            pl.BlockSpec((tm, LANES), lambda ph, i: (i * ph, 0)),
            pl.BlockSpec((8, LANES), lambda ph, i: (0, 0)),
        ],
        out_shape=[
            jax.ShapeDtypeStruct((t, LANES), jnp.int32),
            jax.ShapeDtypeStruct((t, LANES), F32),
            jax.ShapeDtypeStruct((8, LANES), F32),
        ],
        scratch_shapes=[pltpu.VMEM((1, LANES), F32), pltpu.VMEM((1, LANES), F32)],
        compiler_params=pltpu.CompilerParams(
            dimension_semantics=("arbitrary", "arbitrary"), vmem_limit_bytes=VMEM_LIMIT),
        name="route",
    )(logits)


DMA_UNROLL = 8


def _row_copy(src_ref, src_slab, dst_ref, dst_slab, sem):
    src = pl.multiple_of(src_slab * SLAB, SLAB)
    dst = pl.multiple_of(dst_slab * SLAB, SLAB)
    return pltpu.make_async_copy(src_ref.at[pl.ds(src, SLAB), :], dst_ref.at[pl.ds(dst, SLAB), :], sem)


def _dispatch_kernel(dest_ref, padfirst_ref, hf_ref, xs_out, zbuf, sem, zsem):
    n_assign = dest_ref.shape[0]
    blk_rows = zbuf.shape[0]

    @pl.when(pl.program_id(0) == 0)
    def _():
        zbuf[...] = jnp.zeros_like(zbuf)
        for e in range(N_EXPERTS):
            start = pl.multiple_of(padfirst_ref[e] * SLAB, SLAB)
            pltpu.make_async_copy(zbuf, xs_out.at[pl.ds(start, blk_rows), :], zsem).start()
        for e in range(N_EXPERTS):
            pltpu.make_async_copy(zbuf, xs_out.at[pl.ds(0, blk_rows), :], zsem).wait()

        n_used = padfirst_ref[N_EXPERTS]
        n_total = xs_out.shape[0] // blk_rows

        def zero_start(b, carry):
            start = pl.multiple_of(b * blk_rows, blk_rows)
            pltpu.make_async_copy(zbuf, xs_out.at[pl.ds(start, blk_rows), :], zsem).start()
            return carry

        def zero_wait(b, carry):
            pltpu.make_async_copy(zbuf, xs_out.at[pl.ds(0, blk_rows), :], zsem).wait()
            return carry

        lax.fori_loop(n_used, n_total, zero_start, 0)
        lax.fori_loop(n_used, n_total, zero_wait, 0)

    def body(g, carry):
        for u in range(DMA_UNROLL):
            t_local = g * (DMA_UNROLL // TOP_K) + u // TOP_K
            _row_copy(hf_ref, t_local, xs_out, dest_ref[g * DMA_UNROLL + u], sem).start(priority=u % 2)
        return carry

    lax.fori_loop(0, n_assign // DMA_UNROLL, body, 0)
    for _ in range(TOP_K):
        pltpu.make_async_copy(hf_ref, xs_out.at[pl.ds(0, hf_ref.shape[0]), :], sem).wait()


def _dispatch(dest_flat, pad_first, hf_slab, n_blocks):
    n_assign = dest_flat.shape[0]
    tm = 512
    blk_rows = EXPERT_BLOCK * SLAB
    return pl.pallas_call(
        _dispatch_kernel,
        grid=(n_assign // (tm * TOP_K),),
        in_specs=[
            pl.BlockSpec((tm * TOP_K,), lambda i: (i,), memory_space=pltpu.SMEM),
            pl.BlockSpec(memory_space=pltpu.SMEM),
            pl.BlockSpec((tm * SLAB, LANES), lambda i: (i, 0)),
        ],
        out_specs=pl.BlockSpec(memory_space=pl.ANY),
        out_shape=jax.ShapeDtypeStruct(((n_blocks + 1) * blk_rows, LANES), F32),
        scratch_shapes=[pltpu.VMEM((blk_rows, LANES), F32), pltpu.SemaphoreType.DMA, pltpu.SemaphoreType.DMA],
        compiler_params=pltpu.CompilerParams(dimension_semantics=("arbitrary",), vmem_limit_bytes=VMEM_LIMIT),
        name="dispatch",
    )(dest_flat, pad_first, hf_slab)


def _expert_kernel(be_ref, nused_ref, xs_ref, wgu_ref, bgu_ref, wd_ref, bd_ref, ys_ref, wgu_bf, wd_bf):
    b = pl.program_id(0)

    @pl.when(b < nused_ref[0])
    def _():
        _expert_block(b, be_ref, xs_ref, wgu_ref, bgu_ref, wd_ref, bd_ref, ys_ref, wgu_bf, wd_bf)

    @pl.when(b >= nused_ref[0])
    def _():
        ys_ref[...] = jnp.zeros_like(ys_ref)


def _expert_block(b, be_ref, xs_ref, wgu_ref, bgu_ref, wd_ref, bd_ref, ys_ref, wgu_bf, wd_bf):
    prev = be_ref[jnp.maximum(b - 1, 0)]
    first = jnp.logical_or(b == 0, be_ref[b] != prev)
    two = 2 * LANES

    @pl.when(first)
    def _():
        r = lax.broadcasted_iota(jnp.int32, (two, two), 0)
        c = lax.broadcasted_iota(jnp.int32, (two, two), 1)
        src = jnp.where(c < LANES, 2 * c, 2 * (c - LANES) + 1)
        perm = jnp.where(r == src, 1.0, 0.0).astype(BF16)
        for blk in range(wgu_bf.shape[1] // two):
            wb = wgu_ref[0, :, blk * two:(blk + 1) * two].astype(BF16)
            wgu_bf[:, blk * two:(blk + 1) * two] = _dot(wb, perm).astype(BF16)
        wd_bf[...] = wd_ref[0].astype(BF16)

    x = _from_slabs(xs_ref, 0, EXPERT_BLOCK).astype(BF16)
    gu = _dot(x, wgu_bf[...]) + bgu_ref[0]
    n_blk = gu.shape[1] // two
    x_glu = jnp.concatenate([gu[:, i * two:i * two + LANES] for i in range(n_blk)], axis=1)
    x_lin = jnp.concatenate([gu[:, i * two + LANES:(i + 1) * two] for i in range(n_blk)], axis=1)
    x_glu = jnp.minimum(x_glu, SWIGLU_LIMIT)
    x_lin = jnp.clip(x_lin, -SWIGLU_LIMIT, SWIGLU_LIMIT)
    act = _sigmoid_mul(SWIGLU_ALPHA * x_glu, x_glu * (x_lin + 1.0))
    y = _dot(act.astype(BF16), wd_bf[...]) + bd_ref[0]
    _to_slabs(ys_ref, y)


def _experts(block_expert, n_used, xs, w_gate_up, b_gu_perm, w_down, b_down):
    n_blocks = block_expert.shape[0]
    d = D_MODEL
    f2 = w_gate_up.shape[-1]
    blk_rows = EXPERT_BLOCK * SLAB
    grid_spec = pltpu.PrefetchScalarGridSpec(
        num_scalar_prefetch=2,
        grid=(n_blocks,),
        in_specs=[
            pl.BlockSpec((blk_rows, LANES), lambda b, be, nu: (b, 0)),
            pl.BlockSpec((1, d, f2), lambda b, be, nu: (be[b], 0, 0)),
            pl.BlockSpec((1, 1, f2), lambda b, be, nu: (be[b], 0, 0)),
            pl.BlockSpec((1, f2 // 2, d), lambda b, be, nu: (be[b], 0, 0)),
            pl.BlockSpec((1, 1, d), lambda b, be, nu: (be[b], 0, 0)),
        ],
        out_specs=pl.BlockSpec((blk_rows, LANES), lambda b, be, nu: (b, 0)),
        scratch_shapes=[pltpu.VMEM((d, f2), BF16), pltpu.VMEM((f2 // 2, d), BF16)],
    )
    return pl.pallas_call(
        _expert_kernel,
        grid_spec=grid_spec,
        out_shape=jax.ShapeDtypeStruct((n_blocks * blk_rows, LANES), F32),
        compiler_params=pltpu.CompilerParams(
            dimension_semantics=("arbitrary",), vmem_limit_bytes=VMEM_LIMIT),
        name="expert",
    )(block_expert, n_used, xs, w_gate_up, b_gu_perm, w_down, b_down)


def _combine_kernel(dest_ref, ys_hbm, w_ref, x1_ref, mod_ref, g_ref, o_ref, buf, sem):
    tm = x1_ref.shape[0]
    n_assign = tm * TOP_K

    def body(g, carry):
        for u in range(DMA_UNROLL):
            t_local = g * (DMA_UNROLL // TOP_K) + u // TOP_K
            k = u % TOP_K
            _row_copy(ys_hbm, dest_ref[g * DMA_UNROLL + u], buf, k * tm + t_local, sem).start(priority=u % 2)
        return carry

    lax.fori_loop(0, n_assign // DMA_UNROLL, body, 0)
    pltpu.make_async_copy(ys_hbm.at[pl.ds(0, n_assign * SLAB), :], buf, sem).wait()

    w = w_ref[...]
    y = w[:, 0:1] * _from_slabs(buf, 0, tm)
    for k in range(1, TOP_K):
        y = y + w[:, k:k + 1] * _from_slabs(buf, k * tm, tm)
    x = x1_ref[...] + mod_ref[0, 5:6, :] * y
    ms = jnp.mean(x * x, axis=-1, keepdims=True)
    o_ref[...] = x * lax.rsqrt(ms + EPS) * g_ref[...]


def _combine(dest_flat, ys, wts, x1, mod6, g, seq):
    t = x1.shape[0]
    tm = 256
    per_b = seq // tm
    d = D_MODEL
    return pl.pallas_call(
        _combine_kernel,
        grid=(t // tm,),
        in_specs=[
            pl.BlockSpec((tm * TOP_K,), lambda i: (i,), memory_space=pltpu.SMEM),
            pl.BlockSpec(memory_space=pl.ANY),
            pl.BlockSpec((tm, LANES), lambda i: (i, 0)),
            pl.BlockSpec((tm, d), lambda i: (i, 0)),
            pl.BlockSpec((1, 6, d), lambda i: (i // per_b, 0, 0)),
            pl.BlockSpec((1, d), lambda i: (0, 0)),
        ],
        out_specs=pl.BlockSpec((tm, d), lambda i: (i, 0)),
        out_shape=jax.ShapeDtypeStruct((t, d), F32),
        scratch_shapes=[pltpu.VMEM((tm * TOP_K * SLAB, LANES), F32), pltpu.SemaphoreType.DMA],
        compiler_params=pltpu.CompilerParams(
            dimension_semantics=("arbitrary",), vmem_limit_bytes=VMEM_LIMIT),
        name="combine",
    )(dest_flat, ys, wts, x1, mod6, g)


def _split_w(w):
    hi = w.astype(BF16)
    lo = (w - hi.astype(F32)).astype(BF16)
    return hi, lo


def kernel(x, c, positions, w_ada, b_ada, norm_mix_g, w_in, conv_w, conv_b, b_if, ml_norm_g, ret_norm_g,
           w_branch_ml, w_branch_ret, w_out, norm_ffn_g, w_router, b_router, w_gate_up, b_gate_up, w_down,
           b_down, norm_final_g):
    batch, seq, d = x.shape
    t = batch * seq
    x2 = x.reshape(t, d)
    depth = w_ada.shape[0]
    c8 = jnp.concatenate([c, jnp.zeros((8 - batch, d), c.dtype)], axis=0)

    half = D_QK // 2
    inv = ROPE_BASE ** (-jnp.arange(half, dtype=F32) / half)
    ang = positions.astype(F32)[..., None] * inv
    cos = jnp.cos(ang).reshape(t, half)
    sin = jnp.sin(ang).reshape(t, half)
    cos2 = jnp.concatenate([cos, cos], axis=-1)
    sin2 = jnp.concatenate([-sin, sin], axis=-1)
    L = CHUNK
    log_gamma = jnp.log(1.0 - 2.0 ** (-5.0 - jnp.arange(N_HEADS, dtype=F32)))
    pos = jnp.arange(L, dtype=F32)
    rel = pos[:, None] - pos[None, :]
    decay = jnp.where(rel >= 0, jnp.exp(log_gamma[:, None, None] * jnp.maximum(rel, 0.0)), 0.0)
    kdec = jnp.exp(log_gamma[:, None] * (L - 1 - pos))[:, :, None]
    qdec = jnp.exp(log_gamma[:, None] * (pos + 1.0))[:, :, None]
    gch = jnp.broadcast_to(jnp.exp(log_gamma * L)[:, None, None], (N_HEADS, 1, LANES))

    assert depth == 1, "the final norm is fused after the single layer"
    l = 0
    wa_hi, wa_lo = _split_w(w_ada[l])
    mod = _ada(c8, wa_hi, wa_lo, b_ada[l][None, :])
    mod6 = mod[:batch].reshape(batch, 6, d)

    w = w_in[l]
    w_main = jnp.concatenate(
        [w[:, 0:3072], w[:, 3080:8200]], axis=1).astype(BF16)
    w_if = w[:, 3072:3080]
    zpad = jnp.zeros((d, LANES - N_HEADS), F32)
    wif = jnp.concatenate([w_if[:, :N_HEADS], zpad, w_if[:, N_HEADS:], zpad], axis=1)
    wif_hi, wif_lo = _split_w(wif)
    zb = jnp.zeros((LANES - N_HEADS,), F32)
    b_if2 = jnp.concatenate([b_if[l][:N_HEADS], zb, b_if[l][N_HEADS:], zb])[None, :]

    proj, gates = _inproj(x2, mod6, norm_mix_g[l][None, :], w_main, wif_hi, wif_lo, seq)
    hm, hr = _mix(proj, gates, cos2, sin2, conv_w[l], conv_b[l][None, :], b_if2,
                  ml_norm_g[l][None, :], ret_norm_g[l][None, :], decay, kdec, qdec, gch, batch, seq)

    wr = jnp.concatenate([w_router[l], jnp.zeros((d, LANES - N_EXPERTS), F32)], axis=1)
    wr_hi, wr_lo = _split_w(wr)
    br = jnp.concatenate([b_router[l], jnp.full((LANES - N_EXPERTS,), -1e30, F32)])[None, :]
    x1, hf, logits = _post(hm, hr, proj, x2, mod6, w_branch_ml[l].astype(BF16), w_branch_ret[l].astype(BF16),
                           w_out[l].astype(BF16), norm_ffn_g[l][None, :], wr_hi, wr_lo, br, seq)

    n_assign = t * TOP_K
    n_blocks = -(-n_assign // EXPERT_BLOCK) + N_EXPERTS
    dest, wts, tables = _route(logits)
    dest_flat = dest[:, :TOP_K].reshape(n_assign)
    bend = tables[0, :N_EXPERTS].astype(jnp.int32)
    pad_first = tables[1, :N_EXPERTS].astype(jnp.int32)
    n_used = bend[N_EXPERTS - 1:]
    blk = jnp.minimum(jnp.arange(n_blocks, dtype=jnp.int32), n_used - 1)
    block_expert = jnp.minimum(
        jnp.sum((bend[None, :] <= blk[:, None]).astype(jnp.int32), axis=1), N_EXPERTS - 1)
    xs = _dispatch(dest_flat, jnp.concatenate([pad_first, n_used]), hf, n_blocks)
    f2 = w_gate_up.shape[-1]
    bgu = b_gate_up[l].reshape(N_EXPERTS, f2 // (2 * LANES), LANES, 2)
    bgu = jnp.swapaxes(bgu, -1, -2).reshape(N_EXPERTS, 1, f2)
    ys = _experts(block_expert, n_used, xs, w_gate_up[l], bgu, w_down[l], b_down[l][:, None, :])
    out = _combine(dest_flat, ys, wts, x1, mod6, norm_final_g[None, :], seq)
    return out.reshape(batch, seq, d)
```

```python
import functools

import numpy as np
import jax
import jax.numpy as jnp
from jax import lax
from jax.experimental import pallas as pl
from jax.experimental.pallas import tpu as pltpu

D_MODEL = 1024
N_HEADS = 4
D_QK = 128
D_V = 256
CONV_K = 4
ROPE_BASE = 10000.0
CHUNK = 128
N_EXPERTS = 32
TOP_K = 4
SWIGLU_LIMIT = 7.0
SWIGLU_ALPHA = 1.702
EXPERT_BLOCK = 512
EPS = 1e-5

QK_W = N_HEADS * D_QK
V_W = N_HEADS * D_V
LANES = 128
GATE_W = 2 * LANES
V_EXT = D_V + LANES
MIX_GROUP = 4
SLAB = D_MODEL // LANES

C_MLQ, C_MLK, C_MLV, C_MLO = 0, 512, 1024, 2048
C_RQ, C_RK, C_RV, C_RG = 3072, 3584, 4096, 5120
C_GML, C_GRET = 6144, 7168
MIX_W = 6144
PROJ_W = 8192

BF16 = jnp.bfloat16
F32 = jnp.float32
VMEM_LIMIT = 56 * 1024 * 1024


def _dot(a, b):
    return jnp.dot(a, b, preferred_element_type=F32)


def _dot_nt(a, b):
    return lax.dot_general(a, b, (((1,), (1,)), ((), ())), preferred_element_type=F32)


def _split(a):
    hi = a.astype(BF16)
    lo = (a - hi.astype(F32)).astype(BF16)
    return hi, lo


def _dot3(a, b_hi, b_lo):
    a_hi, a_lo = _split(a)
    return _dot(a_hi, b_hi) + (_dot(a_lo, b_hi) + _dot(a_hi, b_lo))


def _sigmoid(x):
    return 0.5 * jnp.tanh(0.5 * x) + 0.5


def _sigmoid_mul(x, y):
    hy = 0.5 * y
    return hy * jnp.tanh(0.5 * x) + hy


def _silu(x):
    hx = 0.5 * x
    return hx * jnp.tanh(hx) + hx


def _rms_mod(x, g, scale, shift):
    ms = jnp.mean(x * x, axis=-1, keepdims=True)
    return (x * lax.rsqrt(ms + EPS) * g) * (1.0 + scale) + shift


def _ada_kernel(c_ref, whi_ref, wlo_ref, b_ref, o_ref):
    c = c_ref[...]
    ca = _silu(c)
    o_ref[...] = _dot3(ca, whi_ref[...], wlo_ref[...]) + b_ref[...]


def _ada(c8, w_hi, w_lo, b):
    n = w_hi.shape[1]
    tn = 1024
    return pl.pallas_call(
        _ada_kernel,
        grid=(n // tn,),
        in_specs=[
            pl.BlockSpec((8, D_MODEL), lambda j: (0, 0)),
            pl.BlockSpec((D_MODEL, tn), lambda j: (0, j)),
            pl.BlockSpec((D_MODEL, tn), lambda j: (0, j)),
            pl.BlockSpec((1, tn), lambda j: (0, j)),
        ],
        out_specs=pl.BlockSpec((8, tn), lambda j: (0, j)),
        out_shape=jax.ShapeDtypeStruct((8, n), F32),
        name="ada",
    )(c8, w_hi, w_lo, b)


def _inproj_kernel(x_ref, mod_ref, g_ref, w_ref, wif_hi_ref, wif_lo_ref, proj_ref, gates_ref, hb_ref):
    j = pl.program_id(1)

    @pl.when(j == 0)
    def _():
        h = _rms_mod(x_ref[...], g_ref[...], mod_ref[0, 1:2, :], mod_ref[0, 0:1, :])
        hb_ref[...] = h.astype(BF16)
        gates_ref[...] = _dot3(h, wif_hi_ref[...], wif_lo_ref[...])

    proj_ref[...] = _dot(hb_ref[...], w_ref[...]).astype(BF16)


def _inproj(x2, mod6, g, w_main, wif_hi, wif_lo, seq):
    t = x2.shape[0]
    tm, tn = 512, 2048
    per_b = seq // tm
    return pl.pallas_call(
        _inproj_kernel,
        grid=(t // tm, PROJ_W // tn),
        in_specs=[
            pl.BlockSpec((tm, D_MODEL), lambda i, j: (i, 0)),
            pl.BlockSpec((1, 6, D_MODEL), lambda i, j: (i // per_b, 0, 0)),
            pl.BlockSpec((1, D_MODEL), lambda i, j: (0, 0)),
            pl.BlockSpec((D_MODEL, tn), lambda i, j: (0, j)),
            pl.BlockSpec((D_MODEL, GATE_W), lambda i, j: (0, 0)),
            pl.BlockSpec((D_MODEL, GATE_W), lambda i, j: (0, 0)),
        ],
        out_specs=[
            pl.BlockSpec((tm, tn), lambda i, j: (i, j)),
            pl.BlockSpec((tm, GATE_W), lambda i, j: (i, 0)),
        ],
        out_shape=[
            jax.ShapeDtypeStruct((t, PROJ_W), BF16),
            jax.ShapeDtypeStruct((t, GATE_W), F32),
        ],
        scratch_shapes=[pltpu.VMEM((tm, D_MODEL), BF16)],
        compiler_params=pltpu.CompilerParams(
            dimension_semantics=("arbitrary", "arbitrary"), vmem_limit_bytes=VMEM_LIMIT),
        name="inproj",
    )(x2, mod6, g, w_main, wif_hi, wif_lo)


def _row_mean(a):
    inv_n = jnp.full((a.shape[1], LANES), 1.0 / a.shape[1], BF16)
    m = _dot(a.astype(BF16), inv_n)
    return jnp.concatenate([m] * (a.shape[1] // LANES), axis=1)


def _head_norm(h, g):
    d = h - _row_mean(h)
    return d * lax.rsqrt(_row_mean(d * d) + EPS) * g


def _mix_kernel(proj_ref, gates_ref, cos_ref, sin_ref, convw_ref, convb_ref, bif_ref, mlg_ref, retg_ref,
                decay_ref, kdec_ref, qdec_ref, gch_ref,
                hm_ref, hr_ref,
                prev_ref, c_st, m_st, r_st):
    group = proj_ref.shape[0]

    @pl.when(pl.program_id(1) == 0)
    def _():
        def zero_state(i, carry):
            c_st[i] = jnp.zeros(c_st.shape[1:], F32)
            r_st[i] = jnp.zeros(r_st.shape[1:], F32)
            return carry

        lax.fori_loop(0, group * N_HEADS, zero_state, 0)
        prev_ref[...] = jnp.zeros_like(prev_ref)
        m_st[...] = jnp.zeros_like(m_st)

    steps = []
    for gb in range(group):
        states = slice(gb * N_HEADS, (gb + 1) * N_HEADS)
        steps.append(_mlstm_steps(proj_ref.at[gb], gates_ref.at[gb], convw_ref, convb_ref, bif_ref, mlg_ref,
                                  hm_ref.at[gb], prev_ref.at[gb], c_st.at[states], m_st.at[gb]))
        steps.append(_retention_steps(proj_ref.at[gb], cos_ref.at[gb], sin_ref.at[gb], retg_ref, decay_ref,
                                      kdec_ref, qdec_ref, gch_ref, hr_ref.at[gb], r_st.at[states]))
    while steps:
        for st in list(steps):
            if next(st, True):
                steps.remove(st)


def _mlstm_steps(proj_ref, gates_ref, convw_ref, convb_ref, bif_ref, mlg_ref, hm_ref, prev_ref, c_st, m_st):
    L = CHUNK
    rows = lax.broadcasted_iota(jnp.int32, (L, L), 0)
    cols = lax.broadcasted_iota(jnp.int32, (L, L), 1)
    causal = rows >= cols
    tril = jnp.where(causal, 1.0, 0.0).astype(BF16)

    cur = proj_ref[:, C_MLQ:C_MLQ + 2 * QK_W]
    xx = jnp.concatenate([prev_ref[...], cur], axis=0)
    r2 = lax.broadcasted_iota(jnp.int32, (L, 2 * L), 0)
    c2 = lax.broadcasted_iota(jnp.int32, (L, 2 * L), 1)
    acc = convb_ref[...] + cur.astype(F32) * convw_ref[CONV_K - 1:CONV_K, :]
    for d in range(1, CONV_K):
        shift = jnp.where(c2 == r2 + (L - d), 1.0, 0.0).astype(BF16)
        acc = acc + _dot(shift, xx) * convw_ref[CONV_K - 1 - d:CONV_K - d, :]
        yield False
    prev_ref[...] = cur
    qk = _silu(acc)
    yield False

    g = gates_ref[...] + bif_ref[...]
    gi = g[:, :LANES]
    gf = g[:, LANES:]
    lf = jnp.minimum(gf, 0.0) - jnp.log(1.0 + jnp.exp(-jnp.abs(gf)))
    lf_hi, lf_lo = _split(lf)
    a_all = _dot(tril, lf_hi) + _dot(tril, lf_lo)
    yield False
    a_last = a_all[L - 1:L, :]
    bm = gi - a_all
    bm_t = bm.T
    w_state = a_last + bm
    m_loc = jnp.max(w_state, axis=0, keepdims=True)
    m_prev = m_st[...]
    inter_log = a_all + m_prev
    m_new = jnp.maximum(a_last + m_prev, m_loc)
    s_prev = jnp.exp(a_last + m_prev - m_new)
    s_loc = jnp.exp(m_loc - m_new)
    ws_all = jnp.exp(w_state - m_loc) * s_loc
    m_st[...] = m_new
    yield False

    ones_blk = jnp.ones((L, LANES), BF16)
    q_scale = D_QK ** -0.5

    for h in range(N_HEADS):
        q = (qk[:, h * D_QK:(h + 1) * D_QK] * q_scale).astype(BF16)
        k_f = qk[:, QK_W + h * D_QK:QK_W + (h + 1) * D_QK]
        k = k_f.astype(BF16)
        v_ext = jnp.concatenate([proj_ref[:, C_MLV + h * D_V:C_MLV + (h + 1) * D_V], ones_blk], axis=1)
        dlog = jnp.where(causal, a_all[:, h:h + 1] + bm_t[h:h + 1, :], -jnp.inf)
        m_intra = jnp.max(dlog, axis=-1, keepdims=True)
        s = _dot_nt(q, k)
        qc = _dot(q, c_st[h].astype(BF16))
        yield False
        il = inter_log[:, h:h + 1]
        m_t = jnp.maximum(il, m_intra)
        p = jnp.exp(dlog - m_t) * s
        isc = jnp.exp(il - m_t)
        tot = _dot(p.astype(BF16), v_ext) + isc * qc
        yield False
        den = tot[:, D_V:D_V + 1]
        hout = tot[:, :D_V] / jnp.maximum(jnp.abs(den), jnp.exp(-m_t))
        kw_t = (k_f * ws_all[:, h:h + 1]).T.astype(BF16)
        c_st[h] = s_prev[:, h:h + 1] * c_st[h] + _dot(kw_t, v_ext)
        yield False
        y = _head_norm(hout, mlg_ref[:, h * D_V:(h + 1) * D_V])
        o = proj_ref[:, C_MLO + h * D_V:C_MLO + (h + 1) * D_V].astype(F32)
        hm_ref[:, h * D_V:(h + 1) * D_V] = _sigmoid_mul(o, y).astype(BF16)
        yield False


def _retention_steps(proj_ref, cos_ref, sin_ref, retg_ref, decay_ref, kdec_ref, qdec_ref, gch_ref, hr_ref, r_st):
    cos2 = cos_ref[...]
    sin2 = sin_ref[...]
    k_scale = D_QK ** -0.5
    for h in range(N_HEADS):
        q_raw = proj_ref[:, C_RQ + h * D_QK:C_RQ + (h + 1) * D_QK].astype(F32)
        k_raw = proj_ref[:, C_RK + h * D_QK:C_RK + (h + 1) * D_QK].astype(F32)
        q = (q_raw * cos2 + pltpu.roll(q_raw, D_QK // 2, 1) * sin2).astype(BF16)
        k_f = (k_raw * cos2 + pltpu.roll(k_raw, D_QK // 2, 1) * sin2) * k_scale
        v = proj_ref[:, C_RV + h * D_V:C_RV + (h + 1) * D_V]
        yield False
        sc = _dot_nt(q, k_f.astype(BF16)) * decay_ref[h]
        hret = _dot(sc.astype(BF16), v) + _dot(q, r_st[h].astype(BF16)) * qdec_ref[h]
        yield False
        kd_t = (k_f * kdec_ref[h]).T.astype(BF16)
        r_st[h] = gch_ref[h][:, 0:1] * r_st[h] + _dot(kd_t, v)
        yield False
        y = _head_norm(hret, retg_ref[:, h * D_V:(h + 1) * D_V])
        gt = proj_ref[:, C_RG + h * D_V:C_RG + (h + 1) * D_V].astype(F32)
        hr_ref[:, h * D_V:(h + 1) * D_V] = (_silu(gt) * y).astype(BF16)
        yield False


def _mix(proj, gates, cos2, sin2, conv_w, conv_b, b_if2, ml_g, ret_g, decay, kdec, qdec, gch, batch, seq):
    t = proj.shape[0]
    nc = seq // CHUNK
    L = CHUNK
    G = MIX_GROUP
    proj = proj.reshape(batch, seq, PROJ_W)
    gates = gates.reshape(batch, seq, GATE_W)
    cos2 = cos2.reshape(batch, seq, LANES)
    sin2 = sin2.reshape(batch, seq, LANES)
    full = lambda shape: pl.BlockSpec(shape, lambda b, c: (0,) * len(shape))
    hm, hr = pl.pallas_call(
        _mix_kernel,
        grid=(batch // G, nc),
        in_specs=[
            pl.BlockSpec((G, L, MIX_W), lambda b, c: (b, c, 0)),
            pl.BlockSpec((G, L, GATE_W), lambda b, c: (b, c, 0)),
            pl.BlockSpec((G, L, LANES), lambda b, c: (b, c, 0)),
            pl.BlockSpec((G, L, LANES), lambda b, c: (b, c, 0)),
            full((CONV_K, 2 * QK_W)),
            full((1, 2 * QK_W)),
            full((1, GATE_W)),
            full((1, V_W)),
            full((1, V_W)),
            full((N_HEADS, L, L)),
            full((N_HEADS, L, 1)),
            full((N_HEADS, L, 1)),
            full((N_HEADS, 1, LANES)),
        ],
        out_specs=[
            pl.BlockSpec((G, L, V_W), lambda b, c: (b, c, 0)),
            pl.BlockSpec((G, L, V_W), lambda b, c: (b, c, 0)),
        ],
        out_shape=[jax.ShapeDtypeStruct((batch, seq, V_W), BF16), jax.ShapeDtypeStruct((batch, seq, V_W), BF16)],
        scratch_shapes=[
            pltpu.VMEM((G, L, 2 * QK_W), BF16),
            pltpu.VMEM((G * N_HEADS, D_QK, V_EXT), F32),
            pltpu.VMEM((G, 1, LANES), F32),
            pltpu.VMEM((G * N_HEADS, D_QK, D_V), F32),
        ],
        compiler_params=pltpu.CompilerParams(
            dimension_semantics=("arbitrary", "arbitrary"), vmem_limit_bytes=VMEM_LIMIT),
        name="mix",
    )(proj, gates, cos2, sin2, conv_w, conv_b, b_if2, ml_g, ret_g, decay, kdec, qdec, gch)
    return hm.reshape(t, V_W), hr.reshape(t, V_W)


def _to_slabs(ref, val):
    rows = val.shape[0]
    for s in range(SLAB):
        ref[pl.ds(s, rows, stride=SLAB), :] = val[:, s * LANES:(s + 1) * LANES]


def _from_slabs(ref, first_slab, rows):
    return jnp.concatenate(
        [ref[pl.ds(first_slab * SLAB + s, rows, stride=SLAB), :] for s in range(SLAB)], axis=1)


def _post_kernel(hm_ref, hr_ref, gm_ref, gr_ref, x_ref, mod_ref, wbm_ref, wbr_ref, wout_ref, g_ref,
                 wr_hi_ref, wr_lo_ref, br_ref, x1_ref, hf_ref, logit_ref):
    ym = _dot(hm_ref[...], wbm_ref[...])
    yr = _dot(hr_ref[...], wbr_ref[...])
    y = _sigmoid_mul(gm_ref[...].astype(F32), ym) + _sigmoid_mul(gr_ref[...].astype(F32), yr)
    o = _dot(y.astype(BF16), wout_ref[...])
    x1 = x_ref[...] + mod_ref[0, 2:3, :] * o
    x1_ref[...] = x1
    hf = _rms_mod(x1, g_ref[...], mod_ref[0, 4:5, :], mod_ref[0, 3:4, :])
    _to_slabs(hf_ref, hf)
    logit_ref[...] = _dot3(hf, wr_hi_ref[...], wr_lo_ref[...]) + br_ref[...]


def _post(hm, hr, proj, x2, mod6, wbm, wbr, wout, g, wr_hi, wr_lo, br, seq):
    t = x2.shape[0]
    tm = 512
    per_b = seq // tm
    d = D_MODEL
    const = lambda shape: pl.BlockSpec(shape, lambda i: (0,) * len(shape))
    return pl.pallas_call(
        _post_kernel,
        grid=(t // tm,),
        in_specs=[
            pl.BlockSpec((tm, d), lambda i: (i, 0)),
            pl.BlockSpec((tm, d), lambda i: (i, 0)),
            pl.BlockSpec((tm, d), lambda i: (i, C_GML // d)),
            pl.BlockSpec((tm, d), lambda i: (i, C_GRET // d)),
            pl.BlockSpec((tm, d), lambda i: (i, 0)),
            pl.BlockSpec((1, 6, d), lambda i: (i // per_b, 0, 0)),
            const((d, d)), const((d, d)), const((d, d)),
            const((1, d)),
            const((d, LANES)), const((d, LANES)), const((1, LANES)),
        ],
        out_specs=[
            pl.BlockSpec((tm, d), lambda i: (i, 0)),
            pl.BlockSpec((tm * SLAB, LANES), lambda i: (i, 0)),
            pl.BlockSpec((tm, LANES), lambda i: (i, 0)),
        ],
        out_shape=[
            jax.ShapeDtypeStruct((t, d), F32),
            jax.ShapeDtypeStruct((t * SLAB, LANES), F32),
            jax.ShapeDtypeStruct((t, LANES), F32),
        ],
        compiler_params=pltpu.CompilerParams(
            dimension_semantics=("arbitrary",), vmem_limit_bytes=VMEM_LIMIT),
        name="post",
    )(hm, hr, proj, proj, x2, mod6, wbm, wbr, wout, g, wr_hi, wr_lo, br)


def _route_kernel(logit_ref, dest_ref, w_ref, bend_ref, cnt_st, pad_st):
    ph = pl.program_id(0)
    i = pl.program_id(1)
    tm = logit_ref.shape[0]
    lane = lax.broadcasted_iota(jnp.int32, (tm, LANES), 1)
    l = logit_ref[...]
    onehots, vals = [], []
    for _ in range(TOP_K):
        m = jnp.max(l, axis=-1, keepdims=True)
        idx = jnp.min(jnp.where(l == m, lane, LANES), axis=-1, keepdims=True)
        oh = lane == idx
        onehots.append(oh)
        vals.append(m)
        l = jnp.where(oh, -jnp.inf, l)
    sel = jnp.where(onehots[0] | onehots[1] | onehots[2] | onehots[3], 1.0, 0.0)
    tile_cnt = jnp.sum(sel, axis=0, keepdims=True)

    @pl.when(jnp.logical_and(ph == 0, i == 0))
    def _():
        cnt_st[...] = jnp.zeros_like(cnt_st)

    @pl.when(ph == 0)
    def _():
        cnt_st[...] = cnt_st[...] + tile_cnt

    @pl.when(jnp.logical_and(ph == 1, i == 0))
    def _():
        blocks = jnp.floor((cnt_st[...] + (EXPERT_BLOCK - 1)) * (1.0 / EXPERT_BLOCK))
        r = lax.broadcasted_iota(jnp.int32, (LANES, LANES), 0)
        c = lax.broadcasted_iota(jnp.int32, (LANES, LANES), 1)
        upper = jnp.where(r < c, 1.0, 0.0).astype(BF16)
        blocks8 = jnp.broadcast_to(blocks, (8, LANES))
        excl = _dot(blocks8.astype(BF16), upper)
        pad_st[...] = excl[0:1, :] * EXPERT_BLOCK
        row = lax.broadcasted_iota(jnp.int32, (8, LANES), 0)
        bend_ref[...] = jnp.where(row == 0, excl + blocks8,
                                  jnp.where(row == 1, excl * EXPERT_BLOCK + cnt_st[...], 0.0))
        cnt_st[...] = jnp.zeros_like(cnt_st)

    @pl.when(ph == 1)
    def _():
        carry = cnt_st[...]
        r = lax.broadcasted_iota(jnp.int32, (tm, tm), 0)
        c = lax.broadcasted_iota(jnp.int32, (tm, tm), 1)
        lower = jnp.where(r > c, 1.0, 0.0).astype(BF16)
        base = pad_st[...] + carry + _dot(lower, sel.astype(BF16))
        dest = jnp.zeros((tm, LANES), F32)
        wts = jnp.zeros((tm, LANES), F32)
        ex = [jnp.exp(v - vals[0]) for v in vals]
        den = ex[0] + ex[1] + ex[2] + ex[3]
        for k in range(TOP_K):
            dk = jnp.sum(jnp.where(onehots[k], base, 0.0), axis=-1, keepdims=True)
            dest = jnp.where(lane == k, dk, dest)
            wts = jnp.where(lane == k, ex[k] / den, wts)
        dest_ref[...] = dest.astype(jnp.int32)
        w_ref[...] = wts
        cnt_st[...] = carry + tile_cnt


def _route(logits):
    t = logits.shape[0]
    tm = 512
    return pl.pallas_call(
        _route_kernel,
        grid=(2, t // tm),
        in_specs=[pl.BlockSpec((tm, LANES), lambda ph, i: (i, 0))],
        out_specs=[
            pl.BlockSpec((tm, LANES), lambda ph, i: (i * ph, 0)),
            pl.BlockSpec((tm, LANES), lambda ph, i: (i * ph, 0)),
            pl.BlockSpec((8, LANES), lambda ph, i: (0, 0)),
        ],
        out_shape=[
            jax.ShapeDtypeStruct((t, LANES), jnp.int32),
            jax.ShapeDtypeStruct((t, LANES), F32),
            jax.ShapeDtypeStruct((8, LANES), F32),
        ],
        scratch_shapes=[pltpu.VMEM((1, LANES), F32), pltpu.VMEM((1, LANES), F32)],
        compiler_params=pltpu.CompilerParams(
            dimension_semantics=("arbitrary", "arbitrary"), vmem_limit_bytes=VMEM_LIMIT),
        name="route",
    )(logits)


DMA_UNROLL = 8


def _row_copy(src_ref, src_slab, dst_ref, dst_slab, sem):
    src = pl.multiple_of(src_slab * SLAB, SLAB)
    dst = pl.multiple_of(dst_slab * SLAB, SLAB)
    return pltpu.make_async_copy(src_ref.at[pl.ds(src, SLAB), :], dst_ref.at[pl.ds(dst, SLAB), :], sem)


def _dispatch_kernel(dest_ref, padfirst_ref, hf_ref, xs_out, zbuf, sem, zsem):
    n_assign = dest_ref.shape[0]
    blk_rows = zbuf.shape[0]

    @pl.when(pl.program_id(0) == 0)
    def _():
        zbuf[...] = jnp.zeros_like(zbuf)
        for e in range(N_EXPERTS):
            start = pl.multiple_of(padfirst_ref[e] * SLAB, SLAB)
            pltpu.make_async_copy(zbuf, xs_out.at[pl.ds(start, blk_rows), :], zsem).start()
        for e in range(N_EXPERTS):
            pltpu.make_async_copy(zbuf, xs_out.at[pl.ds(0, blk_rows), :], zsem).wait()

        n_used = padfirst_ref[N_EXPERTS]
        n_total = xs_out.shape[0] // blk_rows

        def zero_start(b, carry):
            start = pl.multiple_of(b * blk_rows, blk_rows)
            pltpu.make_async_copy(zbuf, xs_out.at[pl.ds(start, blk_rows), :], zsem).start()
            return carry

        def zero_wait(b, carry):
            pltpu.make_async_copy(zbuf, xs_out.at[pl.ds(0, blk_rows), :], zsem).wait()
            return carry

        lax.fori_loop(n_used, n_total, zero_start, 0)
        lax.fori_loop(n_used, n_total, zero_wait, 0)

    def body(g, carry):
        for u in range(DMA_UNROLL):
            t_local = g * (DMA_UNROLL // TOP_K) + u // TOP_K
            _row_copy(hf_ref, t_local, xs_out, dest_ref[g * DMA_UNROLL + u], sem).start(priority=u % 2)
        return carry

    lax.fori_loop(0, n_assign // DMA_UNROLL, body, 0)
    for _ in range(TOP_K):
        pltpu.make_async_copy(hf_ref, xs_out.at[pl.ds(0, hf_ref.shape[0]), :], sem).wait()


def _dispatch(dest_flat, pad_first, hf_slab, n_blocks):
    n_assign = dest_flat.shape[0]
    tm = 512
    blk_rows = EXPERT_BLOCK * SLAB
    return pl.pallas_call(
        _dispatch_kernel,
        grid=(n_assign // (tm * TOP_K),),
        in_specs=[
            pl.BlockSpec((tm * TOP_K,), lambda i: (i,), memory_space=pltpu.SMEM),
            pl.BlockSpec(memory_space=pltpu.SMEM),
            pl.BlockSpec((tm * SLAB, LANES), lambda i: (i, 0)),
        ],
        out_specs=pl.BlockSpec(memory_space=pl.ANY),
        out_shape=jax.ShapeDtypeStruct(((n_blocks + 1) * blk_rows, LANES), F32),
        scratch_shapes=[pltpu.VMEM((blk_rows, LANES), F32), pltpu.SemaphoreType.DMA, pltpu.SemaphoreType.DMA],
        compiler_params=pltpu.CompilerParams(dimension_semantics=("arbitrary",), vmem_limit_bytes=VMEM_LIMIT),
        name="dispatch",
    )(dest_flat, pad_first, hf_slab)


def _weight_copies(wgu_hbm, wd_hbm, wgu_f32, wd_f32, wsem, expert, slot):
    return (pltpu.make_async_copy(wgu_hbm.at[expert], wgu_f32.at[slot], wsem.at[0, slot]),
            pltpu.make_async_copy(wd_hbm.at[expert], wd_f32.at[slot], wsem.at[1, slot]))


def _expert_kernel(be_ref, nused_ref, first_ref, slot_ref, next_ref,
                   xs_ref, wgu_hbm, bgu_ref, wd_hbm, bd_ref, ys_ref,
                   wgu_f32, wd_f32, wgu_bf, wd_bf, wsem):
    b = pl.program_id(0)
    copies = functools.partial(_weight_copies, wgu_hbm, wd_hbm, wgu_f32, wd_f32, wsem)

    @pl.when(jnp.logical_and(b < nused_ref[0], first_ref[b] == 1))
    def _():
        slot = slot_ref[b]

        @pl.when(b == 0)
        def _():
            for cp in copies(be_ref[0], 0):
                cp.start()

        for cp in copies(be_ref[b], slot):
            cp.wait()

        @pl.when(next_ref[b] >= 0)
        def _():
            for cp in copies(next_ref[b], 1 - slot):
                cp.start()

        two = 2 * LANES
        r = lax.broadcasted_iota(jnp.int32, (two, two), 0)
        c = lax.broadcasted_iota(jnp.int32, (two, two), 1)
        src = jnp.where(c < LANES, 2 * c, 2 * (c - LANES) + 1)
        perm = jnp.where(r == src, 1.0, 0.0).astype(BF16)
        for blk in range(wgu_bf.shape[1] // two):
            wb = wgu_f32[slot, :, blk * two:(blk + 1) * two].astype(BF16)
            wgu_bf[:, blk * two:(blk + 1) * two] = _dot(wb, perm).astype(BF16)
        wd_bf[...] = wd_f32[slot].astype(BF16)

    @pl.when(b < nused_ref[0])
    def _():
        two = 2 * LANES
        x = _from_slabs(xs_ref, 0, EXPERT_BLOCK).astype(BF16)
        gu = _dot(x, wgu_bf[...]) + bgu_ref[0]
        n_blk = gu.shape[1] // two
        x_glu = jnp.concatenate([gu[:, i * two:i * two + LANES] for i in range(n_blk)], axis=1)
        x_lin = jnp.concatenate([gu[:, i * two + LANES:(i + 1) * two] for i in range(n_blk)], axis=1)
        x_glu = jnp.minimum(x_glu, SWIGLU_LIMIT)
        x_lin = jnp.clip(x_lin, -SWIGLU_LIMIT, SWIGLU_LIMIT)
        act = _sigmoid_mul(SWIGLU_ALPHA * x_glu, x_glu * (x_lin + 1.0))
        y = _dot(act.astype(BF16), wd_bf[...]) + bd_ref[0]
        _to_slabs(ys_ref, y)

    @pl.when(b >= nused_ref[0])
    def _():
        ys_ref[...] = jnp.zeros_like(ys_ref)


def _experts(block_expert, n_used, xs, w_gate_up, b_gu_perm, w_down, b_down):
    n_blocks = block_expert.shape[0]
    d = D_MODEL
    f2 = w_gate_up.shape[-1]
    blk_rows = EXPERT_BLOCK * SLAB

    idx = jnp.arange(n_blocks, dtype=jnp.int32)
    used = idx < n_used
    prev = jnp.concatenate([jnp.full((1,), -1, jnp.int32), block_expert[:-1]])
    first = (used & (block_expert != prev)).astype(jnp.int32)
    slot = (jnp.cumsum(first) - 1) % 2
    later = used[None, :] & (block_expert[None, :] > block_expert[:, None])
    nxt = jnp.min(jnp.where(later, block_expert[None, :], N_EXPERTS), axis=1)
    nxt = jnp.where(nxt == N_EXPERTS, -1, nxt)

    grid_spec = pltpu.PrefetchScalarGridSpec(
        num_scalar_prefetch=5,
        grid=(n_blocks,),
        in_specs=[
            pl.BlockSpec((blk_rows, LANES), lambda b, be, *_: (b, 0)),
            pl.BlockSpec(memory_space=pl.ANY),
            pl.BlockSpec((1, 1, f2), lambda b, be, *_: (be[b], 0, 0)),
            pl.BlockSpec(memory_space=pl.ANY),
            pl.BlockSpec((1, 1, d), lambda b, be, *_: (be[b], 0, 0)),
        ],
        out_specs=pl.BlockSpec((blk_rows, LANES), lambda b, be, *_: (b, 0)),
        scratch_shapes=[
            pltpu.VMEM((2, d, f2), F32),
            pltpu.VMEM((2, f2 // 2, d), F32),
            pltpu.VMEM((d, f2), BF16),
            pltpu.VMEM((f2 // 2, d), BF16),
            pltpu.SemaphoreType.DMA((2, 2)),
        ],
    )
    return pl.pallas_call(
        _expert_kernel,
        grid_spec=grid_spec,
        out_shape=jax.ShapeDtypeStruct((n_blocks * blk_rows, LANES), F32),
        compiler_params=pltpu.CompilerParams(
            dimension_semantics=("arbitrary",), vmem_limit_bytes=VMEM_LIMIT),
        name="expert",
    )(block_expert, n_used, first, slot.astype(jnp.int32), nxt.astype(jnp.int32),
      xs, w_gate_up, b_gu_perm, w_down, b_down)


def _combine_kernel(dest_ref, dest_next_ref, ys_hbm, w_ref, x1_ref, mod_ref, g_ref, o_ref, buf, sem):
    i = pl.program_id(0)
    n_steps = pl.num_programs(0)
    tm = x1_ref.shape[0]
    n_assign = tm * TOP_K
    slot = i % 2

    def gather(d_ref, to_slot):
        def body(g, carry):
            for u in range(DMA_UNROLL):
                t_local = g * (DMA_UNROLL // TOP_K) + u // TOP_K
                k = u % TOP_K
                _row_copy(ys_hbm, d_ref[g * DMA_UNROLL + u], buf.at[to_slot], k * tm + t_local,
                          sem.at[to_slot]).start(priority=u % 2)
            return carry

        lax.fori_loop(0, n_assign // DMA_UNROLL, body, 0)

    @pl.when(i == 0)
    def _():
        gather(dest_ref, 0)

    @pl.when(i + 1 < n_steps)
    def _():
        gather(dest_next_ref, 1 - slot)

    cur = buf.at[slot]
    pltpu.make_async_copy(ys_hbm.at[pl.ds(0, n_assign * SLAB), :], cur, sem.at[slot]).wait()

    w = w_ref[...]
    y = w[:, 0:1] * _from_slabs(cur, 0, tm)
    for k in range(1, TOP_K):
        y = y + w[:, k:k + 1] * _from_slabs(cur, k * tm, tm)
    x = x1_ref[...] + mod_ref[0, 5:6, :] * y
    ms = jnp.mean(x * x, axis=-1, keepdims=True)
    o_ref[...] = x * lax.rsqrt(ms + EPS) * g_ref[...]


def _combine(dest_flat, ys, wts, x1, mod6, g, seq):
    t = x1.shape[0]
    tm = 256
    per_b = seq // tm
    d = D_MODEL
    last = t // tm - 1
    return pl.pallas_call(
        _combine_kernel,
        grid=(t // tm,),
        in_specs=[
            pl.BlockSpec((tm * TOP_K,), lambda i: (i,), memory_space=pltpu.SMEM),
            pl.BlockSpec((tm * TOP_K,), lambda i: (jnp.minimum(i + 1, last),), memory_space=pltpu.SMEM),
            pl.BlockSpec(memory_space=pl.ANY),
            pl.BlockSpec((tm, LANES), lambda i: (i, 0)),
            pl.BlockSpec((tm, d), lambda i: (i, 0)),
            pl.BlockSpec((1, 6, d), lambda i: (i // per_b, 0, 0)),
            pl.BlockSpec((1, d), lambda i: (0, 0)),
        ],
        out_specs=pl.BlockSpec((tm, d), lambda i: (i, 0)),
        out_shape=jax.ShapeDtypeStruct((t, d), F32),
        scratch_shapes=[pltpu.VMEM((2, tm * TOP_K * SLAB, LANES), F32), pltpu.SemaphoreType.DMA((2,))],
        compiler_params=pltpu.CompilerParams(
            dimension_semantics=("arbitrary",), vmem_limit_bytes=VMEM_LIMIT),
        name="combine",
    )(dest_flat, dest_flat, ys, wts, x1, mod6, g)


def _split_w(w):
    hi = w.astype(BF16)
    lo = (w - hi.astype(F32)).astype(BF16)
    return hi, lo


def kernel(x, c, positions, w_ada, b_ada, norm_mix_g, w_in, conv_w, conv_b, b_if, ml_norm_g, ret_norm_g,
           w_branch_ml, w_branch_ret, w_out, norm_ffn_g, w_router, b_router, w_gate_up, b_gate_up, w_down,
           b_down, norm_final_g):
    batch, seq, d = x.shape
    t = batch * seq
    x2 = x.reshape(t, d)
    depth = w_ada.shape[0]
    c8 = jnp.concatenate([c, jnp.zeros((8 - batch, d), c.dtype)], axis=0)

    half = D_QK // 2
    inv = ROPE_BASE ** (-jnp.arange(half, dtype=F32) / half)
    ang = positions.astype(F32)[..., None] * inv
    cos = jnp.cos(ang).reshape(t, half)
    sin = jnp.sin(ang).reshape(t, half)
    cos2 = jnp.concatenate([cos, cos], axis=-1)
    sin2 = jnp.concatenate([-sin, sin], axis=-1)
    L = CHUNK
    log_gamma = jnp.log(1.0 - 2.0 ** (-5.0 - jnp.arange(N_HEADS, dtype=F32)))
    pos = jnp.arange(L, dtype=F32)
    rel = pos[:, None] - pos[None, :]
    decay = jnp.where(rel >= 0, jnp.exp(log_gamma[:, None, None] * jnp.maximum(rel, 0.0)), 0.0)
    kdec = jnp.exp(log_gamma[:, None] * (L - 1 - pos))[:, :, None]
    qdec = jnp.exp(log_gamma[:, None] * (pos + 1.0))[:, :, None]
    gch = jnp.broadcast_to(jnp.exp(log_gamma * L)[:, None, None], (N_HEADS, 1, LANES))

    assert depth == 1, "the final norm is fused after the single layer"
    l = 0
    wa_hi, wa_lo = _split_w(w_ada[l])
    mod = _ada(c8, wa_hi, wa_lo, b_ada[l][None, :])
    mod6 = mod[:batch].reshape(batch, 6, d)

    w = w_in[l]
    w_main = jnp.concatenate(
        [w[:, 0:3072], w[:, 3080:8200]], axis=1).astype(BF16)
    w_if = w[:, 3072:3080]
    zpad = jnp.zeros((d, LANES - N_HEADS), F32)
    wif = jnp.concatenate([w_if[:, :N_HEADS], zpad, w_if[:, N_HEADS:], zpad], axis=1)
    wif_hi, wif_lo = _split_w(wif)
    zb = jnp.zeros((LANES - N_HEADS,), F32)
    b_if2 = jnp.concatenate([b_if[l][:N_HEADS], zb, b_if[l][N_HEADS:], zb])[None, :]

    proj, gates = _inproj(x2, mod6, norm_mix_g[l][None, :], w_main, wif_hi, wif_lo, seq)
    hm, hr = _mix(proj, gates, cos2, sin2, conv_w[l], conv_b[l][None, :], b_if2,
                  ml_norm_g[l][None, :], ret_norm_g[l][None, :], decay, kdec, qdec, gch, batch, seq)

    wr = jnp.concatenate([w_router[l], jnp.zeros((d, LANES - N_EXPERTS), F32)], axis=1)
    wr_hi, wr_lo = _split_w(wr)
    br = jnp.concatenate([b_router[l], jnp.full((LANES - N_EXPERTS,), -1e30, F32)])[None, :]
    x1, hf, logits = _post(hm, hr, proj, x2, mod6, w_branch_ml[l].astype(BF16), w_branch_ret[l].astype(BF16),
                           w_out[l].astype(BF16), norm_ffn_g[l][None, :], wr_hi, wr_lo, br, seq)

    n_assign = t * TOP_K
    n_blocks = -(-n_assign // EXPERT_BLOCK) + N_EXPERTS
    dest, wts, tables = _route(logits)
    dest_flat = dest[:, :TOP_K].reshape(n_assign)
    bend = tables[0, :N_EXPERTS].astype(jnp.int32)
    pad_first = tables[1, :N_EXPERTS].astype(jnp.int32)
    n_used = bend[N_EXPERTS - 1:]
    blk = jnp.minimum(jnp.arange(n_blocks, dtype=jnp.int32), n_used - 1)
    block_expert = jnp.minimum(
        jnp.sum((bend[None, :] <= blk[:, None]).astype(jnp.int32), axis=1), N_EXPERTS - 1)
    xs = _dispatch(dest_flat, jnp.concatenate([pad_first, n_used]), hf, n_blocks)
    f2 = w_gate_up.shape[-1]
    bgu = b_gate_up[l].reshape(N_EXPERTS, f2 // (2 * LANES), LANES, 2)
    bgu = jnp.swapaxes(bgu, -1, -2).reshape(N_EXPERTS, 1, f2)
    ys = _experts(block_expert, n_used, xs, w_gate_up[l], bgu, w_down[l], b_down[l][:, None, :])
    out = _combine(dest_flat, ys, wts, x1, mod6, norm_final_g[None, :], seq)
    return out.reshape(batch, seq, d)
```

```python
import functools

import numpy as np
import jax
import jax.numpy as jnp
from jax import lax
from jax.experimental import pallas as pl
from jax.experimental.pallas import tpu as pltpu

D_MODEL = 1024
N_HEADS = 4
D_QK = 128
D_V = 256
CONV_K = 4
ROPE_BASE = 10000.0
CHUNK = 128
N_EXPERTS = 32
TOP_K = 4
SWIGLU_LIMIT = 7.0
SWIGLU_ALPHA = 1.702
EXPERT_BLOCK = 512
EPS = 1e-5

QK_W = N_HEADS * D_QK
V_W = N_HEADS * D_V
LANES = 128
GATE_W = 2 * LANES
V_EXT = D_V + LANES
MIX_GROUP = 4
SLAB = D_MODEL // LANES

C_MLQ, C_MLK, C_MLV, C_MLO = 0, 512, 1024, 2048
C_RQ, C_RK, C_RV, C_RG = 3072, 3584, 4096, 5120
C_GML, C_GRET = 6144, 7168
MIX_W = 6144
PROJ_W = 8192

BF16 = jnp.bfloat16
F32 = jnp.float32
VMEM_LIMIT = 56 * 1024 * 1024


def _dot(a, b):
    return jnp.dot(a, b, preferred_element_type=F32)


def _dot_nt(a, b):
    return lax.dot_general(a, b, (((1,), (1,)), ((), ())), preferred_element_type=F32)


def _split(a):
    hi = a.astype(BF16)
    lo = (a - hi.astype(F32)).astype(BF16)
    return hi, lo


def _dot3(a, b_hi, b_lo):
    a_hi, a_lo = _split(a)
    return _dot(a_hi, b_hi) + (_dot(a_lo, b_hi) + _dot(a_hi, b_lo))


def _sigmoid(x):
    return 0.5 * jnp.tanh(0.5 * x) + 0.5


def _sigmoid_mul(x, y):
    hy = 0.5 * y
    return hy * jnp.tanh(0.5 * x) + hy


def _silu(x):
    hx = 0.5 * x
    return hx * jnp.tanh(hx) + hx


def _round_robin(stage_generators):
    pending = list(stage_generators)
    while pending:
        for gen in list(pending):
            if next(gen, True):
                pending.remove(gen)


def _rms_mod(x, g, scale, shift):
    ms = jnp.mean(x * x, axis=-1, keepdims=True)
    return (x * lax.rsqrt(ms + EPS) * g) * (1.0 + scale) + shift


def _ada_kernel(c_ref, whi_ref, wlo_ref, b_ref, o_ref):
    c = c_ref[...]
    ca = _silu(c)
    o_ref[...] = _dot3(ca, whi_ref[...], wlo_ref[...]) + b_ref[...]


def _ada(c8, w_hi, w_lo, b):
    n = w_hi.shape[1]
    tn = 1024
    return pl.pallas_call(
        _ada_kernel,
        grid=(n // tn,),
        in_specs=[
            pl.BlockSpec((8, D_MODEL), lambda j: (0, 0)),
            pl.BlockSpec((D_MODEL, tn), lambda j: (0, j)),
            pl.BlockSpec((D_MODEL, tn), lambda j: (0, j)),
            pl.BlockSpec((1, tn), lambda j: (0, j)),
        ],
        out_specs=pl.BlockSpec((8, tn), lambda j: (0, j)),
        out_shape=jax.ShapeDtypeStruct((8, n), F32),
        name="ada",
    )(c8, w_hi, w_lo, b)


def _inproj_kernel(x_ref, mod_ref, g_ref, w_ref, wif_hi_ref, wif_lo_ref, pos_ref, inv_ref, sign_ref,
                   proj_ref, gates_ref, cos_ref, sin_ref, hb_ref):
    j = pl.program_id(1)

    @pl.when(j == 0)
    def _():
        h = _rms_mod(x_ref[...], g_ref[...], mod_ref[0, 1:2, :], mod_ref[0, 0:1, :])
        hb_ref[...] = h.astype(BF16)
        gates_ref[...] = _dot3(h, wif_hi_ref[...], wif_lo_ref[...])

    proj_ref[...] = _dot(hb_ref[...], w_ref[...]).astype(BF16)

    n = x_ref.shape[0] // pl.num_programs(1)
    rows = pl.ds(pl.multiple_of(j * n, n), n)
    ang = pos_ref[rows, :] * inv_ref[...]
    cos_ref[rows, :] = jnp.cos(ang)
    sin_ref[rows, :] = jnp.sin(ang) * sign_ref[...]


def _inproj(x2, mod6, g, w_main, wif_hi, wif_lo, pos_b, inv2, sign2, seq):
    t = x2.shape[0]
    tm, tn = 512, 2048
    per_b = seq // tm
    return pl.pallas_call(
        _inproj_kernel,
        grid=(t // tm, PROJ_W // tn),
        in_specs=[
            pl.BlockSpec((tm, D_MODEL), lambda i, j: (i, 0)),
            pl.BlockSpec((1, 6, D_MODEL), lambda i, j: (i // per_b, 0, 0)),
            pl.BlockSpec((1, D_MODEL), lambda i, j: (0, 0)),
            pl.BlockSpec((D_MODEL, tn), lambda i, j: (0, j)),
            pl.BlockSpec((D_MODEL, GATE_W), lambda i, j: (0, 0)),
            pl.BlockSpec((D_MODEL, GATE_W), lambda i, j: (0, 0)),
            pl.BlockSpec((tm, LANES), lambda i, j: (i, 0)),
            pl.BlockSpec((1, LANES), lambda i, j: (0, 0)),
            pl.BlockSpec((1, LANES), lambda i, j: (0, 0)),
        ],
        out_specs=[
            pl.BlockSpec((tm, tn), lambda i, j: (i, j)),
            pl.BlockSpec((tm, GATE_W), lambda i, j: (i, 0)),
            pl.BlockSpec((tm, LANES), lambda i, j: (i, 0)),
            pl.BlockSpec((tm, LANES), lambda i, j: (i, 0)),
        ],
        out_shape=[
            jax.ShapeDtypeStruct((t, PROJ_W), BF16),
            jax.ShapeDtypeStruct((t, GATE_W), F32),
            jax.ShapeDtypeStruct((t, LANES), F32),
            jax.ShapeDtypeStruct((t, LANES), F32),
        ],
        scratch_shapes=[pltpu.VMEM((tm, D_MODEL), BF16)],
        compiler_params=pltpu.CompilerParams(
            dimension_semantics=("arbitrary", "arbitrary"), vmem_limit_bytes=VMEM_LIMIT),
        name="inproj",
    )(x2, mod6, g, w_main, wif_hi, wif_lo, pos_b, inv2, sign2)


def _row_mean(a):
    inv_n = jnp.full((a.shape[1], LANES), 1.0 / a.shape[1], BF16)
    m = _dot(a.astype(BF16), inv_n)
    return jnp.concatenate([m] * (a.shape[1] // LANES), axis=1)


def _head_norm(h, g):
    d = h - _row_mean(h)
    return d * lax.rsqrt(_row_mean(d * d) + EPS) * g


def _mix_kernel(proj_ref, gates_ref, cos_ref, sin_ref, convw_ref, convb_ref, bif_ref, mlg_ref, retg_ref,
                decay_ref, kdec_ref, qdec_ref, gch_ref,
                hm_ref, hr_ref,
                prev_ref, c_st, m_st, r_st):
    group = proj_ref.shape[0]

    @pl.when(pl.program_id(1) == 0)
    def _():
        def zero_state(i, carry):
            c_st[i] = jnp.zeros(c_st.shape[1:], F32)
            r_st[i] = jnp.zeros(r_st.shape[1:], F32)
            return carry

        lax.fori_loop(0, group * N_HEADS, zero_state, 0)
        prev_ref[...] = jnp.zeros_like(prev_ref)
        m_st[...] = jnp.zeros_like(m_st)

    steps = []
    for gb in range(group):
        states = slice(gb * N_HEADS, (gb + 1) * N_HEADS)
        steps.append(_mlstm_steps(proj_ref.at[gb], gates_ref.at[gb], convw_ref, convb_ref, bif_ref, mlg_ref,
                                  hm_ref.at[gb], prev_ref.at[gb], c_st.at[states], m_st.at[gb]))
        steps.append(_retention_steps(proj_ref.at[gb], cos_ref.at[gb], sin_ref.at[gb], retg_ref, decay_ref,
                                      kdec_ref, qdec_ref, gch_ref, hr_ref.at[gb], r_st.at[states]))
    _round_robin(steps)


def _mlstm_steps(proj_ref, gates_ref, convw_ref, convb_ref, bif_ref, mlg_ref, hm_ref, prev_ref, c_st, m_st):
    L = CHUNK
    rows = lax.broadcasted_iota(jnp.int32, (L, L), 0)
    cols = lax.broadcasted_iota(jnp.int32, (L, L), 1)
    causal = rows >= cols
    tril = jnp.where(causal, 1.0, 0.0).astype(BF16)

    cur = proj_ref[:, C_MLQ:C_MLQ + 2 * QK_W]
    xx = jnp.concatenate([prev_ref[...], cur], axis=0)
    r2 = lax.broadcasted_iota(jnp.int32, (L, 2 * L), 0)
    c2 = lax.broadcasted_iota(jnp.int32, (L, 2 * L), 1)
    acc = convb_ref[...] + cur.astype(F32) * convw_ref[CONV_K - 1:CONV_K, :]
    for d in range(1, CONV_K):
        shift = jnp.where(c2 == r2 + (L - d), 1.0, 0.0).astype(BF16)
        acc = acc + _dot(shift, xx) * convw_ref[CONV_K - 1 - d:CONV_K - d, :]
        yield False
    prev_ref[...] = cur
    qk = _silu(acc)
    yield False

    g = gates_ref[...] + bif_ref[...]
    gi = g[:, :LANES]
    gf = g[:, LANES:]
    lf = jnp.minimum(gf, 0.0) - jnp.log(1.0 + jnp.exp(-jnp.abs(gf)))
    lf_hi, lf_lo = _split(lf)
    a_all = _dot(tril, lf_hi) + _dot(tril, lf_lo)
    yield False
    a_last = a_all[L - 1:L, :]
    bm = gi - a_all
    bm_t = bm.T
    w_state = a_last + bm
    m_loc = jnp.max(w_state, axis=0, keepdims=True)
    m_prev = m_st[...]
    inter_log = a_all + m_prev
    m_new = jnp.maximum(a_last + m_prev, m_loc)
    s_prev = jnp.exp(a_last + m_prev - m_new)
    s_loc = jnp.exp(m_loc - m_new)
    ws_all = jnp.exp(w_state - m_loc) * s_loc
    m_st[...] = m_new
    yield False

    ones_blk = jnp.ones((L, LANES), BF16)
    q_scale = D_QK ** -0.5

    for h in range(N_HEADS):
        q = (qk[:, h * D_QK:(h + 1) * D_QK] * q_scale).astype(BF16)
        k_f = qk[:, QK_W + h * D_QK:QK_W + (h + 1) * D_QK]
        k = k_f.astype(BF16)
        v_ext = jnp.concatenate([proj_ref[:, C_MLV + h * D_V:C_MLV + (h + 1) * D_V], ones_blk], axis=1)
        dlog = jnp.where(causal, a_all[:, h:h + 1] + bm_t[h:h + 1, :], -jnp.inf)
        m_intra = jnp.max(dlog, axis=-1, keepdims=True)
        s = _dot_nt(q, k)
        qc = _dot(q, c_st[h].astype(BF16))
        yield False
        il = inter_log[:, h:h + 1]
        m_t = jnp.maximum(il, m_intra)
        p = jnp.exp(dlog - m_t) * s
        isc = jnp.exp(il - m_t)
        tot = _dot(p.astype(BF16), v_ext) + isc * qc
        yield False
        den = tot[:, D_V:D_V + 1]
        hout = tot[:, :D_V] / jnp.maximum(jnp.abs(den), jnp.exp(-m_t))
        kw_t = (k_f * ws_all[:, h:h + 1]).T.astype(BF16)
        c_st[h] = s_prev[:, h:h + 1] * c_st[h] + _dot(kw_t, v_ext)
        yield False
        y = _head_norm(hout, mlg_ref[:, h * D_V:(h + 1) * D_V])
        o = proj_ref[:, C_MLO + h * D_V:C_MLO + (h + 1) * D_V].astype(F32)
        hm_ref[:, h * D_V:(h + 1) * D_V] = _sigmoid_mul(o, y).astype(BF16)
        yield False


def _retention_steps(proj_ref, cos_ref, sin_ref, retg_ref, decay_ref, kdec_ref, qdec_ref, gch_ref, hr_ref, r_st):
    cos2 = cos_ref[...]
    sin2 = sin_ref[...]
    k_scale = D_QK ** -0.5
    for h in range(N_HEADS):
        q_raw = proj_ref[:, C_RQ + h * D_QK:C_RQ + (h + 1) * D_QK].astype(F32)
        k_raw = proj_ref[:, C_RK + h * D_QK:C_RK + (h + 1) * D_QK].astype(F32)
        q = (q_raw * cos2 + pltpu.roll(q_raw, D_QK // 2, 1) * sin2).astype(BF16)
        k_f = (k_raw * cos2 + pltpu.roll(k_raw, D_QK // 2, 1) * sin2) * k_scale
        v = proj_ref[:, C_RV + h * D_V:C_RV + (h + 1) * D_V]
        yield False
        sc = _dot_nt(q, k_f.astype(BF16)) * decay_ref[h]
        hret = _dot(sc.astype(BF16), v) + _dot(q, r_st[h].astype(BF16)) * qdec_ref[h]
        yield False
        kd_t = (k_f * kdec_ref[h]).T.astype(BF16)
        r_st[h] = gch_ref[h][:, 0:1] * r_st[h] + _dot(kd_t, v)
        yield False
        y = _head_norm(hret, retg_ref[:, h * D_V:(h + 1) * D_V])
        gt = proj_ref[:, C_RG + h * D_V:C_RG + (h + 1) * D_V].astype(F32)
        hr_ref[:, h * D_V:(h + 1) * D_V] = (_silu(gt) * y).astype(BF16)
        yield False


def _mix(proj, gates, cos2, sin2, conv_w, conv_b, b_if2, ml_g, ret_g, decay, kdec, qdec, gch, batch, seq):
    t = proj.shape[0]
    nc = seq // CHUNK
    L = CHUNK
    G = MIX_GROUP
    proj = proj.reshape(batch, seq, PROJ_W)
    gates = gates.reshape(batch, seq, GATE_W)
    cos2 = cos2.reshape(batch, seq, LANES)
    sin2 = sin2.reshape(batch, seq, LANES)
    full = lambda shape: pl.BlockSpec(shape, lambda b, c: (0,) * len(shape))
    hm, hr = pl.pallas_call(
        _mix_kernel,
        grid=(batch // G, nc),
        in_specs=[
            pl.BlockSpec((G, L, MIX_W), lambda b, c: (b, c, 0)),
            pl.BlockSpec((G, L, GATE_W), lambda b, c: (b, c, 0)),
            pl.BlockSpec((G, L, LANES), lambda b, c: (b, c, 0)),
            pl.BlockSpec((G, L, LANES), lambda b, c: (b, c, 0)),
            full((CONV_K, 2 * QK_W)),
            full((1, 2 * QK_W)),
            full((1, GATE_W)),
            full((1, V_W)),
            full((1, V_W)),
            full((N_HEADS, L, L)),
            full((N_HEADS, L, 1)),
            full((N_HEADS, L, 1)),
            full((N_HEADS, 1, LANES)),
        ],
        out_specs=[
            pl.BlockSpec((G, L, V_W), lambda b, c: (b, c, 0)),
            pl.BlockSpec((G, L, V_W), lambda b, c: (b, c, 0)),
        ],
        out_shape=[jax.ShapeDtypeStruct((batch, seq, V_W), BF16), jax.ShapeDtypeStruct((batch, seq, V_W), BF16)],
        scratch_shapes=[
            pltpu.VMEM((G, L, 2 * QK_W), BF16),
            pltpu.VMEM((G * N_HEADS, D_QK, V_EXT), F32),
            pltpu.VMEM((G, 1, LANES), F32),
            pltpu.VMEM((G * N_HEADS, D_QK, D_V), F32),
        ],
        compiler_params=pltpu.CompilerParams(
            dimension_semantics=("arbitrary", "arbitrary"), vmem_limit_bytes=VMEM_LIMIT),
        name="mix",
    )(proj, gates, cos2, sin2, conv_w, conv_b, b_if2, ml_g, ret_g, decay, kdec, qdec, gch)
    return hm.reshape(t, V_W), hr.reshape(t, V_W)


def _to_slabs(ref, val):
    rows = val.shape[0]
    for s in range(SLAB):
        ref[pl.ds(s, rows, stride=SLAB), :] = val[:, s * LANES:(s + 1) * LANES]


def _from_slabs(ref, first_slab, rows):
    return jnp.concatenate(
        [ref[pl.ds(first_slab * SLAB + s, rows, stride=SLAB), :] for s in range(SLAB)], axis=1)


def _post_kernel(hm_ref, hr_ref, gm_ref, gr_ref, x_ref, mod_ref, wbm_ref, wbr_ref, wout_ref, g_ref,
                 wr_hi_ref, wr_lo_ref, br_ref, x1_ref, hf_ref, logit_ref):
    n_sub = 2
    sub = x_ref.shape[0] // n_sub

    def steps(s):
        r = pl.ds(s * sub, sub)
        ym = _dot(hm_ref[r, :], wbm_ref[...])
        yield False
        yr = _dot(hr_ref[r, :], wbr_ref[...])
        yield False
        y = _sigmoid_mul(gm_ref[r, :].astype(F32), ym) + _sigmoid_mul(gr_ref[r, :].astype(F32), yr)
        o = _dot(y.astype(BF16), wout_ref[...])
        yield False
        x1 = x_ref[r, :] + mod_ref[0, 2:3, :] * o
        x1_ref[r, :] = x1
        hf = _rms_mod(x1, g_ref[...], mod_ref[0, 4:5, :], mod_ref[0, 3:4, :])
        yield False
        _to_slabs(hf_ref.at[pl.ds(s * sub * SLAB, sub * SLAB)], hf)
        logit_ref[r, :] = _dot3(hf, wr_hi_ref[...], wr_lo_ref[...]) + br_ref[...]
        yield False

    _round_robin([steps(s) for s in range(n_sub)])


def _post(hm, hr, proj, x2, mod6, wbm, wbr, wout, g, wr_hi, wr_lo, br, seq):
    t = x2.shape[0]
    tm = 512
    per_b = seq // tm
    d = D_MODEL
    const = lambda shape: pl.BlockSpec(shape, lambda i: (0,) * len(shape))
    return pl.pallas_call(
        _post_kernel,
        grid=(t // tm,),
        in_specs=[
            pl.BlockSpec((tm, d), lambda i: (i, 0)),
            pl.BlockSpec((tm, d), lambda i: (i, 0)),
            pl.BlockSpec((tm, d), lambda i: (i, C_GML // d)),
            pl.BlockSpec((tm, d), lambda i: (i, C_GRET // d)),
            pl.BlockSpec((tm, d), lambda i: (i, 0)),
            pl.BlockSpec((1, 6, d), lambda i: (i // per_b, 0, 0)),
            const((d, d)), const((d, d)), const((d, d)),
            const((1, d)),
            const((d, LANES)), const((d, LANES)), const((1, LANES)),
        ],
        out_specs=[
            pl.BlockSpec((tm, d), lambda i: (i, 0)),
            pl.BlockSpec((tm * SLAB, LANES), lambda i: (i, 0)),
            pl.BlockSpec((tm, LANES), lambda i: (i, 0)),
        ],
        out_shape=[
            jax.ShapeDtypeStruct((t, d), F32),
            jax.ShapeDtypeStruct((t * SLAB, LANES), F32),
            jax.ShapeDtypeStruct((t, LANES), F32),
        ],
        compiler_params=pltpu.CompilerParams(
            dimension_semantics=("arbitrary",), vmem_limit_bytes=VMEM_LIMIT),
        name="post",
    )(hm, hr, proj, proj, x2, mod6, wbm, wbr, wout, g, wr_hi, wr_lo, br)


def _route_kernel(logit_ref, dest_ref, w_ref, bend_ref, cnt_st, pad_st):
    ph = pl.program_id(0)
    i = pl.program_id(1)
    tm = logit_ref.shape[0]
    lane = lax.broadcasted_iota(jnp.int32, (tm, LANES), 1)
    l = logit_ref[...]
    onehots, vals = [], []
    for _ in range(TOP_K):
        m = jnp.max(l, axis=-1, keepdims=True)
        idx = jnp.min(jnp.where(l == m, lane, LANES), axis=-1, keepdims=True)
        oh = lane == idx
        onehots.append(oh)
        vals.append(m)
        l = jnp.where(oh, -jnp.inf, l)
    sel = jnp.where(onehots[0] | onehots[1] | onehots[2] | onehots[3], 1.0, 0.0)
    tile_cnt = jnp.sum(sel, axis=0, keepdims=True)

    @pl.when(jnp.logical_and(ph == 0, i == 0))
    def _():
        cnt_st[...] = jnp.zeros_like(cnt_st)

    @pl.when(ph == 0)
    def _():
        cnt_st[...] = cnt_st[...] + tile_cnt

    @pl.when(jnp.logical_and(ph == 1, i == 0))
    def _():
        blocks = jnp.floor((cnt_st[...] + (EXPERT_BLOCK - 1)) * (1.0 / EXPERT_BLOCK))
        r = lax.broadcasted_iota(jnp.int32, (LANES, LANES), 0)
        c = lax.broadcasted_iota(jnp.int32, (LANES, LANES), 1)
        upper = jnp.where(r < c, 1.0, 0.0).astype(BF16)
        blocks8 = jnp.broadcast_to(blocks, (8, LANES))
        excl = _dot(blocks8.astype(BF16), upper)
        pad_st[...] = excl[0:1, :] * EXPERT_BLOCK
        row = lax.broadcasted_iota(jnp.int32, (8, LANES), 0)
        bend_ref[...] = jnp.where(row == 0, excl + blocks8,
                                  jnp.where(row == 1, excl * EXPERT_BLOCK + cnt_st[...], 0.0))
        cnt_st[...] = jnp.zeros_like(cnt_st)

    @pl.when(ph == 1)
    def _():
        carry = cnt_st[...]
        r = lax.broadcasted_iota(jnp.int32, (tm, tm), 0)
        c = lax.broadcasted_iota(jnp.int32, (tm, tm), 1)
        lower = jnp.where(r > c, 1.0, 0.0).astype(BF16)
        base = pad_st[...] + carry + _dot(lower, sel.astype(BF16))
        dest = jnp.zeros((tm, LANES), F32)
        wts = jnp.zeros((tm, LANES), F32)
        ex = [jnp.exp(v - vals[0]) for v in vals]
        den = ex[0] + ex[1] + ex[2] + ex[3]
        for k in range(TOP_K):
            dk = jnp.sum(jnp.where(onehots[k], base, 0.0), axis=-1, keepdims=True)
            dest = jnp.where(lane == k, dk, dest)
            wts = jnp.where(lane == k, ex[k] / den, wts)
        dest_ref[...] = dest.astype(jnp.int32)
        w_ref[...] = wts
        cnt_st[...] = carry + tile_cnt


def _route(logits):
    t = logits.shape[0]
    tm = 512
    return pl.pallas_call(
        _route_kernel,
        grid=(2, t // tm),
        in_specs=[pl.BlockSpec((tm, LANES), lambda ph, i: (i, 0))],
        out_specs=[
            pl.BlockSpec((tm, LANES), lambda ph, i: (i * ph, 0)),
            pl.BlockSpec((tm, LANES), lambda ph, i: (i * ph, 0)),
            pl.BlockSpec((8, LANES), lambda ph, i: (0, 0)),
        ],
        out_shape=[
            jax.ShapeDtypeStruct((t, LANES), jnp.int32),
            jax.ShapeDtypeStruct((t, LANES), F32),
            jax.ShapeDtypeStruct((8, LANES), F32),
        ],
        scratch_shapes=[pltpu.VMEM((1, LANES), F32), pltpu.VMEM((1, LANES), F32)],
        compiler_params=pltpu.CompilerParams(
            dimension_semantics=("arbitrary", "arbitrary"), vmem_limit_bytes=VMEM_LIMIT),
        name="route",
    )(logits)


DMA_UNROLL = 8


def _row_copy(src_ref, src_slab, dst_ref, dst_slab, sem):
    src = pl.multiple_of(src_slab * SLAB, SLAB)
    dst = pl.multiple_of(dst_slab * SLAB, SLAB)
    return pltpu.make_async_copy(src_ref.at[pl.ds(src, SLAB), :], dst_ref.at[pl.ds(dst, SLAB), :], sem)


def _dispatch_kernel(dest_ref, padfirst_ref, hf_hbm, xs_out, tiles, zbuf, lsem, rsem, zsem):
    i = pl.program_id(0)
    n_steps = pl.num_programs(0)
    n_assign = dest_ref.shape[0]
    tile_rows = tiles.shape[1]
    blk_rows = zbuf.shape[0]
    n_buf = tiles.shape[0]

    def tile_load(step, slot):
        start = pl.multiple_of(step * tile_rows, tile_rows)
        return pltpu.make_async_copy(hf_hbm.at[pl.ds(start, tile_rows), :], tiles.at[slot], lsem.at[slot])

    def wait_rows(slot):
        for _ in range(TOP_K):
            pltpu.make_async_copy(tiles.at[slot], xs_out.at[pl.ds(0, tile_rows), :], rsem.at[slot]).wait()

    @pl.when(i == 0)
    def _():
        tile_load(0, 0).start()
        tile_load(1, 1).start()
        zbuf[...] = jnp.zeros_like(zbuf)
        for e in range(N_EXPERTS):
            start = pl.multiple_of(padfirst_ref[e] * SLAB, SLAB)
            pltpu.make_async_copy(zbuf, xs_out.at[pl.ds(start, blk_rows), :], zsem).start()
        for e in range(N_EXPERTS):
            pltpu.make_async_copy(zbuf, xs_out.at[pl.ds(0, blk_rows), :], zsem).wait()

        n_used = padfirst_ref[N_EXPERTS]
        n_total = xs_out.shape[0] // blk_rows

        def zero_start(b, carry):
            start = pl.multiple_of(b * blk_rows, blk_rows)
            pltpu.make_async_copy(zbuf, xs_out.at[pl.ds(start, blk_rows), :], zsem).start()
            return carry

        def zero_wait(b, carry):
            pltpu.make_async_copy(zbuf, xs_out.at[pl.ds(0, blk_rows), :], zsem).wait()
            return carry

        lax.fori_loop(n_used, n_total, zero_start, 0)
        lax.fori_loop(n_used, n_total, zero_wait, 0)

    slot = i % n_buf
    tile_load(i, slot).wait()
    tile = tiles.at[slot]

    def body(g, carry):
        for u in range(DMA_UNROLL):
            t_local = g * (DMA_UNROLL // TOP_K) + u // TOP_K
            _row_copy(tile, t_local, xs_out, dest_ref[g * DMA_UNROLL + u], rsem.at[slot]).start(priority=u % 2)
        return carry

    lax.fori_loop(0, n_assign // DMA_UNROLL, body, 0)

    @pl.when(i >= 1)
    def _():
        wait_rows((i + n_buf - 1) % n_buf)

    @pl.when(i + 2 < n_steps)
    def _():
        tile_load(i + 2, (i + 2) % n_buf).start()

    @pl.when(i == n_steps - 1)
    def _():
        wait_rows(slot)


def _dispatch(dest_flat, pad_first, hf_slab, n_blocks):
    n_assign = dest_flat.shape[0]
    tm = 512
    blk_rows = EXPERT_BLOCK * SLAB
    return pl.pallas_call(
        _dispatch_kernel,
        grid=(n_assign // (tm * TOP_K),),
        in_specs=[
            pl.BlockSpec((tm * TOP_K,), lambda i: (i,), memory_space=pltpu.SMEM),
            pl.BlockSpec(memory_space=pltpu.SMEM),
            pl.BlockSpec(memory_space=pl.ANY),
        ],
        out_specs=pl.BlockSpec(memory_space=pl.ANY),
        out_shape=jax.ShapeDtypeStruct(((n_blocks + 1) * blk_rows, LANES), F32),
        scratch_shapes=[
            pltpu.VMEM((3, tm * SLAB, LANES), F32),
            pltpu.VMEM((blk_rows, LANES), F32),
            pltpu.SemaphoreType.DMA((3,)),
            pltpu.SemaphoreType.DMA((3,)),
            pltpu.SemaphoreType.DMA,
        ],
        compiler_params=pltpu.CompilerParams(dimension_semantics=("arbitrary",), vmem_limit_bytes=VMEM_LIMIT),
        name="dispatch",
    )(dest_flat, pad_first, hf_slab)


def _weight_copies(wgu_hbm, wd_hbm, wgu_f32, wd_f32, wsem, expert, slot):
    return (pltpu.make_async_copy(wgu_hbm.at[expert], wgu_f32.at[slot], wsem.at[0, slot]),
            pltpu.make_async_copy(wd_hbm.at[expert], wd_f32.at[slot], wsem.at[1, slot]))


def _expert_kernel(be_ref, nused_ref, first_ref, slot_ref, next_ref,
                   xs_ref, wgu_hbm, bgu_ref, wd_hbm, bd_ref, ys_ref,
                   wgu_f32, wd_f32, wgu_bf, wd_bf, wsem):
    b = pl.program_id(0)
    copies = functools.partial(_weight_copies, wgu_hbm, wd_hbm, wgu_f32, wd_f32, wsem)

    @pl.when(jnp.logical_and(b < nused_ref[0], first_ref[b] == 1))
    def _():
        slot = slot_ref[b]

        @pl.when(b == 0)
        def _():
            for cp in copies(be_ref[0], 0):
                cp.start()

        for cp in copies(be_ref[b], slot):
            cp.wait()

        @pl.when(next_ref[b] >= 0)
        def _():
            for cp in copies(next_ref[b], 1 - slot):
                cp.start()

        two = 2 * LANES
        r = lax.broadcasted_iota(jnp.int32, (two, two), 0)
        c = lax.broadcasted_iota(jnp.int32, (two, two), 1)
        src = jnp.where(c < LANES, 2 * c, 2 * (c - LANES) + 1)
        perm = jnp.where(r == src, 1.0, 0.0).astype(BF16)
        for blk in range(wgu_bf.shape[1] // two):
            wb = wgu_f32[slot, :, blk * two:(blk + 1) * two].astype(BF16)
            wgu_bf[:, blk * two:(blk + 1) * two] = _dot(wb, perm).astype(BF16)
        wd_bf[...] = wd_f32[slot].astype(BF16)

    @pl.when(b < nused_ref[0])
    def _():
        two = 2 * LANES
        x = _from_slabs(xs_ref, 0, EXPERT_BLOCK).astype(BF16)
        gu = _dot(x, wgu_bf[...]) + bgu_ref[0]
        n_blk = gu.shape[1] // two
        x_glu = jnp.concatenate([gu[:, i * two:i * two + LANES] for i in range(n_blk)], axis=1)
        x_lin = jnp.concatenate([gu[:, i * two + LANES:(i + 1) * two] for i in range(n_blk)], axis=1)
        x_glu = jnp.minimum(x_glu, SWIGLU_LIMIT)
        x_lin = jnp.clip(x_lin, -SWIGLU_LIMIT, SWIGLU_LIMIT)
        act = _sigmoid_mul(SWIGLU_ALPHA * x_glu, x_glu * (x_lin + 1.0))
        y = _dot(act.astype(BF16), wd_bf[...]) + bd_ref[0]
        _to_slabs(ys_ref, y)

    @pl.when(b >= nused_ref[0])
    def _():
        ys_ref[...] = jnp.zeros_like(ys_ref)


def _experts(block_expert, n_used, xs, w_gate_up, b_gu_perm, w_down, b_down):
    n_blocks = block_expert.shape[0]
    d = D_MODEL
    f2 = w_gate_up.shape[-1]
    blk_rows = EXPERT_BLOCK * SLAB

    idx = jnp.arange(n_blocks, dtype=jnp.int32)
    used = idx < n_used
    prev = jnp.concatenate([jnp.full((1,), -1, jnp.int32), block_expert[:-1]])
    first = (used & (block_expert != prev)).astype(jnp.int32)
    slot = (jnp.cumsum(first) - 1) % 2
    later = used[None, :] & (block_expert[None, :] > block_expert[:, None])
    nxt = jnp.min(jnp.where(later, block_expert[None, :], N_EXPERTS), axis=1)
    nxt = jnp.where(nxt == N_EXPERTS, -1, nxt)

    grid_spec = pltpu.PrefetchScalarGridSpec(
        num_scalar_prefetch=5,
        grid=(n_blocks,),
        in_specs=[
            pl.BlockSpec((blk_rows, LANES), lambda b, be, *_: (b, 0)),
            pl.BlockSpec(memory_space=pl.ANY),
            pl.BlockSpec((1, 1, f2), lambda b, be, *_: (be[b], 0, 0)),
            pl.BlockSpec(memory_space=pl.ANY),
            pl.BlockSpec((1, 1, d), lambda b, be, *_: (be[b], 0, 0)),
        ],
        out_specs=pl.BlockSpec((blk_rows, LANES), lambda b, be, *_: (b, 0)),
        scratch_shapes=[
            pltpu.VMEM((2, d, f2), F32),
            pltpu.VMEM((2, f2 // 2, d), F32),
            pltpu.VMEM((d, f2), BF16),
            pltpu.VMEM((f2 // 2, d), BF16),
            pltpu.SemaphoreType.DMA((2, 2)),
        ],
    )
    return pl.pallas_call(
        _expert_kernel,
        grid_spec=grid_spec,
        out_shape=jax.ShapeDtypeStruct((n_blocks * blk_rows, LANES), F32),
        compiler_params=pltpu.CompilerParams(
            dimension_semantics=("arbitrary",), vmem_limit_bytes=VMEM_LIMIT),
        name="expert",
    )(block_expert, n_used, first, slot.astype(jnp.int32), nxt.astype(jnp.int32),
      xs, w_gate_up, b_gu_perm, w_down, b_down)


def _combine_kernel(dest_ref, dest_next_ref, ys_hbm, w_ref, x1_ref, mod_ref, g_ref, o_ref, buf, sem):
    i = pl.program_id(0)
    n_steps = pl.num_programs(0)
    tm = x1_ref.shape[0]
    n_assign = tm * TOP_K
    slot = i % 2

    def gather(d_ref, to_slot):
        def body(g, carry):
            for u in range(DMA_UNROLL):
                t_local = g * (DMA_UNROLL // TOP_K) + u // TOP_K
                k = u % TOP_K
                _row_copy(ys_hbm, d_ref[g * DMA_UNROLL + u], buf.at[to_slot], k * tm + t_local,
                          sem.at[to_slot]).start(priority=u % 2)
            return carry

        lax.fori_loop(0, n_assign // DMA_UNROLL, body, 0)

    @pl.when(i == 0)
    def _():
        gather(dest_ref, 0)

    @pl.when(i + 1 < n_steps)
    def _():
        gather(dest_next_ref, 1 - slot)

    cur = buf.at[slot]
    pltpu.make_async_copy(ys_hbm.at[pl.ds(0, n_assign * SLAB), :], cur, sem.at[slot]).wait()

    w = w_ref[...]
    y = w[:, 0:1] * _from_slabs(cur, 0, tm)
    for k in range(1, TOP_K):
        y = y + w[:, k:k + 1] * _from_slabs(cur, k * tm, tm)
    x = x1_ref[...] + mod_ref[0, 5:6, :] * y
    ms = jnp.mean(x * x, axis=-1, keepdims=True)
    o_ref[...] = x * lax.rsqrt(ms + EPS) * g_ref[...]


def _combine(dest_flat, ys, wts, x1, mod6, g, seq):
    t = x1.shape[0]
    tm = 256
    per_b = seq // tm
    d = D_MODEL
    last = t // tm - 1
    return pl.pallas_call(
        _combine_kernel,
        grid=(t // tm,),
        in_specs=[
            pl.BlockSpec((tm * TOP_K,), lambda i: (i,), memory_space=pltpu.SMEM),
            pl.BlockSpec((tm * TOP_K,), lambda i: (jnp.minimum(i + 1, last),), memory_space=pltpu.SMEM),
            pl.BlockSpec(memory_space=pl.ANY),
            pl.BlockSpec((tm, LANES), lambda i: (i, 0)),
            pl.BlockSpec((tm, d), lambda i: (i, 0)),
            pl.BlockSpec((1, 6, d), lambda i: (i // per_b, 0, 0)),
            pl.BlockSpec((1, d), lambda i: (0, 0)),
        ],
        out_specs=pl.BlockSpec((tm, d), lambda i: (i, 0)),
        out_shape=jax.ShapeDtypeStruct((t, d), F32),
        scratch_shapes=[pltpu.VMEM((2, tm * TOP_K * SLAB, LANES), F32), pltpu.SemaphoreType.DMA((2,))],
        compiler_params=pltpu.CompilerParams(
            dimension_semantics=("arbitrary",), vmem_limit_bytes=VMEM_LIMIT),
        name="combine",
    )(dest_flat, dest_flat, ys, wts, x1, mod6, g)


def _split_w(w):
    hi = w.astype(BF16)
    lo = (w - hi.astype(F32)).astype(BF16)
    return hi, lo


def kernel(x, c, positions, w_ada, b_ada, norm_mix_g, w_in, conv_w, conv_b, b_if, ml_norm_g, ret_norm_g,
           w_branch_ml, w_branch_ret, w_out, norm_ffn_g, w_router, b_router, w_gate_up, b_gate_up, w_down,
           b_down, norm_final_g):
    batch, seq, d = x.shape
    t = batch * seq
    x2 = x.reshape(t, d)
    depth = w_ada.shape[0]
    c8 = jnp.concatenate([c, jnp.zeros((8 - batch, d), c.dtype)], axis=0)

    half = D_QK // 2
    inv = ROPE_BASE ** (-jnp.arange(half, dtype=F32) / half)
    inv2 = jnp.concatenate([inv, inv])[None, :]
    sign2 = jnp.concatenate([-jnp.ones((half,), F32), jnp.ones((half,), F32)])[None, :]
    pos_b = jnp.broadcast_to(positions.astype(F32).reshape(t, 1), (t, LANES))
    L = CHUNK
    log_gamma = jnp.log(1.0 - 2.0 ** (-5.0 - jnp.arange(N_HEADS, dtype=F32)))
    pos = jnp.arange(L, dtype=F32)
    rel = pos[:, None] - pos[None, :]
    decay = jnp.where(rel >= 0, jnp.exp(log_gamma[:, None, None] * jnp.maximum(rel, 0.0)), 0.0)
    kdec = jnp.exp(log_gamma[:, None] * (L - 1 - pos))[:, :, None]
    qdec = jnp.exp(log_gamma[:, None] * (pos + 1.0))[:, :, None]
    gch = jnp.broadcast_to(jnp.exp(log_gamma * L)[:, None, None], (N_HEADS, 1, LANES))

    assert depth == 1, "the final norm is fused after the single layer"
    l = 0
    wa_hi, wa_lo = _split_w(w_ada[l])
    mod = _ada(c8, wa_hi, wa_lo, b_ada[l][None, :])
    mod6 = mod[:batch].reshape(batch, 6, d)

    w = w_in[l]
    w_main = jnp.concatenate(
        [w[:, 0:3072], w[:, 3080:8200]], axis=1).astype(BF16)
    w_if = w[:, 3072:3080]
    zpad = jnp.zeros((d, LANES - N_HEADS), F32)
    wif = jnp.concatenate([w_if[:, :N_HEADS], zpad, w_if[:, N_HEADS:], zpad], axis=1)
    wif_hi, wif_lo = _split_w(wif)
    zb = jnp.zeros((LANES - N_HEADS,), F32)
    b_if2 = jnp.concatenate([b_if[l][:N_HEADS], zb, b_if[l][N_HEADS:], zb])[None, :]

    proj, gates, cos2, sin2 = _inproj(x2, mod6, norm_mix_g[l][None, :], w_main, wif_hi, wif_lo,
                                      pos_b, inv2, sign2, seq)
    hm, hr = _mix(proj, gates, cos2, sin2, conv_w[l], conv_b[l][None, :], b_if2,
                  ml_norm_g[l][None, :], ret_norm_g[l][None, :], decay, kdec, qdec, gch, batch, seq)

    wr = jnp.concatenate([w_router[l], jnp.zeros((d, LANES - N_EXPERTS), F32)], axis=1)
    wr_hi, wr_lo = _split_w(wr)
    br = jnp.concatenate([b_router[l], jnp.full((LANES - N_EXPERTS,), -1e30, F32)])[None, :]
    x1, hf, logits = _post(hm, hr, proj, x2, mod6, w_branch_ml[l].astype(BF16), w_branch_ret[l].astype(BF16),
                           w_out[l].astype(BF16), norm_ffn_g[l][None, :], wr_hi, wr_lo, br, seq)

    n_assign = t * TOP_K
    n_blocks = -(-n_assign // EXPERT_BLOCK) + N_EXPERTS
    dest, wts, tables = _route(logits)
    dest_flat = dest[:, :TOP_K].reshape(n_assign)
    bend = tables[0, :N_EXPERTS].astype(jnp.int32)
    pad_first = tables[1, :N_EXPERTS].astype(jnp.int32)
    n_used = bend[N_EXPERTS - 1:]
    blk = jnp.minimum(jnp.arange(n_blocks, dtype=jnp.int32), n_used - 1)
    block_expert = jnp.minimum(
        jnp.sum((bend[None, :] <= blk[:, None]).astype(jnp.int32), axis=1), N_EXPERTS - 1)
    xs = _dispatch(dest_flat, jnp.concatenate([pad_first, n_used]), hf, n_blocks)
    f2 = w_gate_up.shape[-1]
    bgu = b_gate_up[l].reshape(N_EXPERTS, f2 // (2 * LANES), LANES, 2)
    bgu = jnp.swapaxes(bgu, -1, -2).reshape(N_EXPERTS, 1, f2)
    ys = _experts(block_expert, n_used, xs, w_gate_up[l], bgu, w_down[l], b_down[l][:, None, :])
    out = _combine(dest_flat, ys, wts, x1, mod6, norm_final_g[None, :], seq)
    return out.reshape(batch, seq, d)
```

```python
import functools

import numpy as np
import jax
import jax.numpy as jnp
from jax import lax
from jax.experimental import pallas as pl
from jax.experimental.pallas import tpu as pltpu

D_MODEL = 1024
N_HEADS = 4
D_QK = 128
D_V = 256
CONV_K = 4
ROPE_BASE = 10000.0
CHUNK = 128
N_EXPERTS = 32
TOP_K = 4
SWIGLU_LIMIT = 7.0
SWIGLU_ALPHA = 1.702
EXPERT_BLOCK = 512
EPS = 1e-5

QK_W = N_HEADS * D_QK
V_W = N_HEADS * D_V
LANES = 128
GATE_W = 2 * LANES
V_EXT = D_V + LANES
MIX_GROUP = 4
SLAB = D_MODEL // LANES

C_MLQ, C_MLK, C_MLV, C_MLO = 0, 512, 1024, 2048
C_RQ, C_RK, C_RV, C_RG = 3072, 3584, 4096, 5120
C_GML, C_GRET = 6144, 7168
MIX_W = 6144
PROJ_W = 8192

BF16 = jnp.bfloat16
F32 = jnp.float32
VMEM_LIMIT = 56 * 1024 * 1024


def _dot(a, b):
    return jnp.dot(a, b, preferred_element_type=F32)


def _dot_nt(a, b):
    return lax.dot_general(a, b, (((1,), (1,)), ((), ())), preferred_element_type=F32)


def _split(a):
    hi = a.astype(BF16)
    lo = (a - hi.astype(F32)).astype(BF16)
    return hi, lo


def _dot3(a, b_hi, b_lo):
    a_hi, a_lo = _split(a)
    return _dot(a_hi, b_hi) + (_dot(a_lo, b_hi) + _dot(a_hi, b_lo))


def _sigmoid(x):
    return 0.5 * jnp.tanh(0.5 * x) + 0.5


def _sigmoid_mul(x, y):
    hy = 0.5 * y
    return hy * jnp.tanh(0.5 * x) + hy


def _silu(x):
    hx = 0.5 * x
    return hx * jnp.tanh(hx) + hx


def _round_robin(stage_generators):
    pending = list(stage_generators)
    while pending:
        for gen in list(pending):
            if next(gen, True):
                pending.remove(gen)


def _rms_mod(x, g, scale, shift):
    ms = jnp.mean(x * x, axis=-1, keepdims=True)
    return (x * lax.rsqrt(ms + EPS) * g) * (1.0 + scale) + shift


def _ada_kernel(c_ref, whi_ref, wlo_ref, b_ref, o_ref):
    c = c_ref[...]
    ca = _silu(c)
    o_ref[...] = _dot3(ca, whi_ref[...], wlo_ref[...]) + b_ref[...]


def _ada(c8, w_hi, w_lo, b):
    n = w_hi.shape[1]
    tn = 1024
    return pl.pallas_call(
        _ada_kernel,
        grid=(n // tn,),
        in_specs=[
            pl.BlockSpec((8, D_MODEL), lambda j: (0, 0)),
            pl.BlockSpec((D_MODEL, tn), lambda j: (0, j)),
            pl.BlockSpec((D_MODEL, tn), lambda j: (0, j)),
            pl.BlockSpec((1, tn), lambda j: (0, j)),
        ],
        out_specs=pl.BlockSpec((8, tn), lambda j: (0, j)),
        out_shape=jax.ShapeDtypeStruct((8, n), F32),
        name="ada",
    )(c8, w_hi, w_lo, b)


def _inproj_kernel(x_ref, xn_ref, mod_ref, modn_ref, g_ref, w_ref, wif_hi_ref, wif_lo_ref, pos_ref, inv_ref,
                   sign_ref, proj_ref, gates_ref, cos_ref, sin_ref, hi_a, lo_a, hi_b, lo_b):
    i = pl.program_id(0)
    j = pl.program_id(1)

    def normalise(x, mod):
        return _split(_rms_mod(x, g_ref[...], mod[0, 1:2, :], mod[0, 0:1, :]))

    @pl.when(jnp.logical_and(i == 0, j == 0))
    def _():
        hi_a[...], lo_a[...] = normalise(x_ref[...], mod_ref)

    n = x_ref.shape[0] // pl.num_programs(1)
    rows = pl.ds(pl.multiple_of(j * n, n), n)
    n_split = 4
    tc = w_ref.shape[1] // n_split

    def step(cur_hi, cur_lo, nxt_hi, nxt_lo):
        def matmul_steps():
            for c in range(n_split):
                proj_ref[:, c * tc:(c + 1) * tc] = _dot(cur_hi[...], w_ref[:, c * tc:(c + 1) * tc]).astype(BF16)
                yield False

        def side_steps():
            h_hi = cur_hi[rows, :]
            h_lo = cur_lo[rows, :]
            gates_ref[rows, :] = (_dot(h_hi, wif_hi_ref[...])
                                  + (_dot(h_lo, wif_hi_ref[...]) + _dot(h_hi, wif_lo_ref[...])))
            yield False
            ang = pos_ref[rows, :] * inv_ref[...]
            cos_ref[rows, :] = jnp.cos(ang)
            yield False
            sin_ref[rows, :] = jnp.sin(ang) * sign_ref[...]
            yield False
            nxt_hi[rows, :], nxt_lo[rows, :] = normalise(xn_ref[rows, :], modn_ref)
            yield False

        _round_robin([matmul_steps(), side_steps()])

    @pl.when(i % 2 == 0)
    def _():
        step(hi_a, lo_a, hi_b, lo_b)

    @pl.when(i % 2 == 1)
    def _():
        step(hi_b, lo_b, hi_a, lo_a)


GATE_COL = 3072


def _wprep_kernel(a_ref, b_ref, o_ref):
    c = pl.program_id(0)
    tn = a_ref.shape[1]

    @pl.when((c + 1) * tn <= GATE_COL)
    def _():
        o_ref[...] = a_ref[...].astype(BF16)

    @pl.when((c + 1) * tn > GATE_COL)
    def _():
        wide = jnp.concatenate([a_ref[...], b_ref[...]], axis=1)
        n = wide.shape[1]
        o_ref[...] = pltpu.roll(wide, n - 2 * N_HEADS, 1)[:, :tn].astype(BF16)


def _wprep(w):
    d = w.shape[0]
    tn = 1024
    assert GATE_COL % tn == 0
    return pl.pallas_call(
        _wprep_kernel,
        grid=(PROJ_W // tn,),
        in_specs=[
            pl.BlockSpec((d, tn), lambda c: (0, c)),
            pl.BlockSpec((d, LANES), lambda c: (0, (c + 1) * (tn // LANES))),
        ],
        out_specs=pl.BlockSpec((d, tn), lambda c: (0, c)),
        out_shape=jax.ShapeDtypeStruct((d, PROJ_W), BF16),
        compiler_params=pltpu.CompilerParams(
            dimension_semantics=("arbitrary",), vmem_limit_bytes=VMEM_LIMIT),
        name="wprep",
    )(w, w)


def _inproj(x2, mod6, g, w_main, wif_hi, wif_lo, pos_b, inv2, sign2, seq):
    t = x2.shape[0]
    tm, tn = 512, 2048
    per_b = seq // tm
    last = t // tm - 1
    nxt = lambda i: jnp.minimum(i + 1, last)
    return pl.pallas_call(
        _inproj_kernel,
        grid=(t // tm, PROJ_W // tn),
        in_specs=[
            pl.BlockSpec((tm, D_MODEL), lambda i, j: (i, 0)),
            pl.BlockSpec((tm, D_MODEL), lambda i, j: (nxt(i), 0)),
            pl.BlockSpec((1, 6, D_MODEL), lambda i, j: (i // per_b, 0, 0)),
            pl.BlockSpec((1, 6, D_MODEL), lambda i, j: (nxt(i) // per_b, 0, 0)),
            pl.BlockSpec((1, D_MODEL), lambda i, j: (0, 0)),
            pl.BlockSpec((D_MODEL, tn), lambda i, j: (0, j)),
            pl.BlockSpec((D_MODEL, GATE_W), lambda i, j: (0, 0)),
            pl.BlockSpec((D_MODEL, GATE_W), lambda i, j: (0, 0)),
            pl.BlockSpec((tm, LANES), lambda i, j: (i, 0)),
            pl.BlockSpec((1, LANES), lambda i, j: (0, 0)),
            pl.BlockSpec((1, LANES), lambda i, j: (0, 0)),
        ],
        out_specs=[
            pl.BlockSpec((tm, tn), lambda i, j: (i, j)),
            pl.BlockSpec((tm, GATE_W), lambda i, j: (i, 0)),
            pl.BlockSpec((tm, LANES), lambda i, j: (i, 0)),
            pl.BlockSpec((tm, LANES), lambda i, j: (i, 0)),
        ],
        out_shape=[
            jax.ShapeDtypeStruct((t, PROJ_W), BF16),
            jax.ShapeDtypeStruct((t, GATE_W), F32),
            jax.ShapeDtypeStruct((t, LANES), F32),
            jax.ShapeDtypeStruct((t, LANES), F32),
        ],
        scratch_shapes=[pltpu.VMEM((tm, D_MODEL), BF16)] * 4,
        compiler_params=pltpu.CompilerParams(
            dimension_semantics=("arbitrary", "arbitrary"), vmem_limit_bytes=VMEM_LIMIT),
        name="inproj",
    )(x2, x2, mod6, mod6, g, w_main, wif_hi, wif_lo, pos_b, inv2, sign2)


def _row_mean(a):
    inv_n = jnp.full((a.shape[1], LANES), 1.0 / a.shape[1], BF16)
    m = _dot(a.astype(BF16), inv_n)
    return jnp.concatenate([m] * (a.shape[1] // LANES), axis=1)


def _head_norm(h, g):
    d = h - _row_mean(h)
    return d * lax.rsqrt(_row_mean(d * d) + EPS) * g


def _mix_kernel(proj_ref, gates_ref, cos_ref, sin_ref, convw_ref, convb_ref, bif_ref, mlg_ref, retg_ref,
                decay_ref, kdec_ref, qdec_ref, gch_ref,
                hm_ref, hr_ref,
                prev_ref, c_st, m_st, r_st):
    group = proj_ref.shape[0]

    @pl.when(pl.program_id(1) == 0)
    def _():
        def zero_state(i, carry):
            c_st[i] = jnp.zeros(c_st.shape[1:], F32)
            r_st[i] = jnp.zeros(r_st.shape[1:], F32)
            return carry

        lax.fori_loop(0, group * N_HEADS, zero_state, 0)
        prev_ref[...] = jnp.zeros_like(prev_ref)
        m_st[...] = jnp.zeros_like(m_st)

    steps = []
    for gb in range(group):
        states = slice(gb * N_HEADS, (gb + 1) * N_HEADS)
        steps.append(_mlstm_steps(proj_ref.at[gb], gates_ref.at[gb], convw_ref, convb_ref, bif_ref, mlg_ref,
                                  hm_ref.at[gb], prev_ref.at[gb], c_st.at[states], m_st.at[gb]))
        steps.append(_retention_steps(proj_ref.at[gb], cos_ref.at[gb], sin_ref.at[gb], retg_ref, decay_ref,
                                      kdec_ref, qdec_ref, gch_ref, hr_ref.at[gb], r_st.at[states]))
    _round_robin(steps)


def _mlstm_steps(proj_ref, gates_ref, convw_ref, convb_ref, bif_ref, mlg_ref, hm_ref, prev_ref, c_st, m_st):
    L = CHUNK
    rows = lax.broadcasted_iota(jnp.int32, (L, L), 0)
    cols = lax.broadcasted_iota(jnp.int32, (L, L), 1)
    causal = rows >= cols
    tril = jnp.where(causal, 1.0, 0.0).astype(BF16)

    cur = proj_ref[:, C_MLQ:C_MLQ + 2 * QK_W]
    xx = jnp.concatenate([prev_ref[...], cur], axis=0)
    r2 = lax.broadcasted_iota(jnp.int32, (L, 2 * L), 0)
    c2 = lax.broadcasted_iota(jnp.int32, (L, 2 * L), 1)
    acc = convb_ref[...] + cur.astype(F32) * convw_ref[CONV_K - 1:CONV_K, :]
    for d in range(1, CONV_K):
        shift = jnp.where(c2 == r2 + (L - d), 1.0, 0.0).astype(BF16)
        acc = acc + _dot(shift, xx) * convw_ref[CONV_K - 1 - d:CONV_K - d, :]
        yield False
    prev_ref[...] = cur
    qk = _silu(acc)
    yield False

    g = gates_ref[...] + bif_ref[...]
    gi = g[:, :LANES]
    gf = g[:, LANES:]
    lf = jnp.minimum(gf, 0.0) - jnp.log(1.0 + jnp.exp(-jnp.abs(gf)))
    lf_hi, lf_lo = _split(lf)
    a_all = _dot(tril, lf_hi) + _dot(tril, lf_lo)
    yield False
    a_last = a_all[L - 1:L, :]
    bm = gi - a_all
    bm_t = bm.T
    w_state = a_last + bm
    m_loc = jnp.max(w_state, axis=0, keepdims=True)
    m_prev = m_st[...]
    inter_log = a_all + m_prev
    m_new = jnp.maximum(a_last + m_prev, m_loc)
    s_prev = jnp.exp(a_last + m_prev - m_new)
    s_loc = jnp.exp(m_loc - m_new)
    ws_all = jnp.exp(w_state - m_loc) * s_loc
    m_st[...] = m_new
    yield False

    ones_blk = jnp.ones((L, LANES), BF16)
    q_scale = D_QK ** -0.5

    for h in range(N_HEADS):
        q = (qk[:, h * D_QK:(h + 1) * D_QK] * q_scale).astype(BF16)
        k_f = qk[:, QK_W + h * D_QK:QK_W + (h + 1) * D_QK]
        k = k_f.astype(BF16)
        v_ext = jnp.concatenate([proj_ref[:, C_MLV + h * D_V:C_MLV + (h + 1) * D_V], ones_blk], axis=1)
        dlog = jnp.where(causal, a_all[:, h:h + 1] + bm_t[h:h + 1, :], -jnp.inf)
        m_intra = jnp.max(dlog, axis=-1, keepdims=True)
        s = _dot_nt(q, k)
        qc = _dot(q, c_st[h].astype(BF16))
        yield False
        il = inter_log[:, h:h + 1]
        m_t = jnp.maximum(il, m_intra)
        p = jnp.exp(dlog - m_t) * s
        isc = jnp.exp(il - m_t)
        tot = _dot(p.astype(BF16), v_ext) + isc * qc
        yield False
        den = tot[:, D_V:D_V + 1]
        hout = tot[:, :D_V] / jnp.maximum(jnp.abs(den), jnp.exp(-m_t))
        kw_t = (k_f * ws_all[:, h:h + 1]).T.astype(BF16)
        c_st[h] = s_prev[:, h:h + 1] * c_st[h] + _dot(kw_t, v_ext)
        yield False
        y = _head_norm(hout, mlg_ref[:, h * D_V:(h + 1) * D_V])
        o = proj_ref[:, C_MLO + h * D_V:C_MLO + (h + 1) * D_V].astype(F32)
        hm_ref[:, h * D_V:(h + 1) * D_V] = _sigmoid_mul(o, y).astype(BF16)
        yield False


def _retention_steps(proj_ref, cos_ref, sin_ref, retg_ref, decay_ref, kdec_ref, qdec_ref, gch_ref, hr_ref, r_st):
    cos2 = cos_ref[...]
    sin2 = sin_ref[...]
    k_scale = D_QK ** -0.5
    for h in range(N_HEADS):
        q_raw = proj_ref[:, C_RQ + h * D_QK:C_RQ + (h + 1) * D_QK].astype(F32)
        k_raw = proj_ref[:, C_RK + h * D_QK:C_RK + (h + 1) * D_QK].astype(F32)
        q = (q_raw * cos2 + pltpu.roll(q_raw, D_QK // 2, 1) * sin2).astype(BF16)
        k_f = (k_raw * cos2 + pltpu.roll(k_raw, D_QK // 2, 1) * sin2) * k_scale
        v = proj_ref[:, C_RV + h * D_V:C_RV + (h + 1) * D_V]
        yield False
        sc = _dot_nt(q, k_f.astype(BF16)) * decay_ref[h]
        hret = _dot(sc.astype(BF16), v) + _dot(q, r_st[h].astype(BF16)) * qdec_ref[h]
        yield False
        kd_t = (k_f * kdec_ref[h]).T.astype(BF16)
        r_st[h] = gch_ref[h][:, 0:1] * r_st[h] + _dot(kd_t, v)
        yield False
        y = _head_norm(hret, retg_ref[:, h * D_V:(h + 1) * D_V])
        gt = proj_ref[:, C_RG + h * D_V:C_RG + (h + 1) * D_V].astype(F32)
        hr_ref[:, h * D_V:(h + 1) * D_V] = (_silu(gt) * y).astype(BF16)
        yield False


def _mix(proj, gates, cos2, sin2, conv_w, conv_b, b_if2, ml_g, ret_g, decay, kdec, qdec, gch, batch, seq):
    t = proj.shape[0]
    nc = seq // CHUNK
    L = CHUNK
    G = MIX_GROUP
    proj = proj.reshape(batch, seq, PROJ_W)
    gates = gates.reshape(batch, seq, GATE_W)
    cos2 = cos2.reshape(batch, seq, LANES)
    sin2 = sin2.reshape(batch, seq, LANES)
    full = lambda shape: pl.BlockSpec(shape, lambda b, c: (0,) * len(shape))
    hm, hr = pl.pallas_call(
        _mix_kernel,
        grid=(batch // G, nc),
        in_specs=[
            pl.BlockSpec((G, L, MIX_W), lambda b, c: (b, c, 0)),
            pl.BlockSpec((G, L, GATE_W), lambda b, c: (b, c, 0)),
            pl.BlockSpec((G, L, LANES), lambda b, c: (b, c, 0)),
            pl.BlockSpec((G, L, LANES), lambda b, c: (b, c, 0)),
            full((CONV_K, 2 * QK_W)),
            full((1, 2 * QK_W)),
            full((1, GATE_W)),
            full((1, V_W)),
            full((1, V_W)),
            full((N_HEADS, L, L)),
            full((N_HEADS, L, 1)),
            full((N_HEADS, L, 1)),
            full((N_HEADS, 1, LANES)),
        ],
        out_specs=[
            pl.BlockSpec((G, L, V_W), lambda b, c: (b, c, 0)),
            pl.BlockSpec((G, L, V_W), lambda b, c: (b, c, 0)),
        ],
        out_shape=[jax.ShapeDtypeStruct((batch, seq, V_W), BF16), jax.ShapeDtypeStruct((batch, seq, V_W), BF16)],
        scratch_shapes=[
            pltpu.VMEM((G, L, 2 * QK_W), BF16),
            pltpu.VMEM((G * N_HEADS, D_QK, V_EXT), F32),
            pltpu.VMEM((G, 1, LANES), F32),
            pltpu.VMEM((G * N_HEADS, D_QK, D_V), F32),
        ],
        compiler_params=pltpu.CompilerParams(
            dimension_semantics=("arbitrary", "arbitrary"), vmem_limit_bytes=VMEM_LIMIT),
        name="mix",
    )(proj, gates, cos2, sin2, conv_w, conv_b, b_if2, ml_g, ret_g, decay, kdec, qdec, gch)
    return hm.reshape(t, V_W), hr.reshape(t, V_W)


def _to_slabs(ref, val):
    rows = val.shape[0]
    for s in range(SLAB):
        ref[pl.ds(s, rows, stride=SLAB), :] = val[:, s * LANES:(s + 1) * LANES]


def _from_slabs(ref, first_slab, rows):
    return jnp.concatenate(
        [ref[pl.ds(first_slab * SLAB + s, rows, stride=SLAB), :] for s in range(SLAB)], axis=1)


def _post_kernel(hm_ref, hr_ref, gm_ref, gr_ref, x_ref, mod_ref, wbm_ref, wbr_ref, wout_ref, g_ref,
                 wr_hi_ref, wr_lo_ref, br_ref, x1_ref, hf_ref, logit_ref):
    n_sub = 2
    sub = x_ref.shape[0] // n_sub

    def steps(s):
        r = pl.ds(s * sub, sub)
        ym = _dot(hm_ref[r, :], wbm_ref[...])
        yield False
        yr = _dot(hr_ref[r, :], wbr_ref[...])
        yield False
        y = _sigmoid_mul(gm_ref[r, :].astype(F32), ym) + _sigmoid_mul(gr_ref[r, :].astype(F32), yr)
        o = _dot(y.astype(BF16), wout_ref[...])
        yield False
        x1 = x_ref[r, :] + mod_ref[0, 2:3, :] * o
        x1_ref[r, :] = x1
        hf = _rms_mod(x1, g_ref[...], mod_ref[0, 4:5, :], mod_ref[0, 3:4, :])
        yield False
        _to_slabs(hf_ref.at[pl.ds(s * sub * SLAB, sub * SLAB)], hf)
        logit_ref[r, :] = _dot3(hf, wr_hi_ref[...], wr_lo_ref[...]) + br_ref[...]
        yield False

    _round_robin([steps(s) for s in range(n_sub)])


def _post(hm, hr, proj, x2, mod6, wbm, wbr, wout, g, wr_hi, wr_lo, br, seq):
    t = x2.shape[0]
    tm = 512
    per_b = seq // tm
    d = D_MODEL
    const = lambda shape: pl.BlockSpec(shape, lambda i: (0,) * len(shape))
    return pl.pallas_call(
        _post_kernel,
        grid=(t // tm,),
        in_specs=[
            pl.BlockSpec((tm, d), lambda i: (i, 0)),
            pl.BlockSpec((tm, d), lambda i: (i, 0)),
            pl.BlockSpec((tm, d), lambda i: (i, C_GML // d)),
            pl.BlockSpec((tm, d), lambda i: (i, C_GRET // d)),
            pl.BlockSpec((tm, d), lambda i: (i, 0)),
            pl.BlockSpec((1, 6, d), lambda i: (i // per_b, 0, 0)),
            const((d, d)), const((d, d)), const((d, d)),
            const((1, d)),
            const((d, LANES)), const((d, LANES)), const((1, LANES)),
        ],
        out_specs=[
            pl.BlockSpec((tm, d), lambda i: (i, 0)),
            pl.BlockSpec((tm * SLAB, LANES), lambda i: (i, 0)),
            pl.BlockSpec((tm, LANES), lambda i: (i, 0)),
        ],
        out_shape=[
            jax.ShapeDtypeStruct((t, d), F32),
            jax.ShapeDtypeStruct((t * SLAB, LANES), F32),
            jax.ShapeDtypeStruct((t, LANES), F32),
        ],
        compiler_params=pltpu.CompilerParams(
            dimension_semantics=("arbitrary",), vmem_limit_bytes=VMEM_LIMIT),
        name="post",
    )(hm, hr, proj, proj, x2, mod6, wbm, wbr, wout, g, wr_hi, wr_lo, br)


def _route_kernel(logit_ref, dest_ref, w_ref, bend_ref, cnt_st, pad_st, sel_st, pick_st, wts_st):
    ph = pl.program_id(0)
    i = pl.program_id(1)
    tm = logit_ref.shape[0]
    rows = pl.ds(pl.multiple_of(i * tm, tm), tm)
    lane = lax.broadcasted_iota(jnp.int32, (tm, LANES), 1)
    lane_f = lane.astype(F32)

    @pl.when(jnp.logical_and(ph == 0, i == 0))
    def _():
        cnt_st[...] = jnp.zeros_like(cnt_st)

    @pl.when(ph == 0)
    def _():
        l = logit_ref[...]
        picks, vals = [], []
        sel = jnp.zeros((tm, LANES), F32)
        for _ in range(TOP_K):
            m = jnp.max(l, axis=-1, keepdims=True)
            idx = jnp.min(jnp.where(l == m, lane_f, float(LANES)), axis=-1, keepdims=True)
            oh = lane_f == idx
            picks.append(idx)
            vals.append(m)
            sel = jnp.where(oh, 1.0, sel)
            l = jnp.where(oh, -jnp.inf, l)
        ex = [jnp.exp(v - vals[0]) for v in vals]
        den = ex[0] + ex[1] + ex[2] + ex[3]
        pick = jnp.zeros((tm, LANES), F32)
        wts = jnp.zeros((tm, LANES), F32)
        for k in range(TOP_K):
            pick = jnp.where(lane == k, picks[k], pick)
            wts = jnp.where(lane == k, ex[k] / den, wts)
        sel_st[rows, :] = sel.astype(BF16)
        pick_st[rows, :] = pick
        wts_st[rows, :] = wts
        cnt_st[...] = cnt_st[...] + jnp.sum(sel, axis=0, keepdims=True)

    @pl.when(jnp.logical_and(ph == 1, i == 0))
    def _():
        blocks = jnp.floor((cnt_st[...] + (EXPERT_BLOCK - 1)) * (1.0 / EXPERT_BLOCK))
        r = lax.broadcasted_iota(jnp.int32, (LANES, LANES), 0)
        c = lax.broadcasted_iota(jnp.int32, (LANES, LANES), 1)
        upper = jnp.where(r < c, 1.0, 0.0).astype(BF16)
        blocks8 = jnp.broadcast_to(blocks, (8, LANES))
        excl = _dot(blocks8.astype(BF16), upper)
        pad_st[...] = excl[0:1, :] * EXPERT_BLOCK
        row = lax.broadcasted_iota(jnp.int32, (8, LANES), 0)
        bend_ref[...] = jnp.where(row == 0, excl + blocks8,
                                  jnp.where(row == 1, excl * EXPERT_BLOCK + cnt_st[...], 0.0))
        cnt_st[...] = jnp.zeros_like(cnt_st)

    @pl.when(ph == 1)
    def _():
        carry = cnt_st[...]
        sel = sel_st[rows, :]
        pick = pick_st[rows, :]
        r = lax.broadcasted_iota(jnp.int32, (tm, tm), 0)
        c = lax.broadcasted_iota(jnp.int32, (tm, tm), 1)
        lower = jnp.where(r > c, 1.0, 0.0).astype(BF16)
        base = pad_st[...] + carry + _dot(lower, sel)
        dest = jnp.zeros((tm, LANES), F32)
        for k in range(TOP_K):
            dk = jnp.sum(jnp.where(lane_f == pick[:, k:k + 1], base, 0.0), axis=-1, keepdims=True)
            dest = jnp.where(lane == k, dk, dest)
        dest_ref[...] = dest.astype(jnp.int32)
        w_ref[...] = wts_st[rows, :]
        cnt_st[...] = carry + jnp.sum(sel.astype(F32), axis=0, keepdims=True)


def _route(logits):
    t = logits.shape[0]
    tm = 512
    return pl.pallas_call(
        _route_kernel,
        grid=(2, t // tm),
        in_specs=[pl.BlockSpec((tm, LANES), lambda ph, i: (i * (1 - ph), 0))],
        out_specs=[
            pl.BlockSpec((tm, LANES), lambda ph, i: (i * ph, 0)),
            pl.BlockSpec((tm, LANES), lambda ph, i: (i * ph, 0)),
            pl.BlockSpec((8, LANES), lambda ph, i: (0, 0)),
        ],
        out_shape=[
            jax.ShapeDtypeStruct((t, LANES), jnp.int32),
            jax.ShapeDtypeStruct((t, LANES), F32),
            jax.ShapeDtypeStruct((8, LANES), F32),
        ],
        scratch_shapes=[
            pltpu.VMEM((1, LANES), F32),
            pltpu.VMEM((1, LANES), F32),
            pltpu.VMEM((t, LANES), BF16),
            pltpu.VMEM((t, LANES), F32),
            pltpu.VMEM((t, LANES), F32),
        ],
        compiler_params=pltpu.CompilerParams(
            dimension_semantics=("arbitrary", "arbitrary"), vmem_limit_bytes=VMEM_LIMIT),
        name="route",
    )(logits)


DMA_UNROLL = 8


def _row_copy(src_ref, src_slab, dst_ref, dst_slab, sem):
    src = pl.multiple_of(src_slab * SLAB, SLAB)
    dst = pl.multiple_of(dst_slab * SLAB, SLAB)
    return pltpu.make_async_copy(src_ref.at[pl.ds(src, SLAB), :], dst_ref.at[pl.ds(dst, SLAB), :], sem)


def _dispatch_kernel(dest_ref, padfirst_ref, hf_hbm, xs_out, tiles, zbuf, lsem, rsem, zsem):
    i = pl.program_id(0)
    n_steps = pl.num_programs(0)
    n_assign = dest_ref.shape[0]
    tile_rows = tiles.shape[1]
    blk_rows = zbuf.shape[0]
    n_buf = tiles.shape[0]

    def tile_load(step, slot):
        start = pl.multiple_of(step * tile_rows, tile_rows)
        return pltpu.make_async_copy(hf_hbm.at[pl.ds(start, tile_rows), :], tiles.at[slot], lsem.at[slot])

    def wait_rows(slot):
        for _ in range(TOP_K):
            pltpu.make_async_copy(tiles.at[slot], xs_out.at[pl.ds(0, tile_rows), :], rsem.at[slot]).wait()

    @pl.when(i == 0)
    def _():
        tile_load(0, 0).start()
        tile_load(1, 1).start()
        zbuf[...] = jnp.zeros_like(zbuf)
        for e in range(N_EXPERTS):
            start = pl.multiple_of(padfirst_ref[e] * SLAB, SLAB)
            pltpu.make_async_copy(zbuf, xs_out.at[pl.ds(start, blk_rows), :], zsem).start()
        for e in range(N_EXPERTS):
            pltpu.make_async_copy(zbuf, xs_out.at[pl.ds(0, blk_rows), :], zsem).wait()

        n_used = padfirst_ref[N_EXPERTS]
        n_total = xs_out.shape[0] // blk_rows

        def zero_start(b, carry):
            start = pl.multiple_of(b * blk_rows, blk_rows)
            pltpu.make_async_copy(zbuf, xs_out.at[pl.ds(start, blk_rows), :], zsem).start()
            return carry

        def zero_wait(b, carry):
            pltpu.make_async_copy(zbuf, xs_out.at[pl.ds(0, blk_rows), :], zsem).wait()
            return carry

        lax.fori_loop(n_used, n_total, zero_start, 0)
        lax.fori_loop(n_used, n_total, zero_wait, 0)

    slot = i % n_buf
    tile_load(i, slot).wait()
    tile = tiles.at[slot]

    def body(g, carry):
        for u in range(DMA_UNROLL):
            t_local = g * (DMA_UNROLL // TOP_K) + u // TOP_K
            _row_copy(tile, t_local, xs_out, dest_ref[g * DMA_UNROLL + u], rsem.at[slot]).start(priority=u % 2)
        return carry

    lax.fori_loop(0, n_assign // DMA_UNROLL, body, 0)

    @pl.when(i >= 1)
    def _():
        wait_rows((i + n_buf - 1) % n_buf)

    @pl.when(i + 2 < n_steps)
    def _():
        tile_load(i + 2, (i + 2) % n_buf).start()

    @pl.when(i == n_steps - 1)
    def _():
        wait_rows(slot)


def _dispatch(dest_flat, pad_first, hf_slab, n_blocks):
    n_assign = dest_flat.shape[0]
    tm = 512
    blk_rows = EXPERT_BLOCK * SLAB
    return pl.pallas_call(
        _dispatch_kernel,
        grid=(n_assign // (tm * TOP_K),),
        in_specs=[
            pl.BlockSpec((tm * TOP_K,), lambda i: (i,), memory_space=pltpu.SMEM),
            pl.BlockSpec(memory_space=pltpu.SMEM),
            pl.BlockSpec(memory_space=pl.ANY),
        ],
        out_specs=pl.BlockSpec(memory_space=pl.ANY),
        out_shape=jax.ShapeDtypeStruct(((n_blocks + 1) * blk_rows, LANES), F32),
        scratch_shapes=[
            pltpu.VMEM((3, tm * SLAB, LANES), F32),
            pltpu.VMEM((blk_rows, LANES), F32),
            pltpu.SemaphoreType.DMA((3,)),
            pltpu.SemaphoreType.DMA((3,)),
            pltpu.SemaphoreType.DMA,
        ],
        compiler_params=pltpu.CompilerParams(dimension_semantics=("arbitrary",), vmem_limit_bytes=VMEM_LIMIT),
        name="dispatch",
    )(dest_flat, pad_first, hf_slab)


def _weight_copies(wgu_hbm, wd_hbm, wgu_f32, wd_f32, wsem, expert, slot):
    return (pltpu.make_async_copy(wgu_hbm.at[expert], wgu_f32.at[slot], wsem.at[0, slot]),
            pltpu.make_async_copy(wd_hbm.at[expert], wd_f32.at[slot], wsem.at[1, slot]))


def _expert_kernel(be_ref, nused_ref, first_ref, slot_ref, next_ref,
                   xs_ref, wgu_hbm, bgu_ref, wd_hbm, bd_ref, ys_ref,
                   wgu_f32, wd_f32, wgu_bf, wd_bf, wsem):
    b = pl.program_id(0)
    copies = functools.partial(_weight_copies, wgu_hbm, wd_hbm, wgu_f32, wd_f32, wsem)

    @pl.when(jnp.logical_and(b < nused_ref[0], first_ref[b] == 1))
    def _():
        slot = slot_ref[b]

        @pl.when(b == 0)
        def _():
            for cp in copies(be_ref[0], 0):
                cp.start()

        for cp in copies(be_ref[b], slot):
            cp.wait()

        @pl.when(next_ref[b] >= 0)
        def _():
            for cp in copies(next_ref[b], 1 - slot):
                cp.start()

        two = 2 * LANES
        r = lax.broadcasted_iota(jnp.int32, (two, two), 0)
        c = lax.broadcasted_iota(jnp.int32, (two, two), 1)
        src = jnp.where(c < LANES, 2 * c, 2 * (c - LANES) + 1)
        perm = jnp.where(r == src, 1.0, 0.0).astype(BF16)
        for blk in range(wgu_bf.shape[1] // two):
            wb = wgu_f32[slot, :, blk * two:(blk + 1) * two].astype(BF16)
            wgu_bf[:, blk * two:(blk + 1) * two] = _dot(wb, perm).astype(BF16)
        wd_bf[...] = wd_f32[slot].astype(BF16)

    @pl.when(b < nused_ref[0])
    def _():
        two = 2 * LANES
        x = _from_slabs(xs_ref, 0, EXPERT_BLOCK).astype(BF16)
        gu = _dot(x, wgu_bf[...]) + bgu_ref[0]
        n_blk = gu.shape[1] // two
        x_glu = jnp.concatenate([gu[:, i * two:i * two + LANES] for i in range(n_blk)], axis=1)
        x_lin = jnp.concatenate([gu[:, i * two + LANES:(i + 1) * two] for i in range(n_blk)], axis=1)
        x_glu = jnp.minimum(x_glu, SWIGLU_LIMIT)
        x_lin = jnp.clip(x_lin, -SWIGLU_LIMIT, SWIGLU_LIMIT)
        act = _sigmoid_mul(SWIGLU_ALPHA * x_glu, x_glu * (x_lin + 1.0))
        y = _dot(act.astype(BF16), wd_bf[...]) + bd_ref[0]
        _to_slabs(ys_ref, y)

    @pl.when(b >= nused_ref[0])
    def _():
        ys_ref[...] = jnp.zeros_like(ys_ref)


def _experts(block_expert, n_used, xs, w_gate_up, b_gu_perm, w_down, b_down):
    n_blocks = block_expert.shape[0]
    d = D_MODEL
    f2 = w_gate_up.shape[-1]
    blk_rows = EXPERT_BLOCK * SLAB

    idx = jnp.arange(n_blocks, dtype=jnp.int32)
    used = idx < n_used
    prev = jnp.concatenate([jnp.full((1,), -1, jnp.int32), block_expert[:-1]])
    first = (used & (block_expert != prev)).astype(jnp.int32)
    slot = (jnp.cumsum(first) - 1) % 2
    later = used[None, :] & (block_expert[None, :] > block_expert[:, None])
    nxt = jnp.min(jnp.where(later, block_expert[None, :], N_EXPERTS), axis=1)
    nxt = jnp.where(nxt == N_EXPERTS, -1, nxt)

    grid_spec = pltpu.PrefetchScalarGridSpec(
        num_scalar_prefetch=5,
        grid=(n_blocks,),
        in_specs=[
            pl.BlockSpec((blk_rows, LANES), lambda b, be, *_: (b, 0)),
            pl.BlockSpec(memory_space=pl.ANY),
            pl.BlockSpec((1, 1, f2), lambda b, be, *_: (be[b], 0, 0)),
            pl.BlockSpec(memory_space=pl.ANY),
            pl.BlockSpec((1, 1, d), lambda b, be, *_: (be[b], 0, 0)),
        ],
        out_specs=pl.BlockSpec((blk_rows, LANES), lambda b, be, *_: (b, 0)),
        scratch_shapes=[
            pltpu.VMEM((2, d, f2), F32),
            pltpu.VMEM((2, f2 // 2, d), F32),
            pltpu.VMEM((d, f2), BF16),
            pltpu.VMEM((f2 // 2, d), BF16),
            pltpu.SemaphoreType.DMA((2, 2)),
        ],
    )
    return pl.pallas_call(
        _expert_kernel,
        grid_spec=grid_spec,
        out_shape=jax.ShapeDtypeStruct((n_blocks * blk_rows, LANES), F32),
        compiler_params=pltpu.CompilerParams(
            dimension_semantics=("arbitrary",), vmem_limit_bytes=VMEM_LIMIT),
        name="expert",
    )(block_expert, n_used, first, slot.astype(jnp.int32), nxt.astype(jnp.int32),
      xs, w_gate_up, b_gu_perm, w_down, b_down)


def _combine_kernel(dest_ref, dest_next_ref, ys_hbm, w_ref, x1_ref, mod_ref, g_ref, o_ref, buf, sem):
    i = pl.program_id(0)
    n_steps = pl.num_programs(0)
    tm = x1_ref.shape[0]
    n_assign = tm * TOP_K
    slot = i % 2

    def gather(d_ref, to_slot):
        def body(g, carry):
            for u in range(DMA_UNROLL):
                t_local = g * (DMA_UNROLL // TOP_K) + u // TOP_K
                k = u % TOP_K
                _row_copy(ys_hbm, d_ref[g * DMA_UNROLL + u], buf.at[to_slot], k * tm + t_local,
                          sem.at[to_slot]).start(priority=u % 2)
            return carry

        lax.fori_loop(0, n_assign // DMA_UNROLL, body, 0)

    @pl.when(i == 0)
    def _():
        gather(dest_ref, 0)

    @pl.when(i + 1 < n_steps)
    def _():
        gather(dest_next_ref, 1 - slot)

    cur = buf.at[slot]
    pltpu.make_async_copy(ys_hbm.at[pl.ds(0, n_assign * SLAB), :], cur, sem.at[slot]).wait()

    w = w_ref[...]
    y = w[:, 0:1] * _from_slabs(cur, 0, tm)
    for k in range(1, TOP_K):
        y = y + w[:, k:k + 1] * _from_slabs(cur, k * tm, tm)
    x = x1_ref[...] + mod_ref[0, 5:6, :] * y
    ms = jnp.mean(x * x, axis=-1, keepdims=True)
    o_ref[...] = x * lax.rsqrt(ms + EPS) * g_ref[...]


def _combine(dest_flat, ys, wts, x1, mod6, g, seq):
    t = x1.shape[0]
    tm = 256
    per_b = seq // tm
    d = D_MODEL
    last = t // tm - 1
    return pl.pallas_call(
        _combine_kernel,
        grid=(t // tm,),
        in_specs=[
            pl.BlockSpec((tm * TOP_K,), lambda i: (i,), memory_space=pltpu.SMEM),
            pl.BlockSpec((tm * TOP_K,), lambda i: (jnp.minimum(i + 1, last),), memory_space=pltpu.SMEM),
            pl.BlockSpec(memory_space=pl.ANY),
            pl.BlockSpec((tm, LANES), lambda i: (i, 0)),
            pl.BlockSpec((tm, d), lambda i: (i, 0)),
            pl.BlockSpec((1, 6, d), lambda i: (i // per_b, 0, 0)),
            pl.BlockSpec((1, d), lambda i: (0, 0)),
        ],
        out_specs=pl.BlockSpec((tm, d), lambda i: (i, 0)),
        out_shape=jax.ShapeDtypeStruct((t, d), F32),
        scratch_shapes=[pltpu.VMEM((2, tm * TOP_K * SLAB, LANES), F32), pltpu.SemaphoreType.DMA((2,))],
        compiler_params=pltpu.CompilerParams(
            dimension_semantics=("arbitrary",), vmem_limit_bytes=VMEM_LIMIT),
        name="combine",
    )(dest_flat, dest_flat, ys, wts, x1, mod6, g)


def _split_w(w):
    hi = w.astype(BF16)
    lo = (w - hi.astype(F32)).astype(BF16)
    return hi, lo


def kernel(x, c, positions, w_ada, b_ada, norm_mix_g, w_in, conv_w, conv_b, b_if, ml_norm_g, ret_norm_g,
           w_branch_ml, w_branch_ret, w_out, norm_ffn_g, w_router, b_router, w_gate_up, b_gate_up, w_down,
           b_down, norm_final_g):
    batch, seq, d = x.shape
    t = batch * seq
    x2 = x.reshape(t, d)
    depth = w_ada.shape[0]
    c8 = jnp.concatenate([c, jnp.zeros((8 - batch, d), c.dtype)], axis=0)

    half = D_QK // 2
    inv = ROPE_BASE ** (-jnp.arange(half, dtype=F32) / half)
    inv2 = jnp.concatenate([inv, inv])[None, :]
    sign2 = jnp.concatenate([-jnp.ones((half,), F32), jnp.ones((half,), F32)])[None, :]
    pos_b = jnp.broadcast_to(positions.astype(F32).reshape(t, 1), (t, LANES))
    L = CHUNK
    log_gamma = jnp.log(1.0 - 2.0 ** (-5.0 - jnp.arange(N_HEADS, dtype=F32)))
    pos = jnp.arange(L, dtype=F32)
    rel = pos[:, None] - pos[None, :]
    decay = jnp.where(rel >= 0, jnp.exp(log_gamma[:, None, None] * jnp.maximum(rel, 0.0)), 0.0)
    kdec = jnp.exp(log_gamma[:, None] * (L - 1 - pos))[:, :, None]
    qdec = jnp.exp(log_gamma[:, None] * (pos + 1.0))[:, :, None]
    gch = jnp.broadcast_to(jnp.exp(log_gamma * L)[:, None, None], (N_HEADS, 1, LANES))

    assert depth == 1, "the final norm is fused after the single layer"
    l = 0
    wa_hi, wa_lo = _split_w(w_ada[l])
    mod = _ada(c8, wa_hi, wa_lo, b_ada[l][None, :])
    mod6 = mod[:batch].reshape(batch, 6, d)

    w = w_in[l]
    assert w.shape[1] == PROJ_W + 2 * N_HEADS
    w_main = _wprep(w)
    w_if = w[:, GATE_COL:GATE_COL + 2 * N_HEADS]
    zpad = jnp.zeros((d, LANES - N_HEADS), F32)
    wif = jnp.concatenate([w_if[:, :N_HEADS], zpad, w_if[:, N_HEADS:], zpad], axis=1)
    wif_hi, wif_lo = _split_w(wif)
    zb = jnp.zeros((LANES - N_HEADS,), F32)
    b_if2 = jnp.concatenate([b_if[l][:N_HEADS], zb, b_if[l][N_HEADS:], zb])[None, :]

    proj, gates, cos2, sin2 = _inproj(x2, mod6, norm_mix_g[l][None, :], w_main, wif_hi, wif_lo,
                                      pos_b, inv2, sign2, seq)
    hm, hr = _mix(proj, gates, cos2, sin2, conv_w[l], conv_b[l][None, :], b_if2,
                  ml_norm_g[l][None, :], ret_norm_g[l][None, :], decay, kdec, qdec, gch, batch, seq)

    wr = jnp.concatenate([w_router[l], jnp.zeros((d, LANES - N_EXPERTS), F32)], axis=1)
    wr_hi, wr_lo = _split_w(wr)
    br = jnp.concatenate([b_router[l], jnp.full((LANES - N_EXPERTS,), -1e30, F32)])[None, :]
    x1, hf, logits = _post(hm, hr, proj, x2, mod6, w_branch_ml[l].astype(BF16), w_branch_ret[l].astype(BF16),
                           w_out[l].astype(BF16), norm_ffn_g[l][None, :], wr_hi, wr_lo, br, seq)

    n_assign = t * TOP_K
    n_blocks = -(-n_assign // EXPERT_BLOCK) + N_EXPERTS
    dest, wts, tables = _route(logits)
    dest_flat = dest[:, :TOP_K].reshape(n_assign)
    bend = tables[0, :N_EXPERTS].astype(jnp.int32)
    pad_first = tables[1, :N_EXPERTS].astype(jnp.int32)
    n_used = bend[N_EXPERTS - 1:]
    blk = jnp.minimum(jnp.arange(n_blocks, dtype=jnp.int32), n_used - 1)
    block_expert = jnp.minimum(
        jnp.sum((bend[None, :] <= blk[:, None]).astype(jnp.int32), axis=1), N_EXPERTS - 1)
    xs = _dispatch(dest_flat, jnp.concatenate([pad_first, n_used]), hf, n_blocks)
    f2 = w_gate_up.shape[-1]
    bgu = b_gate_up[l].reshape(N_EXPERTS, f2 // (2 * LANES), LANES, 2)
    bgu = jnp.swapaxes(bgu, -1, -2).reshape(N_EXPERTS, 1, f2)
    ys = _experts(block_expert, n_used, xs, w_gate_up[l], bgu, w_down[l], b_down[l][:, None, :])
    out = _combine(dest_flat, ys, wts, x1, mod6, norm_final_g[None, :], seq)
    return out.reshape(batch, seq, d)
```

```python
import functools

import numpy as np
import jax
import jax.numpy as jnp
from jax import lax
from jax.experimental import pallas as pl
from jax.experimental.pallas import tpu as pltpu

D_MODEL = 1024
N_HEADS = 4
D_QK = 128
D_V = 256
CONV_K = 4
ROPE_BASE = 10000.0
CHUNK = 128
N_EXPERTS = 32
TOP_K = 4
SWIGLU_LIMIT = 7.0
SWIGLU_ALPHA = 1.702
EXPERT_BLOCK = 512
EPS = 1e-5

QK_W = N_HEADS * D_QK
V_W = N_HEADS * D_V
LANES = 128
GATE_W = 2 * LANES
V_EXT = D_V + LANES
MIX_GROUP = 4
SLAB = D_MODEL // LANES

C_MLQ, C_MLK, C_MLV, C_MLO = 0, 512, 1024, 2048
C_RQ, C_RK, C_RV, C_RG = 3072, 3584, 4096, 5120
C_GML, C_GRET = 6144, 7168
MIX_W = 6144
PROJ_W = 8192

BF16 = jnp.bfloat16
F32 = jnp.float32
VMEM_LIMIT = 56 * 1024 * 1024


def _dot(a, b):
    return jnp.dot(a, b, preferred_element_type=F32)


def _dot_nt(a, b):
    return lax.dot_general(a, b, (((1,), (1,)), ((), ())), preferred_element_type=F32)


def _split(a):
    hi = a.astype(BF16)
    lo = (a - hi.astype(F32)).astype(BF16)
    return hi, lo


def _dot3(a, b_hi, b_lo):
    a_hi, a_lo = _split(a)
    return _dot(a_hi, b_hi) + (_dot(a_lo, b_hi) + _dot(a_hi, b_lo))


def _sigmoid(x):
    return 0.5 * jnp.tanh(0.5 * x) + 0.5


def _sigmoid_mul(x, y):
    hy = 0.5 * y
    return hy * jnp.tanh(0.5 * x) + hy


def _silu(x):
    hx = 0.5 * x
    return hx * jnp.tanh(hx) + hx


def _round_robin(stage_generators):
    pending = list(stage_generators)
    while pending:
        for gen in list(pending):
            if next(gen, True):
                pending.remove(gen)


def _rms_mod(x, g, scale, shift):
    ms = jnp.mean(x * x, axis=-1, keepdims=True)
    return (x * lax.rsqrt(ms + EPS) * g) * (1.0 + scale) + shift


def _ada_kernel(c_ref, whi_ref, wlo_ref, b_ref, o_ref):
    c = c_ref[...]
    ca = _silu(c)
    o_ref[...] = _dot3(ca, whi_ref[...], wlo_ref[...]) + b_ref[...]


def _ada(c8, w_hi, w_lo, b):
    n = w_hi.shape[1]
    tn = 1024
    return pl.pallas_call(
        _ada_kernel,
        grid=(n // tn,),
        in_specs=[
            pl.BlockSpec((8, D_MODEL), lambda j: (0, 0)),
            pl.BlockSpec((D_MODEL, tn), lambda j: (0, j)),
            pl.BlockSpec((D_MODEL, tn), lambda j: (0, j)),
            pl.BlockSpec((1, tn), lambda j: (0, j)),
        ],
        out_specs=pl.BlockSpec((8, tn), lambda j: (0, j)),
        out_shape=jax.ShapeDtypeStruct((8, n), F32),
        name="ada",
    )(c8, w_hi, w_lo, b)


def _inproj_kernel(x_ref, xn_ref, mod_ref, modn_ref, g_ref, w_ref, wif_hi_ref, wif_lo_ref, pos_ref, inv_ref,
                   sign_ref, proj_ref, gates_ref, cos_ref, sin_ref, hi_a, lo_a, hi_b, lo_b):
    i = pl.program_id(0)
    j = pl.program_id(1)

    def normalise(x, mod):
        return _split(_rms_mod(x, g_ref[...], mod[0, 1:2, :], mod[0, 0:1, :]))

    @pl.when(jnp.logical_and(i == 0, j == 0))
    def _():
        hi_a[...], lo_a[...] = normalise(x_ref[...], mod_ref)

    n = x_ref.shape[0] // pl.num_programs(1)
    rows = pl.ds(pl.multiple_of(j * n, n), n)
    n_split = 4
    tc = w_ref.shape[1] // n_split

    def step(cur_hi, cur_lo, nxt_hi, nxt_lo):
        def matmul_steps():
            for c in range(n_split):
                proj_ref[:, c * tc:(c + 1) * tc] = _dot(cur_hi[...], w_ref[:, c * tc:(c + 1) * tc]).astype(BF16)
                yield False

        def side_steps():
            h_hi = cur_hi[rows, :]
            h_lo = cur_lo[rows, :]
            gates_ref[rows, :] = (_dot(h_hi, wif_hi_ref[...])
                                  + (_dot(h_lo, wif_hi_ref[...]) + _dot(h_hi, wif_lo_ref[...])))
            yield False
            ang = pos_ref[rows, :] * inv_ref[...]
            cos_ref[rows, :] = jnp.cos(ang)
            yield False
            sin_ref[rows, :] = jnp.sin(ang) * sign_ref[...]
            yield False
            nxt_hi[rows, :], nxt_lo[rows, :] = normalise(xn_ref[rows, :], modn_ref)
            yield False

        _round_robin([matmul_steps(), side_steps()])

    @pl.when(i % 2 == 0)
    def _():
        step(hi_a, lo_a, hi_b, lo_b)

    @pl.when(i % 2 == 1)
    def _():
        step(hi_b, lo_b, hi_a, lo_a)


GATE_COL = 3072


def _wprep_kernel(a_ref, b_ref, o_ref, gate_ref):
    c = pl.program_id(0)
    tn = a_ref.shape[0]

    @pl.when(c * tn == GATE_COL)
    def _():
        gate_ref[...] = a_ref[0:2 * N_HEADS, :]

    @pl.when((c + 1) * tn <= GATE_COL)
    def _():
        o_ref[...] = a_ref[...].T.astype(BF16)

    @pl.when((c + 1) * tn > GATE_COL)
    def _():
        shifted = jnp.concatenate([a_ref[2 * N_HEADS:, :], b_ref[...]], axis=0)
        o_ref[...] = shifted.T.astype(BF16)


def _wprep(w_t):
    d = w_t.shape[1]
    tn = 1024
    gate = 2 * N_HEADS
    assert GATE_COL % tn == 0 and w_t.shape[0] == PROJ_W + gate
    return pl.pallas_call(
        _wprep_kernel,
        grid=(PROJ_W // tn,),
        in_specs=[
            pl.BlockSpec((tn, d), lambda c: (c, 0)),
            pl.BlockSpec((gate, d), lambda c: ((c + 1) * (tn // gate), 0)),
        ],
        out_specs=[pl.BlockSpec((d, tn), lambda c: (0, c)), pl.BlockSpec((gate, d), lambda c: (0, 0))],
        out_shape=[jax.ShapeDtypeStruct((d, PROJ_W), BF16), jax.ShapeDtypeStruct((gate, d), F32)],
        compiler_params=pltpu.CompilerParams(
            dimension_semantics=("arbitrary",), vmem_limit_bytes=VMEM_LIMIT),
        name="wprep",
    )(w_t, w_t)


def _inproj(x2, mod6, g, w_main, wif_hi, wif_lo, pos_b, inv2, sign2, seq):
    t = x2.shape[0]
    tm, tn = 1024, 2048
    per_b = seq // tm
    last = t // tm - 1
    nxt = lambda i: jnp.minimum(i + 1, last)
    return pl.pallas_call(
        _inproj_kernel,
        grid=(t // tm, PROJ_W // tn),
        in_specs=[
            pl.BlockSpec((tm, D_MODEL), lambda i, j: (i, 0)),
            pl.BlockSpec((tm, D_MODEL), lambda i, j: (nxt(i), 0)),
            pl.BlockSpec((1, 6, D_MODEL), lambda i, j: (i // per_b, 0, 0)),
            pl.BlockSpec((1, 6, D_MODEL), lambda i, j: (nxt(i) // per_b, 0, 0)),
            pl.BlockSpec((1, D_MODEL), lambda i, j: (0, 0)),
            pl.BlockSpec((D_MODEL, tn), lambda i, j: (0, j)),
            pl.BlockSpec((D_MODEL, GATE_W), lambda i, j: (0, 0)),
            pl.BlockSpec((D_MODEL, GATE_W), lambda i, j: (0, 0)),
            pl.BlockSpec((tm, LANES), lambda i, j: (i, 0)),
            pl.BlockSpec((1, LANES), lambda i, j: (0, 0)),
            pl.BlockSpec((1, LANES), lambda i, j: (0, 0)),
        ],
        out_specs=[
            pl.BlockSpec((tm, tn), lambda i, j: (i, j)),
            pl.BlockSpec((tm, GATE_W), lambda i, j: (i, 0)),
            pl.BlockSpec((tm, LANES), lambda i, j: (i, 0)),
            pl.BlockSpec((tm, LANES), lambda i, j: (i, 0)),
        ],
        out_shape=[
            jax.ShapeDtypeStruct((t, PROJ_W), BF16),
            jax.ShapeDtypeStruct((t, GATE_W), F32),
            jax.ShapeDtypeStruct((t, LANES), F32),
            jax.ShapeDtypeStruct((t, LANES), F32),
        ],
        scratch_shapes=[pltpu.VMEM((tm, D_MODEL), BF16)] * 4,
        compiler_params=pltpu.CompilerParams(
            dimension_semantics=("arbitrary", "arbitrary"), vmem_limit_bytes=VMEM_LIMIT),
        name="inproj",
    )(x2, x2, mod6, mod6, g, w_main, wif_hi, wif_lo, pos_b, inv2, sign2)


def _row_mean(a):
    inv_n = jnp.full((a.shape[1], LANES), 1.0 / a.shape[1], BF16)
    m = _dot(a.astype(BF16), inv_n)
    return jnp.concatenate([m] * (a.shape[1] // LANES), axis=1)


def _head_norm(h, g):
    d = h - _row_mean(h)
    return d * lax.rsqrt(_row_mean(d * d) + EPS) * g


def _mix_kernel(proj_ref, gates_ref, cos_ref, sin_ref, convw_ref, convb_ref, bif_ref, mlg_ref, retg_ref,
                decay_ref, kdec_ref, qdec_ref, gch_ref,
                hm_ref, hr_ref,
                prev_ref, c_st, m_st, r_st):
    group = proj_ref.shape[0]

    @pl.when(pl.program_id(1) == 0)
    def _():
        def zero_state(i, carry):
            c_st[i] = jnp.zeros(c_st.shape[1:], F32)
            r_st[i] = jnp.zeros(r_st.shape[1:], F32)
            return carry

        lax.fori_loop(0, group * N_HEADS, zero_state, 0)
        prev_ref[...] = jnp.zeros_like(prev_ref)
        m_st[...] = jnp.zeros_like(m_st)

    steps = []
    for gb in range(group):
        states = slice(gb * N_HEADS, (gb + 1) * N_HEADS)
        steps.append(_mlstm_steps(proj_ref.at[gb], gates_ref.at[gb], convw_ref, convb_ref, bif_ref, mlg_ref,
                                  hm_ref.at[gb], prev_ref.at[gb], c_st.at[states], m_st.at[gb]))
        steps.append(_retention_steps(proj_ref.at[gb], cos_ref.at[gb], sin_ref.at[gb], retg_ref, decay_ref,
                                      kdec_ref, qdec_ref, gch_ref, hr_ref.at[gb], r_st.at[states]))
    _round_robin(steps)


def _mlstm_steps(proj_ref, gates_ref, convw_ref, convb_ref, bif_ref, mlg_ref, hm_ref, prev_ref, c_st, m_st):
    L = CHUNK
    rows = lax.broadcasted_iota(jnp.int32, (L, L), 0)
    cols = lax.broadcasted_iota(jnp.int32, (L, L), 1)
    causal = rows >= cols
    tril = jnp.where(causal, 1.0, 0.0).astype(BF16)

    cur = proj_ref[:, C_MLQ:C_MLQ + 2 * QK_W]
    xx = jnp.concatenate([prev_ref[...], cur], axis=0)
    r2 = lax.broadcasted_iota(jnp.int32, (L, 2 * L), 0)
    c2 = lax.broadcasted_iota(jnp.int32, (L, 2 * L), 1)
    acc = convb_ref[...] + cur.astype(F32) * convw_ref[CONV_K - 1:CONV_K, :]
    for d in range(1, CONV_K):
        shift = jnp.where(c2 == r2 + (L - d), 1.0, 0.0).astype(BF16)
        acc = acc + _dot(shift, xx) * convw_ref[CONV_K - 1 - d:CONV_K - d, :]
        yield False
    prev_ref[...] = cur
    qk = _silu(acc)
    yield False

    g = gates_ref[...] + bif_ref[...]
    gi = g[:, :LANES]
    gf = g[:, LANES:]
    lf = jnp.minimum(gf, 0.0) - jnp.log(1.0 + jnp.exp(-jnp.abs(gf)))
    lf_hi, lf_lo = _split(lf)
    a_all = _dot(tril, lf_hi) + _dot(tril, lf_lo)
    yield False
    a_last = a_all[L - 1:L, :]
    bm = gi - a_all
    bm_t = bm.T
    w_state = a_last + bm
    m_loc = jnp.max(w_state, axis=0, keepdims=True)
    m_prev = m_st[...]
    inter_log = a_all + m_prev
    m_new = jnp.maximum(a_last + m_prev, m_loc)
    s_prev = jnp.exp(a_last + m_prev - m_new)
    s_loc = jnp.exp(m_loc - m_new)
    ws_all = jnp.exp(w_state - m_loc) * s_loc
    m_st[...] = m_new
    yield False

    ones_blk = jnp.ones((L, LANES), BF16)
    q_scale = D_QK ** -0.5

    for h in range(N_HEADS):
        q = (qk[:, h * D_QK:(h + 1) * D_QK] * q_scale).astype(BF16)
        k_f = qk[:, QK_W + h * D_QK:QK_W + (h + 1) * D_QK]
        k = k_f.astype(BF16)
        v_ext = jnp.concatenate([proj_ref[:, C_MLV + h * D_V:C_MLV + (h + 1) * D_V], ones_blk], axis=1)
        dlog = jnp.where(causal, a_all[:, h:h + 1] + bm_t[h:h + 1, :], -jnp.inf)
        m_intra = jnp.max(dlog, axis=-1, keepdims=True)
        s = _dot_nt(q, k)
        qc = _dot(q, c_st[h].astype(BF16))
        yield False
        il = inter_log[:, h:h + 1]
        m_t = jnp.maximum(il, m_intra)
        p = jnp.exp(dlog - m_t) * s
        isc = jnp.exp(il - m_t)
        tot = _dot(p.astype(BF16), v_ext) + isc * qc
        yield False
        den = tot[:, D_V:D_V + 1]
        hout = tot[:, :D_V] / jnp.maximum(jnp.abs(den), jnp.exp(-m_t))
        kw_t = (k_f * ws_all[:, h:h + 1]).T.astype(BF16)
        c_st[h] = s_prev[:, h:h + 1] * c_st[h] + _dot(kw_t, v_ext)
        yield False
        y = _head_norm(hout, mlg_ref[:, h * D_V:(h + 1) * D_V])
        o = proj_ref[:, C_MLO + h * D_V:C_MLO + (h + 1) * D_V].astype(F32)
        hm_ref[:, h * D_V:(h + 1) * D_V] = _sigmoid_mul(o, y).astype(BF16)
        yield False


def _retention_steps(proj_ref, cos_ref, sin_ref, retg_ref, decay_ref, kdec_ref, qdec_ref, gch_ref, hr_ref, r_st):
    cos2 = cos_ref[...]
    sin2 = sin_ref[...]
    k_scale = D_QK ** -0.5
    for h in range(N_HEADS):
        q_raw = proj_ref[:, C_RQ + h * D_QK:C_RQ + (h + 1) * D_QK].astype(F32)
        k_raw = proj_ref[:, C_RK + h * D_QK:C_RK + (h + 1) * D_QK].astype(F32)
        q = (q_raw * cos2 + pltpu.roll(q_raw, D_QK // 2, 1) * sin2).astype(BF16)
        k_f = (k_raw * cos2 + pltpu.roll(k_raw, D_QK // 2, 1) * sin2) * k_scale
        v = proj_ref[:, C_RV + h * D_V:C_RV + (h + 1) * D_V]
        yield False
        sc = _dot_nt(q, k_f.astype(BF16)) * decay_ref[h]
        hret = _dot(sc.astype(BF16), v) + _dot(q, r_st[h].astype(BF16)) * qdec_ref[h]
        yield False
        kd_t = (k_f * kdec_ref[h]).T.astype(BF16)
        r_st[h] = gch_ref[h][:, 0:1] * r_st[h] + _dot(kd_t, v)
        yield False
        y = _head_norm(hret, retg_ref[:, h * D_V:(h + 1) * D_V])
        gt = proj_ref[:, C_RG + h * D_V:C_RG + (h + 1) * D_V].astype(F32)
        hr_ref[:, h * D_V:(h + 1) * D_V] = (_silu(gt) * y).astype(BF16)
        yield False


def _mix(proj, gates, cos2, sin2, conv_w, conv_b, b_if2, ml_g, ret_g, decay, kdec, qdec, gch, batch, seq):
    t = proj.shape[0]
    nc = seq // CHUNK
    L = CHUNK
    G = MIX_GROUP
    proj = proj.reshape(batch, seq, PROJ_W)
    gates = gates.reshape(batch, seq, GATE_W)
    cos2 = cos2.reshape(batch, seq, LANES)
    sin2 = sin2.reshape(batch, seq, LANES)
    full = lambda shape: pl.BlockSpec(shape, lambda b, c: (0,) * len(shape))
    hm, hr = pl.pallas_call(
        _mix_kernel,
        grid=(batch // G, nc),
        in_specs=[
            pl.BlockSpec((G, L, MIX_W), lambda b, c: (b, c, 0)),
            pl.BlockSpec((G, L, GATE_W), lambda b, c: (b, c, 0)),
            pl.BlockSpec((G, L, LANES), lambda b, c: (b, c, 0)),
            pl.BlockSpec((G, L, LANES), lambda b, c: (b, c, 0)),
            full((CONV_K, 2 * QK_W)),
            full((1, 2 * QK_W)),
            full((1, GATE_W)),
            full((1, V_W)),
            full((1, V_W)),
            full((N_HEADS, L, L)),
            full((N_HEADS, L, 1)),
            full((N_HEADS, L, 1)),
            full((N_HEADS, 1, LANES)),
        ],
        out_specs=[
            pl.BlockSpec((G, L, V_W), lambda b, c: (b, c, 0)),
            pl.BlockSpec((G, L, V_W), lambda b, c: (b, c, 0)),
        ],
        out_shape=[jax.ShapeDtypeStruct((batch, seq, V_W), BF16), jax.ShapeDtypeStruct((batch, seq, V_W), BF16)],
        scratch_shapes=[
            pltpu.VMEM((G, L, 2 * QK_W), BF16),
            pltpu.VMEM((G * N_HEADS, D_QK, V_EXT), F32),
            pltpu.VMEM((G, 1, LANES), F32),
            pltpu.VMEM((G * N_HEADS, D_QK, D_V), F32),
        ],
        compiler_params=pltpu.CompilerParams(
            dimension_semantics=("arbitrary", "arbitrary"), vmem_limit_bytes=VMEM_LIMIT),
        name="mix",
    )(proj, gates, cos2, sin2, conv_w, conv_b, b_if2, ml_g, ret_g, decay, kdec, qdec, gch)
    return hm.reshape(t, V_W), hr.reshape(t, V_W)


def _to_slabs(ref, val):
    rows = val.shape[0]
    for s in range(SLAB):
        ref[pl.ds(s, rows, stride=SLAB), :] = val[:, s * LANES:(s + 1) * LANES]


def _from_slabs(ref, first_slab, rows):
    return jnp.concatenate(
        [ref[pl.ds(first_slab * SLAB + s, rows, stride=SLAB), :] for s in range(SLAB)], axis=1)


def _post_kernel(hm_ref, hr_ref, gm_ref, gr_ref, x_ref, mod_ref, wbm_ref, wbr_ref, wout_ref, g_ref,
                 wr_hi_ref, wr_lo_ref, br_ref, x1_ref, hf_ref, logit_ref):
    n_sub = 2
    sub = x_ref.shape[0] // n_sub

    def steps(s):
        r = pl.ds(s * sub, sub)
        ym = _dot(hm_ref[r, :], wbm_ref[...])
        yield False
        yr = _dot(hr_ref[r, :], wbr_ref[...])
        yield False
        y = _sigmoid_mul(gm_ref[r, :].astype(F32), ym) + _sigmoid_mul(gr_ref[r, :].astype(F32), yr)
        o = _dot(y.astype(BF16), wout_ref[...])
        yield False
        x1 = x_ref[r, :] + mod_ref[0, 2:3, :] * o
        x1_ref[r, :] = x1
        hf = _rms_mod(x1, g_ref[...], mod_ref[0, 4:5, :], mod_ref[0, 3:4, :])
        yield False
        _to_slabs(hf_ref.at[pl.ds(s * sub * SLAB, sub * SLAB)], hf)
        logit_ref[r, :] = _dot3(hf, wr_hi_ref[...], wr_lo_ref[...]) + br_ref[...]
        yield False

    _round_robin([steps(s) for s in range(n_sub)])


def _post(hm, hr, proj, x2, mod6, wbm, wbr, wout, g, wr_hi, wr_lo, br, seq):
    t = x2.shape[0]
    tm = 512
    per_b = seq // tm
    d = D_MODEL
    const = lambda shape: pl.BlockSpec(shape, lambda i: (0,) * len(shape))
    return pl.pallas_call(
        _post_kernel,
        grid=(t // tm,),
        in_specs=[
            pl.BlockSpec((tm, d), lambda i: (i, 0)),
            pl.BlockSpec((tm, d), lambda i: (i, 0)),
            pl.BlockSpec((tm, d), lambda i: (i, C_GML // d)),
            pl.BlockSpec((tm, d), lambda i: (i, C_GRET // d)),
            pl.BlockSpec((tm, d), lambda i: (i, 0)),
            pl.BlockSpec((1, 6, d), lambda i: (i // per_b, 0, 0)),
            const((d, d)), const((d, d)), const((d, d)),
            const((1, d)),
            const((d, LANES)), const((d, LANES)), const((1, LANES)),
        ],
        out_specs=[
            pl.BlockSpec((tm, d), lambda i: (i, 0)),
            pl.BlockSpec((tm * SLAB, LANES), lambda i: (i, 0)),
            pl.BlockSpec((tm, LANES), lambda i: (i, 0)),
        ],
        out_shape=[
            jax.ShapeDtypeStruct((t, d), F32),
            jax.ShapeDtypeStruct((t * SLAB, LANES), F32),
            jax.ShapeDtypeStruct((t, LANES), F32),
        ],
        compiler_params=pltpu.CompilerParams(
            dimension_semantics=("arbitrary",), vmem_limit_bytes=VMEM_LIMIT),
        name="post",
    )(hm, hr, proj, proj, x2, mod6, wbm, wbr, wout, g, wr_hi, wr_lo, br)


def _route_kernel(logit_ref, dest_ref, w_ref, bend_ref, cnt_st, pad_st, sel_st, pick_st, wts_st):
    ph = pl.program_id(0)
    i = pl.program_id(1)
    tm = logit_ref.shape[0]
    rows = pl.ds(pl.multiple_of(i * tm, tm), tm)
    lane = lax.broadcasted_iota(jnp.int32, (tm, LANES), 1)
    lane_f = lane.astype(F32)

    @pl.when(jnp.logical_and(ph == 0, i == 0))
    def _():
        cnt_st[...] = jnp.zeros_like(cnt_st)

    @pl.when(ph == 0)
    def _():
        l = logit_ref[...]
        picks, vals = [], []
        sel = jnp.zeros((tm, LANES), F32)
        for _ in range(TOP_K):
            m = jnp.max(l, axis=-1, keepdims=True)
            idx = jnp.min(jnp.where(l == m, lane_f, float(LANES)), axis=-1, keepdims=True)
            oh = lane_f == idx
            picks.append(idx)
            vals.append(m)
            sel = jnp.where(oh, 1.0, sel)
            l = jnp.where(oh, -jnp.inf, l)
        ex = [jnp.exp(v - vals[0]) for v in vals]
        den = ex[0] + ex[1] + ex[2] + ex[3]
        pick = jnp.zeros((tm, LANES), F32)
        wts = jnp.zeros((tm, LANES), F32)
        for k in range(TOP_K):
            pick = jnp.where(lane == k, picks[k], pick)
            wts = jnp.where(lane == k, ex[k] / den, wts)
        sel_st[rows, :] = sel.astype(BF16)
        pick_st[rows, :] = pick
        wts_st[rows, :] = wts
        cnt_st[...] = cnt_st[...] + jnp.sum(sel, axis=0, keepdims=True)

    @pl.when(jnp.logical_and(ph == 1, i == 0))
    def _():
        blocks = jnp.floor((cnt_st[...] + (EXPERT_BLOCK - 1)) * (1.0 / EXPERT_BLOCK))
        r = lax.broadcasted_iota(jnp.int32, (LANES, LANES), 0)
        c = lax.broadcasted_iota(jnp.int32, (LANES, LANES), 1)
        upper = jnp.where(r < c, 1.0, 0.0).astype(BF16)
        blocks8 = jnp.broadcast_to(blocks, (8, LANES))
        excl = _dot(blocks8.astype(BF16), upper)
        pad_st[...] = excl[0:1, :] * EXPERT_BLOCK
        row = lax.broadcasted_iota(jnp.int32, (8, LANES), 0)
        bend_ref[...] = jnp.where(row == 0, excl + blocks8,
                                  jnp.where(row == 1, excl * EXPERT_BLOCK + cnt_st[...], 0.0))
        cnt_st[...] = jnp.zeros_like(cnt_st)

    @pl.when(ph == 1)
    def _():
        carry = cnt_st[...]
        sel = sel_st[rows, :]
        pick = pick_st[rows, :]
        r = lax.broadcasted_iota(jnp.int32, (tm, tm), 0)
        c = lax.broadcasted_iota(jnp.int32, (tm, tm), 1)
        lower = jnp.where(r > c, 1.0, 0.0).astype(BF16)
        base = pad_st[...] + carry + _dot(lower, sel)
        dest = jnp.zeros((tm, LANES), F32)
        for k in range(TOP_K):
            dk = jnp.sum(jnp.where(lane_f == pick[:, k:k + 1], base, 0.0), axis=-1, keepdims=True)
            dest = jnp.where(lane == k, dk, dest)
        dest_ref[...] = dest.astype(jnp.int32)
        w_ref[...] = wts_st[rows, :]
        cnt_st[...] = carry + jnp.sum(sel.astype(F32), axis=0, keepdims=True)


def _route(logits):
    t = logits.shape[0]
    tm = 512
    return pl.pallas_call(
        _route_kernel,
        grid=(2, t // tm),
        in_specs=[pl.BlockSpec((tm, LANES), lambda ph, i: (i * (1 - ph), 0))],
        out_specs=[
            pl.BlockSpec((tm, LANES), lambda ph, i: (i * ph, 0)),
            pl.BlockSpec((tm, LANES), lambda ph, i: (i * ph, 0)),
            pl.BlockSpec((8, LANES), lambda ph, i: (0, 0)),
        ],
        out_shape=[
            jax.ShapeDtypeStruct((t, LANES), jnp.int32),
            jax.ShapeDtypeStruct((t, LANES), F32),
            jax.ShapeDtypeStruct((8, LANES), F32),
        ],
        scratch_shapes=[
            pltpu.VMEM((1, LANES), F32),
            pltpu.VMEM((1, LANES), F32),
            pltpu.VMEM((t, LANES), BF16),
            pltpu.VMEM((t, LANES), F32),
            pltpu.VMEM((t, LANES), F32),
        ],
        compiler_params=pltpu.CompilerParams(
            dimension_semantics=("arbitrary", "arbitrary"), vmem_limit_bytes=VMEM_LIMIT),
        name="route",
    )(logits)


DMA_UNROLL = 8


def _row_copy(src_ref, src_slab, dst_ref, dst_slab, sem):
    src = pl.multiple_of(src_slab * SLAB, SLAB)
    dst = pl.multiple_of(dst_slab * SLAB, SLAB)
    return pltpu.make_async_copy(src_ref.at[pl.ds(src, SLAB), :], dst_ref.at[pl.ds(dst, SLAB), :], sem)


def _dispatch_kernel(dest_ref, padfirst_ref, hf_hbm, xs_out, tiles, zbuf, lsem, rsem, zsem):
    i = pl.program_id(0)
    n_steps = pl.num_programs(0)
    n_assign = dest_ref.shape[0]
    tile_rows = tiles.shape[1]
    blk_rows = zbuf.shape[0]
    n_buf = tiles.shape[0]

    def tile_load(step, slot):
        start = pl.multiple_of(step * tile_rows, tile_rows)
        return pltpu.make_async_copy(hf_hbm.at[pl.ds(start, tile_rows), :], tiles.at[slot], lsem.at[slot])

    def wait_rows(slot):
        for _ in range(TOP_K):
            pltpu.make_async_copy(tiles.at[slot], xs_out.at[pl.ds(0, tile_rows), :], rsem.at[slot]).wait()

    @pl.when(i == 0)
    def _():
        tile_load(0, 0).start()
        tile_load(1, 1).start()
        zbuf[...] = jnp.zeros_like(zbuf)
        for e in range(N_EXPERTS):
            start = pl.multiple_of(padfirst_ref[e] * SLAB, SLAB)
            pltpu.make_async_copy(zbuf, xs_out.at[pl.ds(start, blk_rows), :], zsem).start()
        for e in range(N_EXPERTS):
            pltpu.make_async_copy(zbuf, xs_out.at[pl.ds(0, blk_rows), :], zsem).wait()

        n_used = padfirst_ref[N_EXPERTS]
        n_total = xs_out.shape[0] // blk_rows

        def zero_start(b, carry):
            start = pl.multiple_of(b * blk_rows, blk_rows)
            pltpu.make_async_copy(zbuf, xs_out.at[pl.ds(start, blk_rows), :], zsem).start()
            return carry

        def zero_wait(b, carry):
            pltpu.make_async_copy(zbuf, xs_out.at[pl.ds(0, blk_rows), :], zsem).wait()
            return carry

        lax.fori_loop(n_used, n_total, zero_start, 0)
        lax.fori_loop(n_used, n_total, zero_wait, 0)

    slot = i % n_buf
    tile_load(i, slot).wait()
    tile = tiles.at[slot]

    def body(g, carry):
        for u in range(DMA_UNROLL):
            t_local = g * (DMA_UNROLL // TOP_K) + u // TOP_K
            _row_copy(tile, t_local, xs_out, dest_ref[g * DMA_UNROLL + u], rsem.at[slot]).start(priority=u % 2)
        return carry

    lax.fori_loop(0, n_assign // DMA_UNROLL, body, 0)

    @pl.when(i >= 1)
    def _():
        wait_rows((i + n_buf - 1) % n_buf)

    @pl.when(i + 2 < n_steps)
    def _():
        tile_load(i + 2, (i + 2) % n_buf).start()

    @pl.when(i == n_steps - 1)
    def _():
        wait_rows(slot)


def _dispatch(dest_flat, pad_first, hf_slab, n_blocks):
    n_assign = dest_flat.shape[0]
    tm = 512
    blk_rows = EXPERT_BLOCK * SLAB
    return pl.pallas_call(
        _dispatch_kernel,
        grid=(n_assign // (tm * TOP_K),),
        in_specs=[
            pl.BlockSpec((tm * TOP_K,), lambda i: (i,), memory_space=pltpu.SMEM),
            pl.BlockSpec(memory_space=pltpu.SMEM),
            pl.BlockSpec(memory_space=pl.ANY),
        ],
        out_specs=pl.BlockSpec(memory_space=pl.ANY),
        out_shape=jax.ShapeDtypeStruct(((n_blocks + 1) * blk_rows, LANES), F32),
        scratch_shapes=[
            pltpu.VMEM((3, tm * SLAB, LANES), F32),
            pltpu.VMEM((blk_rows, LANES), F32),
            pltpu.SemaphoreType.DMA((3,)),
            pltpu.SemaphoreType.DMA((3,)),
            pltpu.SemaphoreType.DMA,
        ],
        compiler_params=pltpu.CompilerParams(dimension_semantics=("arbitrary",), vmem_limit_bytes=VMEM_LIMIT),
        name="dispatch",
    )(dest_flat, pad_first, hf_slab)


def _weight_copies(wgu_hbm, wd_hbm, wgu_f32, wd_f32, wsem, expert, slot):
    return (pltpu.make_async_copy(wgu_hbm.at[expert], wgu_f32.at[slot], wsem.at[0, slot]),
            pltpu.make_async_copy(wd_hbm.at[expert], wd_f32.at[slot], wsem.at[1, slot]))


def _expert_kernel(be_ref, nused_ref, first_ref, slot_ref, next_ref,
                   xs_ref, wgu_hbm, bgu_ref, wd_hbm, bd_ref, ys_ref,
                   wgu_f32, wd_f32, wgu_bf, wd_bf, wsem):
    b = pl.program_id(0)
    copies = functools.partial(_weight_copies, wgu_hbm, wd_hbm, wgu_f32, wd_f32, wsem)

    @pl.when(jnp.logical_and(b < nused_ref[0], first_ref[b] == 1))
    def _():
        slot = slot_ref[b]

        @pl.when(b == 0)
        def _():
            for cp in copies(be_ref[0], 0):
                cp.start()

        for cp in copies(be_ref[b], slot):
            cp.wait()

        @pl.when(next_ref[b] >= 0)
        def _():
            for cp in copies(next_ref[b], 1 - slot):
                cp.start()

        two = 2 * LANES
        r = lax.broadcasted_iota(jnp.int32, (two, two), 0)
        c = lax.broadcasted_iota(jnp.int32, (two, two), 1)
        src = jnp.where(c < LANES, 2 * c, 2 * (c - LANES) + 1)
        perm = jnp.where(r == src, 1.0, 0.0).astype(BF16)
        for blk in range(wgu_bf.shape[1] // two):
            wb = wgu_f32[slot, :, blk * two:(blk + 1) * two].astype(BF16)
            wgu_bf[:, blk * two:(blk + 1) * two] = _dot(wb, perm).astype(BF16)
        wd_bf[...] = wd_f32[slot].astype(BF16)

    @pl.when(b < nused_ref[0])
    def _():
        two = 2 * LANES
        x = _from_slabs(xs_ref, 0, EXPERT_BLOCK).astype(BF16)
        gu = _dot(x, wgu_bf[...]) + bgu_ref[0]
        n_blk = gu.shape[1] // two
        x_glu = jnp.concatenate([gu[:, i * two:i * two + LANES] for i in range(n_blk)], axis=1)
        x_lin = jnp.concatenate([gu[:, i * two + LANES:(i + 1) * two] for i in range(n_blk)], axis=1)
        x_glu = jnp.minimum(x_glu, SWIGLU_LIMIT)
        x_lin = jnp.clip(x_lin, -SWIGLU_LIMIT, SWIGLU_LIMIT)
        act = _sigmoid_mul(SWIGLU_ALPHA * x_glu, x_glu * (x_lin + 1.0))
        y = _dot(act.astype(BF16), wd_bf[...]) + bd_ref[0]
        _to_slabs(ys_ref, y)

    @pl.when(b >= nused_ref[0])
    def _():
        ys_ref[...] = jnp.zeros_like(ys_ref)


def _experts(block_expert, n_used, xs, w_gate_up, b_gu_perm, w_down, b_down):
    n_blocks = block_expert.shape[0]
    d = D_MODEL
    f2 = w_gate_up.shape[-1]
    blk_rows = EXPERT_BLOCK * SLAB

    idx = jnp.arange(n_blocks, dtype=jnp.int32)
    used = idx < n_used
    prev = jnp.concatenate([jnp.full((1,), -1, jnp.int32), block_expert[:-1]])
    first = (used & (block_expert != prev)).astype(jnp.int32)
    slot = (jnp.cumsum(first) - 1) % 2
    later = used[None, :] & (block_expert[None, :] > block_expert[:, None])
    nxt = jnp.min(jnp.where(later, block_expert[None, :], N_EXPERTS), axis=1)
    nxt = jnp.where(nxt == N_EXPERTS, -1, nxt)

    grid_spec = pltpu.PrefetchScalarGridSpec(
        num_scalar_prefetch=5,
        grid=(n_blocks,),
        in_specs=[
            pl.BlockSpec((blk_rows, LANES), lambda b, be, *_: (b, 0)),
            pl.BlockSpec(memory_space=pl.ANY),
            pl.BlockSpec((1, 1, f2), lambda b, be, *_: (be[b], 0, 0)),
            pl.BlockSpec(memory_space=pl.ANY),
            pl.BlockSpec((1, 1, d), lambda b, be, *_: (be[b], 0, 0)),
        ],
        out_specs=pl.BlockSpec((blk_rows, LANES), lambda b, be, *_: (b, 0)),
        scratch_shapes=[
            pltpu.VMEM((2, d, f2), F32),
            pltpu.VMEM((2, f2 // 2, d), F32),
            pltpu.VMEM((d, f2), BF16),
            pltpu.VMEM((f2 // 2, d), BF16),
            pltpu.SemaphoreType.DMA((2, 2)),
        ],
    )
    return pl.pallas_call(
        _expert_kernel,
        grid_spec=grid_spec,
        out_shape=jax.ShapeDtypeStruct((n_blocks * blk_rows, LANES), F32),
        compiler_params=pltpu.CompilerParams(
            dimension_semantics=("arbitrary",), vmem_limit_bytes=VMEM_LIMIT),
        name="expert",
    )(block_expert, n_used, first, slot.astype(jnp.int32), nxt.astype(jnp.int32),
      xs, w_gate_up, b_gu_perm, w_down, b_down)


def _combine_kernel(dest_ref, dest_next_ref, ys_hbm, w_ref, x1_ref, mod_ref, g_ref, o_ref, buf, sem):
    i = pl.program_id(0)
    n_steps = pl.num_programs(0)
    tm = x1_ref.shape[0]
    n_assign = tm * TOP_K
    slot = i % 2

    def gather(d_ref, to_slot):
        def body(g, carry):
            for u in range(DMA_UNROLL):
                t_local = g * (DMA_UNROLL // TOP_K) + u // TOP_K
                k = u % TOP_K
                _row_copy(ys_hbm, d_ref[g * DMA_UNROLL + u], buf.at[to_slot], k * tm + t_local,
                          sem.at[to_slot]).start(priority=u % 2)
            return carry

        lax.fori_loop(0, n_assign // DMA_UNROLL, body, 0)

    @pl.when(i == 0)
    def _():
        gather(dest_ref, 0)

    @pl.when(i + 1 < n_steps)
    def _():
        gather(dest_next_ref, 1 - slot)

    cur = buf.at[slot]
    pltpu.make_async_copy(ys_hbm.at[pl.ds(0, n_assign * SLAB), :], cur, sem.at[slot]).wait()

    w = w_ref[...]
    y = w[:, 0:1] * _from_slabs(cur, 0, tm)
    for k in range(1, TOP_K):
        y = y + w[:, k:k + 1] * _from_slabs(cur, k * tm, tm)
    x = x1_ref[...] + mod_ref[0, 5:6, :] * y
    ms = jnp.mean(x * x, axis=-1, keepdims=True)
    o_ref[...] = x * lax.rsqrt(ms + EPS) * g_ref[...]


def _combine(dest_flat, ys, wts, x1, mod6, g, seq):
    t = x1.shape[0]
    tm = 256
    per_b = seq // tm
    d = D_MODEL
    last = t // tm - 1
    return pl.pallas_call(
        _combine_kernel,
        grid=(t // tm,),
        in_specs=[
            pl.BlockSpec((tm * TOP_K,), lambda i: (i,), memory_space=pltpu.SMEM),
            pl.BlockSpec((tm * TOP_K,), lambda i: (jnp.minimum(i + 1, last),), memory_space=pltpu.SMEM),
            pl.BlockSpec(memory_space=pl.ANY),
            pl.BlockSpec((tm, LANES), lambda i: (i, 0)),
            pl.BlockSpec((tm, d), lambda i: (i, 0)),
            pl.BlockSpec((1, 6, d), lambda i: (i // per_b, 0, 0)),
            pl.BlockSpec((1, d), lambda i: (0, 0)),
        ],
        out_specs=pl.BlockSpec((tm, d), lambda i: (i, 0)),
        out_shape=jax.ShapeDtypeStruct((t, d), F32),
        scratch_shapes=[pltpu.VMEM((2, tm * TOP_K * SLAB, LANES), F32), pltpu.SemaphoreType.DMA((2,))],
        compiler_params=pltpu.CompilerParams(
            dimension_semantics=("arbitrary",), vmem_limit_bytes=VMEM_LIMIT),
        name="combine",
    )(dest_flat, dest_flat, ys, wts, x1, mod6, g)


def _split_w(w):
    hi = w.astype(BF16)
    lo = (w - hi.astype(F32)).astype(BF16)
    return hi, lo


def kernel(x, c, positions, w_ada, b_ada, norm_mix_g, w_in, conv_w, conv_b, b_if, ml_norm_g, ret_norm_g,
           w_branch_ml, w_branch_ret, w_out, norm_ffn_g, w_router, b_router, w_gate_up, b_gate_up, w_down,
           b_down, norm_final_g):
    batch, seq, d = x.shape
    t = batch * seq
    x2 = x.reshape(t, d)
    depth = w_ada.shape[0]
    c8 = jnp.concatenate([c, jnp.zeros((8 - batch, d), c.dtype)], axis=0)

    half = D_QK // 2
    inv = ROPE_BASE ** (-jnp.arange(half, dtype=F32) / half)
    inv2 = jnp.concatenate([inv, inv])[None, :]
    sign2 = jnp.concatenate([-jnp.ones((half,), F32), jnp.ones((half,), F32)])[None, :]
    pos_b = jnp.broadcast_to(positions.astype(F32).reshape(t, 1), (t, LANES))
    L = CHUNK
    log_gamma = jnp.log(1.0 - 2.0 ** (-5.0 - jnp.arange(N_HEADS, dtype=F32)))
    pos = jnp.arange(L, dtype=F32)
    rel = pos[:, None] - pos[None, :]
    decay = jnp.where(rel >= 0, jnp.exp(log_gamma[:, None, None] * jnp.maximum(rel, 0.0)), 0.0)
    kdec = jnp.exp(log_gamma[:, None] * (L - 1 - pos))[:, :, None]
    qdec = jnp.exp(log_gamma[:, None] * (pos + 1.0))[:, :, None]
    gch = jnp.broadcast_to(jnp.exp(log_gamma * L)[:, None, None], (N_HEADS, 1, LANES))

    assert depth == 1, "the final norm is fused after the single layer"
    l = 0
    wa_hi, wa_lo = _split_w(w_ada[l])
    mod = _ada(c8, wa_hi, wa_lo, b_ada[l][None, :])
    mod6 = mod[:batch].reshape(batch, 6, d)

    w = w_in[l]
    w_main, w_gate = _wprep(w.T)
    w_if = w_gate.T
    zpad = jnp.zeros((d, LANES - N_HEADS), F32)
    wif = jnp.concatenate([w_if[:, :N_HEADS], zpad, w_if[:, N_HEADS:], zpad], axis=1)
    wif_hi, wif_lo = _split_w(wif)
    zb = jnp.zeros((LANES - N_HEADS,), F32)
    b_if2 = jnp.concatenate([b_if[l][:N_HEADS], zb, b_if[l][N_HEADS:], zb])[None, :]

    proj, gates, cos2, sin2 = _inproj(x2, mod6, norm_mix_g[l][None, :], w_main, wif_hi, wif_lo,
                                      pos_b, inv2, sign2, seq)
    hm, hr = _mix(proj, gates, cos2, sin2, conv_w[l], conv_b[l][None, :], b_if2,
                  ml_norm_g[l][None, :], ret_norm_g[l][None, :], decay, kdec, qdec, gch, batch, seq)

    wr = jnp.concatenate([w_router[l], jnp.zeros((d, LANES - N_EXPERTS), F32)], axis=1)
    wr_hi, wr_lo = _split_w(wr)
    br = jnp.concatenate([b_router[l], jnp.full((LANES - N_EXPERTS,), -1e30, F32)])[None, :]
    x1, hf, logits = _post(hm, hr, proj, x2, mod6, w_branch_ml[l].astype(BF16), w_branch_ret[l].astype(BF16),
                           w_out[l].astype(BF16), norm_ffn_g[l][None, :], wr_hi, wr_lo, br, seq)

    n_assign = t * TOP_K
    n_blocks = -(-n_assign // EXPERT_BLOCK) + N_EXPERTS
    dest, wts, tables = _route(logits)
    dest_flat = dest[:, :TOP_K].reshape(n_assign)
    bend = tables[0, :N_EXPERTS].astype(jnp.int32)
    pad_first = tables[1, :N_EXPERTS].astype(jnp.int32)
    n_used = bend[N_EXPERTS - 1:]
    blk = jnp.minimum(jnp.arange(n_blocks, dtype=jnp.int32), n_used - 1)
    block_expert = jnp.minimum(
        jnp.sum((bend[None, :] <= blk[:, None]).astype(jnp.int32), axis=1), N_EXPERTS - 1)
    xs = _dispatch(dest_flat, jnp.concatenate([pad_first, n_used]), hf, n_blocks)
    f2 = w_gate_up.shape[-1]
    bgu = b_gate_up[l].reshape(N_EXPERTS, f2 // (2 * LANES), LANES, 2)
    bgu = jnp.swapaxes(bgu, -1, -2).reshape(N_EXPERTS, 1, f2)
    ys = _experts(block_expert, n_used, xs, w_gate_up[l], bgu, w_down[l], b_down[l][:, None, :])
    out = _combine(dest_flat, ys, wts, x1, mod6, norm_final_g[None, :], seq)
    return out.reshape(batch, seq, d)
```

```python
import functools

import numpy as np
import jax
import jax.numpy as jnp
from jax import lax
from jax.experimental import pallas as pl
from jax.experimental.pallas import tpu as pltpu

D_MODEL = 1024
N_HEADS = 4
D_QK = 128
D_V = 256
CONV_K = 4
ROPE_BASE = 10000.0
CHUNK = 128
N_EXPERTS = 32
TOP_K = 4
SWIGLU_LIMIT = 7.0
SWIGLU_ALPHA = 1.702
EXPERT_BLOCK = 512
EPS = 1e-5

QK_W = N_HEADS * D_QK
V_W = N_HEADS * D_V
LANES = 128
GATE_W = 2 * LANES
V_EXT = D_V + LANES
MIX_GROUP = 4
SLAB = D_MODEL // LANES

C_MLQ, C_MLK, C_MLV, C_MLO = 0, 512, 1024, 2048
C_RQ, C_RK, C_RV, C_RG = 3072, 3584, 4096, 5120
C_GML, C_GRET = 6144, 7168
MIX_W = 6144
PROJ_W = 8192

BF16 = jnp.bfloat16
F32 = jnp.float32
VMEM_LIMIT = 56 * 1024 * 1024


def _dot(a, b):
    return jnp.dot(a, b, preferred_element_type=F32)


def _dot_nt(a, b):
    return lax.dot_general(a, b, (((1,), (1,)), ((), ())), preferred_element_type=F32)


def _split(a):
    hi = a.astype(BF16)
    lo = (a - hi.astype(F32)).astype(BF16)
    return hi, lo


def _dot3(a, b_hi, b_lo):
    a_hi, a_lo = _split(a)
    return _dot(a_hi, b_hi) + (_dot(a_lo, b_hi) + _dot(a_hi, b_lo))


def _sigmoid(x):
    return 0.5 * jnp.tanh(0.5 * x) + 0.5


def _sigmoid_mul(x, y):
    hy = 0.5 * y
    return hy * jnp.tanh(0.5 * x) + hy


def _silu(x):
    hx = 0.5 * x
    return hx * jnp.tanh(hx) + hx


def _round_robin(stage_generators):
    pending = list(stage_generators)
    while pending:
        for gen in list(pending):
            if next(gen, True):
                pending.remove(gen)


def _rms_mod(x, g, scale, shift):
    ms = jnp.mean(x * x, axis=-1, keepdims=True)
    return (x * lax.rsqrt(ms + EPS) * g) * (1.0 + scale) + shift


def _ada_kernel(c_ref, w_ref, b_ref, o_ref):
    w_hi, w_lo = _split(w_ref[...])
    o_ref[...] = _dot3(_silu(c_ref[...]), w_hi, w_lo) + b_ref[...]


def _ada(c8, w, b):
    n = w.shape[1]
    tn = 1024
    return pl.pallas_call(
        _ada_kernel,
        grid=(n // tn,),
        in_specs=[
            pl.BlockSpec((8, D_MODEL), lambda j: (0, 0)),
            pl.BlockSpec((D_MODEL, tn), lambda j: (0, j)),
            pl.BlockSpec((1, tn), lambda j: (0, j)),
        ],
        out_specs=pl.BlockSpec((8, tn), lambda j: (0, j)),
        out_shape=jax.ShapeDtypeStruct((8, n), F32),
        name="ada",
    )(c8, w, b)


def _inproj_kernel(x_ref, xn_ref, mod_ref, modn_ref, g_ref, w_ref, wif_hi_ref, wif_lo_ref, pos_ref, inv_ref,
                   sign_ref, proj_ref, gates_ref, cos_ref, sin_ref, hi_a, lo_a, hi_b, lo_b):
    i = pl.program_id(0)
    j = pl.program_id(1)

    def normalise(x, mod):
        return _split(_rms_mod(x, g_ref[...], mod[0, 1:2, :], mod[0, 0:1, :]))

    @pl.when(jnp.logical_and(i == 0, j == 0))
    def _():
        hi_a[...], lo_a[...] = normalise(x_ref[...], mod_ref)

    n = x_ref.shape[0] // pl.num_programs(1)
    rows = pl.ds(pl.multiple_of(j * n, n), n)
    n_split = 4
    tc = w_ref.shape[1] // n_split

    def step(cur_hi, cur_lo, nxt_hi, nxt_lo):
        def matmul_steps():
            for c in range(n_split):
                proj_ref[:, c * tc:(c + 1) * tc] = _dot(cur_hi[...], w_ref[:, c * tc:(c + 1) * tc]).astype(BF16)
                yield False

        def side_steps():
            h_hi = cur_hi[rows, :]
            h_lo = cur_lo[rows, :]
            gates_ref[rows, :] = (_dot(h_hi, wif_hi_ref[...])
                                  + (_dot(h_lo, wif_hi_ref[...]) + _dot(h_hi, wif_lo_ref[...])))
            yield False
            ang = pos_ref[rows, :] * inv_ref[...]
            cos_ref[rows, :] = jnp.cos(ang)
            yield False
            sin_ref[rows, :] = jnp.sin(ang) * sign_ref[...]
            yield False
            nxt_hi[rows, :], nxt_lo[rows, :] = normalise(xn_ref[rows, :], modn_ref)
            yield False

        _round_robin([matmul_steps(), side_steps()])

    @pl.when(i % 2 == 0)
    def _():
        step(hi_a, lo_a, hi_b, lo_b)

    @pl.when(i % 2 == 1)
    def _():
        step(hi_b, lo_b, hi_a, lo_a)


GATE_COL = 3072


def _wprep_kernel(a_ref, b_ref, o_ref, gate_ref):
    c = pl.program_id(0)
    tn = a_ref.shape[0]

    @pl.when(c * tn == GATE_COL)
    def _():
        gate_ref[...] = a_ref[0:2 * N_HEADS, :]

    @pl.when((c + 1) * tn <= GATE_COL)
    def _():
        o_ref[...] = a_ref[...].T.astype(BF16)

    @pl.when((c + 1) * tn > GATE_COL)
    def _():
        shifted = jnp.concatenate([a_ref[2 * N_HEADS:, :], b_ref[...]], axis=0)
        o_ref[...] = shifted.T.astype(BF16)


def _wprep(w_t):
    d = w_t.shape[1]
    tn = 1024
    gate = 2 * N_HEADS
    assert GATE_COL % tn == 0 and w_t.shape[0] == PROJ_W + gate
    return pl.pallas_call(
        _wprep_kernel,
        grid=(PROJ_W // tn,),
        in_specs=[
            pl.BlockSpec((tn, d), lambda c: (c, 0)),
            pl.BlockSpec((gate, d), lambda c: ((c + 1) * (tn // gate), 0)),
        ],
        out_specs=[pl.BlockSpec((d, tn), lambda c: (0, c)), pl.BlockSpec((gate, d), lambda c: (0, 0))],
        out_shape=[jax.ShapeDtypeStruct((d, PROJ_W), BF16), jax.ShapeDtypeStruct((gate, d), F32)],
        compiler_params=pltpu.CompilerParams(
            dimension_semantics=("arbitrary",), vmem_limit_bytes=VMEM_LIMIT),
        name="wprep",
    )(w_t, w_t)


def _inproj(x2, mod6, g, w_main, wif_hi, wif_lo, pos_b, inv2, sign2, seq):
    t = x2.shape[0]
    tm, tn = 1024, 2048
    per_b = seq // tm
    last = t // tm - 1
    nxt = lambda i: jnp.minimum(i + 1, last)
    return pl.pallas_call(
        _inproj_kernel,
        grid=(t // tm, PROJ_W // tn),
        in_specs=[
            pl.BlockSpec((tm, D_MODEL), lambda i, j: (i, 0)),
            pl.BlockSpec((tm, D_MODEL), lambda i, j: (nxt(i), 0)),
            pl.BlockSpec((1, 6, D_MODEL), lambda i, j: (i // per_b, 0, 0)),
            pl.BlockSpec((1, 6, D_MODEL), lambda i, j: (nxt(i) // per_b, 0, 0)),
            pl.BlockSpec((1, D_MODEL), lambda i, j: (0, 0)),
            pl.BlockSpec((D_MODEL, tn), lambda i, j: (0, j)),
            pl.BlockSpec((D_MODEL, GATE_W), lambda i, j: (0, 0)),
            pl.BlockSpec((D_MODEL, GATE_W), lambda i, j: (0, 0)),
            pl.BlockSpec((tm, LANES), lambda i, j: (i, 0)),
            pl.BlockSpec((1, LANES), lambda i, j: (0, 0)),
            pl.BlockSpec((1, LANES), lambda i, j: (0, 0)),
        ],
        out_specs=[
            pl.BlockSpec((tm, tn), lambda i, j: (i, j)),
            pl.BlockSpec((tm, GATE_W), lambda i, j: (i, 0)),
            pl.BlockSpec((tm, LANES), lambda i, j: (i, 0)),
            pl.BlockSpec((tm, LANES), lambda i, j: (i, 0)),
        ],
        out_shape=[
            jax.ShapeDtypeStruct((t, PROJ_W), BF16),
            jax.ShapeDtypeStruct((t, GATE_W), F32),
            jax.ShapeDtypeStruct((t, LANES), F32),
            jax.ShapeDtypeStruct((t, LANES), F32),
        ],
        scratch_shapes=[pltpu.VMEM((tm, D_MODEL), BF16)] * 4,
        compiler_params=pltpu.CompilerParams(
            dimension_semantics=("arbitrary", "arbitrary"), vmem_limit_bytes=VMEM_LIMIT),
        name="inproj",
    )(x2, x2, mod6, mod6, g, w_main, wif_hi, wif_lo, pos_b, inv2, sign2)


def _row_mean(a):
    inv_n = jnp.full((a.shape[1], LANES), 1.0 / a.shape[1], BF16)
    m = _dot(a.astype(BF16), inv_n)
    return jnp.concatenate([m] * (a.shape[1] // LANES), axis=1)


def _head_norm(h, g):
    d = h - _row_mean(h)
    return d * lax.rsqrt(_row_mean(d * d) + EPS) * g


def _mix_kernel(proj_ref, gates_ref, cos_ref, sin_ref, convw_ref, convb_ref, bif_ref, mlg_ref, retg_ref,
                decay_ref, kdec_ref, qdec_ref, gch_ref,
                hm_ref, hr_ref,
                prev_ref, c_st, m_st, r_st):
    group = proj_ref.shape[0]

    @pl.when(pl.program_id(1) == 0)
    def _():
        def zero_state(i, carry):
            c_st[i] = jnp.zeros(c_st.shape[1:], F32)
            r_st[i] = jnp.zeros(r_st.shape[1:], F32)
            return carry

        lax.fori_loop(0, group * N_HEADS, zero_state, 0)
        prev_ref[...] = jnp.zeros_like(prev_ref)
        m_st[...] = jnp.zeros_like(m_st)

    ml_steps, ret_steps = [], []
    for gb in range(group):
        states = slice(gb * N_HEADS, (gb + 1) * N_HEADS)
        ml_steps.append(_mlstm_steps(proj_ref.at[gb], gates_ref.at[gb], convw_ref, convb_ref, bif_ref, mlg_ref,
                                     hm_ref.at[gb], prev_ref.at[gb], c_st.at[states], m_st.at[gb]))
        ret_steps.append(_retention_steps(proj_ref.at[gb], cos_ref.at[gb], sin_ref.at[gb], retg_ref, decay_ref,
                                          kdec_ref, qdec_ref, gch_ref, hr_ref.at[gb], r_st.at[states]))
    def delayed(gen, rounds):
        for _ in range(rounds):
            yield False
        yield from gen

    _round_robin(ml_steps + [delayed(g, 4) for g in ret_steps])


def _mlstm_steps(proj_ref, gates_ref, convw_ref, convb_ref, bif_ref, mlg_ref, hm_ref, prev_ref, c_st, m_st):
    L = CHUNK
    rows = lax.broadcasted_iota(jnp.int32, (L, L), 0)
    cols = lax.broadcasted_iota(jnp.int32, (L, L), 1)
    causal = rows >= cols
    tril = jnp.where(causal, 1.0, 0.0).astype(BF16)

    cur = proj_ref[:, C_MLQ:C_MLQ + 2 * QK_W]
    xx = jnp.concatenate([prev_ref[...], cur], axis=0)
    r2 = lax.broadcasted_iota(jnp.int32, (L, 2 * L), 0)
    c2 = lax.broadcasted_iota(jnp.int32, (L, 2 * L), 1)
    acc = convb_ref[...] + cur.astype(F32) * convw_ref[CONV_K - 1:CONV_K, :]
    for d in range(1, CONV_K):
        shift = jnp.where(c2 == r2 + (L - d), 1.0, 0.0).astype(BF16)
        acc = acc + _dot(shift, xx) * convw_ref[CONV_K - 1 - d:CONV_K - d, :]
        yield False
    prev_ref[...] = cur
    qk = _silu(acc)
    yield False

    g = gates_ref[...] + bif_ref[...]
    gi = g[:, :LANES]
    gf = g[:, LANES:]
    lf = jnp.minimum(gf, 0.0) - jnp.log(1.0 + jnp.exp(-jnp.abs(gf)))
    lf_hi, lf_lo = _split(lf)
    a_all = _dot(tril, lf_hi) + _dot(tril, lf_lo)
    yield False
    a_last = a_all[L - 1:L, :]
    bm = gi - a_all
    bm_t = bm.T
    w_state = a_last + bm
    m_loc = jnp.max(w_state, axis=0, keepdims=True)
    m_prev = m_st[...]
    inter_log = a_all + m_prev
    m_new = jnp.maximum(a_last + m_prev, m_loc)
    s_prev = jnp.exp(a_last + m_prev - m_new)
    s_loc = jnp.exp(m_loc - m_new)
    ws_all = jnp.exp(w_state - m_loc) * s_loc
    m_st[...] = m_new
    yield False

    ones_blk = jnp.ones((L, LANES), BF16)
    q_scale = D_QK ** -0.5

    for h in range(N_HEADS):
        q = (qk[:, h * D_QK:(h + 1) * D_QK] * q_scale).astype(BF16)
        k_f = qk[:, QK_W + h * D_QK:QK_W + (h + 1) * D_QK]
        k = k_f.astype(BF16)
        v_ext = jnp.concatenate([proj_ref[:, C_MLV + h * D_V:C_MLV + (h + 1) * D_V], ones_blk], axis=1)
        dlog = jnp.where(causal, a_all[:, h:h + 1] + bm_t[h:h + 1, :], -jnp.inf)
        m_intra = jnp.max(dlog, axis=-1, keepdims=True)
        s = _dot_nt(q, k)
        qc = _dot(q, c_st[h].astype(BF16))
        yield False
        il = inter_log[:, h:h + 1]
        m_t = jnp.maximum(il, m_intra)
        p = jnp.exp(dlog - m_t) * s
        isc = jnp.exp(il - m_t)
        tot = _dot(p.astype(BF16), v_ext) + isc * qc
        yield False
        den = tot[:, D_V:D_V + 1]
        hout = tot[:, :D_V] / jnp.maximum(jnp.abs(den), jnp.exp(-m_t))
        kw_t = (k_f * ws_all[:, h:h + 1]).T.astype(BF16)
        c_st[h] = s_prev[:, h:h + 1] * c_st[h] + _dot(kw_t, v_ext)
        yield False
        y = _head_norm(hout, mlg_ref[:, h * D_V:(h + 1) * D_V])
        o = proj_ref[:, C_MLO + h * D_V:C_MLO + (h + 1) * D_V].astype(F32)
        hm_ref[:, h * D_V:(h + 1) * D_V] = _sigmoid_mul(o, y).astype(BF16)
        yield False


def _retention_steps(proj_ref, cos_ref, sin_ref, retg_ref, decay_ref, kdec_ref, qdec_ref, gch_ref, hr_ref, r_st):
    cos2 = cos_ref[...]
    sin2 = sin_ref[...]
    k_scale = D_QK ** -0.5
    for h in range(N_HEADS):
        q_raw = proj_ref[:, C_RQ + h * D_QK:C_RQ + (h + 1) * D_QK].astype(F32)
        k_raw = proj_ref[:, C_RK + h * D_QK:C_RK + (h + 1) * D_QK].astype(F32)
        q = (q_raw * cos2 + pltpu.roll(q_raw, D_QK // 2, 1) * sin2).astype(BF16)
        k_f = (k_raw * cos2 + pltpu.roll(k_raw, D_QK // 2, 1) * sin2) * k_scale
        v = proj_ref[:, C_RV + h * D_V:C_RV + (h + 1) * D_V]
        yield False
        sc = _dot_nt(q, k_f.astype(BF16)) * decay_ref[h]
        hret = _dot(sc.astype(BF16), v) + _dot(q, r_st[h].astype(BF16)) * qdec_ref[h]
        yield False
        kd_t = (k_f * kdec_ref[h]).T.astype(BF16)
        r_st[h] = gch_ref[h][:, 0:1] * r_st[h] + _dot(kd_t, v)
        yield False
        y = _head_norm(hret, retg_ref[:, h * D_V:(h + 1) * D_V])
        gt = proj_ref[:, C_RG + h * D_V:C_RG + (h + 1) * D_V].astype(F32)
        hr_ref[:, h * D_V:(h + 1) * D_V] = (_silu(gt) * y).astype(BF16)
        yield False


def _mix(proj, gates, cos2, sin2, conv_w, conv_b, b_if2, ml_g, ret_g, decay, kdec, qdec, gch, batch, seq):
    t = proj.shape[0]
    nc = seq // CHUNK
    L = CHUNK
    G = MIX_GROUP
    proj = proj.reshape(batch, seq, PROJ_W)
    gates = gates.reshape(batch, seq, GATE_W)
    cos2 = cos2.reshape(batch, seq, LANES)
    sin2 = sin2.reshape(batch, seq, LANES)
    full = lambda shape: pl.BlockSpec(shape, lambda b, c: (0,) * len(shape))
    hm, hr = pl.pallas_call(
        _mix_kernel,
        grid=(batch // G, nc),
        in_specs=[
            pl.BlockSpec((G, L, MIX_W), lambda b, c: (b, c, 0)),
            pl.BlockSpec((G, L, GATE_W), lambda b, c: (b, c, 0)),
            pl.BlockSpec((G, L, LANES), lambda b, c: (b, c, 0)),
            pl.BlockSpec((G, L, LANES), lambda b, c: (b, c, 0)),
            full((CONV_K, 2 * QK_W)),
            full((1, 2 * QK_W)),
            full((1, GATE_W)),
            full((1, V_W)),
            full((1, V_W)),
            full((N_HEADS, L, L)),
            full((N_HEADS, L, 1)),
            full((N_HEADS, L, 1)),
            full((N_HEADS, 1, LANES)),
        ],
        out_specs=[
            pl.BlockSpec((G, L, V_W), lambda b, c: (b, c, 0)),
            pl.BlockSpec((G, L, V_W), lambda b, c: (b, c, 0)),
        ],
        out_shape=[jax.ShapeDtypeStruct((batch, seq, V_W), BF16), jax.ShapeDtypeStruct((batch, seq, V_W), BF16)],
        scratch_shapes=[
            pltpu.VMEM((G, L, 2 * QK_W), BF16),
            pltpu.VMEM((G * N_HEADS, D_QK, V_EXT), F32),
            pltpu.VMEM((G, 1, LANES), F32),
            pltpu.VMEM((G * N_HEADS, D_QK, D_V), F32),
        ],
        compiler_params=pltpu.CompilerParams(
            dimension_semantics=("arbitrary", "arbitrary"), vmem_limit_bytes=VMEM_LIMIT),
        name="mix",
    )(proj, gates, cos2, sin2, conv_w, conv_b, b_if2, ml_g, ret_g, decay, kdec, qdec, gch)
    return hm.reshape(t, V_W), hr.reshape(t, V_W)


def _to_slabs(ref, val):
    rows = val.shape[0]
    for s in range(SLAB):
        ref[pl.ds(s, rows, stride=SLAB), :] = val[:, s * LANES:(s + 1) * LANES]


def _from_slabs(ref, first_slab, rows):
    return jnp.concatenate(
        [ref[pl.ds(first_slab * SLAB + s, rows, stride=SLAB), :] for s in range(SLAB)], axis=1)


def _post_kernel(hm_ref, hr_ref, gm_ref, gr_ref, x_ref, mod_ref, wbm_ref, wbr_ref, wout_ref, g_ref,
                 wr_hi_ref, wr_lo_ref, br_ref, x1_ref, hf_ref, logit_ref):
    n_sub = 2
    sub = x_ref.shape[0] // n_sub

    def steps(s):
        r = pl.ds(s * sub, sub)
        ym = _dot(hm_ref[r, :], wbm_ref[...])
        yield False
        yr = _dot(hr_ref[r, :], wbr_ref[...])
        yield False
        y = _sigmoid_mul(gm_ref[r, :].astype(F32), ym) + _sigmoid_mul(gr_ref[r, :].astype(F32), yr)
        o = _dot(y.astype(BF16), wout_ref[...])
        yield False
        x1 = x_ref[r, :] + mod_ref[0, 2:3, :] * o
        x1_ref[r, :] = x1
        hf = _rms_mod(x1, g_ref[...], mod_ref[0, 4:5, :], mod_ref[0, 3:4, :])
        yield False
        _to_slabs(hf_ref.at[pl.ds(s * sub * SLAB, sub * SLAB)], hf)
        logit_ref[r, :] = _dot3(hf, wr_hi_ref[...], wr_lo_ref[...]) + br_ref[...]
        yield False

    _round_robin([steps(s) for s in range(n_sub)])


def _post(hm, hr, proj, x2, mod6, wbm, wbr, wout, g, wr_hi, wr_lo, br, seq):
    t = x2.shape[0]
    tm = 512
    per_b = seq // tm
    d = D_MODEL
    const = lambda shape: pl.BlockSpec(shape, lambda i: (0,) * len(shape))
    return pl.pallas_call(
        _post_kernel,
        grid=(t // tm,),
        in_specs=[
            pl.BlockSpec((tm, d), lambda i: (i, 0)),
            pl.BlockSpec((tm, d), lambda i: (i, 0)),
            pl.BlockSpec((tm, d), lambda i: (i, C_GML // d)),
            pl.BlockSpec((tm, d), lambda i: (i, C_GRET // d)),
            pl.BlockSpec((tm, d), lambda i: (i, 0)),
            pl.BlockSpec((1, 6, d), lambda i: (i // per_b, 0, 0)),
            const((d, d)), const((d, d)), const((d, d)),
            const((1, d)),
            const((d, LANES)), const((d, LANES)), const((1, LANES)),
        ],
        out_specs=[
            pl.BlockSpec((tm, d), lambda i: (i, 0)),
            pl.BlockSpec((tm * SLAB, LANES), lambda i: (i, 0)),
            pl.BlockSpec((tm, LANES), lambda i: (i, 0)),
        ],
        out_shape=[
            jax.ShapeDtypeStruct((t, d), F32),
            jax.ShapeDtypeStruct((t * SLAB, LANES), F32),
            jax.ShapeDtypeStruct((t, LANES), F32),
        ],
        compiler_params=pltpu.CompilerParams(
            dimension_semantics=("arbitrary",), vmem_limit_bytes=VMEM_LIMIT),
        name="post",
    )(hm, hr, proj, proj, x2, mod6, wbm, wbr, wout, g, wr_hi, wr_lo, br)


def _route_kernel(logit_ref, dest_ref, w_ref, bend_ref, cnt_st, pad_st, sel_st, pick_st, wts_st):
    ph = pl.program_id(0)
    i = pl.program_id(1)
    tm = logit_ref.shape[0]
    rows = pl.ds(pl.multiple_of(i * tm, tm), tm)
    lane = lax.broadcasted_iota(jnp.int32, (tm, LANES), 1)
    lane_f = lane.astype(F32)

    @pl.when(jnp.logical_and(ph == 0, i == 0))
    def _():
        cnt_st[...] = jnp.zeros_like(cnt_st)

    @pl.when(ph == 0)
    def _():
        l = logit_ref[...]
        picks, vals = [], []
        sel = jnp.zeros((tm, LANES), F32)
        for _ in range(TOP_K):
            m = jnp.max(l, axis=-1, keepdims=True)
            idx = jnp.min(jnp.where(l == m, lane_f, float(LANES)), axis=-1, keepdims=True)
            oh = lane_f == idx
            picks.append(idx)
            vals.append(m)
            sel = jnp.where(oh, 1.0, sel)
            l = jnp.where(oh, -jnp.inf, l)
        ex = [jnp.exp(v - vals[0]) for v in vals]
        den = ex[0] + ex[1] + ex[2] + ex[3]
        pick = jnp.zeros((tm, LANES), F32)
        wts = jnp.zeros((tm, LANES), F32)
        for k in range(TOP_K):
            pick = jnp.where(lane == k, picks[k], pick)
            wts = jnp.where(lane == k, ex[k] / den, wts)
        sel_st[rows, :] = sel.astype(BF16)
        pick_st[rows, :] = pick
        wts_st[rows, :] = wts
        cnt_st[...] = cnt_st[...] + jnp.sum(sel, axis=0, keepdims=True)

    @pl.when(jnp.logical_and(ph == 1, i == 0))
    def _():
        blocks = jnp.floor((cnt_st[...] + (EXPERT_BLOCK - 1)) * (1.0 / EXPERT_BLOCK))
        r = lax.broadcasted_iota(jnp.int32, (LANES, LANES), 0)
        c = lax.broadcasted_iota(jnp.int32, (LANES, LANES), 1)
        upper = jnp.where(r < c, 1.0, 0.0).astype(BF16)
        blocks8 = jnp.broadcast_to(blocks, (8, LANES))
        excl = _dot(blocks8.astype(BF16), upper)
        pad_st[...] = excl[0:1, :] * EXPERT_BLOCK
        row = lax.broadcasted_iota(jnp.int32, (8, LANES), 0)
        bend_ref[...] = jnp.where(row == 0, excl + blocks8,
                                  jnp.where(row == 1, excl * EXPERT_BLOCK + cnt_st[...], 0.0))
        cnt_st[...] = jnp.zeros_like(cnt_st)

    @pl.when(ph == 1)
    def _():
        carry = cnt_st[...]
        sel = sel_st[rows, :]
        pick = pick_st[rows, :]
        r = lax.broadcasted_iota(jnp.int32, (tm, tm), 0)
        c = lax.broadcasted_iota(jnp.int32, (tm, tm), 1)
        lower = jnp.where(r > c, 1.0, 0.0).astype(BF16)
        base = pad_st[...] + carry + _dot(lower, sel)
        dest = jnp.zeros((tm, LANES), F32)
        for k in range(TOP_K):
            dk = jnp.sum(jnp.where(lane_f == pick[:, k:k + 1], base, 0.0), axis=-1, keepdims=True)
            dest = jnp.where(lane == k, dk, dest)
        dest_ref[...] = dest.astype(jnp.int32)
        w_ref[...] = wts_st[rows, :]
        cnt_st[...] = carry + jnp.sum(sel.astype(F32), axis=0, keepdims=True)


def _route(logits):
    t = logits.shape[0]
    tm = 512
    return pl.pallas_call(
        _route_kernel,
        grid=(2, t // tm),
        in_specs=[pl.BlockSpec((tm, LANES), lambda ph, i: (i * (1 - ph), 0))],
        out_specs=[
            pl.BlockSpec((tm, LANES), lambda ph, i: (i * ph, 0)),
            pl.BlockSpec((tm, LANES), lambda ph, i: (i * ph, 0)),
            pl.BlockSpec((8, LANES), lambda ph, i: (0, 0)),
        ],
        out_shape=[
            jax.ShapeDtypeStruct((t, LANES), jnp.int32),
            jax.ShapeDtypeStruct((t, LANES), F32),
            jax.ShapeDtypeStruct((8, LANES), F32),
        ],
        scratch_shapes=[
            pltpu.VMEM((1, LANES), F32),
            pltpu.VMEM((1, LANES), F32),
            pltpu.VMEM((t, LANES), BF16),
            pltpu.VMEM((t, LANES), F32),
            pltpu.VMEM((t, LANES), F32),
        ],
        compiler_params=pltpu.CompilerParams(
            dimension_semantics=("arbitrary", "arbitrary"), vmem_limit_bytes=VMEM_LIMIT),
        name="route",
    )(logits)


DMA_UNROLL = 8


def _row_copy(src_ref, src_slab, dst_ref, dst_slab, sem):
    src = pl.multiple_of(src_slab * SLAB, SLAB)
    dst = pl.multiple_of(dst_slab * SLAB, SLAB)
    return pltpu.make_async_copy(src_ref.at[pl.ds(src, SLAB), :], dst_ref.at[pl.ds(dst, SLAB), :], sem)


def _dispatch_kernel(dest_ref, padfirst_ref, hf_hbm, xs_out, tiles, zbuf, lsem, rsem, zsem):
    i = pl.program_id(0)
    n_steps = pl.num_programs(0)
    n_assign = dest_ref.shape[0]
    tile_rows = tiles.shape[1]
    blk_rows = zbuf.shape[0]
    n_buf = tiles.shape[0]

    def tile_load(step, slot):
        start = pl.multiple_of(step * tile_rows, tile_rows)
        return pltpu.make_async_copy(hf_hbm.at[pl.ds(start, tile_rows), :], tiles.at[slot], lsem.at[slot])

    def wait_rows(slot):
        for _ in range(TOP_K):
            pltpu.make_async_copy(tiles.at[slot], xs_out.at[pl.ds(0, tile_rows), :], rsem.at[slot]).wait()

    @pl.when(i == 0)
    def _():
        tile_load(0, 0).start()
        tile_load(1, 1).start()
        zbuf[...] = jnp.zeros_like(zbuf)
        for e in range(N_EXPERTS):
            start = pl.multiple_of(padfirst_ref[e] * SLAB, SLAB)
            pltpu.make_async_copy(zbuf, xs_out.at[pl.ds(start, blk_rows), :], zsem).start()
        for e in range(N_EXPERTS):
            pltpu.make_async_copy(zbuf, xs_out.at[pl.ds(0, blk_rows), :], zsem).wait()

        n_used = padfirst_ref[N_EXPERTS]
        n_total = xs_out.shape[0] // blk_rows

        def zero_start(b, carry):
            start = pl.multiple_of(b * blk_rows, blk_rows)
            pltpu.make_async_copy(zbuf, xs_out.at[pl.ds(start, blk_rows), :], zsem).start()
            return carry

        def zero_wait(b, carry):
            pltpu.make_async_copy(zbuf, xs_out.at[pl.ds(0, blk_rows), :], zsem).wait()
            return carry

        lax.fori_loop(n_used, n_total, zero_start, 0)
        lax.fori_loop(n_used, n_total, zero_wait, 0)

    slot = i % n_buf
    tile_load(i, slot).wait()
    tile = tiles.at[slot]

    def body(g, carry):
        for u in range(DMA_UNROLL):
            t_local = g * (DMA_UNROLL // TOP_K) + u // TOP_K
            _row_copy(tile, t_local, xs_out, dest_ref[g * DMA_UNROLL + u], rsem.at[slot]).start(priority=u % 2)
        return carry

    lax.fori_loop(0, n_assign // DMA_UNROLL, body, 0)

    @pl.when(i >= 1)
    def _():
        wait_rows((i + n_buf - 1) % n_buf)

    @pl.when(i + 2 < n_steps)
    def _():
        tile_load(i + 2, (i + 2) % n_buf).start()

    @pl.when(i == n_steps - 1)
    def _():
        wait_rows(slot)


def _dispatch(dest_flat, pad_first, hf_slab, n_blocks):
    n_assign = dest_flat.shape[0]
    tm = 512
    blk_rows = EXPERT_BLOCK * SLAB
    return pl.pallas_call(
        _dispatch_kernel,
        grid=(n_assign // (tm * TOP_K),),
        in_specs=[
            pl.BlockSpec((tm * TOP_K,), lambda i: (i,), memory_space=pltpu.SMEM),
            pl.BlockSpec(memory_space=pltpu.SMEM),
            pl.BlockSpec(memory_space=pl.ANY),
        ],
        out_specs=pl.BlockSpec(memory_space=pl.ANY),
        out_shape=jax.ShapeDtypeStruct(((n_blocks + 1) * blk_rows, LANES), F32),
        scratch_shapes=[
            pltpu.VMEM((3, tm * SLAB, LANES), F32),
            pltpu.VMEM((blk_rows, LANES), F32),
            pltpu.SemaphoreType.DMA((3,)),
            pltpu.SemaphoreType.DMA((3,)),
            pltpu.SemaphoreType.DMA,
        ],
        compiler_params=pltpu.CompilerParams(dimension_semantics=("arbitrary",), vmem_limit_bytes=VMEM_LIMIT),
        name="dispatch",
    )(dest_flat, pad_first, hf_slab)


def _weight_copies(wgu_hbm, wd_hbm, wgu_f32, wd_f32, wsem, expert, slot):
    return (pltpu.make_async_copy(wgu_hbm.at[expert], wgu_f32.at[slot], wsem.at[0, slot]),
            pltpu.make_async_copy(wd_hbm.at[expert], wd_f32.at[slot], wsem.at[1, slot]))


def _expert_kernel(be_ref, nused_ref, first_ref, slot_ref, next_ref,
                   xs_ref, wgu_hbm, bgu_ref, wd_hbm, bd_ref, ys_ref,
                   wgu_f32, wd_f32, wgu_bf, wd_bf, wsem):
    b = pl.program_id(0)
    copies = functools.partial(_weight_copies, wgu_hbm, wd_hbm, wgu_f32, wd_f32, wsem)

    @pl.when(jnp.logical_and(b < nused_ref[0], first_ref[b] == 1))
    def _():
        slot = slot_ref[b]

        @pl.when(b == 0)
        def _():
            for cp in copies(be_ref[0], 0):
                cp.start()

        for cp in copies(be_ref[b], slot):
            cp.wait()

        @pl.when(next_ref[b] >= 0)
        def _():
            for cp in copies(next_ref[b], 1 - slot):
                cp.start()

        two = 2 * LANES
        r = lax.broadcasted_iota(jnp.int32, (two, two), 0)
        c = lax.broadcasted_iota(jnp.int32, (two, two), 1)
        src = jnp.where(c < LANES, 2 * c, 2 * (c - LANES) + 1)
        perm = jnp.where(r == src, 1.0, 0.0).astype(BF16)
        for blk in range(wgu_bf.shape[1] // two):
            wb = wgu_f32[slot, :, blk * two:(blk + 1) * two].astype(BF16)
            wgu_bf[:, blk * two:(blk + 1) * two] = _dot(wb, perm).astype(BF16)
        wd_bf[...] = wd_f32[slot].astype(BF16)

    @pl.when(b < nused_ref[0])
    def _():
        two = 2 * LANES
        x = _from_slabs(xs_ref, 0, EXPERT_BLOCK).astype(BF16)
        gu = _dot(x, wgu_bf[...]) + bgu_ref[0]
        n_blk = gu.shape[1] // two
        x_glu = jnp.concatenate([gu[:, i * two:i * two + LANES] for i in range(n_blk)], axis=1)
        x_lin = jnp.concatenate([gu[:, i * two + LANES:(i + 1) * two] for i in range(n_blk)], axis=1)
        x_glu = jnp.minimum(x_glu, SWIGLU_LIMIT)
        x_lin = jnp.clip(x_lin, -SWIGLU_LIMIT, SWIGLU_LIMIT)
        act = _sigmoid_mul(SWIGLU_ALPHA * x_glu, x_glu * (x_lin + 1.0))
        y = _dot(act.astype(BF16), wd_bf[...]) + bd_ref[0]
        _to_slabs(ys_ref, y)

    @pl.when(b >= nused_ref[0])
    def _():
        ys_ref[...] = jnp.zeros_like(ys_ref)


def _experts(block_expert, n_used, xs, w_gate_up, b_gu_perm, w_down, b_down):
    n_blocks = block_expert.shape[0]
    d = D_MODEL
    f2 = w_gate_up.shape[-1]
    blk_rows = EXPERT_BLOCK * SLAB

    idx = jnp.arange(n_blocks, dtype=jnp.int32)
    used = idx < n_used
    prev = jnp.concatenate([jnp.full((1,), -1, jnp.int32), block_expert[:-1]])
    first = (used & (block_expert != prev)).astype(jnp.int32)
    slot = (jnp.cumsum(first) - 1) % 2
    later = used[None, :] & (block_expert[None, :] > block_expert[:, None])
    nxt = jnp.min(jnp.where(later, block_expert[None, :], N_EXPERTS), axis=1)
    nxt = jnp.where(nxt == N_EXPERTS, -1, nxt)

    grid_spec = pltpu.PrefetchScalarGridSpec(
        num_scalar_prefetch=5,
        grid=(n_blocks,),
        in_specs=[
            pl.BlockSpec((blk_rows, LANES), lambda b, be, *_: (b, 0)),
            pl.BlockSpec(memory_space=pl.ANY),
            pl.BlockSpec((1, 1, f2), lambda b, be, *_: (be[b], 0, 0)),
            pl.BlockSpec(memory_space=pl.ANY),
            pl.BlockSpec((1, 1, d), lambda b, be, *_: (be[b], 0, 0)),
        ],
        out_specs=pl.BlockSpec((blk_rows, LANES), lambda b, be, *_: (b, 0)),
        scratch_shapes=[
            pltpu.VMEM((2, d, f2), F32),
            pltpu.VMEM((2, f2 // 2, d), F32),
            pltpu.VMEM((d, f2), BF16),
            pltpu.VMEM((f2 // 2, d), BF16),
            pltpu.SemaphoreType.DMA((2, 2)),
        ],
    )
    return pl.pallas_call(
        _expert_kernel,
        grid_spec=grid_spec,
        out_shape=jax.ShapeDtypeStruct((n_blocks * blk_rows, LANES), F32),
        compiler_params=pltpu.CompilerParams(
            dimension_semantics=("arbitrary",), vmem_limit_bytes=VMEM_LIMIT),
        name="expert",
    )(block_expert, n_used, first, slot.astype(jnp.int32), nxt.astype(jnp.int32),
      xs, w_gate_up, b_gu_perm, w_down, b_down)


def _combine_kernel(dest_ref, dest_next_ref, ys_hbm, w_ref, x1_ref, mod_ref, g_ref, o_ref, buf, sem):
    i = pl.program_id(0)
    n_steps = pl.num_programs(0)
    tm = x1_ref.shape[0]
    n_assign = tm * TOP_K
    slot = i % 2

    def gather(d_ref, to_slot):
        def body(g, carry):
            for u in range(DMA_UNROLL):
                t_local = g * (DMA_UNROLL // TOP_K) + u // TOP_K
                k = u % TOP_K
                _row_copy(ys_hbm, d_ref[g * DMA_UNROLL + u], buf.at[to_slot], k * tm + t_local,
                          sem.at[to_slot]).start(priority=u % 2)
            return carry

        lax.fori_loop(0, n_assign // DMA_UNROLL, body, 0)

    @pl.when(i == 0)
    def _():
        gather(dest_ref, 0)

    @pl.when(i + 1 < n_steps)
    def _():
        gather(dest_next_ref, 1 - slot)

    cur = buf.at[slot]
    pltpu.make_async_copy(ys_hbm.at[pl.ds(0, n_assign * SLAB), :], cur, sem.at[slot]).wait()

    w = w_ref[...]
    y = w[:, 0:1] * _from_slabs(cur, 0, tm)
    for k in range(1, TOP_K):
        y = y + w[:, k:k + 1] * _from_slabs(cur, k * tm, tm)
    x = x1_ref[...] + mod_ref[0, 5:6, :] * y
    ms = jnp.mean(x * x, axis=-1, keepdims=True)
    o_ref[...] = x * lax.rsqrt(ms + EPS) * g_ref[...]


def _combine(dest_flat, ys, wts, x1, mod6, g, seq):
    t = x1.shape[0]
    tm = 256
    per_b = seq // tm
    d = D_MODEL
    last = t // tm - 1
    return pl.pallas_call(
        _combine_kernel,
        grid=(t // tm,),
        in_specs=[
            pl.BlockSpec((tm * TOP_K,), lambda i: (i,), memory_space=pltpu.SMEM),
            pl.BlockSpec((tm * TOP_K,), lambda i: (jnp.minimum(i + 1, last),), memory_space=pltpu.SMEM),
            pl.BlockSpec(memory_space=pl.ANY),
            pl.BlockSpec((tm, LANES), lambda i: (i, 0)),
            pl.BlockSpec((tm, d), lambda i: (i, 0)),
            pl.BlockSpec((1, 6, d), lambda i: (i // per_b, 0, 0)),
            pl.BlockSpec((1, d), lambda i: (0, 0)),
        ],
        out_specs=pl.BlockSpec((tm, d), lambda i: (i, 0)),
        out_shape=jax.ShapeDtypeStruct((t, d), F32),
        scratch_shapes=[pltpu.VMEM((2, tm * TOP_K * SLAB, LANES), F32), pltpu.SemaphoreType.DMA((2,))],
        compiler_params=pltpu.CompilerParams(
            dimension_semantics=("arbitrary",), vmem_limit_bytes=VMEM_LIMIT),
        name="combine",
    )(dest_flat, dest_flat, ys, wts, x1, mod6, g)


def _split_w(w):
    hi = w.astype(BF16)
    lo = (w - hi.astype(F32)).astype(BF16)
    return hi, lo


def kernel(x, c, positions, w_ada, b_ada, norm_mix_g, w_in, conv_w, conv_b, b_if, ml_norm_g, ret_norm_g,
           w_branch_ml, w_branch_ret, w_out, norm_ffn_g, w_router, b_router, w_gate_up, b_gate_up, w_down,
           b_down, norm_final_g):
    batch, seq, d = x.shape
    t = batch * seq
    x2 = x.reshape(t, d)
    depth = w_ada.shape[0]
    c8 = jnp.concatenate([c, jnp.zeros((8 - batch, d), c.dtype)], axis=0)

    half = D_QK // 2
    inv = ROPE_BASE ** (-jnp.arange(half, dtype=F32) / half)
    inv2 = jnp.concatenate([inv, inv])[None, :]
    sign2 = jnp.concatenate([-jnp.ones((half,), F32), jnp.ones((half,), F32)])[None, :]
    pos_b = jnp.broadcast_to(positions.astype(F32).reshape(t, 1), (t, LANES))
    L = CHUNK
    log_gamma = jnp.log(1.0 - 2.0 ** (-5.0 - jnp.arange(N_HEADS, dtype=F32)))
    pos = jnp.arange(L, dtype=F32)
    rel = pos[:, None] - pos[None, :]
    decay = jnp.where(rel >= 0, jnp.exp(log_gamma[:, None, None] * jnp.maximum(rel, 0.0)), 0.0)
    kdec = jnp.exp(log_gamma[:, None] * (L - 1 - pos))[:, :, None]
    qdec = jnp.exp(log_gamma[:, None] * (pos + 1.0))[:, :, None]
    gch = jnp.broadcast_to(jnp.exp(log_gamma * L)[:, None, None], (N_HEADS, 1, LANES))

    assert depth == 1, "the final norm is fused after the single layer"
    l = 0
    mod = _ada(c8, w_ada[l], b_ada[l][None, :])
    mod6 = mod[:batch].reshape(batch, 6, d)

    w = w_in[l]
    w_main, w_gate = _wprep(w.T)
    w_if = w_gate.T
    zpad = jnp.zeros((d, LANES - N_HEADS), F32)
    wif = jnp.concatenate([w_if[:, :N_HEADS], zpad, w_if[:, N_HEADS:], zpad], axis=1)
    wif_hi, wif_lo = _split_w(wif)
    zb = jnp.zeros((LANES - N_HEADS,), F32)
    b_if2 = jnp.concatenate([b_if[l][:N_HEADS], zb, b_if[l][N_HEADS:], zb])[None, :]

    proj, gates, cos2, sin2 = _inproj(x2, mod6, norm_mix_g[l][None, :], w_main, wif_hi, wif_lo,
                                      pos_b, inv2, sign2, seq)
    hm, hr = _mix(proj, gates, cos2, sin2, conv_w[l], conv_b[l][None, :], b_if2,
                  ml_norm_g[l][None, :], ret_norm_g[l][None, :], decay, kdec, qdec, gch, batch, seq)

    wr = jnp.concatenate([w_router[l], jnp.zeros((d, LANES - N_EXPERTS), F32)], axis=1)
    wr_hi, wr_lo = _split_w(wr)
    br = jnp.concatenate([b_router[l], jnp.full((LANES - N_EXPERTS,), -1e30, F32)])[None, :]
    x1, hf, logits = _post(hm, hr, proj, x2, mod6, w_branch_ml[l].astype(BF16), w_branch_ret[l].astype(BF16),
                           w_out[l].astype(BF16), norm_ffn_g[l][None, :], wr_hi, wr_lo, br, seq)

    n_assign = t * TOP_K
    n_blocks = -(-n_assign // EXPERT_BLOCK) + N_EXPERTS
    dest, wts, tables = _route(logits)
    dest_flat = dest[:, :TOP_K].reshape(n_assign)
    bend = tables[0, :N_EXPERTS].astype(jnp.int32)
    pad_first = tables[1, :N_EXPERTS].astype(jnp.int32)
    n_used = bend[N_EXPERTS - 1:]
    blk = jnp.minimum(jnp.arange(n_blocks, dtype=jnp.int32), n_used - 1)
    block_expert = jnp.minimum(
        jnp.sum((bend[None, :] <= blk[:, None]).astype(jnp.int32), axis=1), N_EXPERTS - 1)
    xs = _dispatch(dest_flat, jnp.concatenate([pad_first, n_used]), hf, n_blocks)
    f2 = w_gate_up.shape[-1]
    bgu = b_gate_up[l].reshape(N_EXPERTS, f2 // (2 * LANES), LANES, 2)
    bgu = jnp.swapaxes(bgu, -1, -2).reshape(N_EXPERTS, 1, f2)
    ys = _experts(block_expert, n_used, xs, w_gate_up[l], bgu, w_down[l], b_down[l][:, None, :])
    out = _combine(dest_flat, ys, wts, x1, mod6, norm_final_g[None, :], seq)
    return out.reshape(batch, seq, d)
```

```python
import functools

import numpy as np
import jax
import jax.numpy as jnp
from jax import lax
from jax.experimental import pallas as pl
from jax.experimental.pallas import tpu as pltpu

D_MODEL = 1024
N_HEADS = 4
D_QK = 128
D_V = 256
CONV_K = 4
ROPE_BASE = 10000.0
CHUNK = 128
N_EXPERTS = 32
TOP_K = 4
SWIGLU_LIMIT = 7.0
SWIGLU_ALPHA = 1.702
EXPERT_BLOCK = 512
EPS = 1e-5

QK_W = N_HEADS * D_QK
V_W = N_HEADS * D_V
LANES = 128
GATE_W = 2 * LANES
V_EXT = D_V + LANES
MIX_GROUP = 4
SLAB = D_MODEL // LANES

C_MLQ, C_MLK, C_MLV, C_MLO = 0, 512, 1024, 2048
C_RQ, C_RK, C_RV, C_RG = 3072, 3584, 4096, 5120
C_GML, C_GRET = 6144, 7168
MIX_W = 6144
PROJ_W = 8192

BF16 = jnp.bfloat16
F32 = jnp.float32
VMEM_LIMIT = 56 * 1024 * 1024


def _dot(a, b):
    return jnp.dot(a, b, preferred_element_type=F32)


def _dot_nt(a, b):
    return lax.dot_general(a, b, (((1,), (1,)), ((), ())), preferred_element_type=F32)


def _split(a):
    hi = a.astype(BF16)
    lo = (a - hi.astype(F32)).astype(BF16)
    return hi, lo


def _dot3(a, b_hi, b_lo):
    a_hi, a_lo = _split(a)
    return _dot(a_hi, b_hi) + (_dot(a_lo, b_hi) + _dot(a_hi, b_lo))


def _sigmoid(x):
    return 0.5 * jnp.tanh(0.5 * x) + 0.5


def _sigmoid_mul(x, y):
    hy = 0.5 * y
    return hy * jnp.tanh(0.5 * x) + hy


def _silu(x):
    hx = 0.5 * x
    return hx * jnp.tanh(hx) + hx


def _round_robin(stage_generators):
    pending = list(stage_generators)
    while pending:
        for gen in list(pending):
            if next(gen, True):
                pending.remove(gen)


def _delayed(stage_generator, rounds):
    for _ in range(rounds):
        yield False
    yield from stage_generator


def _rms_mod(x, g, scale, shift):
    ms = jnp.mean(x * x, axis=-1, keepdims=True)
    return (x * lax.rsqrt(ms + EPS) * g) * (1.0 + scale) + shift


def _ada_kernel(c_ref, w_ref, b_ref, o_ref):
    w_hi, w_lo = _split(w_ref[...])
    o_ref[...] = _dot3(_silu(c_ref[...]), w_hi, w_lo) + b_ref[...]


def _ada(c8, w, b):
    n = w.shape[1]
    tn = 1024
    return pl.pallas_call(
        _ada_kernel,
        grid=(n // tn,),
        in_specs=[
            pl.BlockSpec((8, D_MODEL), lambda j: (0, 0)),
            pl.BlockSpec((D_MODEL, tn), lambda j: (0, j)),
            pl.BlockSpec((1, tn), lambda j: (0, j)),
        ],
        out_specs=pl.BlockSpec((8, tn), lambda j: (0, j)),
        out_shape=jax.ShapeDtypeStruct((8, n), F32),
        name="ada",
    )(c8, w, b)


def _inproj_kernel(x_ref, xn_ref, mod_ref, modn_ref, g_ref, w_ref, wif_hi_ref, wif_lo_ref, pos_ref, inv_ref,
                   sign_ref, proj_ref, gates_ref, cos_ref, sin_ref, hi_a, lo_a, hi_b, lo_b):
    i = pl.program_id(0)
    j = pl.program_id(1)

    def normalise(x, mod):
        return _split(_rms_mod(x, g_ref[...], mod[0, 1:2, :], mod[0, 0:1, :]))

    @pl.when(jnp.logical_and(i == 0, j == 0))
    def _():
        hi_a[...], lo_a[...] = normalise(x_ref[...], mod_ref)

    n = x_ref.shape[0] // pl.num_programs(1)
    rows = pl.ds(pl.multiple_of(j * n, n), n)
    n_split = 4
    tc = w_ref.shape[1] // n_split

    def step(cur_hi, cur_lo, nxt_hi, nxt_lo):
        def matmul_steps():
            for c in range(n_split):
                proj_ref[:, c * tc:(c + 1) * tc] = _dot(cur_hi[...], w_ref[:, c * tc:(c + 1) * tc]).astype(BF16)
                yield False

        def side_steps():
            h_hi = cur_hi[rows, :]
            h_lo = cur_lo[rows, :]
            gates_ref[rows, :] = (_dot(h_hi, wif_hi_ref[...])
                                  + (_dot(h_lo, wif_hi_ref[...]) + _dot(h_hi, wif_lo_ref[...])))
            yield False
            ang = pos_ref[rows, :] * inv_ref[...]
            cos_ref[rows, :] = jnp.cos(ang)
            yield False
            sin_ref[rows, :] = jnp.sin(ang) * sign_ref[...]
            yield False
            nxt_hi[rows, :], nxt_lo[rows, :] = normalise(xn_ref[rows, :], modn_ref)
            yield False

        _round_robin([matmul_steps(), side_steps()])

    @pl.when(i % 2 == 0)
    def _():
        step(hi_a, lo_a, hi_b, lo_b)

    @pl.when(i % 2 == 1)
    def _():
        step(hi_b, lo_b, hi_a, lo_a)


GATE_COL = 3072


def _wprep_kernel(a_ref, b_ref, o_ref, gate_ref):
    c = pl.program_id(0)
    tn = a_ref.shape[0]

    @pl.when(c * tn == GATE_COL)
    def _():
        gate_ref[...] = a_ref[0:2 * N_HEADS, :]

    @pl.when((c + 1) * tn <= GATE_COL)
    def _():
        o_ref[...] = a_ref[...].T.astype(BF16)

    @pl.when((c + 1) * tn > GATE_COL)
    def _():
        shifted = jnp.concatenate([a_ref[2 * N_HEADS:, :], b_ref[...]], axis=0)
        o_ref[...] = shifted.T.astype(BF16)


def _wprep(w_t):
    d = w_t.shape[1]
    tn = 1024
    gate = 2 * N_HEADS
    assert GATE_COL % tn == 0 and w_t.shape[0] == PROJ_W + gate
    return pl.pallas_call(
        _wprep_kernel,
        grid=(PROJ_W // tn,),
        in_specs=[
            pl.BlockSpec((tn, d), lambda c: (c, 0)),
            pl.BlockSpec((gate, d), lambda c: ((c + 1) * (tn // gate), 0)),
        ],
        out_specs=[pl.BlockSpec((d, tn), lambda c: (0, c)), pl.BlockSpec((gate, d), lambda c: (0, 0))],
        out_shape=[jax.ShapeDtypeStruct((d, PROJ_W), BF16), jax.ShapeDtypeStruct((gate, d), F32)],
        compiler_params=pltpu.CompilerParams(
            dimension_semantics=("arbitrary",), vmem_limit_bytes=VMEM_LIMIT),
        name="wprep",
    )(w_t, w_t)


def _inproj(x2, mod6, g, w_main, wif_hi, wif_lo, pos_b, inv2, sign2, seq):
    t = x2.shape[0]
    tm, tn = 1024, 2048
    per_b = seq // tm
    last = t // tm - 1
    nxt = lambda i: jnp.minimum(i + 1, last)
    return pl.pallas_call(
        _inproj_kernel,
        grid=(t // tm, PROJ_W // tn),
        in_specs=[
            pl.BlockSpec((tm, D_MODEL), lambda i, j: (i, 0)),
            pl.BlockSpec((tm, D_MODEL), lambda i, j: (nxt(i), 0)),
            pl.BlockSpec((1, 6, D_MODEL), lambda i, j: (i // per_b, 0, 0)),
            pl.BlockSpec((1, 6, D_MODEL), lambda i, j: (nxt(i) // per_b, 0, 0)),
            pl.BlockSpec((1, D_MODEL), lambda i, j: (0, 0)),
            pl.BlockSpec((D_MODEL, tn), lambda i, j: (0, j)),
            pl.BlockSpec((D_MODEL, GATE_W), lambda i, j: (0, 0)),
            pl.BlockSpec((D_MODEL, GATE_W), lambda i, j: (0, 0)),
            pl.BlockSpec((tm, LANES), lambda i, j: (i, 0)),
            pl.BlockSpec((1, LANES), lambda i, j: (0, 0)),
            pl.BlockSpec((1, LANES), lambda i, j: (0, 0)),
        ],
        out_specs=[
            pl.BlockSpec((tm, tn), lambda i, j: (i, j)),
            pl.BlockSpec((tm, GATE_W), lambda i, j: (i, 0)),
            pl.BlockSpec((tm, LANES), lambda i, j: (i, 0)),
            pl.BlockSpec((tm, LANES), lambda i, j: (i, 0)),
        ],
        out_shape=[
            jax.ShapeDtypeStruct((t, PROJ_W), BF16),
            jax.ShapeDtypeStruct((t, GATE_W), F32),
            jax.ShapeDtypeStruct((t, LANES), F32),
            jax.ShapeDtypeStruct((t, LANES), F32),
        ],
        scratch_shapes=[pltpu.VMEM((tm, D_MODEL), BF16)] * 4,
        compiler_params=pltpu.CompilerParams(
            dimension_semantics=("arbitrary", "arbitrary"), vmem_limit_bytes=VMEM_LIMIT),
        name="inproj",
    )(x2, x2, mod6, mod6, g, w_main, wif_hi, wif_lo, pos_b, inv2, sign2)


def _row_mean(a):
    inv_n = jnp.full((a.shape[1], LANES), 1.0 / a.shape[1], BF16)
    m = _dot(a.astype(BF16), inv_n)
    return jnp.concatenate([m] * (a.shape[1] // LANES), axis=1)


def _head_norm(h, g):
    d = h - _row_mean(h)
    return d * lax.rsqrt(_row_mean(d * d) + EPS) * g


def _mix_kernel(proj_ref, gates_ref, cos_ref, sin_ref, convw_ref, convb_ref, bif_ref, mlg_ref, retg_ref,
                decay_ref, kdec_ref, qdec_ref, gch_ref,
                hm_ref, hr_ref,
                prev_ref, c_st, m_st, r_st):
    group = proj_ref.shape[0]

    @pl.when(pl.program_id(1) == 0)
    def _():
        def zero_state(i, carry):
            c_st[i] = jnp.zeros(c_st.shape[1:], F32)
            r_st[i] = jnp.zeros(r_st.shape[1:], F32)
            return carry

        lax.fori_loop(0, group * N_HEADS, zero_state, 0)
        prev_ref[...] = jnp.zeros_like(prev_ref)
        m_st[...] = jnp.zeros_like(m_st)

    ml_steps, ret_steps = [], []
    for gb in range(group):
        states = slice(gb * N_HEADS, (gb + 1) * N_HEADS)
        ml_steps.append(_mlstm_steps(proj_ref.at[gb], gates_ref.at[gb], convw_ref, convb_ref, bif_ref, mlg_ref,
                                     hm_ref.at[gb], prev_ref.at[gb], c_st.at[states], m_st.at[gb]))
        ret_steps.append(_retention_steps(proj_ref.at[gb], cos_ref.at[gb], sin_ref.at[gb], retg_ref, decay_ref,
                                          kdec_ref, qdec_ref, gch_ref, hr_ref.at[gb], r_st.at[states]))
    _round_robin(ml_steps + [_delayed(g, 4) for g in ret_steps])


def _mlstm_steps(proj_ref, gates_ref, convw_ref, convb_ref, bif_ref, mlg_ref, hm_ref, prev_ref, c_st, m_st):
    L = CHUNK
    rows = lax.broadcasted_iota(jnp.int32, (L, L), 0)
    cols = lax.broadcasted_iota(jnp.int32, (L, L), 1)
    causal = rows >= cols
    tril = jnp.where(causal, 1.0, 0.0).astype(BF16)

    cur = proj_ref[:, C_MLQ:C_MLQ + 2 * QK_W]
    xx = jnp.concatenate([prev_ref[...], cur], axis=0)
    r2 = lax.broadcasted_iota(jnp.int32, (L, 2 * L), 0)
    c2 = lax.broadcasted_iota(jnp.int32, (L, 2 * L), 1)
    acc = convb_ref[...] + cur.astype(F32) * convw_ref[CONV_K - 1:CONV_K, :]
    for d in range(1, CONV_K):
        shift = jnp.where(c2 == r2 + (L - d), 1.0, 0.0).astype(BF16)
        acc = acc + _dot(shift, xx) * convw_ref[CONV_K - 1 - d:CONV_K - d, :]
        yield False
    prev_ref[...] = cur
    qk = _silu(acc.astype(BF16))
    yield False

    g = gates_ref[...] + bif_ref[...]
    gi = g[:, :LANES]
    gf = g[:, LANES:]
    lf = jnp.minimum(gf, 0.0) - jnp.log(1.0 + jnp.exp(-jnp.abs(gf)))
    lf_hi, lf_lo = _split(lf)
    a_all = _dot(tril, lf_hi) + _dot(tril, lf_lo)
    yield False
    a_last = a_all[L - 1:L, :]
    bm = gi - a_all
    bm_t = bm.T
    w_state = a_last + bm
    m_loc = jnp.max(w_state, axis=0, keepdims=True)
    m_prev = m_st[...]
    inter_log = a_all + m_prev
    m_new = jnp.maximum(a_last + m_prev, m_loc)
    s_prev = jnp.exp(a_last + m_prev - m_new)
    s_loc = jnp.exp(m_loc - m_new)
    ws_all = jnp.exp(w_state - m_loc) * s_loc
    m_st[...] = m_new
    yield False

    ones_blk = jnp.ones((L, LANES), BF16)
    q_scale = D_QK ** -0.5

    for h in range(N_HEADS):
        q = qk[:, h * D_QK:(h + 1) * D_QK] * q_scale
        k = qk[:, QK_W + h * D_QK:QK_W + (h + 1) * D_QK]
        k_f = k.astype(F32)
        v_ext = jnp.concatenate([proj_ref[:, C_MLV + h * D_V:C_MLV + (h + 1) * D_V], ones_blk], axis=1)
        dlog = jnp.where(causal, a_all[:, h:h + 1] + bm_t[h:h + 1, :], -jnp.inf)
        m_intra = jnp.max(dlog, axis=-1, keepdims=True)
        s = _dot_nt(q, k)
        qc = _dot(q, c_st[h].astype(BF16))
        yield False
        il = inter_log[:, h:h + 1]
        m_t = jnp.maximum(il, m_intra)
        p = jnp.exp(dlog - m_t) * s
        isc = jnp.exp(il - m_t)
        tot = _dot(p.astype(BF16), v_ext) + isc * qc
        yield False
        den = tot[:, D_V:D_V + 1]
        hout = tot[:, :D_V] / jnp.maximum(jnp.abs(den), jnp.exp(-m_t))
        kw_t = (k_f * ws_all[:, h:h + 1]).T.astype(BF16)
        c_st[h] = s_prev[:, h:h + 1] * c_st[h] + _dot(kw_t, v_ext)
        yield False
        y = _head_norm(hout, mlg_ref[:, h * D_V:(h + 1) * D_V])
        o = proj_ref[:, C_MLO + h * D_V:C_MLO + (h + 1) * D_V]
        hm_ref[:, h * D_V:(h + 1) * D_V] = _sigmoid_mul(o, y.astype(BF16))
        yield False


def _retention_steps(proj_ref, cos_ref, sin_ref, retg_ref, decay_ref, kdec_ref, qdec_ref, gch_ref, hr_ref, r_st):
    cos2 = cos_ref[...]
    sin2 = sin_ref[...]
    k_scale = D_QK ** -0.5
    for h in range(N_HEADS):
        q_raw = proj_ref[:, C_RQ + h * D_QK:C_RQ + (h + 1) * D_QK].astype(F32)
        k_raw = proj_ref[:, C_RK + h * D_QK:C_RK + (h + 1) * D_QK].astype(F32)
        q = (q_raw * cos2 + pltpu.roll(q_raw, D_QK // 2, 1) * sin2).astype(BF16)
        k_f = (k_raw * cos2 + pltpu.roll(k_raw, D_QK // 2, 1) * sin2) * k_scale
        v = proj_ref[:, C_RV + h * D_V:C_RV + (h + 1) * D_V]
        yield False
        sc = _dot_nt(q, k_f.astype(BF16)) * decay_ref[h]
        hret = _dot(sc.astype(BF16), v) + _dot(q, r_st[h].astype(BF16)) * qdec_ref[h]
        yield False
        kd_t = (k_f * kdec_ref[h]).T.astype(BF16)
        r_st[h] = gch_ref[h][:, 0:1] * r_st[h] + _dot(kd_t, v)
        yield False
        y = _head_norm(hret, retg_ref[:, h * D_V:(h + 1) * D_V])
        gt = proj_ref[:, C_RG + h * D_V:C_RG + (h + 1) * D_V]
        hr_ref[:, h * D_V:(h + 1) * D_V] = _silu(gt) * y.astype(BF16)
        yield False


def _mix(proj, gates, cos2, sin2, conv_w, conv_b, b_if2, ml_g, ret_g, decay, kdec, qdec, gch, batch, seq):
    t = proj.shape[0]
    nc = seq // CHUNK
    L = CHUNK
    G = MIX_GROUP
    proj = proj.reshape(batch, seq, PROJ_W)
    gates = gates.reshape(batch, seq, GATE_W)
    cos2 = cos2.reshape(batch, seq, LANES)
    sin2 = sin2.reshape(batch, seq, LANES)
    full = lambda shape: pl.BlockSpec(shape, lambda b, c: (0,) * len(shape))
    hm, hr = pl.pallas_call(
        _mix_kernel,
        grid=(batch // G, nc),
        in_specs=[
            pl.BlockSpec((G, L, MIX_W), lambda b, c: (b, c, 0)),
            pl.BlockSpec((G, L, GATE_W), lambda b, c: (b, c, 0)),
            pl.BlockSpec((G, L, LANES), lambda b, c: (b, c, 0)),
            pl.BlockSpec((G, L, LANES), lambda b, c: (b, c, 0)),
            full((CONV_K, 2 * QK_W)),
            full((1, 2 * QK_W)),
            full((1, GATE_W)),
            full((1, V_W)),
            full((1, V_W)),
            full((N_HEADS, L, L)),
            full((N_HEADS, L, 1)),
            full((N_HEADS, L, 1)),
            full((N_HEADS, 1, LANES)),
        ],
        out_specs=[
            pl.BlockSpec((G, L, V_W), lambda b, c: (b, c, 0)),
            pl.BlockSpec((G, L, V_W), lambda b, c: (b, c, 0)),
        ],
        out_shape=[jax.ShapeDtypeStruct((batch, seq, V_W), BF16), jax.ShapeDtypeStruct((batch, seq, V_W), BF16)],
        scratch_shapes=[
            pltpu.VMEM((G, L, 2 * QK_W), BF16),
            pltpu.VMEM((G * N_HEADS, D_QK, V_EXT), F32),
            pltpu.VMEM((G, 1, LANES), F32),
            pltpu.VMEM((G * N_HEADS, D_QK, D_V), F32),
        ],
        compiler_params=pltpu.CompilerParams(
            dimension_semantics=("arbitrary", "arbitrary"), vmem_limit_bytes=VMEM_LIMIT),
        name="mix",
    )(proj, gates, cos2, sin2, conv_w, conv_b, b_if2, ml_g, ret_g, decay, kdec, qdec, gch)
    return hm.reshape(t, V_W), hr.reshape(t, V_W)


def _to_slabs(ref, val):
    rows = val.shape[0]
    for s in range(SLAB):
        ref[pl.ds(s, rows, stride=SLAB), :] = val[:, s * LANES:(s + 1) * LANES]


def _from_slabs(ref, first_slab, rows):
    return jnp.concatenate(
        [ref[pl.ds(first_slab * SLAB + s, rows, stride=SLAB), :] for s in range(SLAB)], axis=1)


def _post_kernel(hm_ref, hr_ref, gm_ref, gr_ref, x_ref, mod_ref, wbm_ref, wbr_ref, wout_ref, g_ref,
                 wr_hi_ref, wr_lo_ref, br_ref, x1_ref, hf_ref, logit_ref):
    n_sub = 2
    sub = x_ref.shape[0] // n_sub

    def steps(s):
        r = pl.ds(s * sub, sub)
        ym = _dot(hm_ref[r, :], wbm_ref[...])
        yield False
        yr = _dot(hr_ref[r, :], wbr_ref[...])
        yield False
        y = _sigmoid_mul(gm_ref[r, :].astype(F32), ym) + _sigmoid_mul(gr_ref[r, :].astype(F32), yr)
        o = _dot(y.astype(BF16), wout_ref[...])
        yield False
        x1 = x_ref[r, :] + mod_ref[0, 2:3, :] * o
        x1_ref[r, :] = x1
        hf = _rms_mod(x1, g_ref[...], mod_ref[0, 4:5, :], mod_ref[0, 3:4, :])
        yield False
        _to_slabs(hf_ref.at[pl.ds(s * sub * SLAB, sub * SLAB)], hf)
        logit_ref[r, :] = _dot3(hf, wr_hi_ref[...], wr_lo_ref[...]) + br_ref[...]
        yield False

    _round_robin([steps(s) for s in range(n_sub)])


def _post(hm, hr, proj, x2, mod6, wbm, wbr, wout, g, wr_hi, wr_lo, br, seq):
    t = x2.shape[0]
    tm = 512
    per_b = seq // tm
    d = D_MODEL
    const = lambda shape: pl.BlockSpec(shape, lambda i: (0,) * len(shape))
    return pl.pallas_call(
        _post_kernel,
        grid=(t // tm,),
        in_specs=[
            pl.BlockSpec((tm, d), lambda i: (i, 0)),
            pl.BlockSpec((tm, d), lambda i: (i, 0)),
            pl.BlockSpec((tm, d), lambda i: (i, C_GML // d)),
            pl.BlockSpec((tm, d), lambda i: (i, C_GRET // d)),
            pl.BlockSpec((tm, d), lambda i: (i, 0)),
            pl.BlockSpec((1, 6, d), lambda i: (i // per_b, 0, 0)),
            const((d, d)), const((d, d)), const((d, d)),
            const((1, d)),
            const((d, LANES)), const((d, LANES)), const((1, LANES)),
        ],
        out_specs=[
            pl.BlockSpec((tm, d), lambda i: (i, 0)),
            pl.BlockSpec((tm * SLAB, LANES), lambda i: (i, 0)),
            pl.BlockSpec((tm, LANES), lambda i: (i, 0)),
        ],
        out_shape=[
            jax.ShapeDtypeStruct((t, d), F32),
            jax.ShapeDtypeStruct((t * SLAB, LANES), F32),
            jax.ShapeDtypeStruct((t, LANES), F32),
        ],
        compiler_params=pltpu.CompilerParams(
            dimension_semantics=("arbitrary",), vmem_limit_bytes=VMEM_LIMIT),
        name="post",
    )(hm, hr, proj, proj, x2, mod6, wbm, wbr, wout, g, wr_hi, wr_lo, br)


def _route_kernel(logit_ref, dest_ref, w_ref, bend_ref, cnt_st, pad_st, sel_st, pick_st, wts_st):
    ph = pl.program_id(0)
    i = pl.program_id(1)
    tm = logit_ref.shape[0]
    rows = pl.ds(pl.multiple_of(i * tm, tm), tm)
    lane = lax.broadcasted_iota(jnp.int32, (tm, LANES), 1)
    lane_f = lane.astype(F32)

    @pl.when(jnp.logical_and(ph == 0, i == 0))
    def _():
        cnt_st[...] = jnp.zeros_like(cnt_st)

    @pl.when(ph == 0)
    def _():
        l = logit_ref[...]
        picks, vals = [], []
        sel = jnp.zeros((tm, LANES), F32)
        for _ in range(TOP_K):
            m = jnp.max(l, axis=-1, keepdims=True)
            idx = jnp.min(jnp.where(l == m, lane_f, float(LANES)), axis=-1, keepdims=True)
            oh = lane_f == idx
            picks.append(idx)
            vals.append(m)
            sel = jnp.where(oh, 1.0, sel)
            l = jnp.where(oh, -jnp.inf, l)
        ex = [jnp.exp(v - vals[0]) for v in vals]
        den = ex[0] + ex[1] + ex[2] + ex[3]
        pick = jnp.zeros((tm, LANES), F32)
        wts = jnp.zeros((tm, LANES), F32)
        for k in range(TOP_K):
            pick = jnp.where(lane == k, picks[k], pick)
            wts = jnp.where(lane == k, ex[k] / den, wts)
        sel_st[rows, :] = sel.astype(BF16)
        pick_st[rows, :] = pick
        wts_st[rows, :] = wts
        cnt_st[...] = cnt_st[...] + jnp.sum(sel, axis=0, keepdims=True)

    @pl.when(jnp.logical_and(ph == 1, i == 0))
    def _():
        blocks = jnp.floor((cnt_st[...] + (EXPERT_BLOCK - 1)) * (1.0 / EXPERT_BLOCK))
        r = lax.broadcasted_iota(jnp.int32, (LANES, LANES), 0)
        c = lax.broadcasted_iota(jnp.int32, (LANES, LANES), 1)
        upper = jnp.where(r < c, 1.0, 0.0).astype(BF16)
        blocks8 = jnp.broadcast_to(blocks, (8, LANES))
        excl = _dot(blocks8.astype(BF16), upper)
        pad_st[...] = excl[0:1, :] * EXPERT_BLOCK
        row = lax.broadcasted_iota(jnp.int32, (8, LANES), 0)
        bend_ref[...] = jnp.where(row == 0, excl + blocks8,
                                  jnp.where(row == 1, excl * EXPERT_BLOCK + cnt_st[...], 0.0))
        cnt_st[...] = jnp.zeros_like(cnt_st)

    @pl.when(ph == 1)
    def _():
        carry = cnt_st[...]
        sel = sel_st[rows, :]
        pick = pick_st[rows, :]
        r = lax.broadcasted_iota(jnp.int32, (tm, tm), 0)
        c = lax.broadcasted_iota(jnp.int32, (tm, tm), 1)
        lower = jnp.where(r > c, 1.0, 0.0).astype(BF16)
        base = pad_st[...] + carry + _dot(lower, sel)
        dest = jnp.zeros((tm, LANES), F32)
        for k in range(TOP_K):
            dk = jnp.sum(jnp.where(lane_f == pick[:, k:k + 1], base, 0.0), axis=-1, keepdims=True)
            dest = jnp.where(lane == k, dk, dest)
        dest_ref[...] = dest.astype(jnp.int32)
        w_ref[...] = wts_st[rows, :]
        cnt_st[...] = carry + jnp.sum(sel.astype(F32), axis=0, keepdims=True)


def _route(logits):
    t = logits.shape[0]
    tm = 512
    return pl.pallas_call(
        _route_kernel,
        grid=(2, t // tm),
        in_specs=[pl.BlockSpec((tm, LANES), lambda ph, i: (i * (1 - ph), 0))],
        out_specs=[
            pl.BlockSpec((tm, LANES), lambda ph, i: (i * ph, 0)),
            pl.BlockSpec((tm, LANES), lambda ph, i: (i * ph, 0)),
            pl.BlockSpec((8, LANES), lambda ph, i: (0, 0)),
        ],
        out_shape=[
            jax.ShapeDtypeStruct((t, LANES), jnp.int32),
            jax.ShapeDtypeStruct((t, LANES), F32),
            jax.ShapeDtypeStruct((8, LANES), F32),
        ],
        scratch_shapes=[
            pltpu.VMEM((1, LANES), F32),
            pltpu.VMEM((1, LANES), F32),
            pltpu.VMEM((t, LANES), BF16),
            pltpu.VMEM((t, LANES), F32),
            pltpu.VMEM((t, LANES), F32),
        ],
        compiler_params=pltpu.CompilerParams(
            dimension_semantics=("arbitrary", "arbitrary"), vmem_limit_bytes=VMEM_LIMIT),
        name="route",
    )(logits)


DMA_UNROLL = 8


def _row_copy(src_ref, src_slab, dst_ref, dst_slab, sem):
    src = pl.multiple_of(src_slab * SLAB, SLAB)
    dst = pl.multiple_of(dst_slab * SLAB, SLAB)
    return pltpu.make_async_copy(src_ref.at[pl.ds(src, SLAB), :], dst_ref.at[pl.ds(dst, SLAB), :], sem)


def _dispatch_kernel(dest_ref, padfirst_ref, hf_hbm, xs_out, tiles, zbuf, lsem, rsem, zsem):
    i = pl.program_id(0)
    n_steps = pl.num_programs(0)
    n_assign = dest_ref.shape[0]
    tile_rows = tiles.shape[1]
    blk_rows = zbuf.shape[0]
    n_buf = tiles.shape[0]

    def tile_load(step, slot):
        start = pl.multiple_of(step * tile_rows, tile_rows)
        return pltpu.make_async_copy(hf_hbm.at[pl.ds(start, tile_rows), :], tiles.at[slot], lsem.at[slot])

    def wait_rows(slot):
        for _ in range(TOP_K):
            pltpu.make_async_copy(tiles.at[slot], xs_out.at[pl.ds(0, tile_rows), :], rsem.at[slot]).wait()

    @pl.when(i == 0)
    def _():
        tile_load(0, 0).start()
        tile_load(1, 1).start()
        zbuf[...] = jnp.zeros_like(zbuf)
        for e in range(N_EXPERTS):
            start = pl.multiple_of(padfirst_ref[e] * SLAB, SLAB)
            pltpu.make_async_copy(zbuf, xs_out.at[pl.ds(start, blk_rows), :], zsem).start()
        for e in range(N_EXPERTS):
            pltpu.make_async_copy(zbuf, xs_out.at[pl.ds(0, blk_rows), :], zsem).wait()

        n_used = padfirst_ref[N_EXPERTS]
        n_total = xs_out.shape[0] // blk_rows

        def zero_start(b, carry):
            start = pl.multiple_of(b * blk_rows, blk_rows)
            pltpu.make_async_copy(zbuf, xs_out.at[pl.ds(start, blk_rows), :], zsem).start()
            return carry

        def zero_wait(b, carry):
            pltpu.make_async_copy(zbuf, xs_out.at[pl.ds(0, blk_rows), :], zsem).wait()
            return carry

        lax.fori_loop(n_used, n_total, zero_start, 0)
        lax.fori_loop(n_used, n_total, zero_wait, 0)

    slot = i % n_buf
    tile_load(i, slot).wait()
    tile = tiles.at[slot]

    def body(g, carry):
        for u in range(DMA_UNROLL):
            t_local = g * (DMA_UNROLL // TOP_K) + u // TOP_K
            _row_copy(tile, t_local, xs_out, dest_ref[g * DMA_UNROLL + u], rsem.at[slot]).start(priority=u % 2)
        return carry

    lax.fori_loop(0, n_assign // DMA_UNROLL, body, 0)

    @pl.when(i >= 1)
    def _():
        wait_rows((i + n_buf - 1) % n_buf)

    @pl.when(i + 2 < n_steps)
    def _():
        tile_load(i + 2, (i + 2) % n_buf).start()

    @pl.when(i == n_steps - 1)
    def _():
        wait_rows(slot)


def _dispatch(dest_flat, pad_first, hf_slab, n_blocks):
    n_assign = dest_flat.shape[0]
    tm = 512
    blk_rows = EXPERT_BLOCK * SLAB
    return pl.pallas_call(
        _dispatch_kernel,
        grid=(n_assign // (tm * TOP_K),),
        in_specs=[
            pl.BlockSpec((tm * TOP_K,), lambda i: (i,), memory_space=pltpu.SMEM),
            pl.BlockSpec(memory_space=pltpu.SMEM),
            pl.BlockSpec(memory_space=pl.ANY),
        ],
        out_specs=pl.BlockSpec(memory_space=pl.ANY),
        out_shape=jax.ShapeDtypeStruct(((n_blocks + 1) * blk_rows, LANES), F32),
        scratch_shapes=[
            pltpu.VMEM((3, tm * SLAB, LANES), F32),
            pltpu.VMEM((blk_rows, LANES), F32),
            pltpu.SemaphoreType.DMA((3,)),
            pltpu.SemaphoreType.DMA((3,)),
            pltpu.SemaphoreType.DMA,
        ],
        compiler_params=pltpu.CompilerParams(dimension_semantics=("arbitrary",), vmem_limit_bytes=VMEM_LIMIT),
        name="dispatch",
    )(dest_flat, pad_first, hf_slab)


def _weight_copies(wgu_hbm, wd_hbm, wgu_f32, wd_f32, wsem, expert, slot):
    return (pltpu.make_async_copy(wgu_hbm.at[expert], wgu_f32.at[slot], wsem.at[0, slot]),
            pltpu.make_async_copy(wd_hbm.at[expert], wd_f32.at[slot], wsem.at[1, slot]))


def _expert_kernel(be_ref, nused_ref, first_ref, slot_ref, next_ref,
                   xs_ref, wgu_hbm, bgu_ref, wd_hbm, bd_ref, ys_ref,
                   wgu_f32, wd_f32, wgu_bf, wd_bf, wsem):
    b = pl.program_id(0)
    copies = functools.partial(_weight_copies, wgu_hbm, wd_hbm, wgu_f32, wd_f32, wsem)

    @pl.when(jnp.logical_and(b < nused_ref[0], first_ref[b] == 1))
    def _():
        slot = slot_ref[b]

        @pl.when(b == 0)
        def _():
            for cp in copies(be_ref[0], 0):
                cp.start()

        for cp in copies(be_ref[b], slot):
            cp.wait()

        @pl.when(next_ref[b] >= 0)
        def _():
            for cp in copies(next_ref[b], 1 - slot):
                cp.start()

        two = 2 * LANES
        r = lax.broadcasted_iota(jnp.int32, (two, two), 0)
        c = lax.broadcasted_iota(jnp.int32, (two, two), 1)
        src = jnp.where(c < LANES, 2 * c, 2 * (c - LANES) + 1)
        perm = jnp.where(r == src, 1.0, 0.0).astype(BF16)
        for blk in range(wgu_bf.shape[1] // two):
            wb = wgu_f32[slot, :, blk * two:(blk + 1) * two].astype(BF16)
            wgu_bf[:, blk * two:(blk + 1) * two] = _dot(wb, perm).astype(BF16)
        wd_bf[...] = wd_f32[slot].astype(BF16)

    @pl.when(b < nused_ref[0])
    def _():
        two = 2 * LANES
        x = _from_slabs(xs_ref, 0, EXPERT_BLOCK).astype(BF16)
        gu = _dot(x, wgu_bf[...]) + bgu_ref[0]
        n_blk = gu.shape[1] // two
        x_glu = jnp.concatenate([gu[:, i * two:i * two + LANES] for i in range(n_blk)], axis=1)
        x_lin = jnp.concatenate([gu[:, i * two + LANES:(i + 1) * two] for i in range(n_blk)], axis=1)
        x_glu = jnp.minimum(x_glu, SWIGLU_LIMIT)
        x_lin = jnp.clip(x_lin, -SWIGLU_LIMIT, SWIGLU_LIMIT)
        act = _sigmoid_mul(SWIGLU_ALPHA * x_glu, x_glu * (x_lin + 1.0))
        y = _dot(act.astype(BF16), wd_bf[...]) + bd_ref[0]
        _to_slabs(ys_ref, y)

    @pl.when(b >= nused_ref[0])
    def _():
        ys_ref[...] = jnp.zeros_like(ys_ref)


def _experts(block_expert, n_used, xs, w_gate_up, b_gu_perm, w_down, b_down):
    n_blocks = block_expert.shape[0]
    d = D_MODEL
    f2 = w_gate_up.shape[-1]
    blk_rows = EXPERT_BLOCK * SLAB

    idx = jnp.arange(n_blocks, dtype=jnp.int32)
    used = idx < n_used
    prev = jnp.concatenate([jnp.full((1,), -1, jnp.int32), block_expert[:-1]])
    first = (used & (block_expert != prev)).astype(jnp.int32)
    slot = (jnp.cumsum(first) - 1) % 2
    later = used[None, :] & (block_expert[None, :] > block_expert[:, None])
    nxt = jnp.min(jnp.where(later, block_expert[None, :], N_EXPERTS), axis=1)
    nxt = jnp.where(nxt == N_EXPERTS, -1, nxt)

    grid_spec = pltpu.PrefetchScalarGridSpec(
        num_scalar_prefetch=5,
        grid=(n_blocks,),
        in_specs=[
            pl.BlockSpec((blk_rows, LANES), lambda b, be, *_: (b, 0)),
            pl.BlockSpec(memory_space=pl.ANY),
            pl.BlockSpec((1, 1, f2), lambda b, be, *_: (be[b], 0, 0)),
            pl.BlockSpec(memory_space=pl.ANY),
            pl.BlockSpec((1, 1, d), lambda b, be, *_: (be[b], 0, 0)),
        ],
        out_specs=pl.BlockSpec((blk_rows, LANES), lambda b, be, *_: (b, 0)),
        scratch_shapes=[
            pltpu.VMEM((2, d, f2), F32),
            pltpu.VMEM((2, f2 // 2, d), F32),
            pltpu.VMEM((d, f2), BF16),
            pltpu.VMEM((f2 // 2, d), BF16),
            pltpu.SemaphoreType.DMA((2, 2)),
        ],
    )
    return pl.pallas_call(
        _expert_kernel,
        grid_spec=grid_spec,
        out_shape=jax.ShapeDtypeStruct((n_blocks * blk_rows, LANES), F32),
        compiler_params=pltpu.CompilerParams(
            dimension_semantics=("arbitrary",), vmem_limit_bytes=VMEM_LIMIT),
        name="expert",
    )(block_expert, n_used, first, slot.astype(jnp.int32), nxt.astype(jnp.int32),
      xs, w_gate_up, b_gu_perm, w_down, b_down)


def _combine_kernel(dest_ref, dest_next_ref, ys_hbm, w_ref, x1_ref, mod_ref, g_ref, o_ref, buf, sem):
    i = pl.program_id(0)
    n_steps = pl.num_programs(0)
    tm = x1_ref.shape[0]
    n_assign = tm * TOP_K
    slot = i % 2

    def gather(d_ref, to_slot):
        def body(g, carry):
            for u in range(DMA_UNROLL):
                t_local = g * (DMA_UNROLL // TOP_K) + u // TOP_K
                k = u % TOP_K
                _row_copy(ys_hbm, d_ref[g * DMA_UNROLL + u], buf.at[to_slot], k * tm + t_local,
                          sem.at[to_slot]).start(priority=u % 2)
            return carry

        lax.fori_loop(0, n_assign // DMA_UNROLL, body, 0)

    @pl.when(i == 0)
    def _():
        gather(dest_ref, 0)

    @pl.when(i + 1 < n_steps)
    def _():
        gather(dest_next_ref, 1 - slot)

    cur = buf.at[slot]
    pltpu.make_async_copy(ys_hbm.at[pl.ds(0, n_assign * SLAB), :], cur, sem.at[slot]).wait()

    w = w_ref[...]
    y = w[:, 0:1] * _from_slabs(cur, 0, tm)
    for k in range(1, TOP_K):
        y = y + w[:, k:k + 1] * _from_slabs(cur, k * tm, tm)
    x = x1_ref[...] + mod_ref[0, 5:6, :] * y
    ms = jnp.mean(x * x, axis=-1, keepdims=True)
    o_ref[...] = x * lax.rsqrt(ms + EPS) * g_ref[...]


def _combine(dest_flat, ys, wts, x1, mod6, g, seq):
    t = x1.shape[0]
    tm = 256
    per_b = seq // tm
    d = D_MODEL
    last = t // tm - 1
    return pl.pallas_call(
        _combine_kernel,
        grid=(t // tm,),
        in_specs=[
            pl.BlockSpec((tm * TOP_K,), lambda i: (i,), memory_space=pltpu.SMEM),
            pl.BlockSpec((tm * TOP_K,), lambda i: (jnp.minimum(i + 1, last),), memory_space=pltpu.SMEM),
            pl.BlockSpec(memory_space=pl.ANY),
            pl.BlockSpec((tm, LANES), lambda i: (i, 0)),
            pl.BlockSpec((tm, d), lambda i: (i, 0)),
            pl.BlockSpec((1, 6, d), lambda i: (i // per_b, 0, 0)),
            pl.BlockSpec((1, d), lambda i: (0, 0)),
        ],
        out_specs=pl.BlockSpec((tm, d), lambda i: (i, 0)),
        out_shape=jax.ShapeDtypeStruct((t, d), F32),
        scratch_shapes=[pltpu.VMEM((2, tm * TOP_K * SLAB, LANES), F32), pltpu.SemaphoreType.DMA((2,))],
        compiler_params=pltpu.CompilerParams(
            dimension_semantics=("arbitrary",), vmem_limit_bytes=VMEM_LIMIT),
        name="combine",
    )(dest_flat, dest_flat, ys, wts, x1, mod6, g)


def _split_w(w):
    hi = w.astype(BF16)
    lo = (w - hi.astype(F32)).astype(BF16)
    return hi, lo


def kernel(x, c, positions, w_ada, b_ada, norm_mix_g, w_in, conv_w, conv_b, b_if, ml_norm_g, ret_norm_g,
           w_branch_ml, w_branch_ret, w_out, norm_ffn_g, w_router, b_router, w_gate_up, b_gate_up, w_down,
           b_down, norm_final_g):
    batch, seq, d = x.shape
    t = batch * seq
    x2 = x.reshape(t, d)
    depth = w_ada.shape[0]
    c8 = jnp.concatenate([c, jnp.zeros((8 - batch, d), c.dtype)], axis=0)

    half = D_QK // 2
    inv = ROPE_BASE ** (-jnp.arange(half, dtype=F32) / half)
    inv2 = jnp.concatenate([inv, inv])[None, :]
    sign2 = jnp.concatenate([-jnp.ones((half,), F32), jnp.ones((half,), F32)])[None, :]
    pos_b = jnp.broadcast_to(positions.astype(F32).reshape(t, 1), (t, LANES))
    L = CHUNK
    log_gamma = jnp.log(1.0 - 2.0 ** (-5.0 - jnp.arange(N_HEADS, dtype=F32)))
    pos = jnp.arange(L, dtype=F32)
    rel = pos[:, None] - pos[None, :]
    decay = jnp.where(rel >= 0, jnp.exp(log_gamma[:, None, None] * jnp.maximum(rel, 0.0)), 0.0)
    kdec = jnp.exp(log_gamma[:, None] * (L - 1 - pos))[:, :, None]
    qdec = jnp.exp(log_gamma[:, None] * (pos + 1.0))[:, :, None]
    gch = jnp.broadcast_to(jnp.exp(log_gamma * L)[:, None, None], (N_HEADS, 1, LANES))

    assert depth == 1, "the final norm is fused after the single layer"
    l = 0
    mod = _ada(c8, w_ada[l], b_ada[l][None, :])
    mod6 = mod[:batch].reshape(batch, 6, d)

    w = w_in[l]
    w_main, w_gate = _wprep(w.T)
    w_if = w_gate.T
    zpad = jnp.zeros((d, LANES - N_HEADS), F32)
    wif = jnp.concatenate([w_if[:, :N_HEADS], zpad, w_if[:, N_HEADS:], zpad], axis=1)
    wif_hi, wif_lo = _split_w(wif)
    zb = jnp.zeros((LANES - N_HEADS,), F32)
    b_if2 = jnp.concatenate([b_if[l][:N_HEADS], zb, b_if[l][N_HEADS:], zb])[None, :]

    proj, gates, cos2, sin2 = _inproj(x2, mod6, norm_mix_g[l][None, :], w_main, wif_hi, wif_lo,
                                      pos_b, inv2, sign2, seq)
    hm, hr = _mix(proj, gates, cos2, sin2, conv_w[l], conv_b[l][None, :], b_if2,
                  ml_norm_g[l][None, :], ret_norm_g[l][None, :], decay, kdec, qdec, gch, batch, seq)

    wr = jnp.concatenate([w_router[l], jnp.zeros((d, LANES - N_EXPERTS), F32)], axis=1)
    wr_hi, wr_lo = _split_w(wr)
    br = jnp.concatenate([b_router[l], jnp.full((LANES - N_EXPERTS,), -1e30, F32)])[None, :]
    x1, hf, logits = _post(hm, hr, proj, x2, mod6, w_branch_ml[l].astype(BF16), w_branch_ret[l].astype(BF16),
                           w_out[l].astype(BF16), norm_ffn_g[l][None, :], wr_hi, wr_lo, br, seq)

    n_assign = t * TOP_K
    n_blocks = -(-n_assign // EXPERT_BLOCK) + N_EXPERTS
    dest, wts, tables = _route(logits)
    dest_flat = dest[:, :TOP_K].reshape(n_assign)
    bend = tables[0, :N_EXPERTS].astype(jnp.int32)
    pad_first = tables[1, :N_EXPERTS].astype(jnp.int32)
    n_used = bend[N_EXPERTS - 1:]
    blk = jnp.minimum(jnp.arange(n_blocks, dtype=jnp.int32), n_used - 1)
    block_expert = jnp.minimum(
        jnp.sum((bend[None, :] <= blk[:, None]).astype(jnp.int32), axis=1), N_EXPERTS - 1)
    xs = _dispatch(dest_flat, jnp.concatenate([pad_first, n_used]), hf, n_blocks)
    f2 = w_gate_up.shape[-1]
    bgu = b_gate_up[l].reshape(N_EXPERTS, f2 // (2 * LANES), LANES, 2)
    bgu = jnp.swapaxes(bgu, -1, -2).reshape(N_EXPERTS, 1, f2)
    ys = _experts(block_expert, n_used, xs, w_gate_up[l], bgu, w_down[l], b_down[l][:, None, :])
    out = _combine(dest_flat, ys, wts, x1, mod6, norm_final_g[None, :], seq)
    return out.reshape(batch, seq, d)
```

```python
import functools

import jax
import jax.numpy as jnp
from jax import lax
from jax.experimental import pallas as pl
from jax.experimental.pallas import tpu as pltpu

D_MODEL = 1024
N_HEADS = 4
D_QK = 128
D_V = 256
CONV_K = 4
ROPE_BASE = 10000.0
CHUNK = 128
N_EXPERTS = 32
TOP_K = 4
SWIGLU_LIMIT = 7.0
SWIGLU_ALPHA = 1.702
EXPERT_BLOCK = 512
EPS = 1e-5

QK_W = N_HEADS * D_QK
V_W = N_HEADS * D_V
LANES = 128
SUBLANES = 8
GATE_W = 2 * LANES
V_EXT = D_V + LANES
MIX_GROUP = 4
SLAB = D_MODEL // LANES

C_MLQ, C_MLK, C_MLV, C_MLO = 0, 512, 1024, 2048
C_RQ, C_RK, C_RV, C_RG = 3072, 3584, 4096, 5120
C_GML, C_GRET = 6144, 7168
MIX_W = 6144
PROJ_W = 8192

BF16 = jnp.bfloat16
F32 = jnp.float32
NEG_BIG = -1e30

ADA_TN = 1024
WPREP_TN = 1024
INPROJ_TM, INPROJ_TN = 1024, 2048
POST_TM = 512
ROUTE_TM = 512
DISPATCH_TM = 512
COMBINE_TM = 256
VMEM_LIMIT = 56 * 1024 * 1024


def _dot(a, b):
    return jnp.dot(a, b, preferred_element_type=F32)


def _dot_nt(a, b):
    return lax.dot_general(a, b, (((1,), (1,)), ((), ())), preferred_element_type=F32)


def _split(a):
    hi = a.astype(BF16)
    lo = (a - hi.astype(F32)).astype(BF16)
    return hi, lo


def _dot3(a, b_hi, b_lo):
    a_hi, a_lo = _split(a)
    return _dot(a_hi, b_hi) + (_dot(a_lo, b_hi) + _dot(a_hi, b_lo))


def _sigmoid_mul(x, y):
    hy = 0.5 * y
    return hy * jnp.tanh(0.5 * x) + hy


def _silu(x):
    hx = 0.5 * x
    return hx * jnp.tanh(hx) + hx


def _round_robin(stage_generators):
    pending = list(stage_generators)
    while pending:
        for gen in list(pending):
            if next(gen, True):
                pending.remove(gen)


def _rms_mod(x, g, scale, shift):
    ms = jnp.mean(x * x, axis=-1, keepdims=True)
    return (x * lax.rsqrt(ms + EPS) * g) * (1.0 + scale) + shift


def _ada_kernel(c_ref, w_ref, b_ref, o_ref):
    w_hi, w_lo = _split(w_ref[...])
    o_ref[...] = _dot3(_silu(c_ref[...]), w_hi, w_lo) + b_ref[...]


def _ada(c8, w, b):
    n = w.shape[1]
    tn = ADA_TN
    return pl.pallas_call(
        _ada_kernel,
        grid=(n // tn,),
        in_specs=[
            pl.BlockSpec((SUBLANES, D_MODEL), lambda j: (0, 0)),
            pl.BlockSpec((D_MODEL, tn), lambda j: (0, j)),
            pl.BlockSpec((1, tn), lambda j: (0, j)),
        ],
        out_specs=pl.BlockSpec((SUBLANES, tn), lambda j: (0, j)),
        out_shape=jax.ShapeDtypeStruct((SUBLANES, n), F32),
        name="ada",
    )(c8, w, b)


def _inproj_kernel(x_ref, xn_ref, mod_ref, modn_ref, g_ref, w_ref, wif_hi_ref, wif_lo_ref, pos_ref, inv_ref,
                   sign_ref, proj_ref, gates_ref, cos_ref, sin_ref, hi_a, lo_a, hi_b, lo_b):
    i = pl.program_id(0)
    j = pl.program_id(1)

    def normalise(x, mod):
        return _split(_rms_mod(x, g_ref[...], mod[0, 1:2, :], mod[0, 0:1, :]))

    @pl.when(jnp.logical_and(i == 0, j == 0))
    def _():
        hi_a[...], lo_a[...] = normalise(x_ref[...], mod_ref)

    n = x_ref.shape[0] // pl.num_programs(1)
    rows = pl.ds(pl.multiple_of(j * n, n), n)
    n_split = 4
    tc = w_ref.shape[1] // n_split

    def step(cur_hi, cur_lo, nxt_hi, nxt_lo):
        def matmul_steps():
            for c in range(n_split):
                proj_ref[:, c * tc:(c + 1) * tc] = _dot(cur_hi[...], w_ref[:, c * tc:(c + 1) * tc]).astype(BF16)
                yield False

        def side_steps():
            h_hi = cur_hi[rows, :]
            h_lo = cur_lo[rows, :]
            gates_ref[rows, :] = (_dot(h_hi, wif_hi_ref[...])
                                  + (_dot(h_lo, wif_hi_ref[...]) + _dot(h_hi, wif_lo_ref[...])))
            yield False
            ang = pos_ref[rows, :] * inv_ref[...]
            cos_ref[rows, :] = jnp.cos(ang)
            yield False
            sin_ref[rows, :] = jnp.sin(ang) * sign_ref[...]
            yield False
            nxt_hi[rows, :], nxt_lo[rows, :] = normalise(xn_ref[rows, :], modn_ref)
            yield False

        _round_robin([matmul_steps(), side_steps()])

    @pl.when(i % 2 == 0)
    def _():
        step(hi_a, lo_a, hi_b, lo_b)

    @pl.when(i % 2 == 1)
    def _():
        step(hi_b, lo_b, hi_a, lo_a)


GATE_COL = 3072


def _wprep_kernel(a_ref, b_ref, o_ref, gate_ref):
    c = pl.program_id(0)
    tn = a_ref.shape[0]

    @pl.when(c * tn == GATE_COL)
    def _():
        gate_ref[...] = a_ref[0:2 * N_HEADS, :]

    @pl.when((c + 1) * tn <= GATE_COL)
    def _():
        o_ref[...] = a_ref[...].T.astype(BF16)

    @pl.when((c + 1) * tn > GATE_COL)
    def _():
        shifted = jnp.concatenate([a_ref[2 * N_HEADS:, :], b_ref[...]], axis=0)
        o_ref[...] = shifted.T.astype(BF16)


def _wprep(w_t):
    d = w_t.shape[1]
    tn = WPREP_TN
    gate = 2 * N_HEADS
    assert GATE_COL % tn == 0 and w_t.shape[0] == PROJ_W + gate
    return pl.pallas_call(
        _wprep_kernel,
        grid=(PROJ_W // tn,),
        in_specs=[
            pl.BlockSpec((tn, d), lambda c: (c, 0)),
            pl.BlockSpec((gate, d), lambda c: ((c + 1) * (tn // gate), 0)),
        ],
        out_specs=[pl.BlockSpec((d, tn), lambda c: (0, c)), pl.BlockSpec((gate, d), lambda c: (0, 0))],
        out_shape=[jax.ShapeDtypeStruct((d, PROJ_W), BF16), jax.ShapeDtypeStruct((gate, d), F32)],
        compiler_params=pltpu.CompilerParams(
            dimension_semantics=("arbitrary",), vmem_limit_bytes=VMEM_LIMIT),
        name="wprep",
    )(w_t, w_t)


def _inproj(x2, mod6, g, w_main, wif_hi, wif_lo, pos_b, inv2, sign2, seq):
    t = x2.shape[0]
    tm, tn = INPROJ_TM, INPROJ_TN
    per_b = seq // tm
    last = t // tm - 1
    nxt = lambda i: jnp.minimum(i + 1, last)
    return pl.pallas_call(
        _inproj_kernel,
        grid=(t // tm, PROJ_W // tn),
        in_specs=[
            pl.BlockSpec((tm, D_MODEL), lambda i, j: (i, 0)),
            pl.BlockSpec((tm, D_MODEL), lambda i, j: (nxt(i), 0)),
            pl.BlockSpec((1, 6, D_MODEL), lambda i, j: (i // per_b, 0, 0)),
            pl.BlockSpec((1, 6, D_MODEL), lambda i, j: (nxt(i) // per_b, 0, 0)),
            pl.BlockSpec((1, D_MODEL), lambda i, j: (0, 0)),
            pl.BlockSpec((D_MODEL, tn), lambda i, j: (0, j)),
            pl.BlockSpec((D_MODEL, GATE_W), lambda i, j: (0, 0)),
            pl.BlockSpec((D_MODEL, GATE_W), lambda i, j: (0, 0)),
            pl.BlockSpec((tm, LANES), lambda i, j: (i, 0)),
            pl.BlockSpec((1, LANES), lambda i, j: (0, 0)),
            pl.BlockSpec((1, LANES), lambda i, j: (0, 0)),
        ],
        out_specs=[
            pl.BlockSpec((tm, tn), lambda i, j: (i, j)),
            pl.BlockSpec((tm, GATE_W), lambda i, j: (i, 0)),
            pl.BlockSpec((tm, LANES), lambda i, j: (i, 0)),
            pl.BlockSpec((tm, LANES), lambda i, j: (i, 0)),
        ],
        out_shape=[
            jax.ShapeDtypeStruct((t, PROJ_W), BF16),
            jax.ShapeDtypeStruct((t, GATE_W), F32),
            jax.ShapeDtypeStruct((t, LANES), F32),
            jax.ShapeDtypeStruct((t, LANES), F32),
        ],
        scratch_shapes=[pltpu.VMEM((tm, D_MODEL), BF16)] * 4,
        compiler_params=pltpu.CompilerParams(
            dimension_semantics=("arbitrary", "arbitrary"), vmem_limit_bytes=VMEM_LIMIT),
        name="inproj",
    )(x2, x2, mod6, mod6, g, w_main, wif_hi, wif_lo, pos_b, inv2, sign2)


def _row_mean(a):
    inv_n = jnp.full((a.shape[1], LANES), 1.0 / a.shape[1], BF16)
    m = _dot(a.astype(BF16), inv_n)
    return jnp.concatenate([m] * (a.shape[1] // LANES), axis=1)


def _head_norm(h, g):
    d = h - _row_mean(h)
    return d * lax.rsqrt(_row_mean(d * d) + EPS) * g


def _mix_kernel(proj_ref, gates_ref, cos_ref, sin_ref, convw_ref, convb_ref, bif_ref, mlg_ref, retg_ref,
                decay_ref, kdec_ref, qdec_ref, gch_ref,
                hm_ref, hr_ref,
                prev_ref, c_st, m_st, r_st):
    group = proj_ref.shape[0]

    @pl.when(pl.program_id(1) == 0)
    def _():
        def zero_state(i, carry):
            c_st[i] = jnp.zeros(c_st.shape[1:], F32)
            r_st[i] = jnp.zeros(r_st.shape[1:], F32)
            return carry

        lax.fori_loop(0, group * N_HEADS, zero_state, 0)
        prev_ref[...] = jnp.zeros_like(prev_ref)
        m_st[...] = jnp.zeros_like(m_st)

    ml_steps, ret_steps = [], []
    for gb in range(group):
        states = slice(gb * N_HEADS, (gb + 1) * N_HEADS)
        ml_steps.append(_mlstm_steps(proj_ref.at[gb], gates_ref.at[gb], convw_ref, convb_ref, bif_ref, mlg_ref,
                                     hm_ref.at[gb], prev_ref.at[gb], c_st.at[states], m_st.at[gb]))
        ret_steps.append(_retention_steps(proj_ref.at[gb], cos_ref.at[gb], sin_ref.at[gb], retg_ref, decay_ref,
                                          kdec_ref, qdec_ref, gch_ref, hr_ref.at[gb], r_st.at[states]))
    def delayed(gen, rounds):
        for _ in range(rounds):
            yield False
        yield from gen

    _round_robin(ml_steps + [delayed(g, 4) for g in ret_steps])


def _mlstm_steps(proj_ref, gates_ref, convw_ref, convb_ref, bif_ref, mlg_ref, hm_ref, prev_ref, c_st, m_st):
    L = CHUNK
    rows = lax.broadcasted_iota(jnp.int32, (L, L), 0)
    cols = lax.broadcasted_iota(jnp.int32, (L, L), 1)
    causal = rows >= cols
    tril = jnp.where(causal, 1.0, 0.0).astype(BF16)

    cur = proj_ref[:, C_MLQ:C_MLQ + 2 * QK_W]
    xx = jnp.concatenate([prev_ref[...], cur], axis=0)
    r2 = lax.broadcasted_iota(jnp.int32, (L, 2 * L), 0)
    c2 = lax.broadcasted_iota(jnp.int32, (L, 2 * L), 1)
    acc = convb_ref[...] + cur.astype(F32) * convw_ref[CONV_K - 1:CONV_K, :]
    for d in range(1, CONV_K):
        shift = jnp.where(c2 == r2 + (L - d), 1.0, 0.0).astype(BF16)
        acc = acc + _dot(shift, xx) * convw_ref[CONV_K - 1 - d:CONV_K - d, :]
        yield False
    prev_ref[...] = cur
    qk = _silu(acc)
    yield False

    g = gates_ref[...] + bif_ref[...]
    gi = g[:, :LANES]
    gf = g[:, LANES:]
    lf = jnp.minimum(gf, 0.0) - jnp.log(1.0 + jnp.exp(-jnp.abs(gf)))
    lf_hi, lf_lo = _split(lf)
    a_all = _dot(tril, lf_hi) + _dot(tril, lf_lo)
    yield False
    a_last = a_all[L - 1:L, :]
    bm = gi - a_all
    bm_t = bm.T
    w_state = a_last + bm
    m_loc = jnp.max(w_state, axis=0, keepdims=True)
    m_prev = m_st[...]
    inter_log = a_all + m_prev
    m_new = jnp.maximum(a_last + m_prev, m_loc)
    s_prev = jnp.exp(a_last + m_prev - m_new)
    s_loc = jnp.exp(m_loc - m_new)
    ws_all = jnp.exp(w_state - m_loc) * s_loc
    m_st[...] = m_new
    yield False

    ones_blk = jnp.ones((L, LANES), BF16)
    q_scale = D_QK ** -0.5

    for h in range(N_HEADS):
        q = (qk[:, h * D_QK:(h + 1) * D_QK] * q_scale).astype(BF16)
        k_f = qk[:, QK_W + h * D_QK:QK_W + (h + 1) * D_QK]
        k = k_f.astype(BF16)
        v_ext = jnp.concatenate([proj_ref[:, C_MLV + h * D_V:C_MLV + (h + 1) * D_V], ones_blk], axis=1)
        dlog = jnp.where(causal, a_all[:, h:h + 1] + bm_t[h:h + 1, :], -jnp.inf)
        m_intra = jnp.max(dlog, axis=-1, keepdims=True)
        s = _dot_nt(q, k)
        qc = _dot(q, c_st[h].astype(BF16))
        yield False
        il = inter_log[:, h:h + 1]
        m_t = jnp.maximum(il, m_intra)
        p = jnp.exp(dlog - m_t) * s
        isc = jnp.exp(il - m_t)
        tot = _dot(p.astype(BF16), v_ext) + isc * qc
        yield False
        den = tot[:, D_V:D_V + 1]
        hout = tot[:, :D_V] / jnp.maximum(jnp.abs(den), jnp.exp(-m_t))
        kw_t = (k_f * ws_all[:, h:h + 1]).T.astype(BF16)
        c_st[h] = s_prev[:, h:h + 1] * c_st[h] + _dot(kw_t, v_ext)
        yield False
        y = _head_norm(hout, mlg_ref[:, h * D_V:(h + 1) * D_V])
        o = proj_ref[:, C_MLO + h * D_V:C_MLO + (h + 1) * D_V].astype(F32)
        hm_ref[:, h * D_V:(h + 1) * D_V] = _sigmoid_mul(o, y).astype(BF16)
        yield False


def _retention_steps(proj_ref, cos_ref, sin_ref, retg_ref, decay_ref, kdec_ref, qdec_ref, gch_ref, hr_ref, r_st):
    cos2 = cos_ref[...]
    sin2 = sin_ref[...]
    k_scale = D_QK ** -0.5
    for h in range(N_HEADS):
        q_raw = proj_ref[:, C_RQ + h * D_QK:C_RQ + (h + 1) * D_QK].astype(F32)
        k_raw = proj_ref[:, C_RK + h * D_QK:C_RK + (h + 1) * D_QK].astype(F32)
        q = (q_raw * cos2 + pltpu.roll(q_raw, D_QK // 2, 1) * sin2).astype(BF16)
        k_f = (k_raw * cos2 + pltpu.roll(k_raw, D_QK // 2, 1) * sin2) * k_scale
        v = proj_ref[:, C_RV + h * D_V:C_RV + (h + 1) * D_V]
        yield False
        sc = _dot_nt(q, k_f.astype(BF16)) * decay_ref[h]
        hret = _dot(sc.astype(BF16), v) + _dot(q, r_st[h].astype(BF16)) * qdec_ref[h]
        yield False
        kd_t = (k_f * kdec_ref[h]).T.astype(BF16)
        r_st[h] = gch_ref[h][:, 0:1] * r_st[h] + _dot(kd_t, v)
        yield False
        y = _head_norm(hret, retg_ref[:, h * D_V:(h + 1) * D_V])
        gt = proj_ref[:, C_RG + h * D_V:C_RG + (h + 1) * D_V].astype(F32)
        hr_ref[:, h * D_V:(h + 1) * D_V] = (_silu(gt) * y).astype(BF16)
        yield False


def _mix(proj, gates, cos2, sin2, conv_w, conv_b, b_if2, ml_g, ret_g, decay, kdec, qdec, gch, batch, seq):
    t = proj.shape[0]
    nc = seq // CHUNK
    L = CHUNK
    G = MIX_GROUP
    proj = proj.reshape(batch, seq, PROJ_W)
    gates = gates.reshape(batch, seq, GATE_W)
    cos2 = cos2.reshape(batch, seq, LANES)
    sin2 = sin2.reshape(batch, seq, LANES)
    full = lambda shape: pl.BlockSpec(shape, lambda b, c: (0,) * len(shape))
    hm, hr = pl.pallas_call(
        _mix_kernel,
        grid=(batch // G, nc),
        in_specs=[
            pl.BlockSpec((G, L, MIX_W), lambda b, c: (b, c, 0)),
            pl.BlockSpec((G, L, GATE_W), lambda b, c: (b, c, 0)),
            pl.BlockSpec((G, L, LANES), lambda b, c: (b, c, 0)),
            pl.BlockSpec((G, L, LANES), lambda b, c: (b, c, 0)),
            full((CONV_K, 2 * QK_W)),
            full((1, 2 * QK_W)),
            full((1, GATE_W)),
            full((1, V_W)),
            full((1, V_W)),
            full((N_HEADS, L, L)),
            full((N_HEADS, L, 1)),
            full((N_HEADS, L, 1)),
            full((N_HEADS, 1, LANES)),
        ],
        out_specs=[
            pl.BlockSpec((G, L, V_W), lambda b, c: (b, c, 0)),
            pl.BlockSpec((G, L, V_W), lambda b, c: (b, c, 0)),
        ],
        out_shape=[jax.ShapeDtypeStruct((batch, seq, V_W), BF16), jax.ShapeDtypeStruct((batch, seq, V_W), BF16)],
        scratch_shapes=[
            pltpu.VMEM((G, L, 2 * QK_W), BF16),
            pltpu.VMEM((G * N_HEADS, D_QK, V_EXT), F32),
            pltpu.VMEM((G, 1, LANES), F32),
            pltpu.VMEM((G * N_HEADS, D_QK, D_V), F32),
        ],
        compiler_params=pltpu.CompilerParams(
            dimension_semantics=("arbitrary", "arbitrary"), vmem_limit_bytes=VMEM_LIMIT),
        name="mix",
    )(proj, gates, cos2, sin2, conv_w, conv_b, b_if2, ml_g, ret_g, decay, kdec, qdec, gch)
    return hm.reshape(t, V_W), hr.reshape(t, V_W)


def _to_slabs(ref, val):
    rows = val.shape[0]
    for s in range(SLAB):
        ref[pl.ds(s, rows, stride=SLAB), :] = val[:, s * LANES:(s + 1) * LANES]


def _from_slabs(ref, first_slab, rows):
    return jnp.concatenate(
        [ref[pl.ds(first_slab * SLAB + s, rows, stride=SLAB), :] for s in range(SLAB)], axis=1)


def _post_kernel(hm_ref, hr_ref, gm_ref, gr_ref, x_ref, mod_ref, wbm_ref, wbr_ref, wout_ref, g_ref,
                 wr_hi_ref, wr_lo_ref, br_ref, x1_ref, hf_ref, logit_ref):
    n_sub = 2
    sub = x_ref.shape[0] // n_sub

    def steps(s):
        r = pl.ds(s * sub, sub)
        ym = _dot(hm_ref[r, :], wbm_ref[...])
        yield False
        yr = _dot(hr_ref[r, :], wbr_ref[...])
        yield False
        y = _sigmoid_mul(gm_ref[r, :].astype(F32), ym) + _sigmoid_mul(gr_ref[r, :].astype(F32), yr)
        o = _dot(y.astype(BF16), wout_ref[...])
        yield False
        x1 = x_ref[r, :] + mod_ref[0, 2:3, :] * o
        x1_ref[r, :] = x1
        hf = _rms_mod(x1, g_ref[...], mod_ref[0, 4:5, :], mod_ref[0, 3:4, :])
        yield False
        _to_slabs(hf_ref.at[pl.ds(s * sub * SLAB, sub * SLAB)], hf)
        logit_ref[r, :] = _dot3(hf, wr_hi_ref[...], wr_lo_ref[...]) + br_ref[...]
        yield False

    _round_robin([steps(s) for s in range(n_sub)])


def _post(hm, hr, proj, x2, mod6, wbm, wbr, wout, g, wr_hi, wr_lo, br, seq):
    t = x2.shape[0]
    tm = POST_TM
    per_b = seq // tm
    d = D_MODEL
    const = lambda shape: pl.BlockSpec(shape, lambda i: (0,) * len(shape))
    return pl.pallas_call(
        _post_kernel,
        grid=(t // tm,),
        in_specs=[
            pl.BlockSpec((tm, d), lambda i: (i, 0)),
            pl.BlockSpec((tm, d), lambda i: (i, 0)),
            pl.BlockSpec((tm, d), lambda i: (i, C_GML // d)),
            pl.BlockSpec((tm, d), lambda i: (i, C_GRET // d)),
            pl.BlockSpec((tm, d), lambda i: (i, 0)),
            pl.BlockSpec((1, 6, d), lambda i: (i // per_b, 0, 0)),
            const((d, d)), const((d, d)), const((d, d)),
            const((1, d)),
            const((d, LANES)), const((d, LANES)), const((1, LANES)),
        ],
        out_specs=[
            pl.BlockSpec((tm, d), lambda i: (i, 0)),
            pl.BlockSpec((tm * SLAB, LANES), lambda i: (i, 0)),
            pl.BlockSpec((tm, LANES), lambda i: (i, 0)),
        ],
        out_shape=[
            jax.ShapeDtypeStruct((t, d), F32),
            jax.ShapeDtypeStruct((t * SLAB, LANES), F32),
            jax.ShapeDtypeStruct((t, LANES), F32),
        ],
        compiler_params=pltpu.CompilerParams(
            dimension_semantics=("arbitrary",), vmem_limit_bytes=VMEM_LIMIT),
        name="post",
    )(hm, hr, proj, proj, x2, mod6, wbm, wbr, wout, g, wr_hi, wr_lo, br)


def _route_kernel(logit_ref, dest_ref, w_ref, bend_ref, cnt_st, pad_st, sel_st, pick_st, wts_st):
    ph = pl.program_id(0)
    i = pl.program_id(1)
    tm = logit_ref.shape[0]
    rows = pl.ds(pl.multiple_of(i * tm, tm), tm)
    lane = lax.broadcasted_iota(jnp.int32, (tm, LANES), 1)
    lane_f = lane.astype(F32)

    @pl.when(jnp.logical_and(ph == 0, i == 0))
    def _():
        cnt_st[...] = jnp.zeros_like(cnt_st)

    @pl.when(ph == 0)
    def _():
        l = logit_ref[...]
        picks, vals = [], []
        sel = jnp.zeros((tm, LANES), F32)
        for _ in range(TOP_K):
            m = jnp.max(l, axis=-1, keepdims=True)
            idx = jnp.min(jnp.where(l == m, lane_f, float(LANES)), axis=-1, keepdims=True)
            oh = lane_f == idx
            picks.append(idx)
            vals.append(m)
            sel = jnp.where(oh, 1.0, sel)
            l = jnp.where(oh, -jnp.inf, l)
        ex = [jnp.exp(v - vals[0]) for v in vals]
        den = ex[0] + ex[1] + ex[2] + ex[3]
        pick = jnp.zeros((tm, LANES), F32)
        wts = jnp.zeros((tm, LANES), F32)
        for k in range(TOP_K):
            pick = jnp.where(lane == k, picks[k], pick)
            wts = jnp.where(lane == k, ex[k] / den, wts)
        sel_st[rows, :] = sel.astype(BF16)
        pick_st[rows, :] = pick
        wts_st[rows, :] = wts
        cnt_st[...] = cnt_st[...] + jnp.sum(sel, axis=0, keepdims=True)

    @pl.when(jnp.logical_and(ph == 1, i == 0))
    def _():
        blocks = jnp.floor((cnt_st[...] + (EXPERT_BLOCK - 1)) * (1.0 / EXPERT_BLOCK))
        r = lax.broadcasted_iota(jnp.int32, (LANES, LANES), 0)
        c = lax.broadcasted_iota(jnp.int32, (LANES, LANES), 1)
        upper = jnp.where(r < c, 1.0, 0.0).astype(BF16)
        blocks8 = jnp.broadcast_to(blocks, (SUBLANES, LANES))
        excl = _dot(blocks8.astype(BF16), upper)
        pad_st[...] = excl[0:1, :] * EXPERT_BLOCK
        row = lax.broadcasted_iota(jnp.int32, (SUBLANES, LANES), 0)
        bend_ref[...] = jnp.where(row == 0, excl + blocks8,
                                  jnp.where(row == 1, excl * EXPERT_BLOCK + cnt_st[...], 0.0))
        cnt_st[...] = jnp.zeros_like(cnt_st)

    @pl.when(ph == 1)
    def _():
        carry = cnt_st[...]
        sel = sel_st[rows, :]
        pick = pick_st[rows, :]
        r = lax.broadcasted_iota(jnp.int32, (tm, tm), 0)
        c = lax.broadcasted_iota(jnp.int32, (tm, tm), 1)
        lower = jnp.where(r > c, 1.0, 0.0).astype(BF16)
        base = pad_st[...] + carry + _dot(lower, sel)
        dest = jnp.zeros((tm, LANES), F32)
        for k in range(TOP_K):
            dk = jnp.sum(jnp.where(lane_f == pick[:, k:k + 1], base, 0.0), axis=-1, keepdims=True)
            dest = jnp.where(lane == k, dk, dest)
        dest_ref[...] = dest.astype(jnp.int32)
        w_ref[...] = wts_st[rows, :]
        cnt_st[...] = carry + jnp.sum(sel.astype(F32), axis=0, keepdims=True)


def _route(logits):
    t = logits.shape[0]
    tm = ROUTE_TM
    return pl.pallas_call(
        _route_kernel,
        grid=(2, t // tm),
        in_specs=[pl.BlockSpec((tm, LANES), lambda ph, i: (i * (1 - ph), 0))],
        out_specs=[
            pl.BlockSpec((tm, LANES), lambda ph, i: (i * ph, 0)),
            pl.BlockSpec((tm, LANES), lambda ph, i: (i * ph, 0)),
            pl.BlockSpec((SUBLANES, LANES), lambda ph, i: (0, 0)),
        ],
        out_shape=[
            jax.ShapeDtypeStruct((t, LANES), jnp.int32),
            jax.ShapeDtypeStruct((t, LANES), F32),
            jax.ShapeDtypeStruct((SUBLANES, LANES), F32),
        ],
        scratch_shapes=[
            pltpu.VMEM((1, LANES), F32),
            pltpu.VMEM((1, LANES), F32),
            pltpu.VMEM((t, LANES), BF16),
            pltpu.VMEM((t, LANES), F32),
            pltpu.VMEM((t, LANES), F32),
        ],
        compiler_params=pltpu.CompilerParams(
            dimension_semantics=("arbitrary", "arbitrary"), vmem_limit_bytes=VMEM_LIMIT),
        name="route",
    )(logits)


DMA_UNROLL = 8


def _row_copy(src_ref, src_slab, dst_ref, dst_slab, sem):
    src = pl.multiple_of(src_slab * SLAB, SLAB)
    dst = pl.multiple_of(dst_slab * SLAB, SLAB)
    return pltpu.make_async_copy(src_ref.at[pl.ds(src, SLAB), :], dst_ref.at[pl.ds(dst, SLAB), :], sem)


def _dispatch_kernel(dest_ref, padfirst_ref, hf_hbm, xs_out, tiles, zbuf, lsem, rsem, zsem):
    i = pl.program_id(0)
    n_steps = pl.num_programs(0)
    n_assign = dest_ref.shape[0]
    tile_rows = tiles.shape[1]
    blk_rows = zbuf.shape[0]
    n_buf = tiles.shape[0]

    def tile_load(step, slot):
        start = pl.multiple_of(step * tile_rows, tile_rows)
        return pltpu.make_async_copy(hf_hbm.at[pl.ds(start, tile_rows), :], tiles.at[slot], lsem.at[slot])

    def wait_rows(slot):
        for _ in range(TOP_K):
            pltpu.make_async_copy(tiles.at[slot], xs_out.at[pl.ds(0, tile_rows), :], rsem.at[slot]).wait()

    @pl.when(i == 0)
    def _():
        tile_load(0, 0).start()
        tile_load(1, 1).start()
        zbuf[...] = jnp.zeros_like(zbuf)
        for e in range(N_EXPERTS):
            start = pl.multiple_of(padfirst_ref[e] * SLAB, SLAB)
            pltpu.make_async_copy(zbuf, xs_out.at[pl.ds(start, blk_rows), :], zsem).start()
        for e in range(N_EXPERTS):
            pltpu.make_async_copy(zbuf, xs_out.at[pl.ds(0, blk_rows), :], zsem).wait()

        n_used = padfirst_ref[N_EXPERTS]
        n_total = xs_out.shape[0] // blk_rows

        def zero_start(b, carry):
            start = pl.multiple_of(b * blk_rows, blk_rows)
            pltpu.make_async_copy(zbuf, xs_out.at[pl.ds(start, blk_rows), :], zsem).start()
            return carry

        def zero_wait(b, carry):
            pltpu.make_async_copy(zbuf, xs_out.at[pl.ds(0, blk_rows), :], zsem).wait()
            return carry

        lax.fori_loop(n_used, n_total, zero_start, 0)
        lax.fori_loop(n_used, n_total, zero_wait, 0)

    slot = i % n_buf
    tile_load(i, slot).wait()
    tile = tiles.at[slot]

    def body(g, carry):
        for u in range(DMA_UNROLL):
            t_local = g * (DMA_UNROLL // TOP_K) + u // TOP_K
            _row_copy(tile, t_local, xs_out, dest_ref[g * DMA_UNROLL + u], rsem.at[slot]).start(priority=u % 2)
        return carry

    lax.fori_loop(0, n_assign // DMA_UNROLL, body, 0)

    @pl.when(i >= 1)
    def _():
        wait_rows((i + n_buf - 1) % n_buf)

    @pl.when(i + 2 < n_steps)
    def _():
        tile_load(i + 2, (i + 2) % n_buf).start()

    @pl.when(i == n_steps - 1)
    def _():
        wait_rows(slot)


def _dispatch(dest_flat, pad_first, hf_slab, n_blocks):
    n_assign = dest_flat.shape[0]
    tm = DISPATCH_TM
    blk_rows = EXPERT_BLOCK * SLAB
    return pl.pallas_call(
        _dispatch_kernel,
        grid=(n_assign // (tm * TOP_K),),
        in_specs=[
            pl.BlockSpec((tm * TOP_K,), lambda i: (i,), memory_space=pltpu.SMEM),
            pl.BlockSpec(memory_space=pltpu.SMEM),
            pl.BlockSpec(memory_space=pl.ANY),
        ],
        out_specs=pl.BlockSpec(memory_space=pl.ANY),
        out_shape=jax.ShapeDtypeStruct(((n_blocks + 1) * blk_rows, LANES), F32),
        scratch_shapes=[
            pltpu.VMEM((3, tm * SLAB, LANES), F32),
            pltpu.VMEM((blk_rows, LANES), F32),
            pltpu.SemaphoreType.DMA((3,)),
            pltpu.SemaphoreType.DMA((3,)),
            pltpu.SemaphoreType.DMA,
        ],
        compiler_params=pltpu.CompilerParams(dimension_semantics=("arbitrary",), vmem_limit_bytes=VMEM_LIMIT),
        name="dispatch",
    )(dest_flat, pad_first, hf_slab)


def _weight_copies(wgu_hbm, wd_hbm, wgu_f32, wd_f32, wsem, expert, slot):
    return (pltpu.make_async_copy(wgu_hbm.at[expert], wgu_f32.at[slot], wsem.at[0, slot]),
            pltpu.make_async_copy(wd_hbm.at[expert], wd_f32.at[slot], wsem.at[1, slot]))


def _expert_kernel(be_ref, nused_ref, first_ref, slot_ref, next_ref,
                   xs_ref, wgu_hbm, bgu_ref, wd_hbm, bd_ref, ys_ref,
                   wgu_f32, wd_f32, wgu_bf, wd_bf, wsem):
    b = pl.program_id(0)
    copies = functools.partial(_weight_copies, wgu_hbm, wd_hbm, wgu_f32, wd_f32, wsem)

    @pl.when(jnp.logical_and(b < nused_ref[0], first_ref[b] == 1))
    def _():
        slot = slot_ref[b]

        @pl.when(b == 0)
        def _():
            for cp in copies(be_ref[0], 0):
                cp.start()

        for cp in copies(be_ref[b], slot):
            cp.wait()

        @pl.when(next_ref[b] >= 0)
        def _():
            for cp in copies(next_ref[b], 1 - slot):
                cp.start()

        two = 2 * LANES
        r = lax.broadcasted_iota(jnp.int32, (two, two), 0)
        c = lax.broadcasted_iota(jnp.int32, (two, two), 1)
        src = jnp.where(c < LANES, 2 * c, 2 * (c - LANES) + 1)
        perm = jnp.where(r == src, 1.0, 0.0).astype(BF16)
        for blk in range(wgu_bf.shape[1] // two):
            wb = wgu_f32[slot, :, blk * two:(blk + 1) * two].astype(BF16)
            wgu_bf[:, blk * two:(blk + 1) * two] = _dot(wb, perm).astype(BF16)
        wd_bf[...] = wd_f32[slot].astype(BF16)

    @pl.when(b < nused_ref[0])
    def _():
        two = 2 * LANES
        x = _from_slabs(xs_ref, 0, EXPERT_BLOCK).astype(BF16)
        gu = _dot(x, wgu_bf[...]) + bgu_ref[0]
        n_blk = gu.shape[1] // two
        x_glu = jnp.concatenate([gu[:, i * two:i * two + LANES] for i in range(n_blk)], axis=1)
        x_lin = jnp.concatenate([gu[:, i * two + LANES:(i + 1) * two] for i in range(n_blk)], axis=1)
        x_glu = jnp.minimum(x_glu, SWIGLU_LIMIT)
        x_lin = jnp.clip(x_lin, -SWIGLU_LIMIT, SWIGLU_LIMIT)
        act = _sigmoid_mul(SWIGLU_ALPHA * x_glu, x_glu * (x_lin + 1.0))
        y = _dot(act.astype(BF16), wd_bf[...]) + bd_ref[0]
        _to_slabs(ys_ref, y)

    @pl.when(b >= nused_ref[0])
    def _():
        ys_ref[...] = jnp.zeros_like(ys_ref)


def _experts(block_expert, n_used, xs, w_gate_up, b_gu_perm, w_down, b_down):
    n_blocks = block_expert.shape[0]
    d = D_MODEL
    f2 = w_gate_up.shape[-1]
    blk_rows = EXPERT_BLOCK * SLAB

    idx = jnp.arange(n_blocks, dtype=jnp.int32)
    used = idx < n_used
    prev = jnp.concatenate([jnp.full((1,), -1, jnp.int32), block_expert[:-1]])
    first = (used & (block_expert != prev)).astype(jnp.int32)
    slot = (jnp.cumsum(first) - 1) % 2
    later = used[None, :] & (block_expert[None, :] > block_expert[:, None])
    nxt = jnp.min(jnp.where(later, block_expert[None, :], N_EXPERTS), axis=1)
    nxt = jnp.where(nxt == N_EXPERTS, -1, nxt)

    grid_spec = pltpu.PrefetchScalarGridSpec(
        num_scalar_prefetch=5,
        grid=(n_blocks,),
        in_specs=[
            pl.BlockSpec((blk_rows, LANES), lambda b, be, *_: (b, 0)),
            pl.BlockSpec(memory_space=pl.ANY),
            pl.BlockSpec((1, 1, f2), lambda b, be, *_: (be[b], 0, 0)),
            pl.BlockSpec(memory_space=pl.ANY),
            pl.BlockSpec((1, 1, d), lambda b, be, *_: (be[b], 0, 0)),
        ],
        out_specs=pl.BlockSpec((blk_rows, LANES), lambda b, be, *_: (b, 0)),
        scratch_shapes=[
            pltpu.VMEM((2, d, f2), F32),
            pltpu.VMEM((2, f2 // 2, d), F32),
            pltpu.VMEM((d, f2), BF16),
            pltpu.VMEM((f2 // 2, d), BF16),
            pltpu.SemaphoreType.DMA((2, 2)),
        ],
    )
    return pl.pallas_call(
        _expert_kernel,
        grid_spec=grid_spec,
        out_shape=jax.ShapeDtypeStruct((n_blocks * blk_rows, LANES), F32),
        compiler_params=pltpu.CompilerParams(
            dimension_semantics=("arbitrary",), vmem_limit_bytes=VMEM_LIMIT),
        name="expert",
    )(block_expert, n_used, first, slot.astype(jnp.int32), nxt.astype(jnp.int32),
      xs, w_gate_up, b_gu_perm, w_down, b_down)


def _combine_kernel(dest_ref, dest_next_ref, ys_hbm, w_ref, x1_ref, mod_ref, g_ref, o_ref, buf, sem):
    i = pl.program_id(0)
    n_steps = pl.num_programs(0)
    tm = x1_ref.shape[0]
    n_assign = tm * TOP_K
    slot = i % 2

    def gather(d_ref, to_slot):
        def body(g, carry):
            for u in range(DMA_UNROLL):
                t_local = g * (DMA_UNROLL // TOP_K) + u // TOP_K
                k = u % TOP_K
                _row_copy(ys_hbm, d_ref[g * DMA_UNROLL + u], buf.at[to_slot], k * tm + t_local,
                          sem.at[to_slot]).start(priority=u % 2)
            return carry

        lax.fori_loop(0, n_assign // DMA_UNROLL, body, 0)

    @pl.when(i == 0)
    def _():
        gather(dest_ref, 0)

    @pl.when(i + 1 < n_steps)
    def _():
        gather(dest_next_ref, 1 - slot)

    cur = buf.at[slot]
    pltpu.make_async_copy(ys_hbm.at[pl.ds(0, n_assign * SLAB), :], cur, sem.at[slot]).wait()

    w = w_ref[...]
    y = w[:, 0:1] * _from_slabs(cur, 0, tm)
    for k in range(1, TOP_K):
        y = y + w[:, k:k + 1] * _from_slabs(cur, k * tm, tm)
    x = x1_ref[...] + mod_ref[0, 5:6, :] * y
    ms = jnp.mean(x * x, axis=-1, keepdims=True)
    o_ref[...] = x * lax.rsqrt(ms + EPS) * g_ref[...]


def _combine(dest_flat, ys, wts, x1, mod6, g, seq):
    t = x1.shape[0]
    tm = COMBINE_TM
    per_b = seq // tm
    d = D_MODEL
    last = t // tm - 1
    return pl.pallas_call(
        _combine_kernel,
        grid=(t // tm,),
        in_specs=[
            pl.BlockSpec((tm * TOP_K,), lambda i: (i,), memory_space=pltpu.SMEM),
            pl.BlockSpec((tm * TOP_K,), lambda i: (jnp.minimum(i + 1, last),), memory_space=pltpu.SMEM),
            pl.BlockSpec(memory_space=pl.ANY),
            pl.BlockSpec((tm, LANES), lambda i: (i, 0)),
            pl.BlockSpec((tm, d), lambda i: (i, 0)),
            pl.BlockSpec((1, 6, d), lambda i: (i // per_b, 0, 0)),
            pl.BlockSpec((1, d), lambda i: (0, 0)),
        ],
        out_specs=pl.BlockSpec((tm, d), lambda i: (i, 0)),
        out_shape=jax.ShapeDtypeStruct((t, d), F32),
        scratch_shapes=[pltpu.VMEM((2, tm * TOP_K * SLAB, LANES), F32), pltpu.SemaphoreType.DMA((2,))],
        compiler_params=pltpu.CompilerParams(
            dimension_semantics=("arbitrary",), vmem_limit_bytes=VMEM_LIMIT),
        name="combine",
    )(dest_flat, dest_flat, ys, wts, x1, mod6, g)


def kernel(x, c, positions, w_ada, b_ada, norm_mix_g, w_in, conv_w, conv_b, b_if, ml_norm_g, ret_norm_g,
           w_branch_ml, w_branch_ret, w_out, norm_ffn_g, w_router, b_router, w_gate_up, b_gate_up, w_down,
           b_down, norm_final_g):
    batch, seq, d = x.shape
    t = batch * seq
    x2 = x.reshape(t, d)
    depth = w_ada.shape[0]
    c8 = jnp.concatenate([c, jnp.zeros((SUBLANES - batch, d), c.dtype)], axis=0)

    half = D_QK // 2
    inv = ROPE_BASE ** (-jnp.arange(half, dtype=F32) / half)
    inv2 = jnp.concatenate([inv, inv])[None, :]
    sign2 = jnp.concatenate([-jnp.ones((half,), F32), jnp.ones((half,), F32)])[None, :]
    pos_b = jnp.broadcast_to(positions.astype(F32).reshape(t, 1), (t, LANES))
    L = CHUNK
    log_gamma = jnp.log(1.0 - 2.0 ** (-5.0 - jnp.arange(N_HEADS, dtype=F32)))
    pos = jnp.arange(L, dtype=F32)
    rel = pos[:, None] - pos[None, :]
    decay = jnp.where(rel >= 0, jnp.exp(log_gamma[:, None, None] * jnp.maximum(rel, 0.0)), 0.0)
    kdec = jnp.exp(log_gamma[:, None] * (L - 1 - pos))[:, :, None]
    qdec = jnp.exp(log_gamma[:, None] * (pos + 1.0))[:, :, None]
    gch = jnp.broadcast_to(jnp.exp(log_gamma * L)[:, None, None], (N_HEADS, 1, LANES))

    assert depth == 1, "the final norm is fused after the single layer"
    l = 0
    mod = _ada(c8, w_ada[l], b_ada[l][None, :])
    mod6 = mod[:batch].reshape(batch, 6, d)

    w = w_in[l]
    w_main, w_gate = _wprep(w.T)
    w_if = w_gate.T
    zpad = jnp.zeros((d, LANES - N_HEADS), F32)
    wif = jnp.concatenate([w_if[:, :N_HEADS], zpad, w_if[:, N_HEADS:], zpad], axis=1)
    wif_hi, wif_lo = _split(wif)
    zb = jnp.zeros((LANES - N_HEADS,), F32)
    b_if2 = jnp.concatenate([b_if[l][:N_HEADS], zb, b_if[l][N_HEADS:], zb])[None, :]

    proj, gates, cos2, sin2 = _inproj(x2, mod6, norm_mix_g[l][None, :], w_main, wif_hi, wif_lo,
                                      pos_b, inv2, sign2, seq)
    hm, hr = _mix(proj, gates, cos2, sin2, conv_w[l], conv_b[l][None, :], b_if2,
                  ml_norm_g[l][None, :], ret_norm_g[l][None, :], decay, kdec, qdec, gch, batch, seq)

    wr = jnp.concatenate([w_router[l], jnp.zeros((d, LANES - N_EXPERTS), F32)], axis=1)
    wr_hi, wr_lo = _split(wr)
    br = jnp.concatenate([b_router[l], jnp.full((LANES - N_EXPERTS,), NEG_BIG, F32)])[None, :]
    x1, hf, logits = _post(hm, hr, proj, x2, mod6, w_branch_ml[l].astype(BF16), w_branch_ret[l].astype(BF16),
                           w_out[l].astype(BF16), norm_ffn_g[l][None, :], wr_hi, wr_lo, br, seq)

    n_assign = t * TOP_K
    n_blocks = -(-n_assign // EXPERT_BLOCK) + N_EXPERTS
    dest, wts, tables = _route(logits)
    dest_flat = dest[:, :TOP_K].reshape(n_assign)
    bend = tables[0, :N_EXPERTS].astype(jnp.int32)
    pad_first = tables[1, :N_EXPERTS].astype(jnp.int32)
    n_used = bend[N_EXPERTS - 1:]
    blk = jnp.minimum(jnp.arange(n_blocks, dtype=jnp.int32), n_used - 1)
    block_expert = jnp.minimum(
        jnp.sum((bend[None, :] <= blk[:, None]).astype(jnp.int32), axis=1), N_EXPERTS - 1)
    xs = _dispatch(dest_flat, jnp.concatenate([pad_first, n_used]), hf, n_blocks)
    f2 = w_gate_up.shape[-1]
    bgu = b_gate_up[l].reshape(N_EXPERTS, f2 // (2 * LANES), LANES, 2)
    bgu = jnp.swapaxes(bgu, -1, -2).reshape(N_EXPERTS, 1, f2)
    ys = _experts(block_expert, n_used, xs, w_gate_up[l], bgu, w_down[l], b_down[l][:, None, :])
    out = _combine(dest_flat, ys, wts, x1, mod6, norm_final_g[None, :], seq)
    return out.reshape(batch, seq, d)
```

```python
import functools

import jax
import jax.numpy as jnp
from jax import lax
from jax.experimental import pallas as pl
from jax.experimental.pallas import tpu as pltpu

D_MODEL = 1024
N_HEADS = 4
D_QK = 128
D_V = 256
CONV_K = 4
ROPE_BASE = 10000.0
CHUNK = 128
N_EXPERTS = 32
TOP_K = 4
SWIGLU_LIMIT = 7.0
SWIGLU_ALPHA = 1.702
EXPERT_BLOCK = 512
EPS = 1e-5

QK_W = N_HEADS * D_QK
V_W = N_HEADS * D_V
LANES = 128
SUBLANES = 8
GATE_W = 2 * LANES
V_EXT = D_V + LANES
MIX_GROUP = 4
SLAB = D_MODEL // LANES

C_MLQ, C_MLK, C_MLV, C_MLO = 0, 512, 1024, 2048
C_RQ, C_RK, C_RV, C_RG = 3072, 3584, 4096, 5120
C_GML, C_GRET = 6144, 7168
MIX_W = 6144
PROJ_W = 8192

BF16 = jnp.bfloat16
F32 = jnp.float32
NEG_BIG = -1e30

ADA_TN = 1024
WPREP_TN = 1024
INPROJ_TM, INPROJ_TN = 1024, 2048
POST_TM = 512
ROUTE_TM = 512
DISPATCH_TM = 512
COMBINE_TM = 256
VMEM_LIMIT = 56 * 1024 * 1024


def _dot(a, b):
    return jnp.dot(a, b, preferred_element_type=F32)


def _dot_nt(a, b):
    return lax.dot_general(a, b, (((1,), (1,)), ((), ())), preferred_element_type=F32)


def _split(a):
    hi = a.astype(BF16)
    lo = (a - hi.astype(F32)).astype(BF16)
    return hi, lo


def _dot3(a, b_hi, b_lo):
    a_hi, a_lo = _split(a)
    return _dot(a_hi, b_hi) + (_dot(a_lo, b_hi) + _dot(a_hi, b_lo))


def _sigmoid_mul(x, y):
    hy = 0.5 * y
    return hy * jnp.tanh(0.5 * x) + hy


def _silu(x):
    hx = 0.5 * x
    return hx * jnp.tanh(hx) + hx


def _round_robin(stage_generators):
    pending = list(stage_generators)
    while pending:
        for gen in list(pending):
            if next(gen, True):
                pending.remove(gen)


def _rms_mod(x, g, scale, shift):
    ms = jnp.mean(x * x, axis=-1, keepdims=True)
    return (x * lax.rsqrt(ms + EPS) * g) * (1.0 + scale) + shift


def _ada_kernel(c_ref, w_ref, b_ref, o_ref):
    w_hi, w_lo = _split(w_ref[...])
    o_ref[...] = _dot3(_silu(c_ref[...]), w_hi, w_lo) + b_ref[...]


def _ada(c8, w, b):
    n = w.shape[1]
    tn = ADA_TN
    return pl.pallas_call(
        _ada_kernel,
        grid=(n // tn,),
        in_specs=[
            pl.BlockSpec((SUBLANES, D_MODEL), lambda j: (0, 0)),
            pl.BlockSpec((D_MODEL, tn), lambda j: (0, j)),
            pl.BlockSpec((1, tn), lambda j: (0, j)),
        ],
        out_specs=pl.BlockSpec((SUBLANES, tn), lambda j: (0, j)),
        out_shape=jax.ShapeDtypeStruct((SUBLANES, n), F32),
        name="ada",
    )(c8, w, b)


def _inproj_kernel(x_ref, xn_ref, mod_ref, modn_ref, g_ref, w_ref, wif_hi_ref, wif_lo_ref, pos_ref, inv_ref,
                   sign_ref, proj_ref, gates_ref, cos_ref, sin_ref, hi_a, lo_a, hi_b, lo_b):
    i = pl.program_id(0)
    j = pl.program_id(1)

    def normalise(x, mod):
        return _split(_rms_mod(x, g_ref[...], mod[0, 1:2, :], mod[0, 0:1, :]))

    @pl.when(jnp.logical_and(i == 0, j == 0))
    def _():
        hi_a[...], lo_a[...] = normalise(x_ref[...], mod_ref)

    n = x_ref.shape[0] // pl.num_programs(1)
    rows = pl.ds(pl.multiple_of(j * n, n), n)
    n_split = 4
    tc = w_ref.shape[1] // n_split

    def step(cur_hi, cur_lo, nxt_hi, nxt_lo):
        def matmul_steps():
            for c in range(n_split):
                proj_ref[:, c * tc:(c + 1) * tc] = _dot(cur_hi[...], w_ref[:, c * tc:(c + 1) * tc]).astype(BF16)
                yield False

        def side_steps():
            h_hi = cur_hi[rows, :]
            h_lo = cur_lo[rows, :]
            gates_ref[rows, :] = (_dot(h_hi, wif_hi_ref[...])
                                  + (_dot(h_lo, wif_hi_ref[...]) + _dot(h_hi, wif_lo_ref[...])))
            yield False
            ang = pos_ref[rows, :] * inv_ref[...]
            cos_ref[rows, :] = jnp.cos(ang)
            yield False
            sin_ref[rows, :] = jnp.sin(ang) * sign_ref[...]
            yield False
            nxt_hi[rows, :], nxt_lo[rows, :] = normalise(xn_ref[rows, :], modn_ref)
            yield False

        _round_robin([matmul_steps(), side_steps()])

    @pl.when(i % 2 == 0)
    def _():
        step(hi_a, lo_a, hi_b, lo_b)

    @pl.when(i % 2 == 1)
    def _():
        step(hi_b, lo_b, hi_a, lo_a)


GATE_COL = 3072


def _wprep_kernel(a_ref, b_ref, o_ref, gate_ref):
    c = pl.program_id(0)
    tn = a_ref.shape[0]

    @pl.when(c * tn == GATE_COL)
    def _():
        gate_ref[...] = a_ref[0:2 * N_HEADS, :]

    @pl.when((c + 1) * tn <= GATE_COL)
    def _():
        o_ref[...] = a_ref[...].T.astype(BF16)

    @pl.when((c + 1) * tn > GATE_COL)
    def _():
        shifted = jnp.concatenate([a_ref[2 * N_HEADS:, :], b_ref[...]], axis=0)
        o_ref[...] = shifted.T.astype(BF16)


def _wprep(w_t):
    d = w_t.shape[1]
    tn = WPREP_TN
    gate = 2 * N_HEADS
    assert GATE_COL % tn == 0 and w_t.shape[0] == PROJ_W + gate
    return pl.pallas_call(
        _wprep_kernel,
        grid=(PROJ_W // tn,),
        in_specs=[
            pl.BlockSpec((tn, d), lambda c: (c, 0)),
            pl.BlockSpec((gate, d), lambda c: ((c + 1) * (tn // gate), 0)),
        ],
        out_specs=[pl.BlockSpec((d, tn), lambda c: (0, c)), pl.BlockSpec((gate, d), lambda c: (0, 0))],
        out_shape=[jax.ShapeDtypeStruct((d, PROJ_W), BF16), jax.ShapeDtypeStruct((gate, d), F32)],
        compiler_params=pltpu.CompilerParams(
            dimension_semantics=("arbitrary",), vmem_limit_bytes=VMEM_LIMIT),
        name="wprep",
    )(w_t, w_t)


def _inproj(x2, mod6, g, w_main, wif_hi, wif_lo, pos_b, inv2, sign2, seq):
    t = x2.shape[0]
    tm, tn = INPROJ_TM, INPROJ_TN
    per_b = seq // tm
    last = t // tm - 1
    nxt = lambda i: jnp.minimum(i + 1, last)
    return pl.pallas_call(
        _inproj_kernel,
        grid=(t // tm, PROJ_W // tn),
        in_specs=[
            pl.BlockSpec((tm, D_MODEL), lambda i, j: (i, 0)),
            pl.BlockSpec((tm, D_MODEL), lambda i, j: (nxt(i), 0)),
            pl.BlockSpec((1, 6, D_MODEL), lambda i, j: (i // per_b, 0, 0)),
            pl.BlockSpec((1, 6, D_MODEL), lambda i, j: (nxt(i) // per_b, 0, 0)),
            pl.BlockSpec((1, D_MODEL), lambda i, j: (0, 0)),
            pl.BlockSpec((D_MODEL, tn), lambda i, j: (0, j)),
            pl.BlockSpec((D_MODEL, GATE_W), lambda i, j: (0, 0)),
            pl.BlockSpec((D_MODEL, GATE_W), lambda i, j: (0, 0)),
            pl.BlockSpec((tm, LANES), lambda i, j: (i, 0)),
            pl.BlockSpec((1, LANES), lambda i, j: (0, 0)),
            pl.BlockSpec((1, LANES), lambda i, j: (0, 0)),
        ],
        out_specs=[
            pl.BlockSpec((tm, tn), lambda i, j: (i, j)),
            pl.BlockSpec((tm, GATE_W), lambda i, j: (i, 0)),
            pl.BlockSpec((tm, LANES), lambda i, j: (i, 0)),
            pl.BlockSpec((tm, LANES), lambda i, j: (i, 0)),
        ],
        out_shape=[
            jax.ShapeDtypeStruct((t, PROJ_W), BF16),
            jax.ShapeDtypeStruct((t, GATE_W), F32),
            jax.ShapeDtypeStruct((t, LANES), F32),
            jax.ShapeDtypeStruct((t, LANES), F32),
        ],
        scratch_shapes=[pltpu.VMEM((tm, D_MODEL), BF16)] * 4,
        compiler_params=pltpu.CompilerParams(
            dimension_semantics=("arbitrary", "arbitrary"), vmem_limit_bytes=VMEM_LIMIT),
        name="inproj",
    )(x2, x2, mod6, mod6, g, w_main, wif_hi, wif_lo, pos_b, inv2, sign2)


def _row_mean(a):
    inv_n = jnp.full((a.shape[1], LANES), 1.0 / a.shape[1], BF16)
    m = _dot(a.astype(BF16), inv_n)
    return jnp.concatenate([m] * (a.shape[1] // LANES), axis=1)


def _head_norm(h, g):
    d = h - _row_mean(h)
    return d * lax.rsqrt(_row_mean(d * d) + EPS) * g


def _mix_kernel(proj_ref, gates_ref, cos_ref, sin_ref, convw_ref, convb_ref, bif_ref, mlg_ref, retg_ref,
                decay_ref, kdec_ref, qdec_ref, gch_ref,
                hm_ref, hr_ref,
                prev_ref, c_st, m_st, r_st):
    group = proj_ref.shape[0]

    @pl.when(pl.program_id(1) == 0)
    def _():
        def zero_state(i, carry):
            c_st[i] = jnp.zeros(c_st.shape[1:], F32)
            r_st[i] = jnp.zeros(r_st.shape[1:], F32)
            return carry

        lax.fori_loop(0, group * N_HEADS, zero_state, 0)
        prev_ref[...] = jnp.zeros_like(prev_ref)
        m_st[...] = jnp.zeros_like(m_st)

    ml_steps, ret_steps = [], []
    for gb in range(group):
        states = slice(gb * N_HEADS, (gb + 1) * N_HEADS)
        ml_steps.append(_mlstm_steps(proj_ref.at[gb], gates_ref.at[gb], convw_ref, convb_ref, bif_ref, mlg_ref,
                                     hm_ref.at[gb], prev_ref.at[gb], c_st.at[states], m_st.at[gb]))
        ret_steps.append(_retention_steps(proj_ref.at[gb], cos_ref.at[gb], sin_ref.at[gb], retg_ref, decay_ref,
                                          kdec_ref, qdec_ref, gch_ref, hr_ref.at[gb], r_st.at[states]))
    def delayed(gen, rounds):
        for _ in range(rounds):
            yield False
        yield from gen

    _round_robin(ml_steps + [delayed(g, 4) for g in ret_steps])


def _mlstm_steps(proj_ref, gates_ref, convw_ref, convb_ref, bif_ref, mlg_ref, hm_ref, prev_ref, c_st, m_st):
    L = CHUNK
    rows = lax.broadcasted_iota(jnp.int32, (L, L), 0)
    cols = lax.broadcasted_iota(jnp.int32, (L, L), 1)
    causal = rows >= cols
    tril = jnp.where(causal, 1.0, 0.0).astype(BF16)

    cur = proj_ref[:, C_MLQ:C_MLQ + 2 * QK_W]
    xx = jnp.concatenate([prev_ref[...], cur], axis=0)
    r2 = lax.broadcasted_iota(jnp.int32, (L, 2 * L), 0)
    c2 = lax.broadcasted_iota(jnp.int32, (L, 2 * L), 1)
    acc = convb_ref[...] + cur.astype(F32) * convw_ref[CONV_K - 1:CONV_K, :]
    for d in range(1, CONV_K):
        shift = jnp.where(c2 == r2 + (L - d), 1.0, 0.0).astype(BF16)
        acc = acc + _dot(shift, xx) * convw_ref[CONV_K - 1 - d:CONV_K - d, :]
        yield False
    prev_ref[...] = cur
    qk = _silu(acc)
    yield False

    g = gates_ref[...] + bif_ref[...]
    gi = g[:, :LANES]
    gf = g[:, LANES:]
    lf = jnp.minimum(gf, 0.0) - jnp.log(1.0 + jnp.exp(-jnp.abs(gf)))
    lf_hi, lf_lo = _split(lf)
    a_all = _dot(tril, lf_hi) + _dot(tril, lf_lo)
    yield False
    a_last = a_all[L - 1:L, :]
    bm = gi - a_all
    bm_t = bm.T
    w_state = a_last + bm
    m_loc = jnp.max(w_state, axis=0, keepdims=True)
    m_prev = m_st[...]
    inter_log = a_all + m_prev
    m_new = jnp.maximum(a_last + m_prev, m_loc)
    s_prev = jnp.exp(a_last + m_prev - m_new)
    s_loc = jnp.exp(m_loc - m_new)
    ws_all = jnp.exp(w_state - m_loc) * s_loc
    m_st[...] = m_new
    yield False

    ones_blk = jnp.ones((L, LANES), BF16)
    q_scale = D_QK ** -0.5

    for h in range(N_HEADS):
        q = (qk[:, h * D_QK:(h + 1) * D_QK] * q_scale).astype(BF16)
        k_f = qk[:, QK_W + h * D_QK:QK_W + (h + 1) * D_QK]
        k = k_f.astype(BF16)
        v_ext = jnp.concatenate([proj_ref[:, C_MLV + h * D_V:C_MLV + (h + 1) * D_V], ones_blk], axis=1)
        dlog = jnp.where(causal, a_all[:, h:h + 1] + bm_t[h:h + 1, :], -jnp.inf)
        m_intra = jnp.max(dlog, axis=-1, keepdims=True)
        s = _dot_nt(q, k)
        qc = _dot(q, c_st[h].astype(BF16))
        yield False
        il = inter_log[:, h:h + 1]
        m_t = jnp.maximum(il, m_intra)
        p = jnp.exp(dlog - m_t) * s
        isc = jnp.exp(il - m_t)
        tot = _dot(p.astype(BF16), v_ext) + isc * qc
        yield False
        den = tot[:, D_V:D_V + 1]
        hout = tot[:, :D_V] / jnp.maximum(jnp.abs(den), jnp.exp(-m_t))
        kw_t = (k_f * ws_all[:, h:h + 1]).T.astype(BF16)
        c_st[h] = s_prev[:, h:h + 1] * c_st[h] + _dot(kw_t, v_ext)
        yield False
        y = _head_norm(hout, mlg_ref[:, h * D_V:(h + 1) * D_V])
        o = proj_ref[:, C_MLO + h * D_V:C_MLO + (h + 1) * D_V].astype(F32)
        hm_ref[:, h * D_V:(h + 1) * D_V] = _sigmoid_mul(o, y).astype(BF16)
        yield False


def _retention_steps(proj_ref, cos_ref, sin_ref, retg_ref, decay_ref, kdec_ref, qdec_ref, gch_ref, hr_ref, r_st):
    cos2 = cos_ref[...]
    sin2 = sin_ref[...]
    k_scale = D_QK ** -0.5
    for h in range(N_HEADS):
        q_raw = proj_ref[:, C_RQ + h * D_QK:C_RQ + (h + 1) * D_QK].astype(F32)
        k_raw = proj_ref[:, C_RK + h * D_QK:C_RK + (h + 1) * D_QK].astype(F32)
        q = (q_raw * cos2 + pltpu.roll(q_raw, D_QK // 2, 1) * sin2).astype(BF16)
        k_f = (k_raw * cos2 + pltpu.roll(k_raw, D_QK // 2, 1) * sin2) * k_scale
        v = proj_ref[:, C_RV + h * D_V:C_RV + (h + 1) * D_V]
        yield False
        sc = _dot_nt(q, k_f.astype(BF16)) * decay_ref[h]
        hret = _dot(sc.astype(BF16), v) + _dot(q, r_st[h].astype(BF16)) * qdec_ref[h]
        yield False
        kd_t = (k_f * kdec_ref[h]).T.astype(BF16)
        r_st[h] = gch_ref[h][:, 0:1] * r_st[h] + _dot(kd_t, v)
        yield False
        y = _head_norm(hret, retg_ref[:, h * D_V:(h + 1) * D_V])
        gt = proj_ref[:, C_RG + h * D_V:C_RG + (h + 1) * D_V].astype(F32)
        hr_ref[:, h * D_V:(h + 1) * D_V] = (_silu(gt) * y).astype(BF16)
        yield False


def _mix(proj, gates, cos2, sin2, conv_w, conv_b, b_if2, ml_g, ret_g, decay, kdec, qdec, gch, batch, seq):
    t = proj.shape[0]
    nc = seq // CHUNK
    L = CHUNK
    G = MIX_GROUP
    proj = proj.reshape(batch, seq, PROJ_W)
    gates = gates.reshape(batch, seq, GATE_W)
    cos2 = cos2.reshape(batch, seq, LANES)
    sin2 = sin2.reshape(batch, seq, LANES)
    full = lambda shape: pl.BlockSpec(shape, lambda b, c: (0,) * len(shape))
    hm, hr = pl.pallas_call(
        _mix_kernel,
        grid=(batch // G, nc),
        in_specs=[
            pl.BlockSpec((G, L, MIX_W), lambda b, c: (b, c, 0)),
            pl.BlockSpec((G, L, GATE_W), lambda b, c: (b, c, 0)),
            pl.BlockSpec((G, L, LANES), lambda b, c: (b, c, 0)),
            pl.BlockSpec((G, L, LANES), lambda b, c: (b, c, 0)),
            full((CONV_K, 2 * QK_W)),
            full((1, 2 * QK_W)),
            full((1, GATE_W)),
            full((1, V_W)),
            full((1, V_W)),
            full((N_HEADS, L, L)),
            full((N_HEADS, L, 1)),
            full((N_HEADS, L, 1)),
            full((N_HEADS, 1, LANES)),
        ],
        out_specs=[
            pl.BlockSpec((G, L, V_W), lambda b, c: (b, c, 0)),
            pl.BlockSpec((G, L, V_W), lambda b, c: (b, c, 0)),
        ],
        out_shape=[jax.ShapeDtypeStruct((batch, seq, V_W), BF16), jax.ShapeDtypeStruct((batch, seq, V_W), BF16)],
        scratch_shapes=[
            pltpu.VMEM((G, L, 2 * QK_W), BF16),
            pltpu.VMEM((G * N_HEADS, D_QK, V_EXT), F32),
            pltpu.VMEM((G, 1, LANES), F32),
            pltpu.VMEM((G * N_HEADS, D_QK, D_V), F32),
        ],
        compiler_params=pltpu.CompilerParams(
            dimension_semantics=("arbitrary", "arbitrary"), vmem_limit_bytes=VMEM_LIMIT),
        name="mix",
    )(proj, gates, cos2, sin2, conv_w, conv_b, b_if2, ml_g, ret_g, decay, kdec, qdec, gch)
    return hm.reshape(t, V_W), hr.reshape(t, V_W)


def _to_slabs(ref, val):
    rows = val.shape[0]
    for s in range(SLAB):
        ref[pl.ds(s, rows, stride=SLAB), :] = val[:, s * LANES:(s + 1) * LANES]


def _from_slabs(ref, first_slab, rows):
    return jnp.concatenate(
        [ref[pl.ds(first_slab * SLAB + s, rows, stride=SLAB), :] for s in range(SLAB)], axis=1)


def _post_kernel(hm_ref, hr_ref, gm_ref, gr_ref, x_ref, mod_ref, wbm_ref, wbr_ref, wout_ref, g_ref,
                 wr_hi_ref, wr_lo_ref, br_ref, x1_ref, hf_ref, logit_ref):
    n_sub = 2
    sub = x_ref.shape[0] // n_sub

    def steps(s):
        r = pl.ds(s * sub, sub)
        ym = _dot(hm_ref[r, :], wbm_ref[...])
        yield False
        yr = _dot(hr_ref[r, :], wbr_ref[...])
        yield False
        y = _sigmoid_mul(gm_ref[r, :].astype(F32), ym) + _sigmoid_mul(gr_ref[r, :].astype(F32), yr)
        o = _dot(y.astype(BF16), wout_ref[...])
        yield False
        x1 = x_ref[r, :] + mod_ref[0, 2:3, :] * o
        x1_ref[r, :] = x1
        hf = _rms_mod(x1, g_ref[...], mod_ref[0, 4:5, :], mod_ref[0, 3:4, :])
        yield False
        _to_slabs(hf_ref.at[pl.ds(s * sub * SLAB, sub * SLAB)], hf)
        logit_ref[r, :] = _dot3(hf, wr_hi_ref[...], wr_lo_ref[...]) + br_ref[...]
        yield False

    _round_robin([steps(s) for s in range(n_sub)])


def _post(hm, hr, proj, x2, mod6, wbm, wbr, wout, g, wr_hi, wr_lo, br, seq):
    t = x2.shape[0]
    tm = POST_TM
    per_b = seq // tm
    d = D_MODEL
    const = lambda shape: pl.BlockSpec(shape, lambda i: (0,) * len(shape))
    return pl.pallas_call(
        _post_kernel,
        grid=(t // tm,),
        in_specs=[
            pl.BlockSpec((tm, d), lambda i: (i, 0)),
            pl.BlockSpec((tm, d), lambda i: (i, 0)),
            pl.BlockSpec((tm, d), lambda i: (i, C_GML // d)),
            pl.BlockSpec((tm, d), lambda i: (i, C_GRET // d)),
            pl.BlockSpec((tm, d), lambda i: (i, 0)),
            pl.BlockSpec((1, 6, d), lambda i: (i // per_b, 0, 0)),
            const((d, d)), const((d, d)), const((d, d)),
            const((1, d)),
            const((d, LANES)), const((d, LANES)), const((1, LANES)),
        ],
        out_specs=[
            pl.BlockSpec((tm, d), lambda i: (i, 0)),
            pl.BlockSpec((tm * SLAB, LANES), lambda i: (i, 0)),
            pl.BlockSpec((tm, LANES), lambda i: (i, 0)),
        ],
        out_shape=[
            jax.ShapeDtypeStruct((t, d), F32),
            jax.ShapeDtypeStruct((t * SLAB, LANES), F32),
            jax.ShapeDtypeStruct((t, LANES), F32),
        ],
        compiler_params=pltpu.CompilerParams(
            dimension_semantics=("arbitrary",), vmem_limit_bytes=VMEM_LIMIT),
        name="post",
    )(hm, hr, proj, proj, x2, mod6, wbm, wbr, wout, g, wr_hi, wr_lo, br)


def _route_kernel(logit_ref, dest_ref, w_ref, bend_ref, cnt_st, pad_st, sel_st, pick_st, wts_st):
    ph = pl.program_id(0)
    i = pl.program_id(1)
    tm = logit_ref.shape[0]
    rows = pl.ds(pl.multiple_of(i * tm, tm), tm)
    lane = lax.broadcasted_iota(jnp.int32, (tm, LANES), 1)
    lane_f = lane.astype(F32)

    @pl.when(jnp.logical_and(ph == 0, i == 0))
    def _():
        cnt_st[...] = jnp.zeros_like(cnt_st)

    @pl.when(ph == 0)
    def _():
        l = logit_ref[...]
        picks, vals = [], []
        sel = jnp.zeros((tm, LANES), F32)
        for _ in range(TOP_K):
            m = jnp.max(l, axis=-1, keepdims=True)
            idx = jnp.min(jnp.where(l == m, lane_f, float(LANES)), axis=-1, keepdims=True)
            oh = lane_f == idx
            picks.append(idx)
            vals.append(m)
            sel = jnp.where(oh, 1.0, sel)
            l = jnp.where(oh, -jnp.inf, l)
        ex = [jnp.exp(v - vals[0]) for v in vals]
        den = ex[0] + ex[1] + ex[2] + ex[3]
        pick = jnp.zeros((tm, LANES), F32)
        wts = jnp.zeros((tm, LANES), F32)
        for k in range(TOP_K):
            pick = jnp.where(lane == k, picks[k], pick)
            wts = jnp.where(lane == k, ex[k] / den, wts)
        sel_st[rows, :] = sel.astype(BF16)
        pick_st[rows, :] = pick
        wts_st[rows, :] = wts
        cnt_st[...] = cnt_st[...] + jnp.sum(sel, axis=0, keepdims=True)

    @pl.when(jnp.logical_and(ph == 1, i == 0))
    def _():
        blocks = jnp.floor((cnt_st[...] + (EXPERT_BLOCK - 1)) * (1.0 / EXPERT_BLOCK))
        r = lax.broadcasted_iota(jnp.int32, (LANES, LANES), 0)
        c = lax.broadcasted_iota(jnp.int32, (LANES, LANES), 1)
        upper = jnp.where(r < c, 1.0, 0.0).astype(BF16)
        blocks8 = jnp.broadcast_to(blocks, (SUBLANES, LANES))
        excl = _dot(blocks8.astype(BF16), upper)
        pad_st[...] = excl[0:1, :] * EXPERT_BLOCK
        row = lax.broadcasted_iota(jnp.int32, (SUBLANES, LANES), 0)
        bend_ref[...] = jnp.where(row == 0, excl + blocks8,
                                  jnp.where(row == 1, excl * EXPERT_BLOCK + cnt_st[...], 0.0))
        cnt_st[...] = jnp.zeros_like(cnt_st)

    @pl.when(ph == 1)
    def _():
        carry = cnt_st[...]
        sel = sel_st[rows, :]
        pick = pick_st[rows, :]
        r = lax.broadcasted_iota(jnp.int32, (tm, tm), 0)
        c = lax.broadcasted_iota(jnp.int32, (tm, tm), 1)
        lower = jnp.where(r > c, 1.0, 0.0).astype(BF16)
        base = pad_st[...] + carry + _dot(lower, sel)
        dest = jnp.zeros((tm, LANES), F32)
        for k in range(TOP_K):
            dk = jnp.sum(jnp.where(lane_f == pick[:, k:k + 1], base, 0.0), axis=-1, keepdims=True)
            dest = jnp.where(lane == k, dk, dest)
        dest_ref[...] = dest.astype(jnp.int32)
        w_ref[...] = wts_st[rows, :]
        cnt_st[...] = carry + jnp.sum(sel.astype(F32), axis=0, keepdims=True)


def _route(logits):
    t = logits.shape[0]
    tm = ROUTE_TM
    return pl.pallas_call(
        _route_kernel,
        grid=(2, t // tm),
        in_specs=[pl.BlockSpec((tm, LANES), lambda ph, i: (i * (1 - ph), 0))],
        out_specs=[
            pl.BlockSpec((tm, LANES), lambda ph, i: (i * ph, 0)),
            pl.BlockSpec((tm, LANES), lambda ph, i: (i * ph, 0)),
            pl.BlockSpec((SUBLANES, LANES), lambda ph, i: (0, 0)),
        ],
        out_shape=[
            jax.ShapeDtypeStruct((t, LANES), jnp.int32),
            jax.ShapeDtypeStruct((t, LANES), F32),
            jax.ShapeDtypeStruct((SUBLANES, LANES), F32),
        ],
        scratch_shapes=[
            pltpu.VMEM((1, LANES), F32),
            pltpu.VMEM((1, LANES), F32),
            pltpu.VMEM((t, LANES), BF16),
            pltpu.VMEM((t, LANES), F32),
            pltpu.VMEM((t, LANES), F32),
        ],
        compiler_params=pltpu.CompilerParams(
            dimension_semantics=("arbitrary", "arbitrary"), vmem_limit_bytes=VMEM_LIMIT),
        name="route",
    )(logits)


DMA_UNROLL = 8


def _row_copy(src_ref, src_slab, dst_ref, dst_slab, sem):
    src = pl.multiple_of(src_slab * SLAB, SLAB)
    dst = pl.multiple_of(dst_slab * SLAB, SLAB)
    return pltpu.make_async_copy(src_ref.at[pl.ds(src, SLAB), :], dst_ref.at[pl.ds(dst, SLAB), :], sem)


def _dispatch_kernel(dest_ref, padfirst_ref, hf_hbm, xs_out, tiles, zbuf, lsem, rsem, zsem, tsem):
    i = pl.program_id(0)
    n_steps = pl.num_programs(0)
    n_assign = dest_ref.shape[0]
    tile_rows = tiles.shape[1]
    blk_rows = zbuf.shape[0]
    n_buf = tiles.shape[0]

    def tile_load(step, slot):
        start = pl.multiple_of(step * tile_rows, tile_rows)
        return pltpu.make_async_copy(hf_hbm.at[pl.ds(start, tile_rows), :], tiles.at[slot], lsem.at[slot])

    def wait_rows(slot):
        for _ in range(TOP_K):
            pltpu.make_async_copy(tiles.at[slot], xs_out.at[pl.ds(0, tile_rows), :], rsem.at[slot]).wait()

    @pl.when(i == 0)
    def _():
        tile_load(0, 0).start()
        tile_load(1, 1).start()
        zbuf[...] = jnp.zeros_like(zbuf)
        for e in range(N_EXPERTS):
            start = pl.multiple_of(padfirst_ref[e] * SLAB, SLAB)
            pltpu.make_async_copy(zbuf, xs_out.at[pl.ds(start, blk_rows), :], zsem).start()
        for e in range(N_EXPERTS):
            pltpu.make_async_copy(zbuf, xs_out.at[pl.ds(0, blk_rows), :], zsem).wait()

        n_used = padfirst_ref[N_EXPERTS]
        n_total = xs_out.shape[0] // blk_rows

        def zero_start(b, carry):
            start = pl.multiple_of(b * blk_rows, blk_rows)
            pltpu.make_async_copy(zbuf, xs_out.at[pl.ds(start, blk_rows), :], tsem).start()
            return carry

        lax.fori_loop(n_used, n_total, zero_start, 0)

    slot = i % n_buf
    tile_load(i, slot).wait()
    tile = tiles.at[slot]

    def body(g, carry):
        for u in range(DMA_UNROLL):
            t_local = g * (DMA_UNROLL // TOP_K) + u // TOP_K
            _row_copy(tile, t_local, xs_out, dest_ref[g * DMA_UNROLL + u], rsem.at[slot]).start(priority=u % 2)
        return carry

    lax.fori_loop(0, n_assign // DMA_UNROLL, body, 0)

    @pl.when(i >= 1)
    def _():
        wait_rows((i + n_buf - 1) % n_buf)

    @pl.when(i + 2 < n_steps)
    def _():
        tile_load(i + 2, (i + 2) % n_buf).start()

    @pl.when(i == n_steps - 1)
    def _():
        wait_rows(slot)

        def zero_wait(b, carry):
            pltpu.make_async_copy(zbuf, xs_out.at[pl.ds(0, blk_rows), :], tsem).wait()
            return carry

        lax.fori_loop(padfirst_ref[N_EXPERTS], xs_out.shape[0] // blk_rows, zero_wait, 0)


def _dispatch(dest_flat, pad_first, hf_slab, n_blocks):
    n_assign = dest_flat.shape[0]
    tm = DISPATCH_TM
    blk_rows = EXPERT_BLOCK * SLAB
    return pl.pallas_call(
        _dispatch_kernel,
        grid=(n_assign // (tm * TOP_K),),
        in_specs=[
            pl.BlockSpec((tm * TOP_K,), lambda i: (i,), memory_space=pltpu.SMEM),
            pl.BlockSpec(memory_space=pltpu.SMEM),
            pl.BlockSpec(memory_space=pl.ANY),
        ],
        out_specs=pl.BlockSpec(memory_space=pl.ANY),
        out_shape=jax.ShapeDtypeStruct(((n_blocks + 1) * blk_rows, LANES), F32),
        scratch_shapes=[
            pltpu.VMEM((3, tm * SLAB, LANES), F32),
            pltpu.VMEM((blk_rows, LANES), F32),
            pltpu.SemaphoreType.DMA((3,)),
            pltpu.SemaphoreType.DMA((3,)),
            pltpu.SemaphoreType.DMA,
            pltpu.SemaphoreType.DMA,
        ],
        compiler_params=pltpu.CompilerParams(dimension_semantics=("arbitrary",), vmem_limit_bytes=VMEM_LIMIT),
        name="dispatch",
    )(dest_flat, pad_first, hf_slab)


def _weight_copies(wgu_hbm, wd_hbm, wgu_f32, wd_f32, wsem, expert, slot):
    return (pltpu.make_async_copy(wgu_hbm.at[expert], wgu_f32.at[slot], wsem.at[0, slot]),
            pltpu.make_async_copy(wd_hbm.at[expert], wd_f32.at[slot], wsem.at[1, slot]))


def _expert_kernel(be_ref, nused_ref, first_ref, slot_ref, next_ref,
                   xs_ref, wgu_hbm, bgu_ref, wd_hbm, bd_ref, ys_ref,
                   wgu_f32, wd_f32, wgu_bf, wd_bf, wsem):
    b = pl.program_id(0)
    copies = functools.partial(_weight_copies, wgu_hbm, wd_hbm, wgu_f32, wd_f32, wsem)

    @pl.when(jnp.logical_and(b < nused_ref[0], first_ref[b] == 1))
    def _():
        slot = slot_ref[b]

        @pl.when(b == 0)
        def _():
            for cp in copies(be_ref[0], 0):
                cp.start()

        for cp in copies(be_ref[b], slot):
            cp.wait()

        @pl.when(next_ref[b] >= 0)
        def _():
            for cp in copies(next_ref[b], 1 - slot):
                cp.start()

        two = 2 * LANES
        r = lax.broadcasted_iota(jnp.int32, (two, two), 0)
        c = lax.broadcasted_iota(jnp.int32, (two, two), 1)
        src = jnp.where(c < LANES, 2 * c, 2 * (c - LANES) + 1)
        perm = jnp.where(r == src, 1.0, 0.0).astype(BF16)
        for blk in range(wgu_bf.shape[1] // two):
            wb = wgu_f32[slot, :, blk * two:(blk + 1) * two].astype(BF16)
            wgu_bf[:, blk * two:(blk + 1) * two] = _dot(wb, perm).astype(BF16)
        wd_bf[...] = wd_f32[slot].astype(BF16)

    @pl.when(b < nused_ref[0])
    def _():
        two = 2 * LANES
        x = _from_slabs(xs_ref, 0, EXPERT_BLOCK).astype(BF16)
        gu = _dot(x, wgu_bf[...]) + bgu_ref[0]
        n_blk = gu.shape[1] // two
        x_glu = jnp.concatenate([gu[:, i * two:i * two + LANES] for i in range(n_blk)], axis=1)
        x_lin = jnp.concatenate([gu[:, i * two + LANES:(i + 1) * two] for i in range(n_blk)], axis=1)
        x_glu = jnp.minimum(x_glu, SWIGLU_LIMIT)
        x_lin = jnp.clip(x_lin, -SWIGLU_LIMIT, SWIGLU_LIMIT)
        act = _sigmoid_mul(SWIGLU_ALPHA * x_glu, x_glu * (x_lin + 1.0))
        y = _dot(act.astype(BF16), wd_bf[...]) + bd_ref[0]
        _to_slabs(ys_ref, y)

    @pl.when(b >= nused_ref[0])
    def _():
        ys_ref[...] = jnp.zeros_like(ys_ref)


def _experts(block_expert, n_used, xs, w_gate_up, b_gu_perm, w_down, b_down):
    n_blocks = block_expert.shape[0]
    d = D_MODEL
    f2 = w_gate_up.shape[-1]
    blk_rows = EXPERT_BLOCK * SLAB

    idx = jnp.arange(n_blocks, dtype=jnp.int32)
    used = idx < n_used
    prev = jnp.concatenate([jnp.full((1,), -1, jnp.int32), block_expert[:-1]])
    first = (used & (block_expert != prev)).astype(jnp.int32)
    slot = (jnp.cumsum(first) - 1) % 2
    later = used[None, :] & (block_expert[None, :] > block_expert[:, None])
    nxt = jnp.min(jnp.where(later, block_expert[None, :], N_EXPERTS), axis=1)
    nxt = jnp.where(nxt == N_EXPERTS, -1, nxt)

    grid_spec = pltpu.PrefetchScalarGridSpec(
        num_scalar_prefetch=5,
        grid=(n_blocks,),
        in_specs=[
            pl.BlockSpec((blk_rows, LANES), lambda b, be, nu, *_: (jnp.minimum(b, nu[0] - 1), 0)),
            pl.BlockSpec(memory_space=pl.ANY),
            pl.BlockSpec((1, 1, f2), lambda b, be, *_: (be[b], 0, 0)),
            pl.BlockSpec(memory_space=pl.ANY),
            pl.BlockSpec((1, 1, d), lambda b, be, *_: (be[b], 0, 0)),
        ],
        out_specs=pl.BlockSpec((blk_rows, LANES), lambda b, be, *_: (b, 0)),
        scratch_shapes=[
            pltpu.VMEM((2, d, f2), F32),
            pltpu.VMEM((2, f2 // 2, d), F32),
            pltpu.VMEM((d, f2), BF16),
            pltpu.VMEM((f2 // 2, d), BF16),
            pltpu.SemaphoreType.DMA((2, 2)),
        ],
    )
    return pl.pallas_call(
        _expert_kernel,
        grid_spec=grid_spec,
        out_shape=jax.ShapeDtypeStruct((n_blocks * blk_rows, LANES), F32),
        compiler_params=pltpu.CompilerParams(
            dimension_semantics=("arbitrary",), vmem_limit_bytes=VMEM_LIMIT),
        name="expert",
    )(block_expert, n_used, first, slot.astype(jnp.int32), nxt.astype(jnp.int32),
      xs, w_gate_up, b_gu_perm, w_down, b_down)


def _combine_kernel(dest_ref, dest_next_ref, ys_hbm, w_ref, x1_ref, mod_ref, g_ref, o_ref, buf, sem):
    i = pl.program_id(0)
    n_steps = pl.num_programs(0)
    tm = x1_ref.shape[0]
    n_assign = tm * TOP_K
    slot = i % 2

    def gather(d_ref, to_slot):
        def body(g, carry):
            for u in range(DMA_UNROLL):
                t_local = g * (DMA_UNROLL // TOP_K) + u // TOP_K
                k = u % TOP_K
                _row_copy(ys_hbm, d_ref[g * DMA_UNROLL + u], buf.at[to_slot], k * tm + t_local,
                          sem.at[to_slot]).start(priority=u % 2)
            return carry

        lax.fori_loop(0, n_assign // DMA_UNROLL, body, 0)

    @pl.when(i == 0)
    def _():
        gather(dest_ref, 0)

    @pl.when(i + 1 < n_steps)
    def _():
        gather(dest_next_ref, 1 - slot)

    cur = buf.at[slot]
    pltpu.make_async_copy(ys_hbm.at[pl.ds(0, n_assign * SLAB), :], cur, sem.at[slot]).wait()

    w = w_ref[...]
    y = w[:, 0:1] * _from_slabs(cur, 0, tm)
    for k in range(1, TOP_K):
        y = y + w[:, k:k + 1] * _from_slabs(cur, k * tm, tm)
    x = x1_ref[...] + mod_ref[0, 5:6, :] * y
    ms = jnp.mean(x * x, axis=-1, keepdims=True)
    o_ref[...] = x * lax.rsqrt(ms + EPS) * g_ref[...]


def _combine(dest_flat, ys, wts, x1, mod6, g, seq):
    t = x1.shape[0]
    tm = COMBINE_TM
    per_b = seq // tm
    d = D_MODEL
    last = t // tm - 1
    return pl.pallas_call(
        _combine_kernel,
        grid=(t // tm,),
        in_specs=[
            pl.BlockSpec((tm * TOP_K,), lambda i: (i,), memory_space=pltpu.SMEM),
            pl.BlockSpec((tm * TOP_K,), lambda i: (jnp.minimum(i + 1, last),), memory_space=pltpu.SMEM),
            pl.BlockSpec(memory_space=pl.ANY),
            pl.BlockSpec((tm, LANES), lambda i: (i, 0)),
            pl.BlockSpec((tm, d), lambda i: (i, 0)),
            pl.BlockSpec((1, 6, d), lambda i: (i // per_b, 0, 0)),
            pl.BlockSpec((1, d), lambda i: (0, 0)),
        ],
        out_specs=pl.BlockSpec((tm, d), lambda i: (i, 0)),
        out_shape=jax.ShapeDtypeStruct((t, d), F32),
        scratch_shapes=[pltpu.VMEM((2, tm * TOP_K * SLAB, LANES), F32), pltpu.SemaphoreType.DMA((2,))],
        compiler_params=pltpu.CompilerParams(
            dimension_semantics=("arbitrary",), vmem_limit_bytes=VMEM_LIMIT),
        name="combine",
    )(dest_flat, dest_flat, ys, wts, x1, mod6, g)


def kernel(x, c, positions, w_ada, b_ada, norm_mix_g, w_in, conv_w, conv_b, b_if, ml_norm_g, ret_norm_g,
           w_branch_ml, w_branch_ret, w_out, norm_ffn_g, w_router, b_router, w_gate_up, b_gate_up, w_down,
           b_down, norm_final_g):
    batch, seq, d = x.shape
    t = batch * seq
    x2 = x.reshape(t, d)
    depth = w_ada.shape[0]
    c8 = jnp.concatenate([c, jnp.zeros((SUBLANES - batch, d), c.dtype)], axis=0)

    half = D_QK // 2
    inv = ROPE_BASE ** (-jnp.arange(half, dtype=F32) / half)
    inv2 = jnp.concatenate([inv, inv])[None, :]
    sign2 = jnp.concatenate([-jnp.ones((half,), F32), jnp.ones((half,), F32)])[None, :]
    pos_b = jnp.broadcast_to(positions.astype(F32).reshape(t, 1), (t, LANES))
    L = CHUNK
    log_gamma = jnp.log(1.0 - 2.0 ** (-5.0 - jnp.arange(N_HEADS, dtype=F32)))
    pos = jnp.arange(L, dtype=F32)
    rel = pos[:, None] - pos[None, :]
    decay = jnp.where(rel >= 0, jnp.exp(log_gamma[:, None, None] * jnp.maximum(rel, 0.0)), 0.0)
    kdec = jnp.exp(log_gamma[:, None] * (L - 1 - pos))[:, :, None]
    qdec = jnp.exp(log_gamma[:, None] * (pos + 1.0))[:, :, None]
    gch = jnp.broadcast_to(jnp.exp(log_gamma * L)[:, None, None], (N_HEADS, 1, LANES))

    assert depth == 1, "the final norm is fused after the single layer"
    l = 0
    mod = _ada(c8, w_ada[l], b_ada[l][None, :])
    mod6 = mod[:batch].reshape(batch, 6, d)

    w = w_in[l]
    w_main, w_gate = _wprep(w.T)
    w_if = w_gate.T
    zpad = jnp.zeros((d, LANES - N_HEADS), F32)
    wif = jnp.concatenate([w_if[:, :N_HEADS], zpad, w_if[:, N_HEADS:], zpad], axis=1)
    wif_hi, wif_lo = _split(wif)
    zb = jnp.zeros((LANES - N_HEADS,), F32)
    b_if2 = jnp.concatenate([b_if[l][:N_HEADS], zb, b_if[l][N_HEADS:], zb])[None, :]

    proj, gates, cos2, sin2 = _inproj(x2, mod6, norm_mix_g[l][None, :], w_main, wif_hi, wif_lo,
                                      pos_b, inv2, sign2, seq)
    hm, hr = _mix(proj, gates, cos2, sin2, conv_w[l], conv_b[l][None, :], b_if2,
                  ml_norm_g[l][None, :], ret_norm_g[l][None, :], decay, kdec, qdec, gch, batch, seq)

    wr = jnp.concatenate([w_router[l], jnp.zeros((d, LANES - N_EXPERTS), F32)], axis=1)
    wr_hi, wr_lo = _split(wr)
    br = jnp.concatenate([b_router[l], jnp.full((LANES - N_EXPERTS,), NEG_BIG, F32)])[None, :]
    x1, hf, logits = _post(hm, hr, proj, x2, mod6, w_branch_ml[l].astype(BF16), w_branch_ret[l].astype(BF16),
                           w_out[l].astype(BF16), norm_ffn_g[l][None, :], wr_hi, wr_lo, br, seq)

    n_assign = t * TOP_K
    n_blocks = -(-n_assign // EXPERT_BLOCK) + N_EXPERTS
    dest, wts, tables = _route(logits)
    dest_flat = dest[:, :TOP_K].reshape(n_assign)
    bend = tables[0, :N_EXPERTS].astype(jnp.int32)
    pad_first = tables[1, :N_EXPERTS].astype(jnp.int32)
    n_used = bend[N_EXPERTS - 1:]
    blk = jnp.minimum(jnp.arange(n_blocks, dtype=jnp.int32), n_used - 1)
    block_expert = jnp.minimum(
        jnp.sum((bend[None, :] <= blk[:, None]).astype(jnp.int32), axis=1), N_EXPERTS - 1)
    xs = _dispatch(dest_flat, jnp.concatenate([pad_first, n_used]), hf, n_blocks)
    f2 = w_gate_up.shape[-1]
    bgu = b_gate_up[l].reshape(N_EXPERTS, f2 // (2 * LANES), LANES, 2)
    bgu = jnp.swapaxes(bgu, -1, -2).reshape(N_EXPERTS, 1, f2)
    ys = _experts(block_expert, n_used, xs, w_gate_up[l], bgu, w_down[l], b_down[l][:, None, :])
    out = _combine(dest_flat, ys, wts, x1, mod6, norm_final_g[None, :], seq)
    return out.reshape(batch, seq, d)
```

```python
import functools

import jax
import jax.numpy as jnp
from jax import lax
from jax.experimental import pallas as pl
from jax.experimental.pallas import tpu as pltpu

D_MODEL = 1024
N_HEADS = 4
D_QK = 128
D_V = 256
CONV_K = 4
ROPE_BASE = 10000.0
CHUNK = 128
N_EXPERTS = 32
TOP_K = 4
SWIGLU_LIMIT = 7.0
SWIGLU_ALPHA = 1.702
EXPERT_BLOCK = 512
EPS = 1e-5

QK_W = N_HEADS * D_QK
V_W = N_HEADS * D_V
LANES = 128
SUBLANES = 8
GATE_W = 2 * LANES
V_EXT = D_V + LANES
MIX_GROUP = 4
SLAB = D_MODEL // LANES

C_MLQ, C_MLK, C_MLV, C_MLO = 0, 512, 1024, 2048
C_RQ, C_RK, C_RV, C_RG = 3072, 3584, 4096, 5120
C_GML, C_GRET = 6144, 7168
MIX_W = 6144
PROJ_W = 8192

BF16 = jnp.bfloat16
F32 = jnp.float32
NEG_BIG = -1e30

ADA_TN = 1024
WPREP_TN = 1024
INPROJ_TM, INPROJ_TN = 1024, 2048
POST_TM = 512
ROUTE_TM = 512
DISPATCH_TM = 512
COMBINE_TM = 256
VMEM_LIMIT = 56 * 1024 * 1024


def _dot(a, b):
    return jnp.dot(a, b, preferred_element_type=F32)


def _dot_nt(a, b):
    return lax.dot_general(a, b, (((1,), (1,)), ((), ())), preferred_element_type=F32)


def _split(a):
    hi = a.astype(BF16)
    lo = (a - hi.astype(F32)).astype(BF16)
    return hi, lo


def _dot3(a, b_hi, b_lo):
    a_hi, a_lo = _split(a)
    return _dot(a_hi, b_hi) + (_dot(a_lo, b_hi) + _dot(a_hi, b_lo))


def _sigmoid_mul(x, y):
    hy = 0.5 * y
    return hy * jnp.tanh(0.5 * x) + hy


def _silu(x):
    hx = 0.5 * x
    return hx * jnp.tanh(hx) + hx


def _round_robin(stage_generators):
    pending = list(stage_generators)
    while pending:
        for gen in list(pending):
            if next(gen, True):
                pending.remove(gen)


def _rms_mod(x, g, scale, shift):
    ms = jnp.mean(x * x, axis=-1, keepdims=True)
    return (x * lax.rsqrt(ms + EPS) * g) * (1.0 + scale) + shift


def _ada_kernel(c_ref, w_ref, b_ref, o_ref):
    w_hi, w_lo = _split(w_ref[...])
    o_ref[...] = _dot3(_silu(c_ref[...]), w_hi, w_lo) + b_ref[...]


def _ada(c8, w, b):
    n = w.shape[1]
    tn = ADA_TN
    return pl.pallas_call(
        _ada_kernel,
        grid=(n // tn,),
        in_specs=[
            pl.BlockSpec((SUBLANES, D_MODEL), lambda j: (0, 0)),
            pl.BlockSpec((D_MODEL, tn), lambda j: (0, j)),
            pl.BlockSpec((1, tn), lambda j: (0, j)),
        ],
        out_specs=pl.BlockSpec((SUBLANES, tn), lambda j: (0, j)),
        out_shape=jax.ShapeDtypeStruct((SUBLANES, n), F32),
        name="ada",
    )(c8, w, b)


def _inproj_kernel(x_ref, xn_ref, mod_ref, modn_ref, g_ref, w_ref, wif_hi_ref, wif_lo_ref, pos_ref, inv_ref,
                   sign_ref, proj_ref, gates_ref, cos_ref, sin_ref, hi_a, lo_a, hi_b, lo_b):
    i = pl.program_id(0)
    j = pl.program_id(1)

    def normalise(x, mod):
        return _split(_rms_mod(x, g_ref[...], mod[0, 1:2, :], mod[0, 0:1, :]))

    @pl.when(jnp.logical_and(i == 0, j == 0))
    def _():
        hi_a[...], lo_a[...] = normalise(x_ref[...], mod_ref)

    n = x_ref.shape[0] // pl.num_programs(1)
    rows = pl.ds(pl.multiple_of(j * n, n), n)
    n_split = 4
    tc = w_ref.shape[1] // n_split

    def step(cur_hi, cur_lo, nxt_hi, nxt_lo):
        def matmul_steps():
            for c in range(n_split):
                proj_ref[:, c * tc:(c + 1) * tc] = _dot(cur_hi[...], w_ref[:, c * tc:(c + 1) * tc]).astype(BF16)
                yield False

        def side_steps():
            h_hi = cur_hi[rows, :]
            h_lo = cur_lo[rows, :]
            gates_ref[rows, :] = (_dot(h_hi, wif_hi_ref[...])
                                  + (_dot(h_lo, wif_hi_ref[...]) + _dot(h_hi, wif_lo_ref[...])))
            yield False
            ang = pos_ref[rows, :] * inv_ref[...]
            cos_ref[rows, :] = jnp.cos(ang)
            yield False
            sin_ref[rows, :] = jnp.sin(ang) * sign_ref[...]
            yield False
            nxt_hi[rows, :], nxt_lo[rows, :] = normalise(xn_ref[rows, :], modn_ref)
            yield False

        _round_robin([matmul_steps(), side_steps()])

    @pl.when(i % 2 == 0)
    def _():
        step(hi_a, lo_a, hi_b, lo_b)

    @pl.when(i % 2 == 1)
    def _():
        step(hi_b, lo_b, hi_a, lo_a)


GATE_COL = 3072


def _wprep_kernel(a_ref, b_ref, o_ref, gate_ref):
    c = pl.program_id(0)
    tn = a_ref.shape[0]

    @pl.when(c * tn == GATE_COL)
    def _():
        gate_ref[...] = a_ref[0:2 * N_HEADS, :]

    @pl.when((c + 1) * tn <= GATE_COL)
    def _():
        o_ref[...] = a_ref[...].T.astype(BF16)

    @pl.when((c + 1) * tn > GATE_COL)
    def _():
        shifted = jnp.concatenate([a_ref[2 * N_HEADS:, :], b_ref[...]], axis=0)
        o_ref[...] = shifted.T.astype(BF16)


def _wprep(w_t):
    d = w_t.shape[1]
    tn = WPREP_TN
    gate = 2 * N_HEADS
    assert GATE_COL % tn == 0 and w_t.shape[0] == PROJ_W + gate
    return pl.pallas_call(
        _wprep_kernel,
        grid=(PROJ_W // tn,),
        in_specs=[
            pl.BlockSpec((tn, d), lambda c: (c, 0)),
            pl.BlockSpec((gate, d), lambda c: ((c + 1) * (tn // gate), 0)),
        ],
        out_specs=[pl.BlockSpec((d, tn), lambda c: (0, c)), pl.BlockSpec((gate, d), lambda c: (0, 0))],
        out_shape=[jax.ShapeDtypeStruct((d, PROJ_W), BF16), jax.ShapeDtypeStruct((gate, d), F32)],
        compiler_params=pltpu.CompilerParams(
            dimension_semantics=("arbitrary",), vmem_limit_bytes=VMEM_LIMIT),
        name="wprep",
    )(w_t, w_t)


def _inproj(x2, mod6, g, w_main, wif_hi, wif_lo, pos_b, inv2, sign2, seq):
    t = x2.shape[0]
    tm, tn = INPROJ_TM, INPROJ_TN
    per_b = seq // tm
    last = t // tm - 1
    nxt = lambda i: jnp.minimum(i + 1, last)
    return pl.pallas_call(
        _inproj_kernel,
        grid=(t // tm, PROJ_W // tn),
        in_specs=[
            pl.BlockSpec((tm, D_MODEL), lambda i, j: (i, 0)),
            pl.BlockSpec((tm, D_MODEL), lambda i, j: (nxt(i), 0)),
            pl.BlockSpec((1, 6, D_MODEL), lambda i, j: (i // per_b, 0, 0)),
            pl.BlockSpec((1, 6, D_MODEL), lambda i, j: (nxt(i) // per_b, 0, 0)),
            pl.BlockSpec((1, D_MODEL), lambda i, j: (0, 0)),
            pl.BlockSpec((D_MODEL, tn), lambda i, j: (0, j)),
            pl.BlockSpec((D_MODEL, GATE_W), lambda i, j: (0, 0)),
            pl.BlockSpec((D_MODEL, GATE_W), lambda i, j: (0, 0)),
            pl.BlockSpec((tm, LANES), lambda i, j: (i, 0)),
            pl.BlockSpec((1, LANES), lambda i, j: (0, 0)),
            pl.BlockSpec((1, LANES), lambda i, j: (0, 0)),
        ],
        out_specs=[
            pl.BlockSpec((tm, tn), lambda i, j: (i, j)),
            pl.BlockSpec((tm, GATE_W), lambda i, j: (i, 0)),
            pl.BlockSpec((tm, LANES), lambda i, j: (i, 0)),
            pl.BlockSpec((tm, LANES), lambda i, j: (i, 0)),
        ],
        out_shape=[
            jax.ShapeDtypeStruct((t, PROJ_W), BF16),
            jax.ShapeDtypeStruct((t, GATE_W), F32),
            jax.ShapeDtypeStruct((t, LANES), F32),
            jax.ShapeDtypeStruct((t, LANES), F32),
        ],
        scratch_shapes=[pltpu.VMEM((tm, D_MODEL), BF16)] * 4,
        compiler_params=pltpu.CompilerParams(
            dimension_semantics=("arbitrary", "arbitrary"), vmem_limit_bytes=VMEM_LIMIT),
        name="inproj",
    )(x2, x2, mod6, mod6, g, w_main, wif_hi, wif_lo, pos_b, inv2, sign2)


def _row_mean(a):
    inv_n = jnp.full((a.shape[1], LANES), 1.0 / a.shape[1], BF16)
    m = _dot(a.astype(BF16), inv_n)
    return jnp.concatenate([m] * (a.shape[1] // LANES), axis=1)


def _head_norm(h, g):
    d = h - _row_mean(h)
    return d * lax.rsqrt(_row_mean(d * d) + EPS) * g


def _mix_kernel(proj_ref, gates_ref, cos_ref, sin_ref, convw_ref, convb_ref, bif_ref, mlg_ref, retg_ref,
                decay_ref, kdec_ref, qdec_ref, gch_ref,
                hm_ref, hr_ref,
                prev_ref, c_st, m_st, r_st):
    group = proj_ref.shape[0]

    @pl.when(pl.program_id(1) == 0)
    def _():
        def zero_state(i, carry):
            c_st[i] = jnp.zeros(c_st.shape[1:], F32)
            r_st[i] = jnp.zeros(r_st.shape[1:], F32)
            return carry

        lax.fori_loop(0, group * N_HEADS, zero_state, 0)
        prev_ref[...] = jnp.zeros_like(prev_ref)
        m_st[...] = jnp.zeros_like(m_st)

    ml_steps, ret_steps = [], []
    for gb in range(group):
        states = slice(gb * N_HEADS, (gb + 1) * N_HEADS)
        ml_steps.append(_mlstm_steps(proj_ref.at[gb], gates_ref.at[gb], convw_ref, convb_ref, bif_ref, mlg_ref,
                                     hm_ref.at[gb], prev_ref.at[gb], c_st.at[states], m_st.at[gb]))
        ret_steps.append(_retention_steps(proj_ref.at[gb], cos_ref.at[gb], sin_ref.at[gb], retg_ref, decay_ref,
                                          kdec_ref, qdec_ref, gch_ref, hr_ref.at[gb], r_st.at[states]))
    def delayed(gen, rounds):
        for _ in range(rounds):
            yield False
        yield from gen

    _round_robin(ml_steps + [delayed(g, 4) for g in ret_steps])


def _mlstm_steps(proj_ref, gates_ref, convw_ref, convb_ref, bif_ref, mlg_ref, hm_ref, prev_ref, c_st, m_st):
    L = CHUNK
    rows = lax.broadcasted_iota(jnp.int32, (L, L), 0)
    cols = lax.broadcasted_iota(jnp.int32, (L, L), 1)
    causal = rows >= cols
    tril = jnp.where(causal, 1.0, 0.0).astype(BF16)

    cur = proj_ref[:, C_MLQ:C_MLQ + 2 * QK_W]
    xx = jnp.concatenate([prev_ref[...], cur], axis=0)
    r2 = lax.broadcasted_iota(jnp.int32, (L, 2 * L), 0)
    c2 = lax.broadcasted_iota(jnp.int32, (L, 2 * L), 1)
    acc = convb_ref[...] + cur.astype(F32) * convw_ref[CONV_K - 1:CONV_K, :]
    for d in range(1, CONV_K):
        shift = jnp.where(c2 == r2 + (L - d), 1.0, 0.0).astype(BF16)
        acc = acc + _dot(shift, xx) * convw_ref[CONV_K - 1 - d:CONV_K - d, :]
        yield False
    prev_ref[...] = cur
    qk = _silu(acc)
    yield False

    g = gates_ref[...] + bif_ref[...]
    gi = g[:, :LANES]
    gf = g[:, LANES:]
    lf = jnp.minimum(gf, 0.0) - jnp.log(1.0 + jnp.exp(-jnp.abs(gf)))
    lf_hi, lf_lo = _split(lf)
    a_all = _dot(tril, lf_hi) + _dot(tril, lf_lo)
    yield False
    a_last = a_all[L - 1:L, :]
    bm = gi - a_all
    bm_t = bm.T
    w_state = a_last + bm
    m_loc = jnp.max(w_state, axis=0, keepdims=True)
    m_prev = m_st[...]
    inter_log = a_all + m_prev
    m_new = jnp.maximum(a_last + m_prev, m_loc)
    s_prev = jnp.exp(a_last + m_prev - m_new)
    s_loc = jnp.exp(m_loc - m_new)
    ws_all = jnp.exp(w_state - m_loc) * s_loc
    m_st[...] = m_new
    yield False

    ones_blk = jnp.ones((L, LANES), BF16)
    q_scale = D_QK ** -0.5

    for h in range(N_HEADS):
        q = (qk[:, h * D_QK:(h + 1) * D_QK] * q_scale).astype(BF16)
        k_f = qk[:, QK_W + h * D_QK:QK_W + (h + 1) * D_QK]
        k = k_f.astype(BF16)
        v_ext = jnp.concatenate([proj_ref[:, C_MLV + h * D_V:C_MLV + (h + 1) * D_V], ones_blk], axis=1)
        dlog = jnp.where(causal, a_all[:, h:h + 1] + bm_t[h:h + 1, :], -jnp.inf)
        m_intra = jnp.max(dlog, axis=-1, keepdims=True)
        s = _dot_nt(q, k)
        qc = _dot(q, c_st[h].astype(BF16))
        yield False
        il = inter_log[:, h:h + 1]
        m_t = jnp.maximum(il, m_intra)
        p = jnp.exp(dlog - m_t) * s
        isc = jnp.exp(il - m_t)
        tot = _dot(p.astype(BF16), v_ext) + isc * qc
        yield False
        den = tot[:, D_V:D_V + 1]
        hout = tot[:, :D_V] / jnp.maximum(jnp.abs(den), jnp.exp(-m_t))
        kw_t = (k_f * ws_all[:, h:h + 1]).T.astype(BF16)
        c_st[h] = s_prev[:, h:h + 1] * c_st[h] + _dot(kw_t, v_ext)
        yield False
        y = _head_norm(hout, mlg_ref[:, h * D_V:(h + 1) * D_V])
        o = proj_ref[:, C_MLO + h * D_V:C_MLO + (h + 1) * D_V].astype(F32)
        hm_ref[:, h * D_V:(h + 1) * D_V] = _sigmoid_mul(o, y).astype(BF16)
        yield False


def _retention_steps(proj_ref, cos_ref, sin_ref, retg_ref, decay_ref, kdec_ref, qdec_ref, gch_ref, hr_ref, r_st):
    cos2 = cos_ref[...]
    sin2 = sin_ref[...]
    k_scale = D_QK ** -0.5
    for h in range(N_HEADS):
        q_raw = proj_ref[:, C_RQ + h * D_QK:C_RQ + (h + 1) * D_QK].astype(F32)
        k_raw = proj_ref[:, C_RK + h * D_QK:C_RK + (h + 1) * D_QK].astype(F32)
        q = (q_raw * cos2 + pltpu.roll(q_raw, D_QK // 2, 1) * sin2).astype(BF16)
        k_f = (k_raw * cos2 + pltpu.roll(k_raw, D_QK // 2, 1) * sin2) * k_scale
        v = proj_ref[:, C_RV + h * D_V:C_RV + (h + 1) * D_V]
        yield False
        sc = _dot_nt(q, k_f.astype(BF16)) * decay_ref[h]
        hret = _dot(sc.astype(BF16), v) + _dot(q, r_st[h].astype(BF16)) * qdec_ref[h]
        yield False
        kd_t = (k_f * kdec_ref[h]).T.astype(BF16)
        r_st[h] = gch_ref[h][:, 0:1] * r_st[h] + _dot(kd_t, v)
        yield False
        y = _head_norm(hret, retg_ref[:, h * D_V:(h + 1) * D_V])
        gt = proj_ref[:, C_RG + h * D_V:C_RG + (h + 1) * D_V].astype(F32)
        hr_ref[:, h * D_V:(h + 1) * D_V] = (_silu(gt) * y).astype(BF16)
        yield False


def _mix(proj, gates, cos2, sin2, conv_w, conv_b, b_if2, ml_g, ret_g, decay, kdec, qdec, gch, batch, seq):
    t = proj.shape[0]
    nc = seq // CHUNK
    L = CHUNK
    G = MIX_GROUP
    proj = proj.reshape(batch, seq, PROJ_W)
    gates = gates.reshape(batch, seq, GATE_W)
    cos2 = cos2.reshape(batch, seq, LANES)
    sin2 = sin2.reshape(batch, seq, LANES)
    full = lambda shape: pl.BlockSpec(shape, lambda b, c: (0,) * len(shape))
    hm, hr = pl.pallas_call(
        _mix_kernel,
        grid=(batch // G, nc),
        in_specs=[
            pl.BlockSpec((G, L, MIX_W), lambda b, c: (b, c, 0)),
            pl.BlockSpec((G, L, GATE_W), lambda b, c: (b, c, 0)),
            pl.BlockSpec((G, L, LANES), lambda b, c: (b, c, 0)),
            pl.BlockSpec((G, L, LANES), lambda b, c: (b, c, 0)),
            full((CONV_K, 2 * QK_W)),
            full((1, 2 * QK_W)),
            full((1, GATE_W)),
            full((1, V_W)),
            full((1, V_W)),
            full((N_HEADS, L, L)),
            full((N_HEADS, L, 1)),
            full((N_HEADS, L, 1)),
            full((N_HEADS, 1, LANES)),
        ],
        out_specs=[
            pl.BlockSpec((G, L, V_W), lambda b, c: (b, c, 0)),
            pl.BlockSpec((G, L, V_W), lambda b, c: (b, c, 0)),
        ],
        out_shape=[jax.ShapeDtypeStruct((batch, seq, V_W), BF16), jax.ShapeDtypeStruct((batch, seq, V_W), BF16)],
        scratch_shapes=[
            pltpu.VMEM((G, L, 2 * QK_W), BF16),
            pltpu.VMEM((G * N_HEADS, D_QK, V_EXT), F32),
            pltpu.VMEM((G, 1, LANES), F32),
            pltpu.VMEM((G * N_HEADS, D_QK, D_V), F32),
        ],
        compiler_params=pltpu.CompilerParams(
            dimension_semantics=("arbitrary", "arbitrary"), vmem_limit_bytes=VMEM_LIMIT),
        name="mix",
    )(proj, gates, cos2, sin2, conv_w, conv_b, b_if2, ml_g, ret_g, decay, kdec, qdec, gch)
    return hm.reshape(t, V_W), hr.reshape(t, V_W)


def _to_slabs(ref, val):
    rows = val.shape[0]
    for s in range(SLAB):
        ref[pl.ds(s, rows, stride=SLAB), :] = val[:, s * LANES:(s + 1) * LANES]


def _from_slabs(ref, first_slab, rows):
    return jnp.concatenate(
        [ref[pl.ds(first_slab * SLAB + s, rows, stride=SLAB), :] for s in range(SLAB)], axis=1)


def _post_kernel(hm_ref, hr_ref, gm_ref, gr_ref, x_ref, mod_ref, wbm_ref, wbr_ref, wout_ref, g_ref,
                 wr_hi_ref, wr_lo_ref, br_ref, x1_ref, hf_ref, logit_ref):
    n_sub = 2
    sub = x_ref.shape[0] // n_sub

    def steps(s):
        r = pl.ds(s * sub, sub)
        ym = _dot(hm_ref[r, :], wbm_ref[...])
        yield False
        yr = _dot(hr_ref[r, :], wbr_ref[...])
        yield False
        y = _sigmoid_mul(gm_ref[r, :].astype(F32), ym) + _sigmoid_mul(gr_ref[r, :].astype(F32), yr)
        o = _dot(y.astype(BF16), wout_ref[...])
        yield False
        x1 = x_ref[r, :] + mod_ref[0, 2:3, :] * o
        x1_ref[r, :] = x1
        hf = _rms_mod(x1, g_ref[...], mod_ref[0, 4:5, :], mod_ref[0, 3:4, :])
        yield False
        _to_slabs(hf_ref.at[pl.ds(s * sub * SLAB, sub * SLAB)], hf)
        logit_ref[r, :] = _dot3(hf, wr_hi_ref[...], wr_lo_ref[...]) + br_ref[...]
        yield False

    _round_robin([steps(s) for s in range(n_sub)])


def _post(hm, hr, proj, x2, mod6, wbm, wbr, wout, g, wr_hi, wr_lo, br, seq):
    t = x2.shape[0]
    tm = POST_TM
    per_b = seq // tm
    d = D_MODEL
    const = lambda shape: pl.BlockSpec(shape, lambda i: (0,) * len(shape))
    return pl.pallas_call(
        _post_kernel,
        grid=(t // tm,),
        in_specs=[
            pl.BlockSpec((tm, d), lambda i: (i, 0)),
            pl.BlockSpec((tm, d), lambda i: (i, 0)),
            pl.BlockSpec((tm, d), lambda i: (i, C_GML // d)),
            pl.BlockSpec((tm, d), lambda i: (i, C_GRET // d)),
            pl.BlockSpec((tm, d), lambda i: (i, 0)),
            pl.BlockSpec((1, 6, d), lambda i: (i // per_b, 0, 0)),
            const((d, d)), const((d, d)), const((d, d)),
            const((1, d)),
            const((d, LANES)), const((d, LANES)), const((1, LANES)),
        ],
        out_specs=[
            pl.BlockSpec((tm, d), lambda i: (i, 0)),
            pl.BlockSpec((tm * SLAB, LANES), lambda i: (i, 0)),
            pl.BlockSpec((tm, LANES), lambda i: (i, 0)),
        ],
        out_shape=[
            jax.ShapeDtypeStruct((t, d), F32),
            jax.ShapeDtypeStruct((t * SLAB, LANES), F32),
            jax.ShapeDtypeStruct((t, LANES), F32),
        ],
        compiler_params=pltpu.CompilerParams(
            dimension_semantics=("arbitrary",), vmem_limit_bytes=VMEM_LIMIT),
        name="post",
    )(hm, hr, proj, proj, x2, mod6, wbm, wbr, wout, g, wr_hi, wr_lo, br)


def _route_kernel(logit_ref, dest_ref, w_ref, bend_ref, cnt_st, pad_st, sel_st, pick_st, wts_st):
    ph = pl.program_id(0)
    i = pl.program_id(1)
    tm = logit_ref.shape[0]
    rows = pl.ds(pl.multiple_of(i * tm, tm), tm)
    lane = lax.broadcasted_iota(jnp.int32, (tm, LANES), 1)
    lane_f = lane.astype(F32)

    @pl.when(jnp.logical_and(ph == 0, i == 0))
    def _():
        cnt_st[...] = jnp.zeros_like(cnt_st)

    @pl.when(ph == 0)
    def _():
        l = logit_ref[...]
        picks, vals = [], []
        sel = jnp.zeros((tm, LANES), F32)
        for _ in range(TOP_K):
            m = jnp.max(l, axis=-1, keepdims=True)
            idx = jnp.min(jnp.where(l == m, lane_f, float(LANES)), axis=-1, keepdims=True)
            oh = lane_f == idx
            picks.append(idx)
            vals.append(m)
            sel = jnp.where(oh, 1.0, sel)
            l = jnp.where(oh, -jnp.inf, l)
        ex = [jnp.exp(v - vals[0]) for v in vals]
        den = ex[0] + ex[1] + ex[2] + ex[3]
        pick = jnp.zeros((tm, LANES), F32)
        wts = jnp.zeros((tm, LANES), F32)
        for k in range(TOP_K):
            pick = jnp.where(lane == k, picks[k], pick)
            wts = jnp.where(lane == k, ex[k] / den, wts)
        sel_st[rows, :] = sel.astype(BF16)
        pick_st[rows, :] = pick
        wts_st[rows, :] = wts
        cnt_st[...] = cnt_st[...] + jnp.sum(sel, axis=0, keepdims=True)

    @pl.when(jnp.logical_and(ph == 1, i == 0))
    def _():
        blocks = jnp.floor((cnt_st[...] + (EXPERT_BLOCK - 1)) * (1.0 / EXPERT_BLOCK))
        r = lax.broadcasted_iota(jnp.int32, (LANES, LANES), 0)
        c = lax.broadcasted_iota(jnp.int32, (LANES, LANES), 1)
        upper = jnp.where(r < c, 1.0, 0.0).astype(BF16)
        blocks8 = jnp.broadcast_to(blocks, (SUBLANES, LANES))
        excl = _dot(blocks8.astype(BF16), upper)
        pad_st[...] = excl[0:1, :] * EXPERT_BLOCK
        row = lax.broadcasted_iota(jnp.int32, (SUBLANES, LANES), 0)
        bend_ref[...] = jnp.where(row == 0, excl + blocks8,
                                  jnp.where(row == 1, excl * EXPERT_BLOCK + cnt_st[...], 0.0))
        cnt_st[...] = jnp.zeros_like(cnt_st)

    @pl.when(ph == 1)
    def _():
        carry = cnt_st[...]
        sel = sel_st[rows, :]
        pick = pick_st[rows, :]
        r = lax.broadcasted_iota(jnp.int32, (tm, tm), 0)
        c = lax.broadcasted_iota(jnp.int32, (tm, tm), 1)
        lower = jnp.where(r > c, 1.0, 0.0).astype(BF16)
        base = pad_st[...] + carry + _dot(lower, sel)
        dest = jnp.zeros((tm, LANES), F32)
        for k in range(TOP_K):
            dk = jnp.sum(jnp.where(lane_f == pick[:, k:k + 1], base, 0.0), axis=-1, keepdims=True)
            dest = jnp.where(lane == k, dk, dest)
        dest_ref[...] = dest.astype(jnp.int32)
        w_ref[...] = wts_st[rows, :]
        cnt_st[...] = carry + jnp.sum(sel.astype(F32), axis=0, keepdims=True)


def _route(logits):
    t = logits.shape[0]
    tm = ROUTE_TM
    return pl.pallas_call(
        _route_kernel,
        grid=(2, t // tm),
        in_specs=[pl.BlockSpec((tm, LANES), lambda ph, i: (i * (1 - ph), 0))],
        out_specs=[
            pl.BlockSpec((tm, LANES), lambda ph, i: (i * ph, 0)),
            pl.BlockSpec((tm, LANES), lambda ph, i: (i * ph, 0)),
            pl.BlockSpec((SUBLANES, LANES), lambda ph, i: (0, 0)),
        ],
        out_shape=[
            jax.ShapeDtypeStruct((t, LANES), jnp.int32),
            jax.ShapeDtypeStruct((t, LANES), F32),
            jax.ShapeDtypeStruct((SUBLANES, LANES), F32),
        ],
        scratch_shapes=[
            pltpu.VMEM((1, LANES), F32),
            pltpu.VMEM((1, LANES), F32),
            pltpu.VMEM((t, LANES), BF16),
            pltpu.VMEM((t, LANES), F32),
            pltpu.VMEM((t, LANES), F32),
        ],
        compiler_params=pltpu.CompilerParams(
            dimension_semantics=("arbitrary", "arbitrary"), vmem_limit_bytes=VMEM_LIMIT),
        name="route",
    )(logits)


DMA_UNROLL = 32


def _row_copy(src_ref, src_slab, dst_ref, dst_slab, sem):
    src = pl.multiple_of(src_slab * SLAB, SLAB)
    dst = pl.multiple_of(dst_slab * SLAB, SLAB)
    return pltpu.make_async_copy(src_ref.at[pl.ds(src, SLAB), :], dst_ref.at[pl.ds(dst, SLAB), :], sem)


def _dispatch_kernel(dest_ref, padfirst_ref, hf_hbm, xs_out, tiles, zbuf, lsem, rsem, zsem, tsem):
    i = pl.program_id(0)
    n_steps = pl.num_programs(0)
    n_assign = dest_ref.shape[0]
    tile_rows = tiles.shape[1]
    blk_rows = zbuf.shape[0]
    n_buf = tiles.shape[0]

    def tile_load(step, slot):
        start = pl.multiple_of(step * tile_rows, tile_rows)
        return pltpu.make_async_copy(hf_hbm.at[pl.ds(start, tile_rows), :], tiles.at[slot], lsem.at[slot])

    def wait_rows(slot):
        for _ in range(TOP_K):
            pltpu.make_async_copy(tiles.at[slot], xs_out.at[pl.ds(0, tile_rows), :], rsem.at[slot]).wait()

    @pl.when(i == 0)
    def _():
        tile_load(0, 0).start()
        tile_load(1, 1).start()
        zbuf[...] = jnp.zeros_like(zbuf)
        for e in range(N_EXPERTS):
            start = pl.multiple_of(padfirst_ref[e] * SLAB, SLAB)
            pltpu.make_async_copy(zbuf, xs_out.at[pl.ds(start, blk_rows), :], zsem).start()
        for e in range(N_EXPERTS):
            pltpu.make_async_copy(zbuf, xs_out.at[pl.ds(0, blk_rows), :], zsem).wait()

        n_used = padfirst_ref[N_EXPERTS]
        n_total = xs_out.shape[0] // blk_rows

        def zero_start(b, carry):
            start = pl.multiple_of(b * blk_rows, blk_rows)
            pltpu.make_async_copy(zbuf, xs_out.at[pl.ds(start, blk_rows), :], tsem).start()
            return carry

        lax.fori_loop(n_used, n_total, zero_start, 0)

    slot = i % n_buf
    tile_load(i, slot).wait()
    tile = tiles.at[slot]

    def body(g, carry):
        for u in range(DMA_UNROLL):
            t_local = g * (DMA_UNROLL // TOP_K) + u // TOP_K
            _row_copy(tile, t_local, xs_out, dest_ref[g * DMA_UNROLL + u], rsem.at[slot]).start(priority=u % 2)
        return carry

    lax.fori_loop(0, n_assign // DMA_UNROLL, body, 0)

    @pl.when(i >= 1)
    def _():
        wait_rows((i + n_buf - 1) % n_buf)

    @pl.when(i + 2 < n_steps)
    def _():
        tile_load(i + 2, (i + 2) % n_buf).start()

    @pl.when(i == n_steps - 1)
    def _():
        wait_rows(slot)

        def zero_wait(b, carry):
            pltpu.make_async_copy(zbuf, xs_out.at[pl.ds(0, blk_rows), :], tsem).wait()
            return carry

        lax.fori_loop(padfirst_ref[N_EXPERTS], xs_out.shape[0] // blk_rows, zero_wait, 0)


def _dispatch(dest_flat, pad_first, hf_slab, n_blocks):
    n_assign = dest_flat.shape[0]
    tm = DISPATCH_TM
    blk_rows = EXPERT_BLOCK * SLAB
    return pl.pallas_call(
        _dispatch_kernel,
        grid=(n_assign // (tm * TOP_K),),
        in_specs=[
            pl.BlockSpec((tm * TOP_K,), lambda i: (i,), memory_space=pltpu.SMEM),
            pl.BlockSpec(memory_space=pltpu.SMEM),
            pl.BlockSpec(memory_space=pl.ANY),
        ],
        out_specs=pl.BlockSpec(memory_space=pl.ANY),
        out_shape=jax.ShapeDtypeStruct(((n_blocks + 1) * blk_rows, LANES), F32),
        scratch_shapes=[
            pltpu.VMEM((3, tm * SLAB, LANES), F32),
            pltpu.VMEM((blk_rows, LANES), F32),
            pltpu.SemaphoreType.DMA((3,)),
            pltpu.SemaphoreType.DMA((3,)),
            pltpu.SemaphoreType.DMA,
            pltpu.SemaphoreType.DMA,
        ],
        compiler_params=pltpu.CompilerParams(dimension_semantics=("arbitrary",), vmem_limit_bytes=VMEM_LIMIT),
        name="dispatch",
    )(dest_flat, pad_first, hf_slab)


def _weight_copies(wgu_hbm, wd_hbm, wgu_f32, wd_f32, wsem, expert, slot):
    return (pltpu.make_async_copy(wgu_hbm.at[expert], wgu_f32.at[slot], wsem.at[0, slot]),
            pltpu.make_async_copy(wd_hbm.at[expert], wd_f32.at[slot], wsem.at[1, slot]))


def _expert_kernel(be_ref, nused_ref, first_ref, slot_ref, next_ref,
                   xs_ref, wgu_hbm, bgu_ref, wd_hbm, bd_ref, ys_ref,
                   wgu_f32, wd_f32, wgu_bf, wd_bf, wsem):
    b = pl.program_id(0)
    copies = functools.partial(_weight_copies, wgu_hbm, wd_hbm, wgu_f32, wd_f32, wsem)

    @pl.when(jnp.logical_and(b < nused_ref[0], first_ref[b] == 1))
    def _():
        slot = slot_ref[b]

        @pl.when(b == 0)
        def _():
            for cp in copies(be_ref[0], 0):
                cp.start()

        for cp in copies(be_ref[b], slot):
            cp.wait()

        @pl.when(next_ref[b] >= 0)
        def _():
            for cp in copies(next_ref[b], 1 - slot):
                cp.start()

        two = 2 * LANES
        r = lax.broadcasted_iota(jnp.int32, (two, two), 0)
        c = lax.broadcasted_iota(jnp.int32, (two, two), 1)
        src = jnp.where(c < LANES, 2 * c, 2 * (c - LANES) + 1)
        perm = jnp.where(r == src, 1.0, 0.0).astype(BF16)
        for blk in range(wgu_bf.shape[1] // two):
            wb = wgu_f32[slot, :, blk * two:(blk + 1) * two].astype(BF16)
            wgu_bf[:, blk * two:(blk + 1) * two] = _dot(wb, perm).astype(BF16)
        wd_bf[...] = wd_f32[slot].astype(BF16)

    @pl.when(b < nused_ref[0])
    def _():
        two = 2 * LANES
        x = _from_slabs(xs_ref, 0, EXPERT_BLOCK).astype(BF16)
        gu = _dot(x, wgu_bf[...]) + bgu_ref[0]
        n_blk = gu.shape[1] // two
        x_glu = jnp.concatenate([gu[:, i * two:i * two + LANES] for i in range(n_blk)], axis=1)
        x_lin = jnp.concatenate([gu[:, i * two + LANES:(i + 1) * two] for i in range(n_blk)], axis=1)
        x_glu = jnp.minimum(x_glu, SWIGLU_LIMIT)
        x_lin = jnp.clip(x_lin, -SWIGLU_LIMIT, SWIGLU_LIMIT)
        act = _sigmoid_mul(SWIGLU_ALPHA * x_glu, x_glu * (x_lin + 1.0))
        y = _dot(act.astype(BF16), wd_bf[...]) + bd_ref[0]
        _to_slabs(ys_ref, y)

    @pl.when(b >= nused_ref[0])
    def _():
        ys_ref[...] = jnp.zeros_like(ys_ref)


def _experts(block_expert, n_used, xs, w_gate_up, b_gu_perm, w_down, b_down):
    n_blocks = block_expert.shape[0]
    d = D_MODEL
    f2 = w_gate_up.shape[-1]
    blk_rows = EXPERT_BLOCK * SLAB

    idx = jnp.arange(n_blocks, dtype=jnp.int32)
    used = idx < n_used
    prev = jnp.concatenate([jnp.full((1,), -1, jnp.int32), block_expert[:-1]])
    first = (used & (block_expert != prev)).astype(jnp.int32)
    slot = (jnp.cumsum(first) - 1) % 2
    later = used[None, :] & (block_expert[None, :] > block_expert[:, None])
    nxt = jnp.min(jnp.where(later, block_expert[None, :], N_EXPERTS), axis=1)
    nxt = jnp.where(nxt == N_EXPERTS, -1, nxt)

    grid_spec = pltpu.PrefetchScalarGridSpec(
        num_scalar_prefetch=5,
        grid=(n_blocks,),
        in_specs=[
            pl.BlockSpec((blk_rows, LANES), lambda b, be, nu, *_: (jnp.minimum(b, nu[0] - 1), 0)),
            pl.BlockSpec(memory_space=pl.ANY),
            pl.BlockSpec((1, 1, f2), lambda b, be, *_: (be[b], 0, 0)),
            pl.BlockSpec(memory_space=pl.ANY),
            pl.BlockSpec((1, 1, d), lambda b, be, *_: (be[b], 0, 0)),
        ],
        out_specs=pl.BlockSpec((blk_rows, LANES), lambda b, be, *_: (b, 0)),
        scratch_shapes=[
            pltpu.VMEM((2, d, f2), F32),
            pltpu.VMEM((2, f2 // 2, d), F32),
            pltpu.VMEM((d, f2), BF16),
            pltpu.VMEM((f2 // 2, d), BF16),
            pltpu.SemaphoreType.DMA((2, 2)),
        ],
    )
    return pl.pallas_call(
        _expert_kernel,
        grid_spec=grid_spec,
        out_shape=jax.ShapeDtypeStruct((n_blocks * blk_rows, LANES), F32),
        compiler_params=pltpu.CompilerParams(
            dimension_semantics=("arbitrary",), vmem_limit_bytes=VMEM_LIMIT),
        name="expert",
    )(block_expert, n_used, first, slot.astype(jnp.int32), nxt.astype(jnp.int32),
      xs, w_gate_up, b_gu_perm, w_down, b_down)


def _combine_kernel(dest_ref, dest_next_ref, ys_hbm, w_ref, x1_ref, mod_ref, g_ref, o_ref, buf, sem):
    i = pl.program_id(0)
    n_steps = pl.num_programs(0)
    tm = x1_ref.shape[0]
    n_assign = tm * TOP_K
    slot = i % 2

    def gather(d_ref, to_slot):
        def body(g, carry):
            for u in range(DMA_UNROLL):
                t_local = g * (DMA_UNROLL // TOP_K) + u // TOP_K
                k = u % TOP_K
                _row_copy(ys_hbm, d_ref[g * DMA_UNROLL + u], buf.at[to_slot], k * tm + t_local,
                          sem.at[to_slot]).start(priority=u % 2)
            return carry

        lax.fori_loop(0, n_assign // DMA_UNROLL, body, 0)

    @pl.when(i == 0)
    def _():
        gather(dest_ref, 0)

    @pl.when(i + 1 < n_steps)
    def _():
        gather(dest_next_ref, 1 - slot)

    cur = buf.at[slot]
    pltpu.make_async_copy(ys_hbm.at[pl.ds(0, n_assign * SLAB), :], cur, sem.at[slot]).wait()

    w = w_ref[...]
    y = w[:, 0:1] * _from_slabs(cur, 0, tm)
    for k in range(1, TOP_K):
        y = y + w[:, k:k + 1] * _from_slabs(cur, k * tm, tm)
    x = x1_ref[...] + mod_ref[0, 5:6, :] * y
    ms = jnp.mean(x * x, axis=-1, keepdims=True)
    o_ref[...] = x * lax.rsqrt(ms + EPS) * g_ref[...]


def _combine(dest_flat, ys, wts, x1, mod6, g, seq):
    t = x1.shape[0]
    tm = COMBINE_TM
    per_b = seq // tm
    d = D_MODEL
    last = t // tm - 1
    return pl.pallas_call(
        _combine_kernel,
        grid=(t // tm,),
        in_specs=[
            pl.BlockSpec((tm * TOP_K,), lambda i: (i,), memory_space=pltpu.SMEM),
            pl.BlockSpec((tm * TOP_K,), lambda i: (jnp.minimum(i + 1, last),), memory_space=pltpu.SMEM),
            pl.BlockSpec(memory_space=pl.ANY),
            pl.BlockSpec((tm, LANES), lambda i: (i, 0)),
            pl.BlockSpec((tm, d), lambda i: (i, 0)),
            pl.BlockSpec((1, 6, d), lambda i: (i // per_b, 0, 0)),
            pl.BlockSpec((1, d), lambda i: (0, 0)),
        ],
        out_specs=pl.BlockSpec((tm, d), lambda i: (i, 0)),
        out_shape=jax.ShapeDtypeStruct((t, d), F32),
        scratch_shapes=[pltpu.VMEM((2, tm * TOP_K * SLAB, LANES), F32), pltpu.SemaphoreType.DMA((2,))],
        compiler_params=pltpu.CompilerParams(
            dimension_semantics=("arbitrary",), vmem_limit_bytes=VMEM_LIMIT),
        name="combine",
    )(dest_flat, dest_flat, ys, wts, x1, mod6, g)


def kernel(x, c, positions, w_ada, b_ada, norm_mix_g, w_in, conv_w, conv_b, b_if, ml_norm_g, ret_norm_g,
           w_branch_ml, w_branch_ret, w_out, norm_ffn_g, w_router, b_router, w_gate_up, b_gate_up, w_down,
           b_down, norm_final_g):
    batch, seq, d = x.shape
    t = batch * seq
    x2 = x.reshape(t, d)
    depth = w_ada.shape[0]
    c8 = jnp.concatenate([c, jnp.zeros((SUBLANES - batch, d), c.dtype)], axis=0)

    half = D_QK // 2
    inv = ROPE_BASE ** (-jnp.arange(half, dtype=F32) / half)
    inv2 = jnp.concatenate([inv, inv])[None, :]
    sign2 = jnp.concatenate([-jnp.ones((half,), F32), jnp.ones((half,), F32)])[None, :]
    pos_b = jnp.broadcast_to(positions.astype(F32).reshape(t, 1), (t, LANES))
    L = CHUNK
    log_gamma = jnp.log(1.0 - 2.0 ** (-5.0 - jnp.arange(N_HEADS, dtype=F32)))
    pos = jnp.arange(L, dtype=F32)
    rel = pos[:, None] - pos[None, :]
    decay = jnp.where(rel >= 0, jnp.exp(log_gamma[:, None, None] * jnp.maximum(rel, 0.0)), 0.0)
    kdec = jnp.exp(log_gamma[:, None] * (L - 1 - pos))[:, :, None]
    qdec = jnp.exp(log_gamma[:, None] * (pos + 1.0))[:, :, None]
    gch = jnp.broadcast_to(jnp.exp(log_gamma * L)[:, None, None], (N_HEADS, 1, LANES))

    assert depth == 1, "the final norm is fused after the single layer"
    l = 0
    mod = _ada(c8, w_ada[l], b_ada[l][None, :])
    mod6 = mod[:batch].reshape(batch, 6, d)

    w = w_in[l]
    w_main, w_gate = _wprep(w.T)
    w_if = w_gate.T
    zpad = jnp.zeros((d, LANES - N_HEADS), F32)
    wif = jnp.concatenate([w_if[:, :N_HEADS], zpad, w_if[:, N_HEADS:], zpad], axis=1)
    wif_hi, wif_lo = _split(wif)
    zb = jnp.zeros((LANES - N_HEADS,), F32)
    b_if2 = jnp.concatenate([b_if[l][:N_HEADS], zb, b_if[l][N_HEADS:], zb])[None, :]

    proj, gates, cos2, sin2 = _inproj(x2, mod6, norm_mix_g[l][None, :], w_main, wif_hi, wif_lo,
                                      pos_b, inv2, sign2, seq)
    hm, hr = _mix(proj, gates, cos2, sin2, conv_w[l], conv_b[l][None, :], b_if2,
                  ml_norm_g[l][None, :], ret_norm_g[l][None, :], decay, kdec, qdec, gch, batch, seq)

    wr = jnp.concatenate([w_router[l], jnp.zeros((d, LANES - N_EXPERTS), F32)], axis=1)
    wr_hi, wr_lo = _split(wr)
    br = jnp.concatenate([b_router[l], jnp.full((LANES - N_EXPERTS,), NEG_BIG, F32)])[None, :]
    x1, hf, logits = _post(hm, hr, proj, x2, mod6, w_branch_ml[l].astype(BF16), w_branch_ret[l].astype(BF16),
                           w_out[l].astype(BF16), norm_ffn_g[l][None, :], wr_hi, wr_lo, br, seq)

    n_assign = t * TOP_K
    n_blocks = -(-n_assign // EXPERT_BLOCK) + N_EXPERTS
    dest, wts, tables = _route(logits)
    dest_flat = dest[:, :TOP_K].reshape(n_assign)
    bend = tables[0, :N_EXPERTS].astype(jnp.int32)
    pad_first = tables[1, :N_EXPERTS].astype(jnp.int32)
    n_used = bend[N_EXPERTS - 1:]
    blk = jnp.minimum(jnp.arange(n_blocks, dtype=jnp.int32), n_used - 1)
    block_expert = jnp.minimum(
        jnp.sum((bend[None, :] <= blk[:, None]).astype(jnp.int32), axis=1), N_EXPERTS - 1)
    xs = _dispatch(dest_flat, jnp.concatenate([pad_first, n_used]), hf, n_blocks)
    f2 = w_gate_up.shape[-1]
    bgu = b_gate_up[l].reshape(N_EXPERTS, f2 // (2 * LANES), LANES, 2)
    bgu = jnp.swapaxes(bgu, -1, -2).reshape(N_EXPERTS, 1, f2)
    ys = _experts(block_expert, n_used, xs, w_gate_up[l], bgu, w_down[l], b_down[l][:, None, :])
    out = _combine(dest_flat, ys, wts, x1, mod6, norm_final_g[None, :], seq)
    return out.reshape(batch, seq, d)
```

```python
import functools

import jax
import jax.numpy as jnp
from jax import lax
from jax.experimental import pallas as pl
from jax.experimental.pallas import tpu as pltpu

D_MODEL = 1024
N_HEADS = 4
D_QK = 128
D_V = 256
CONV_K = 4
ROPE_BASE = 10000.0
CHUNK = 128
N_EXPERTS = 32
TOP_K = 4
SWIGLU_LIMIT = 7.0
SWIGLU_ALPHA = 1.702
EXPERT_BLOCK = 512
EPS = 1e-5

QK_W = N_HEADS * D_QK
V_W = N_HEADS * D_V
LANES = 128
SUBLANES = 8
GATE_W = 2 * LANES
V_EXT = D_V + LANES
MIX_GROUP = 4
SLAB = D_MODEL // LANES

C_MLQ, C_MLK, C_MLV, C_MLO = 0, 512, 1024, 2048
C_RQ, C_RK, C_RV, C_RG = 3072, 3584, 4096, 5120
C_GML, C_GRET = 6144, 7168
MIX_W = 6144
PROJ_W = 8192

BF16 = jnp.bfloat16
F32 = jnp.float32
NEG_BIG = -1e30

ADA_TN = 1024
WPREP_TN = 1024
INPROJ_TM, INPROJ_TN = 1024, 2048
POST_TM = 512
ROUTE_TM = 512
DISPATCH_TM = 1024
COMBINE_TM = 512
VMEM_LIMIT = 56 * 1024 * 1024


def _dot(a, b):
    return jnp.dot(a, b, preferred_element_type=F32)


def _dot_nt(a, b):
    return lax.dot_general(a, b, (((1,), (1,)), ((), ())), preferred_element_type=F32)


def _split(a):
    hi = a.astype(BF16)
    lo = (a - hi.astype(F32)).astype(BF16)
    return hi, lo


def _dot3(a, b_hi, b_lo):
    a_hi, a_lo = _split(a)
    return _dot(a_hi, b_hi) + (_dot(a_lo, b_hi) + _dot(a_hi, b_lo))


def _sigmoid_mul(x, y):
    hy = 0.5 * y
    return hy * jnp.tanh(0.5 * x) + hy


def _silu(x):
    hx = 0.5 * x
    return hx * jnp.tanh(hx) + hx


def _round_robin(stage_generators):
    pending = list(stage_generators)
    while pending:
        for gen in list(pending):
            if next(gen, True):
                pending.remove(gen)


def _rms_mod(x, g, scale, shift):
    ms = jnp.mean(x * x, axis=-1, keepdims=True)
    return (x * lax.rsqrt(ms + EPS) * g) * (1.0 + scale) + shift


def _ada_kernel(c_ref, w_ref, b_ref, o_ref):
    w_hi, w_lo = _split(w_ref[...])
    o_ref[...] = _dot3(_silu(c_ref[...]), w_hi, w_lo) + b_ref[...]


def _ada(c8, w, b):
    n = w.shape[1]
    tn = ADA_TN
    return pl.pallas_call(
        _ada_kernel,
        grid=(n // tn,),
        in_specs=[
            pl.BlockSpec((SUBLANES, D_MODEL), lambda j: (0, 0)),
            pl.BlockSpec((D_MODEL, tn), lambda j: (0, j)),
            pl.BlockSpec((1, tn), lambda j: (0, j)),
        ],
        out_specs=pl.BlockSpec((SUBLANES, tn), lambda j: (0, j)),
        out_shape=jax.ShapeDtypeStruct((SUBLANES, n), F32),
        name="ada",
    )(c8, w, b)


def _inproj_kernel(x_ref, xn_ref, mod_ref, modn_ref, g_ref, w_ref, wif_hi_ref, wif_lo_ref, pos_ref, inv_ref,
                   sign_ref, proj_ref, gates_ref, cos_ref, sin_ref, hi_a, lo_a, hi_b, lo_b):
    i = pl.program_id(0)
    j = pl.program_id(1)

    def normalise(x, mod):
        return _split(_rms_mod(x, g_ref[...], mod[0, 1:2, :], mod[0, 0:1, :]))

    @pl.when(jnp.logical_and(i == 0, j == 0))
    def _():
        hi_a[...], lo_a[...] = normalise(x_ref[...], mod_ref)

    n = x_ref.shape[0] // pl.num_programs(1)
    rows = pl.ds(pl.multiple_of(j * n, n), n)
    n_split = 4
    tc = w_ref.shape[1] // n_split

    def step(cur_hi, cur_lo, nxt_hi, nxt_lo):
        def matmul_steps():
            for c in range(n_split):
                proj_ref[:, c * tc:(c + 1) * tc] = _dot(cur_hi[...], w_ref[:, c * tc:(c + 1) * tc]).astype(BF16)
                yield False

        def side_steps():
            h_hi = cur_hi[rows, :]
            h_lo = cur_lo[rows, :]
            gates_ref[rows, :] = (_dot(h_hi, wif_hi_ref[...])
                                  + (_dot(h_lo, wif_hi_ref[...]) + _dot(h_hi, wif_lo_ref[...])))
            yield False
            ang = pos_ref[rows, :] * inv_ref[...]
            cos_ref[rows, :] = jnp.cos(ang)
            yield False
            sin_ref[rows, :] = jnp.sin(ang) * sign_ref[...]
            yield False
            nxt_hi[rows, :], nxt_lo[rows, :] = normalise(xn_ref[rows, :], modn_ref)
            yield False

        _round_robin([matmul_steps(), side_steps()])

    @pl.when(i % 2 == 0)
    def _():
        step(hi_a, lo_a, hi_b, lo_b)

    @pl.when(i % 2 == 1)
    def _():
        step(hi_b, lo_b, hi_a, lo_a)


GATE_COL = 3072


def _wprep_kernel(a_ref, b_ref, o_ref, gate_ref):
    c = pl.program_id(0)
    tn = a_ref.shape[0]

    @pl.when(c * tn == GATE_COL)
    def _():
        gate_ref[...] = a_ref[0:2 * N_HEADS, :]

    @pl.when((c + 1) * tn <= GATE_COL)
    def _():
        o_ref[...] = a_ref[...].T.astype(BF16)

    @pl.when((c + 1) * tn > GATE_COL)
    def _():
        shifted = jnp.concatenate([a_ref[2 * N_HEADS:, :], b_ref[...]], axis=0)
        o_ref[...] = shifted.T.astype(BF16)


def _wprep(w_t):
    d = w_t.shape[1]
    tn = WPREP_TN
    gate = 2 * N_HEADS
    assert GATE_COL % tn == 0 and w_t.shape[0] == PROJ_W + gate
    return pl.pallas_call(
        _wprep_kernel,
        grid=(PROJ_W // tn,),
        in_specs=[
            pl.BlockSpec((tn, d), lambda c: (c, 0)),
            pl.BlockSpec((gate, d), lambda c: ((c + 1) * (tn // gate), 0)),
        ],
        out_specs=[pl.BlockSpec((d, tn), lambda c: (0, c)), pl.BlockSpec((gate, d), lambda c: (0, 0))],
        out_shape=[jax.ShapeDtypeStruct((d, PROJ_W), BF16), jax.ShapeDtypeStruct((gate, d), F32)],
        compiler_params=pltpu.CompilerParams(
            dimension_semantics=("arbitrary",), vmem_limit_bytes=VMEM_LIMIT),
        name="wprep",
    )(w_t, w_t)


def _inproj(x2, mod6, g, w_main, wif_hi, wif_lo, pos_b, inv2, sign2, seq):
    t = x2.shape[0]
    tm, tn = INPROJ_TM, INPROJ_TN
    per_b = seq // tm
    last = t // tm - 1
    nxt = lambda i: jnp.minimum(i + 1, last)
    return pl.pallas_call(
        _inproj_kernel,
        grid=(t // tm, PROJ_W // tn),
        in_specs=[
            pl.BlockSpec((tm, D_MODEL), lambda i, j: (i, 0)),
            pl.BlockSpec((tm, D_MODEL), lambda i, j: (nxt(i), 0)),
            pl.BlockSpec((1, 6, D_MODEL), lambda i, j: (i // per_b, 0, 0)),
            pl.BlockSpec((1, 6, D_MODEL), lambda i, j: (nxt(i) // per_b, 0, 0)),
            pl.BlockSpec((1, D_MODEL), lambda i, j: (0, 0)),
            pl.BlockSpec((D_MODEL, tn), lambda i, j: (0, j)),
            pl.BlockSpec((D_MODEL, GATE_W), lambda i, j: (0, 0)),
            pl.BlockSpec((D_MODEL, GATE_W), lambda i, j: (0, 0)),
            pl.BlockSpec((tm, LANES), lambda i, j: (i, 0)),
            pl.BlockSpec((1, LANES), lambda i, j: (0, 0)),
            pl.BlockSpec((1, LANES), lambda i, j: (0, 0)),
        ],
        out_specs=[
            pl.BlockSpec((tm, tn), lambda i, j: (i, j)),
            pl.BlockSpec((tm, GATE_W), lambda i, j: (i, 0)),
            pl.BlockSpec((tm, LANES), lambda i, j: (i, 0)),
            pl.BlockSpec((tm, LANES), lambda i, j: (i, 0)),
        ],
        out_shape=[
            jax.ShapeDtypeStruct((t, PROJ_W), BF16),
            jax.ShapeDtypeStruct((t, GATE_W), F32),
            jax.ShapeDtypeStruct((t, LANES), F32),
            jax.ShapeDtypeStruct((t, LANES), F32),
        ],
        scratch_shapes=[pltpu.VMEM((tm, D_MODEL), BF16)] * 4,
        compiler_params=pltpu.CompilerParams(
            dimension_semantics=("arbitrary", "arbitrary"), vmem_limit_bytes=VMEM_LIMIT),
        name="inproj",
    )(x2, x2, mod6, mod6, g, w_main, wif_hi, wif_lo, pos_b, inv2, sign2)


def _row_mean(a):
    inv_n = jnp.full((a.shape[1], LANES), 1.0 / a.shape[1], BF16)
    m = _dot(a.astype(BF16), inv_n)
    return jnp.concatenate([m] * (a.shape[1] // LANES), axis=1)


def _head_norm(h, g):
    d = h - _row_mean(h)
    return d * lax.rsqrt(_row_mean(d * d) + EPS) * g


def _mix_kernel(proj_ref, gates_ref, cos_ref, sin_ref, convw_ref, convb_ref, bif_ref, mlg_ref, retg_ref,
                decay_ref, kdec_ref, qdec_ref, gch_ref,
                hm_ref, hr_ref,
                prev_ref, c_st, m_st, r_st):
    group = proj_ref.shape[0]

    @pl.when(pl.program_id(1) == 0)
    def _():
        def zero_state(i, carry):
            c_st[i] = jnp.zeros(c_st.shape[1:], F32)
            r_st[i] = jnp.zeros(r_st.shape[1:], F32)
            return carry

        lax.fori_loop(0, group * N_HEADS, zero_state, 0)
        prev_ref[...] = jnp.zeros_like(prev_ref)
        m_st[...] = jnp.zeros_like(m_st)

    ml_steps, ret_steps = [], []
    for gb in range(group):
        states = slice(gb * N_HEADS, (gb + 1) * N_HEADS)
        ml_steps.append(_mlstm_steps(proj_ref.at[gb], gates_ref.at[gb], convw_ref, convb_ref, bif_ref, mlg_ref,
                                     hm_ref.at[gb], prev_ref.at[gb], c_st.at[states], m_st.at[gb]))
        ret_steps.append(_retention_steps(proj_ref.at[gb], cos_ref.at[gb], sin_ref.at[gb], retg_ref, decay_ref,
                                          kdec_ref, qdec_ref, gch_ref, hr_ref.at[gb], r_st.at[states]))
    def delayed(gen, rounds):
        for _ in range(rounds):
            yield False
        yield from gen

    _round_robin(ml_steps + [delayed(g, 4) for g in ret_steps])


def _mlstm_steps(proj_ref, gates_ref, convw_ref, convb_ref, bif_ref, mlg_ref, hm_ref, prev_ref, c_st, m_st):
    L = CHUNK
    rows = lax.broadcasted_iota(jnp.int32, (L, L), 0)
    cols = lax.broadcasted_iota(jnp.int32, (L, L), 1)
    causal = rows >= cols
    tril = jnp.where(causal, 1.0, 0.0).astype(BF16)

    cur = proj_ref[:, C_MLQ:C_MLQ + 2 * QK_W]
    xx = jnp.concatenate([prev_ref[...], cur], axis=0)
    r2 = lax.broadcasted_iota(jnp.int32, (L, 2 * L), 0)
    c2 = lax.broadcasted_iota(jnp.int32, (L, 2 * L), 1)
    acc = convb_ref[...] + cur.astype(F32) * convw_ref[CONV_K - 1:CONV_K, :]
    for d in range(1, CONV_K):
        shift = jnp.where(c2 == r2 + (L - d), 1.0, 0.0).astype(BF16)
        acc = acc + _dot(shift, xx) * convw_ref[CONV_K - 1 - d:CONV_K - d, :]
        yield False
    prev_ref[...] = cur
    qk = _silu(acc)
    yield False

    g = gates_ref[...] + bif_ref[...]
    gi = g[:, :LANES]
    gf = g[:, LANES:]
    lf = jnp.minimum(gf, 0.0) - jnp.log(1.0 + jnp.exp(-jnp.abs(gf)))
    lf_hi, lf_lo = _split(lf)
    a_all = _dot(tril, lf_hi) + _dot(tril, lf_lo)
    yield False
    a_last = a_all[L - 1:L, :]
    bm = gi - a_all
    bm_t = bm.T
    w_state = a_last + bm
    m_loc = jnp.max(w_state, axis=0, keepdims=True)
    m_prev = m_st[...]
    inter_log = a_all + m_prev
    m_new = jnp.maximum(a_last + m_prev, m_loc)
    s_prev = jnp.exp(a_last + m_prev - m_new)
    s_loc = jnp.exp(m_loc - m_new)
    ws_all = jnp.exp(w_state - m_loc) * s_loc
    m_st[...] = m_new
    yield False

    ones_blk = jnp.ones((L, LANES), BF16)
    q_scale = D_QK ** -0.5

    for h in range(N_HEADS):
        q = (qk[:, h * D_QK:(h + 1) * D_QK] * q_scale).astype(BF16)
        k_f = qk[:, QK_W + h * D_QK:QK_W + (h + 1) * D_QK]
        k = k_f.astype(BF16)
        v_ext = jnp.concatenate([proj_ref[:, C_MLV + h * D_V:C_MLV + (h + 1) * D_V], ones_blk], axis=1)
        dlog = jnp.where(causal, a_all[:, h:h + 1] + bm_t[h:h + 1, :], -jnp.inf)
        m_intra = jnp.max(dlog, axis=-1, keepdims=True)
        s = _dot_nt(q, k)
        qc = _dot(q, c_st[h].astype(BF16))
        yield False
        il = inter_log[:, h:h + 1]
        m_t = jnp.maximum(il, m_intra)
        p = jnp.exp(dlog - m_t) * s
        isc = jnp.exp(il - m_t)
        tot = _dot(p.astype(BF16), v_ext) + isc * qc
        yield False
        den = tot[:, D_V:D_V + 1]
        hout = tot[:, :D_V] / jnp.maximum(jnp.abs(den), jnp.exp(-m_t))
        kw_t = (k_f * ws_all[:, h:h + 1]).T.astype(BF16)
        c_st[h] = s_prev[:, h:h + 1] * c_st[h] + _dot(kw_t, v_ext)
        yield False
        y = _head_norm(hout, mlg_ref[:, h * D_V:(h + 1) * D_V])
        o = proj_ref[:, C_MLO + h * D_V:C_MLO + (h + 1) * D_V].astype(F32)
        hm_ref[:, h * D_V:(h + 1) * D_V] = _sigmoid_mul(o, y).astype(BF16)
        yield False


def _retention_steps(proj_ref, cos_ref, sin_ref, retg_ref, decay_ref, kdec_ref, qdec_ref, gch_ref, hr_ref, r_st):
    cos2 = cos_ref[...]
    sin2 = sin_ref[...]
    k_scale = D_QK ** -0.5
    for h in range(N_HEADS):
        q_raw = proj_ref[:, C_RQ + h * D_QK:C_RQ + (h + 1) * D_QK].astype(F32)
        k_raw = proj_ref[:, C_RK + h * D_QK:C_RK + (h + 1) * D_QK].astype(F32)
        q = (q_raw * cos2 + pltpu.roll(q_raw, D_QK // 2, 1) * sin2).astype(BF16)
        k_f = (k_raw * cos2 + pltpu.roll(k_raw, D_QK // 2, 1) * sin2) * k_scale
        v = proj_ref[:, C_RV + h * D_V:C_RV + (h + 1) * D_V]
        yield False
        sc = _dot_nt(q, k_f.astype(BF16)) * decay_ref[h]
        hret = _dot(sc.astype(BF16), v) + _dot(q, r_st[h].astype(BF16)) * qdec_ref[h]
        yield False
        kd_t = (k_f * kdec_ref[h]).T.astype(BF16)
        r_st[h] = gch_ref[h][:, 0:1] * r_st[h] + _dot(kd_t, v)
        yield False
        y = _head_norm(hret, retg_ref[:, h * D_V:(h + 1) * D_V])
        gt = proj_ref[:, C_RG + h * D_V:C_RG + (h + 1) * D_V].astype(F32)
        hr_ref[:, h * D_V:(h + 1) * D_V] = (_silu(gt) * y).astype(BF16)
        yield False


def _mix(proj, gates, cos2, sin2, conv_w, conv_b, b_if2, ml_g, ret_g, decay, kdec, qdec, gch, batch, seq):
    t = proj.shape[0]
    nc = seq // CHUNK
    L = CHUNK
    G = MIX_GROUP
    proj = proj.reshape(batch, seq, PROJ_W)
    gates = gates.reshape(batch, seq, GATE_W)
    cos2 = cos2.reshape(batch, seq, LANES)
    sin2 = sin2.reshape(batch, seq, LANES)
    full = lambda shape: pl.BlockSpec(shape, lambda b, c: (0,) * len(shape))
    hm, hr = pl.pallas_call(
        _mix_kernel,
        grid=(batch // G, nc),
        in_specs=[
            pl.BlockSpec((G, L, MIX_W), lambda b, c: (b, c, 0)),
            pl.BlockSpec((G, L, GATE_W), lambda b, c: (b, c, 0)),
            pl.BlockSpec((G, L, LANES), lambda b, c: (b, c, 0)),
            pl.BlockSpec((G, L, LANES), lambda b, c: (b, c, 0)),
            full((CONV_K, 2 * QK_W)),
            full((1, 2 * QK_W)),
            full((1, GATE_W)),
            full((1, V_W)),
            full((1, V_W)),
            full((N_HEADS, L, L)),
            full((N_HEADS, L, 1)),
            full((N_HEADS, L, 1)),
            full((N_HEADS, 1, LANES)),
        ],
        out_specs=[
            pl.BlockSpec((G, L, V_W), lambda b, c: (b, c, 0)),
            pl.BlockSpec((G, L, V_W), lambda b, c: (b, c, 0)),
        ],
        out_shape=[jax.ShapeDtypeStruct((batch, seq, V_W), BF16), jax.ShapeDtypeStruct((batch, seq, V_W), BF16)],
        scratch_shapes=[
            pltpu.VMEM((G, L, 2 * QK_W), BF16),
            pltpu.VMEM((G * N_HEADS, D_QK, V_EXT), F32),
            pltpu.VMEM((G, 1, LANES), F32),
            pltpu.VMEM((G * N_HEADS, D_QK, D_V), F32),
        ],
        compiler_params=pltpu.CompilerParams(
            dimension_semantics=("arbitrary", "arbitrary"), vmem_limit_bytes=VMEM_LIMIT),
        name="mix",
    )(proj, gates, cos2, sin2, conv_w, conv_b, b_if2, ml_g, ret_g, decay, kdec, qdec, gch)
    return hm.reshape(t, V_W), hr.reshape(t, V_W)


def _to_slabs(ref, val):
    rows = val.shape[0]
    for s in range(SLAB):
        ref[pl.ds(s, rows, stride=SLAB), :] = val[:, s * LANES:(s + 1) * LANES]


def _from_slabs(ref, first_slab, rows):
    return jnp.concatenate(
        [ref[pl.ds(first_slab * SLAB + s, rows, stride=SLAB), :] for s in range(SLAB)], axis=1)


def _post_kernel(hm_ref, hr_ref, gm_ref, gr_ref, x_ref, mod_ref, wbm_ref, wbr_ref, wout_ref, g_ref,
                 wr_hi_ref, wr_lo_ref, br_ref, x1_ref, hf_ref, logit_ref):
    n_sub = 2
    sub = x_ref.shape[0] // n_sub

    def steps(s):
        r = pl.ds(s * sub, sub)
        ym = _dot(hm_ref[r, :], wbm_ref[...])
        yield False
        yr = _dot(hr_ref[r, :], wbr_ref[...])
        yield False
        y = _sigmoid_mul(gm_ref[r, :].astype(F32), ym) + _sigmoid_mul(gr_ref[r, :].astype(F32), yr)
        o = _dot(y.astype(BF16), wout_ref[...])
        yield False
        x1 = x_ref[r, :] + mod_ref[0, 2:3, :] * o
        x1_ref[r, :] = x1
        hf = _rms_mod(x1, g_ref[...], mod_ref[0, 4:5, :], mod_ref[0, 3:4, :])
        yield False
        _to_slabs(hf_ref.at[pl.ds(s * sub * SLAB, sub * SLAB)], hf)
        logit_ref[r, :] = _dot3(hf, wr_hi_ref[...], wr_lo_ref[...]) + br_ref[...]
        yield False

    _round_robin([steps(s) for s in range(n_sub)])


def _post(hm, hr, proj, x2, mod6, wbm, wbr, wout, g, wr_hi, wr_lo, br, seq):
    t = x2.shape[0]
    tm = POST_TM
    per_b = seq // tm
    d = D_MODEL
    const = lambda shape: pl.BlockSpec(shape, lambda i: (0,) * len(shape))
    return pl.pallas_call(
        _post_kernel,
        grid=(t // tm,),
        in_specs=[
            pl.BlockSpec((tm, d), lambda i: (i, 0)),
            pl.BlockSpec((tm, d), lambda i: (i, 0)),
            pl.BlockSpec((tm, d), lambda i: (i, C_GML // d)),
            pl.BlockSpec((tm, d), lambda i: (i, C_GRET // d)),
            pl.BlockSpec((tm, d), lambda i: (i, 0)),
            pl.BlockSpec((1, 6, d), lambda i: (i // per_b, 0, 0)),
            const((d, d)), const((d, d)), const((d, d)),
            const((1, d)),
            const((d, LANES)), const((d, LANES)), const((1, LANES)),
        ],
        out_specs=[
            pl.BlockSpec((tm, d), lambda i: (i, 0)),
            pl.BlockSpec((tm * SLAB, LANES), lambda i: (i, 0)),
            pl.BlockSpec((tm, LANES), lambda i: (i, 0)),
        ],
        out_shape=[
            jax.ShapeDtypeStruct((t, d), F32),
            jax.ShapeDtypeStruct((t * SLAB, LANES), F32),
            jax.ShapeDtypeStruct((t, LANES), F32),
        ],
        compiler_params=pltpu.CompilerParams(
            dimension_semantics=("arbitrary",), vmem_limit_bytes=VMEM_LIMIT),
        name="post",
    )(hm, hr, proj, proj, x2, mod6, wbm, wbr, wout, g, wr_hi, wr_lo, br)


def _route_kernel(logit_ref, dest_ref, w_ref, bend_ref, cnt_st, pad_st, sel_st, pick_st, wts_st):
    ph = pl.program_id(0)
    i = pl.program_id(1)
    tm = logit_ref.shape[0]
    rows = pl.ds(pl.multiple_of(i * tm, tm), tm)
    lane = lax.broadcasted_iota(jnp.int32, (tm, LANES), 1)
    lane_f = lane.astype(F32)

    @pl.when(jnp.logical_and(ph == 0, i == 0))
    def _():
        cnt_st[...] = jnp.zeros_like(cnt_st)

    @pl.when(ph == 0)
    def _():
        l = logit_ref[...]
        picks, vals = [], []
        sel = jnp.zeros((tm, LANES), F32)
        for _ in range(TOP_K):
            m = jnp.max(l, axis=-1, keepdims=True)
            idx = jnp.min(jnp.where(l == m, lane_f, float(LANES)), axis=-1, keepdims=True)
            oh = lane_f == idx
            picks.append(idx)
            vals.append(m)
            sel = jnp.where(oh, 1.0, sel)
            l = jnp.where(oh, -jnp.inf, l)
        ex = [jnp.exp(v - vals[0]) for v in vals]
        den = ex[0] + ex[1] + ex[2] + ex[3]
        pick = jnp.zeros((tm, LANES), F32)
        wts = jnp.zeros((tm, LANES), F32)
        for k in range(TOP_K):
            pick = jnp.where(lane == k, picks[k], pick)
            wts = jnp.where(lane == k, ex[k] / den, wts)
        sel_st[rows, :] = sel.astype(BF16)
        pick_st[rows, :] = pick
        wts_st[rows, :] = wts
        cnt_st[...] = cnt_st[...] + jnp.sum(sel, axis=0, keepdims=True)

    @pl.when(jnp.logical_and(ph == 1, i == 0))
    def _():
        blocks = jnp.floor((cnt_st[...] + (EXPERT_BLOCK - 1)) * (1.0 / EXPERT_BLOCK))
        r = lax.broadcasted_iota(jnp.int32, (LANES, LANES), 0)
        c = lax.broadcasted_iota(jnp.int32, (LANES, LANES), 1)
        upper = jnp.where(r < c, 1.0, 0.0).astype(BF16)
        blocks8 = jnp.broadcast_to(blocks, (SUBLANES, LANES))
        excl = _dot(blocks8.astype(BF16), upper)
        pad_st[...] = excl[0:1, :] * EXPERT_BLOCK
        row = lax.broadcasted_iota(jnp.int32, (SUBLANES, LANES), 0)
        bend_ref[...] = jnp.where(row == 0, excl + blocks8,
                                  jnp.where(row == 1, excl * EXPERT_BLOCK + cnt_st[...], 0.0))
        cnt_st[...] = jnp.zeros_like(cnt_st)

    @pl.when(ph == 1)
    def _():
        carry = cnt_st[...]
        sel = sel_st[rows, :]
        pick = pick_st[rows, :]
        r = lax.broadcasted_iota(jnp.int32, (tm, tm), 0)
        c = lax.broadcasted_iota(jnp.int32, (tm, tm), 1)
        lower = jnp.where(r > c, 1.0, 0.0).astype(BF16)
        base = pad_st[...] + carry + _dot(lower, sel)
        dest = jnp.zeros((tm, LANES), F32)
        for k in range(TOP_K):
            dk = jnp.sum(jnp.where(lane_f == pick[:, k:k + 1], base, 0.0), axis=-1, keepdims=True)
            dest = jnp.where(lane == k, dk, dest)
        dest_ref[...] = dest.astype(jnp.int32)
        w_ref[...] = wts_st[rows, :]
        cnt_st[...] = carry + jnp.sum(sel.astype(F32), axis=0, keepdims=True)


def _route(logits):
    t = logits.shape[0]
    tm = ROUTE_TM
    return pl.pallas_call(
        _route_kernel,
        grid=(2, t // tm),
        in_specs=[pl.BlockSpec((tm, LANES), lambda ph, i: (i * (1 - ph), 0))],
        out_specs=[
            pl.BlockSpec((tm, LANES), lambda ph, i: (i * ph, 0)),
            pl.BlockSpec((tm, LANES), lambda ph, i: (i * ph, 0)),
            pl.BlockSpec((SUBLANES, LANES), lambda ph, i: (0, 0)),
        ],
        out_shape=[
            jax.ShapeDtypeStruct((t, LANES), jnp.int32),
            jax.ShapeDtypeStruct((t, LANES), F32),
            jax.ShapeDtypeStruct((SUBLANES, LANES), F32),
        ],
        scratch_shapes=[
            pltpu.VMEM((1, LANES), F32),
            pltpu.VMEM((1, LANES), F32),
            pltpu.VMEM((t, LANES), BF16),
            pltpu.VMEM((t, LANES), F32),
            pltpu.VMEM((t, LANES), F32),
        ],
        compiler_params=pltpu.CompilerParams(
            dimension_semantics=("arbitrary", "arbitrary"), vmem_limit_bytes=VMEM_LIMIT),
        name="route",
    )(logits)


DMA_UNROLL = 32


def _row_copy(src_ref, src_slab, dst_ref, dst_slab, sem):
    src = pl.multiple_of(src_slab * SLAB, SLAB)
    dst = pl.multiple_of(dst_slab * SLAB, SLAB)
    return pltpu.make_async_copy(src_ref.at[pl.ds(src, SLAB), :], dst_ref.at[pl.ds(dst, SLAB), :], sem)


def _dispatch_kernel(dest_ref, padfirst_ref, hf_hbm, xs_out, tiles, zbuf, lsem, rsem, zsem, tsem):
    i = pl.program_id(0)
    n_steps = pl.num_programs(0)
    n_assign = dest_ref.shape[0]
    tile_rows = tiles.shape[1]
    blk_rows = zbuf.shape[0]
    n_buf = tiles.shape[0]

    def tile_load(step, slot):
        start = pl.multiple_of(step * tile_rows, tile_rows)
        return pltpu.make_async_copy(hf_hbm.at[pl.ds(start, tile_rows), :], tiles.at[slot], lsem.at[slot])

    def wait_rows(slot):
        for _ in range(TOP_K):
            pltpu.make_async_copy(tiles.at[slot], xs_out.at[pl.ds(0, tile_rows), :], rsem.at[slot]).wait()

    @pl.when(i == 0)
    def _():
        tile_load(0, 0).start()
        tile_load(1, 1).start()
        zbuf[...] = jnp.zeros_like(zbuf)
        for e in range(N_EXPERTS):
            start = pl.multiple_of(padfirst_ref[e] * SLAB, SLAB)
            pltpu.make_async_copy(zbuf, xs_out.at[pl.ds(start, blk_rows), :], zsem).start()
        for e in range(N_EXPERTS):
            pltpu.make_async_copy(zbuf, xs_out.at[pl.ds(0, blk_rows), :], zsem).wait()

        n_used = padfirst_ref[N_EXPERTS]
        n_total = xs_out.shape[0] // blk_rows

        def zero_start(b, carry):
            start = pl.multiple_of(b * blk_rows, blk_rows)
            pltpu.make_async_copy(zbuf, xs_out.at[pl.ds(start, blk_rows), :], tsem).start()
            return carry

        lax.fori_loop(n_used, n_total, zero_start, 0)

    slot = i % n_buf
    tile_load(i, slot).wait()
    tile = tiles.at[slot]

    def body(g, carry):
        for u in range(DMA_UNROLL):
            t_local = g * (DMA_UNROLL // TOP_K) + u // TOP_K
            _row_copy(tile, t_local, xs_out, dest_ref[g * DMA_UNROLL + u], rsem.at[slot]).start(priority=u % 2)
        return carry

    lax.fori_loop(0, n_assign // DMA_UNROLL, body, 0)

    @pl.when(i >= 1)
    def _():
        wait_rows((i + n_buf - 1) % n_buf)

    @pl.when(i + 2 < n_steps)
    def _():
        tile_load(i + 2, (i + 2) % n_buf).start()

    @pl.when(i == n_steps - 1)
    def _():
        wait_rows(slot)

        def zero_wait(b, carry):
            pltpu.make_async_copy(zbuf, xs_out.at[pl.ds(0, blk_rows), :], tsem).wait()
            return carry

        lax.fori_loop(padfirst_ref[N_EXPERTS], xs_out.shape[0] // blk_rows, zero_wait, 0)


def _dispatch(dest_flat, pad_first, hf_slab, n_blocks):
    n_assign = dest_flat.shape[0]
    tm = DISPATCH_TM
    blk_rows = EXPERT_BLOCK * SLAB
    return pl.pallas_call(
        _dispatch_kernel,
        grid=(n_assign // (tm * TOP_K),),
        in_specs=[
            pl.BlockSpec((tm * TOP_K,), lambda i: (i,), memory_space=pltpu.SMEM),
            pl.BlockSpec(memory_space=pltpu.SMEM),
            pl.BlockSpec(memory_space=pl.ANY),
        ],
        out_specs=pl.BlockSpec(memory_space=pl.ANY),
        out_shape=jax.ShapeDtypeStruct(((n_blocks + 1) * blk_rows, LANES), F32),
        scratch_shapes=[
            pltpu.VMEM((3, tm * SLAB, LANES), F32),
            pltpu.VMEM((blk_rows, LANES), F32),
            pltpu.SemaphoreType.DMA((3,)),
            pltpu.SemaphoreType.DMA((3,)),
            pltpu.SemaphoreType.DMA,
            pltpu.SemaphoreType.DMA,
        ],
        compiler_params=pltpu.CompilerParams(dimension_semantics=("arbitrary",), vmem_limit_bytes=VMEM_LIMIT),
        name="dispatch",
    )(dest_flat, pad_first, hf_slab)


def _weight_copies(wgu_hbm, wd_hbm, wgu_f32, wd_f32, wsem, expert, slot):
    return (pltpu.make_async_copy(wgu_hbm.at[expert], wgu_f32.at[slot], wsem.at[0, slot]),
            pltpu.make_async_copy(wd_hbm.at[expert], wd_f32.at[slot], wsem.at[1, slot]))


def _expert_kernel(be_ref, nused_ref, first_ref, slot_ref, next_ref,
                   xs_ref, wgu_hbm, bgu_ref, wd_hbm, bd_ref, ys_ref,
                   wgu_f32, wd_f32, wgu_bf, wd_bf, wsem):
    b = pl.program_id(0)
    copies = functools.partial(_weight_copies, wgu_hbm, wd_hbm, wgu_f32, wd_f32, wsem)

    @pl.when(jnp.logical_and(b < nused_ref[0], first_ref[b] == 1))
    def _():
        slot = slot_ref[b]

        @pl.when(b == 0)
        def _():
            for cp in copies(be_ref[0], 0):
                cp.start()

        for cp in copies(be_ref[b], slot):
            cp.wait()

        @pl.when(next_ref[b] >= 0)
        def _():
            for cp in copies(next_ref[b], 1 - slot):
                cp.start()

        two = 2 * LANES
        r = lax.broadcasted_iota(jnp.int32, (two, two), 0)
        c = lax.broadcasted_iota(jnp.int32, (two, two), 1)
        src = jnp.where(c < LANES, 2 * c, 2 * (c - LANES) + 1)
        perm = jnp.where(r == src, 1.0, 0.0).astype(BF16)
        for blk in range(wgu_bf.shape[1] // two):
            wb = wgu_f32[slot, :, blk * two:(blk + 1) * two].astype(BF16)
            wgu_bf[:, blk * two:(blk + 1) * two] = _dot(wb, perm).astype(BF16)
        wd_bf[...] = wd_f32[slot].astype(BF16)

    @pl.when(b < nused_ref[0])
    def _():
        two = 2 * LANES
        x = _from_slabs(xs_ref, 0, EXPERT_BLOCK).astype(BF16)
        gu = _dot(x, wgu_bf[...]) + bgu_ref[0]
        n_blk = gu.shape[1] // two
        x_glu = jnp.concatenate([gu[:, i * two:i * two + LANES] for i in range(n_blk)], axis=1)
        x_lin = jnp.concatenate([gu[:, i * two + LANES:(i + 1) * two] for i in range(n_blk)], axis=1)
        x_glu = jnp.minimum(x_glu, SWIGLU_LIMIT)
        x_lin = jnp.clip(x_lin, -SWIGLU_LIMIT, SWIGLU_LIMIT)
        act = _sigmoid_mul(SWIGLU_ALPHA * x_glu, x_glu * (x_lin + 1.0))
        y = _dot(act.astype(BF16), wd_bf[...]) + bd_ref[0]
        _to_slabs(ys_ref, y)

    @pl.when(b >= nused_ref[0])
    def _():
        ys_ref[...] = jnp.zeros_like(ys_ref)


def _experts(block_expert, n_used, xs, w_gate_up, b_gu_perm, w_down, b_down):
    n_blocks = block_expert.shape[0]
    d = D_MODEL
    f2 = w_gate_up.shape[-1]
    blk_rows = EXPERT_BLOCK * SLAB

    idx = jnp.arange(n_blocks, dtype=jnp.int32)
    used = idx < n_used
    prev = jnp.concatenate([jnp.full((1,), -1, jnp.int32), block_expert[:-1]])
    first = (used & (block_expert != prev)).astype(jnp.int32)
    slot = (jnp.cumsum(first) - 1) % 2
    later = used[None, :] & (block_expert[None, :] > block_expert[:, None])
    nxt = jnp.min(jnp.where(later, block_expert[None, :], N_EXPERTS), axis=1)
    nxt = jnp.where(nxt == N_EXPERTS, -1, nxt)

    grid_spec = pltpu.PrefetchScalarGridSpec(
        num_scalar_prefetch=5,
        grid=(n_blocks,),
        in_specs=[
            pl.BlockSpec((blk_rows, LANES), lambda b, be, nu, *_: (jnp.minimum(b, nu[0] - 1), 0)),
            pl.BlockSpec(memory_space=pl.ANY),
            pl.BlockSpec((1, 1, f2), lambda b, be, *_: (be[b], 0, 0)),
            pl.BlockSpec(memory_space=pl.ANY),
            pl.BlockSpec((1, 1, d), lambda b, be, *_: (be[b], 0, 0)),
        ],
        out_specs=pl.BlockSpec((blk_rows, LANES), lambda b, be, *_: (b, 0)),
        scratch_shapes=[
            pltpu.VMEM((2, d, f2), F32),
            pltpu.VMEM((2, f2 // 2, d), F32),
            pltpu.VMEM((d, f2), BF16),
            pltpu.VMEM((f2 // 2, d), BF16),
            pltpu.SemaphoreType.DMA((2, 2)),
        ],
    )
    return pl.pallas_call(
        _expert_kernel,
        grid_spec=grid_spec,
        out_shape=jax.ShapeDtypeStruct((n_blocks * blk_rows, LANES), F32),
        compiler_params=pltpu.CompilerParams(
            dimension_semantics=("arbitrary",), vmem_limit_bytes=VMEM_LIMIT),
        name="expert",
    )(block_expert, n_used, first, slot.astype(jnp.int32), nxt.astype(jnp.int32),
      xs, w_gate_up, b_gu_perm, w_down, b_down)


def _combine_kernel(dest_ref, dest_next_ref, ys_hbm, w_ref, x1_ref, mod_ref, g_ref, o_ref, buf, sem):
    i = pl.program_id(0)
    n_steps = pl.num_programs(0)
    tm = x1_ref.shape[0]
    n_assign = tm * TOP_K
    slot = i % 2

    def gather(d_ref, to_slot):
        def body(g, carry):
            for u in range(DMA_UNROLL):
                t_local = g * (DMA_UNROLL // TOP_K) + u // TOP_K
                k = u % TOP_K
                _row_copy(ys_hbm, d_ref[g * DMA_UNROLL + u], buf.at[to_slot], k * tm + t_local,
                          sem.at[to_slot]).start(priority=u % 2)
            return carry

        lax.fori_loop(0, n_assign // DMA_UNROLL, body, 0)

    @pl.when(i == 0)
    def _():
        gather(dest_ref, 0)

    @pl.when(i + 1 < n_steps)
    def _():
        gather(dest_next_ref, 1 - slot)

    cur = buf.at[slot]
    pltpu.make_async_copy(ys_hbm.at[pl.ds(0, n_assign * SLAB), :], cur, sem.at[slot]).wait()

    w = w_ref[...]
    y = w[:, 0:1] * _from_slabs(cur, 0, tm)
    for k in range(1, TOP_K):
        y = y + w[:, k:k + 1] * _from_slabs(cur, k * tm, tm)
    x = x1_ref[...] + mod_ref[0, 5:6, :] * y
    ms = jnp.mean(x * x, axis=-1, keepdims=True)
    o_ref[...] = x * lax.rsqrt(ms + EPS) * g_ref[...]


def _combine(dest_flat, ys, wts, x1, mod6, g, seq):
    t = x1.shape[0]
    tm = COMBINE_TM
    per_b = seq // tm
    d = D_MODEL
    last = t // tm - 1
    return pl.pallas_call(
        _combine_kernel,
        grid=(t // tm,),
        in_specs=[
            pl.BlockSpec((tm * TOP_K,), lambda i: (i,), memory_space=pltpu.SMEM),
            pl.BlockSpec((tm * TOP_K,), lambda i: (jnp.minimum(i + 1, last),), memory_space=pltpu.SMEM),
            pl.BlockSpec(memory_space=pl.ANY),
            pl.BlockSpec((tm, LANES), lambda i: (i, 0)),
            pl.BlockSpec((tm, d), lambda i: (i, 0)),
            pl.BlockSpec((1, 6, d), lambda i: (i // per_b, 0, 0)),
            pl.BlockSpec((1, d), lambda i: (0, 0)),
        ],
        out_specs=pl.BlockSpec((tm, d), lambda i: (i, 0)),
        out_shape=jax.ShapeDtypeStruct((t, d), F32),
        scratch_shapes=[pltpu.VMEM((2, tm * TOP_K * SLAB, LANES), F32), pltpu.SemaphoreType.DMA((2,))],
        compiler_params=pltpu.CompilerParams(
            dimension_semantics=("arbitrary",), vmem_limit_bytes=VMEM_LIMIT),
        name="combine",
    )(dest_flat, dest_flat, ys, wts, x1, mod6, g)


def kernel(x, c, positions, w_ada, b_ada, norm_mix_g, w_in, conv_w, conv_b, b_if, ml_norm_g, ret_norm_g,
           w_branch_ml, w_branch_ret, w_out, norm_ffn_g, w_router, b_router, w_gate_up, b_gate_up, w_down,
           b_down, norm_final_g):
    batch, seq, d = x.shape
    t = batch * seq
    x2 = x.reshape(t, d)
    depth = w_ada.shape[0]
    c8 = jnp.concatenate([c, jnp.zeros((SUBLANES - batch, d), c.dtype)], axis=0)

    half = D_QK // 2
    inv = ROPE_BASE ** (-jnp.arange(half, dtype=F32) / half)
    inv2 = jnp.concatenate([inv, inv])[None, :]
    sign2 = jnp.concatenate([-jnp.ones((half,), F32), jnp.ones((half,), F32)])[None, :]
    pos_b = jnp.broadcast_to(positions.astype(F32).reshape(t, 1), (t, LANES))
    L = CHUNK
    log_gamma = jnp.log(1.0 - 2.0 ** (-5.0 - jnp.arange(N_HEADS, dtype=F32)))
    pos = jnp.arange(L, dtype=F32)
    rel = pos[:, None] - pos[None, :]
    decay = jnp.where(rel >= 0, jnp.exp(log_gamma[:, None, None] * jnp.maximum(rel, 0.0)), 0.0)
    kdec = jnp.exp(log_gamma[:, None] * (L - 1 - pos))[:, :, None]
    qdec = jnp.exp(log_gamma[:, None] * (pos + 1.0))[:, :, None]
    gch = jnp.broadcast_to(jnp.exp(log_gamma * L)[:, None, None], (N_HEADS, 1, LANES))

    assert depth == 1, "the final norm is fused after the single layer"
    l = 0
    mod = _ada(c8, w_ada[l], b_ada[l][None, :])
    mod6 = mod[:batch].reshape(batch, 6, d)

    w = w_in[l]
    w_main, w_gate = _wprep(w.T)
    w_if = w_gate.T
    zpad = jnp.zeros((d, LANES - N_HEADS), F32)
    wif = jnp.concatenate([w_if[:, :N_HEADS], zpad, w_if[:, N_HEADS:], zpad], axis=1)
    wif_hi, wif_lo = _split(wif)
    zb = jnp.zeros((LANES - N_HEADS,), F32)
    b_if2 = jnp.concatenate([b_if[l][:N_HEADS], zb, b_if[l][N_HEADS:], zb])[None, :]

    proj, gates, cos2, sin2 = _inproj(x2, mod6, norm_mix_g[l][None, :], w_main, wif_hi, wif_lo,
                                      pos_b, inv2, sign2, seq)
    hm, hr = _mix(proj, gates, cos2, sin2, conv_w[l], conv_b[l][None, :], b_if2,
                  ml_norm_g[l][None, :], ret_norm_g[l][None, :], decay, kdec, qdec, gch, batch, seq)

    wr = jnp.concatenate([w_router[l], jnp.zeros((d, LANES - N_EXPERTS), F32)], axis=1)
    wr_hi, wr_lo = _split(wr)
    br = jnp.concatenate([b_router[l], jnp.full((LANES - N_EXPERTS,), NEG_BIG, F32)])[None, :]
    x1, hf, logits = _post(hm, hr, proj, x2, mod6, w_branch_ml[l].astype(BF16), w_branch_ret[l].astype(BF16),
                           w_out[l].astype(BF16), norm_ffn_g[l][None, :], wr_hi, wr_lo, br, seq)

    n_assign = t * TOP_K
    n_blocks = -(-n_assign // EXPERT_BLOCK) + N_EXPERTS
    dest, wts, tables = _route(logits)
    dest_flat = dest[:, :TOP_K].reshape(n_assign)
    bend = tables[0, :N_EXPERTS].astype(jnp.int32)
    pad_first = tables[1, :N_EXPERTS].astype(jnp.int32)
    n_used = bend[N_EXPERTS - 1:]
    blk = jnp.minimum(jnp.arange(n_blocks, dtype=jnp.int32), n_used - 1)
    block_expert = jnp.minimum(
        jnp.sum((bend[None, :] <= blk[:, None]).astype(jnp.int32), axis=1), N_EXPERTS - 1)
    xs = _dispatch(dest_flat, jnp.concatenate([pad_first, n_used]), hf, n_blocks)
    f2 = w_gate_up.shape[-1]
    bgu = b_gate_up[l].reshape(N_EXPERTS, f2 // (2 * LANES), LANES, 2)
    bgu = jnp.swapaxes(bgu, -1, -2).reshape(N_EXPERTS, 1, f2)
    ys = _experts(block_expert, n_used, xs, w_gate_up[l], bgu, w_down[l], b_down[l][:, None, :])
    out = _combine(dest_flat, ys, wts, x1, mod6, norm_final_g[None, :], seq)
    return out.reshape(batch, seq, d)
```

```python
import functools

import jax
import jax.numpy as jnp
from jax import lax
from jax.experimental import pallas as pl
from jax.experimental.pallas import tpu as pltpu

D_MODEL = 1024
N_HEADS = 4
D_QK = 128
D_V = 256
CONV_K = 4
ROPE_BASE = 10000.0
CHUNK = 128
N_EXPERTS = 32
TOP_K = 4
SWIGLU_LIMIT = 7.0
SWIGLU_ALPHA = 1.702
EXPERT_BLOCK = 512
EPS = 1e-5

QK_W = N_HEADS * D_QK
V_W = N_HEADS * D_V
LANES = 128
SUBLANES = 8
GATE_W = 2 * LANES
V_EXT = D_V + LANES
MIX_GROUP = 4
SLAB = D_MODEL // LANES

C_MLQ, C_MLK, C_MLV, C_MLO = 0, 512, 1024, 2048
C_RQ, C_RK, C_RV, C_RG = 3072, 3584, 4096, 5120
C_GML, C_GRET = 6144, 7168
MIX_W = 6144
PROJ_W = 8192

BF16 = jnp.bfloat16
F32 = jnp.float32
NEG_BIG = -1e30

ADA_TN = 1024
WPREP_TN = 1024
INPROJ_TM, INPROJ_TN = 1024, 2048
POST_TM = 512
ROUTE_TM = 1024
DISPATCH_TM = 512
COMBINE_TM = 256
VMEM_LIMIT = 56 * 1024 * 1024


def _dot(a, b):
    return jnp.dot(a, b, preferred_element_type=F32)


def _dot_nt(a, b):
    return lax.dot_general(a, b, (((1,), (1,)), ((), ())), preferred_element_type=F32)


def _split(a):
    hi = a.astype(BF16)
    lo = (a - hi.astype(F32)).astype(BF16)
    return hi, lo


def _dot3(a, b_hi, b_lo):
    a_hi, a_lo = _split(a)
    return _dot(a_hi, b_hi) + (_dot(a_lo, b_hi) + _dot(a_hi, b_lo))


def _sigmoid_mul(x, y):
    hy = 0.5 * y
    return hy * jnp.tanh(0.5 * x) + hy


def _silu(x):
    hx = 0.5 * x
    return hx * jnp.tanh(hx) + hx


def _round_robin(stage_generators):
    pending = list(stage_generators)
    while pending:
        for gen in list(pending):
            if next(gen, True):
                pending.remove(gen)


def _rms_mod(x, g, scale, shift):
    ms = jnp.mean(x * x, axis=-1, keepdims=True)
    return (x * lax.rsqrt(ms + EPS) * g) * (1.0 + scale) + shift


def _ada_kernel(c_ref, w_ref, b_ref, o_ref):
    w_hi, w_lo = _split(w_ref[...])
    o_ref[...] = _dot3(_silu(c_ref[...]), w_hi, w_lo) + b_ref[...]


def _ada(c8, w, b):
    n = w.shape[1]
    tn = ADA_TN
    return pl.pallas_call(
        _ada_kernel,
        grid=(n // tn,),
        in_specs=[
            pl.BlockSpec((SUBLANES, D_MODEL), lambda j: (0, 0)),
            pl.BlockSpec((D_MODEL, tn), lambda j: (0, j)),
            pl.BlockSpec((1, tn), lambda j: (0, j)),
        ],
        out_specs=pl.BlockSpec((SUBLANES, tn), lambda j: (0, j)),
        out_shape=jax.ShapeDtypeStruct((SUBLANES, n), F32),
        name="ada",
    )(c8, w, b)


def _inproj_kernel(x_ref, xn_ref, mod_ref, modn_ref, g_ref, w_ref, wif_hi_ref, wif_lo_ref, pos_ref, inv_ref,
                   sign_ref, proj_ref, gates_ref, cos_ref, sin_ref, hi_a, lo_a, hi_b, lo_b):
    i = pl.program_id(0)
    j = pl.program_id(1)

    def normalise(x, mod):
        return _split(_rms_mod(x, g_ref[...], mod[0, 1:2, :], mod[0, 0:1, :]))

    @pl.when(jnp.logical_and(i == 0, j == 0))
    def _():
        hi_a[...], lo_a[...] = normalise(x_ref[...], mod_ref)

    n = x_ref.shape[0] // pl.num_programs(1)
    rows = pl.ds(pl.multiple_of(j * n, n), n)
    n_split = 4
    tc = w_ref.shape[1] // n_split

    def step(cur_hi, cur_lo, nxt_hi, nxt_lo):
        def matmul_steps():
            for c in range(n_split):
                proj_ref[:, c * tc:(c + 1) * tc] = _dot(cur_hi[...], w_ref[:, c * tc:(c + 1) * tc]).astype(BF16)
                yield False

        def side_steps():
            h_hi = cur_hi[rows, :]
            h_lo = cur_lo[rows, :]
            gates_ref[rows, :] = (_dot(h_hi, wif_hi_ref[...])
                                  + (_dot(h_lo, wif_hi_ref[...]) + _dot(h_hi, wif_lo_ref[...])))
            yield False
            ang = pos_ref[rows, :] * inv_ref[...]
            cos_ref[rows, :] = jnp.cos(ang)
            yield False
            sin_ref[rows, :] = jnp.sin(ang) * sign_ref[...]
            yield False
            nxt_hi[rows, :], nxt_lo[rows, :] = normalise(xn_ref[rows, :], modn_ref)
            yield False

        _round_robin([matmul_steps(), side_steps()])

    @pl.when(i % 2 == 0)
    def _():
        step(hi_a, lo_a, hi_b, lo_b)

    @pl.when(i % 2 == 1)
    def _():
        step(hi_b, lo_b, hi_a, lo_a)


GATE_COL = 3072


def _wprep_kernel(a_ref, b_ref, o_ref, gate_ref):
    c = pl.program_id(0)
    tn = a_ref.shape[0]

    @pl.when(c * tn == GATE_COL)
    def _():
        gate_ref[...] = a_ref[0:2 * N_HEADS, :]

    @pl.when((c + 1) * tn <= GATE_COL)
    def _():
        o_ref[...] = a_ref[...].T.astype(BF16)

    @pl.when((c + 1) * tn > GATE_COL)
    def _():
        shifted = jnp.concatenate([a_ref[2 * N_HEADS:, :], b_ref[...]], axis=0)
        o_ref[...] = shifted.T.astype(BF16)


def _wprep(w_t):
    d = w_t.shape[1]
    tn = WPREP_TN
    gate = 2 * N_HEADS
    assert GATE_COL % tn == 0 and w_t.shape[0] == PROJ_W + gate
    return pl.pallas_call(
        _wprep_kernel,
        grid=(PROJ_W // tn,),
        in_specs=[
            pl.BlockSpec((tn, d), lambda c: (c, 0)),
            pl.BlockSpec((gate, d), lambda c: ((c + 1) * (tn // gate), 0)),
        ],
        out_specs=[pl.BlockSpec((d, tn), lambda c: (0, c)), pl.BlockSpec((gate, d), lambda c: (0, 0))],
        out_shape=[jax.ShapeDtypeStruct((d, PROJ_W), BF16), jax.ShapeDtypeStruct((gate, d), F32)],
        compiler_params=pltpu.CompilerParams(
            dimension_semantics=("arbitrary",), vmem_limit_bytes=VMEM_LIMIT),
        name="wprep",
    )(w_t, w_t)


def _inproj(x2, mod6, g, w_main, wif_hi, wif_lo, pos_b, inv2, sign2, seq):
    t = x2.shape[0]
    tm, tn = INPROJ_TM, INPROJ_TN
    per_b = seq // tm
    last = t // tm - 1
    nxt = lambda i: jnp.minimum(i + 1, last)
    return pl.pallas_call(
        _inproj_kernel,
        grid=(t // tm, PROJ_W // tn),
        in_specs=[
            pl.BlockSpec((tm, D_MODEL), lambda i, j: (i, 0)),
            pl.BlockSpec((tm, D_MODEL), lambda i, j: (nxt(i), 0)),
            pl.BlockSpec((1, 6, D_MODEL), lambda i, j: (i // per_b, 0, 0)),
            pl.BlockSpec((1, 6, D_MODEL), lambda i, j: (nxt(i) // per_b, 0, 0)),
            pl.BlockSpec((1, D_MODEL), lambda i, j: (0, 0)),
            pl.BlockSpec((D_MODEL, tn), lambda i, j: (0, j)),
            pl.BlockSpec((D_MODEL, GATE_W), lambda i, j: (0, 0)),
            pl.BlockSpec((D_MODEL, GATE_W), lambda i, j: (0, 0)),
            pl.BlockSpec((tm, LANES), lambda i, j: (i, 0)),
            pl.BlockSpec((1, LANES), lambda i, j: (0, 0)),
            pl.BlockSpec((1, LANES), lambda i, j: (0, 0)),
        ],
        out_specs=[
            pl.BlockSpec((tm, tn), lambda i, j: (i, j)),
            pl.BlockSpec((tm, GATE_W), lambda i, j: (i, 0)),
            pl.BlockSpec((tm, LANES), lambda i, j: (i, 0)),
            pl.BlockSpec((tm, LANES), lambda i, j: (i, 0)),
        ],
        out_shape=[
            jax.ShapeDtypeStruct((t, PROJ_W), BF16),
            jax.ShapeDtypeStruct((t, GATE_W), F32),
            jax.ShapeDtypeStruct((t, LANES), F32),
            jax.ShapeDtypeStruct((t, LANES), F32),
        ],
        scratch_shapes=[pltpu.VMEM((tm, D_MODEL), BF16)] * 4,
        compiler_params=pltpu.CompilerParams(
            dimension_semantics=("arbitrary", "arbitrary"), vmem_limit_bytes=VMEM_LIMIT),
        name="inproj",
    )(x2, x2, mod6, mod6, g, w_main, wif_hi, wif_lo, pos_b, inv2, sign2)


def _row_mean(a):
    inv_n = jnp.full((a.shape[1], LANES), 1.0 / a.shape[1], BF16)
    m = _dot(a.astype(BF16), inv_n)
    return jnp.concatenate([m] * (a.shape[1] // LANES), axis=1)


def _head_norm(h, g):
    d = h - _row_mean(h)
    return d * lax.rsqrt(_row_mean(d * d) + EPS) * g


def _mix_kernel(proj_ref, gates_ref, cos_ref, sin_ref, convw_ref, convb_ref, bif_ref, mlg_ref, retg_ref,
                decay_ref, kdec_ref, qdec_ref, gch_ref,
                hm_ref, hr_ref,
                prev_ref, c_st, m_st, r_st):
    group = proj_ref.shape[0]

    @pl.when(pl.program_id(1) == 0)
    def _():
        def zero_state(i, carry):
            c_st[i] = jnp.zeros(c_st.shape[1:], F32)
            r_st[i] = jnp.zeros(r_st.shape[1:], F32)
            return carry

        lax.fori_loop(0, group * N_HEADS, zero_state, 0)
        prev_ref[...] = jnp.zeros_like(prev_ref)
        m_st[...] = jnp.zeros_like(m_st)

    ml_steps, ret_steps = [], []
    for gb in range(group):
        states = slice(gb * N_HEADS, (gb + 1) * N_HEADS)
        ml_steps.append(_mlstm_steps(proj_ref.at[gb], gates_ref.at[gb], convw_ref, convb_ref, bif_ref, mlg_ref,
                                     hm_ref.at[gb], prev_ref.at[gb], c_st.at[states], m_st.at[gb]))
        ret_steps.append(_retention_steps(proj_ref.at[gb], cos_ref.at[gb], sin_ref.at[gb], retg_ref, decay_ref,
                                          kdec_ref, qdec_ref, gch_ref, hr_ref.at[gb], r_st.at[states]))
    def delayed(gen, rounds):
        for _ in range(rounds):
            yield False
        yield from gen

    _round_robin(ml_steps + [delayed(g, 4) for g in ret_steps])


def _mlstm_steps(proj_ref, gates_ref, convw_ref, convb_ref, bif_ref, mlg_ref, hm_ref, prev_ref, c_st, m_st):
    L = CHUNK
    rows = lax.broadcasted_iota(jnp.int32, (L, L), 0)
    cols = lax.broadcasted_iota(jnp.int32, (L, L), 1)
    causal = rows >= cols
    tril = jnp.where(causal, 1.0, 0.0).astype(BF16)

    cur = proj_ref[:, C_MLQ:C_MLQ + 2 * QK_W]
    xx = jnp.concatenate([prev_ref[...], cur], axis=0)
    r2 = lax.broadcasted_iota(jnp.int32, (L, 2 * L), 0)
    c2 = lax.broadcasted_iota(jnp.int32, (L, 2 * L), 1)
    acc = convb_ref[...] + cur.astype(F32) * convw_ref[CONV_K - 1:CONV_K, :]
    for d in range(1, CONV_K):
        shift = jnp.where(c2 == r2 + (L - d), 1.0, 0.0).astype(BF16)
        acc = acc + _dot(shift, xx) * convw_ref[CONV_K - 1 - d:CONV_K - d, :]
        yield False
    prev_ref[...] = cur
    qk = _silu(acc)
    yield False

    g = gates_ref[...] + bif_ref[...]
    gi = g[:, :LANES]
    gf = g[:, LANES:]
    lf = jnp.minimum(gf, 0.0) - jnp.log(1.0 + jnp.exp(-jnp.abs(gf)))
    lf_hi, lf_lo = _split(lf)
    a_all = _dot(tril, lf_hi) + _dot(tril, lf_lo)
    yield False
    a_last = a_all[L - 1:L, :]
    bm = gi - a_all
    bm_t = bm.T
    w_state = a_last + bm
    m_loc = jnp.max(w_state, axis=0, keepdims=True)
    m_prev = m_st[...]
    inter_log = a_all + m_prev
    m_new = jnp.maximum(a_last + m_prev, m_loc)
    s_prev = jnp.exp(a_last + m_prev - m_new)
    s_loc = jnp.exp(m_loc - m_new)
    ws_all = jnp.exp(w_state - m_loc) * s_loc
    m_st[...] = m_new
    yield False

    ones_blk = jnp.ones((L, LANES), BF16)
    q_scale = D_QK ** -0.5

    for h in range(N_HEADS):
        q = (qk[:, h * D_QK:(h + 1) * D_QK] * q_scale).astype(BF16)
        k_f = qk[:, QK_W + h * D_QK:QK_W + (h + 1) * D_QK]
        k = k_f.astype(BF16)
        v_ext = jnp.concatenate([proj_ref[:, C_MLV + h * D_V:C_MLV + (h + 1) * D_V], ones_blk], axis=1)
        dlog = jnp.where(causal, a_all[:, h:h + 1] + bm_t[h:h + 1, :], -jnp.inf)
        m_intra = jnp.max(dlog, axis=-1, keepdims=True)
        s = _dot_nt(q, k)
        qc = _dot(q, c_st[h].astype(BF16))
        yield False
        il = inter_log[:, h:h + 1]
        m_t = jnp.maximum(il, m_intra)
        p = jnp.exp(dlog - m_t) * s
        isc = jnp.exp(il - m_t)
        tot = _dot(p.astype(BF16), v_ext) + isc * qc
        yield False
        den = tot[:, D_V:D_V + 1]
        hout = tot[:, :D_V] / jnp.maximum(jnp.abs(den), jnp.exp(-m_t))
        kw_t = (k_f * ws_all[:, h:h + 1]).T.astype(BF16)
        c_st[h] = s_prev[:, h:h + 1] * c_st[h] + _dot(kw_t, v_ext)
        yield False
        y = _head_norm(hout, mlg_ref[:, h * D_V:(h + 1) * D_V])
        o = proj_ref[:, C_MLO + h * D_V:C_MLO + (h + 1) * D_V].astype(F32)
        hm_ref[:, h * D_V:(h + 1) * D_V] = _sigmoid_mul(o, y).astype(BF16)
        yield False


def _retention_steps(proj_ref, cos_ref, sin_ref, retg_ref, decay_ref, kdec_ref, qdec_ref, gch_ref, hr_ref, r_st):
    cos2 = cos_ref[...]
    sin2 = sin_ref[...]
    k_scale = D_QK ** -0.5
    for h in range(N_HEADS):
        q_raw = proj_ref[:, C_RQ + h * D_QK:C_RQ + (h + 1) * D_QK].astype(F32)
        k_raw = proj_ref[:, C_RK + h * D_QK:C_RK + (h + 1) * D_QK].astype(F32)
        q = (q_raw * cos2 + pltpu.roll(q_raw, D_QK // 2, 1) * sin2).astype(BF16)
        k_f = (k_raw * cos2 + pltpu.roll(k_raw, D_QK // 2, 1) * sin2) * k_scale
        v = proj_ref[:, C_RV + h * D_V:C_RV + (h + 1) * D_V]
        yield False
        sc = _dot_nt(q, k_f.astype(BF16)) * decay_ref[h]
        hret = _dot(sc.astype(BF16), v) + _dot(q, r_st[h].astype(BF16)) * qdec_ref[h]
        yield False
        kd_t = (k_f * kdec_ref[h]).T.astype(BF16)
        r_st[h] = gch_ref[h][:, 0:1] * r_st[h] + _dot(kd_t, v)
        yield False
        y = _head_norm(hret, retg_ref[:, h * D_V:(h + 1) * D_V])
        gt = proj_ref[:, C_RG + h * D_V:C_RG + (h + 1) * D_V].astype(F32)
        hr_ref[:, h * D_V:(h + 1) * D_V] = (_silu(gt) * y).astype(BF16)
        yield False


def _mix(proj, gates, cos2, sin2, conv_w, conv_b, b_if2, ml_g, ret_g, decay, kdec, qdec, gch, batch, seq):
    t = proj.shape[0]
    nc = seq // CHUNK
    L = CHUNK
    G = MIX_GROUP
    proj = proj.reshape(batch, seq, PROJ_W)
    gates = gates.reshape(batch, seq, GATE_W)
    cos2 = cos2.reshape(batch, seq, LANES)
    sin2 = sin2.reshape(batch, seq, LANES)
    full = lambda shape: pl.BlockSpec(shape, lambda b, c: (0,) * len(shape))
    hm, hr = pl.pallas_call(
        _mix_kernel,
        grid=(batch // G, nc),
        in_specs=[
            pl.BlockSpec((G, L, MIX_W), lambda b, c: (b, c, 0)),
            pl.BlockSpec((G, L, GATE_W), lambda b, c: (b, c, 0)),
            pl.BlockSpec((G, L, LANES), lambda b, c: (b, c, 0)),
            pl.BlockSpec((G, L, LANES), lambda b, c: (b, c, 0)),
            full((CONV_K, 2 * QK_W)),
            full((1, 2 * QK_W)),
            full((1, GATE_W)),
            full((1, V_W)),
            full((1, V_W)),
            full((N_HEADS, L, L)),
            full((N_HEADS, L, 1)),
            full((N_HEADS, L, 1)),
            full((N_HEADS, 1, LANES)),
        ],
        out_specs=[
            pl.BlockSpec((G, L, V_W), lambda b, c: (b, c, 0)),
            pl.BlockSpec((G, L, V_W), lambda b, c: (b, c, 0)),
        ],
        out_shape=[jax.ShapeDtypeStruct((batch, seq, V_W), BF16), jax.ShapeDtypeStruct((batch, seq, V_W), BF16)],
        scratch_shapes=[
            pltpu.VMEM((G, L, 2 * QK_W), BF16),
            pltpu.VMEM((G * N_HEADS, D_QK, V_EXT), F32),
            pltpu.VMEM((G, 1, LANES), F32),
            pltpu.VMEM((G * N_HEADS, D_QK, D_V), F32),
        ],
        compiler_params=pltpu.CompilerParams(
            dimension_semantics=("arbitrary", "arbitrary"), vmem_limit_bytes=VMEM_LIMIT),
        name="mix",
    )(proj, gates, cos2, sin2, conv_w, conv_b, b_if2, ml_g, ret_g, decay, kdec, qdec, gch)
    return hm.reshape(t, V_W), hr.reshape(t, V_W)


def _to_slabs(ref, val):
    rows = val.shape[0]
    for s in range(SLAB):
        ref[pl.ds(s, rows, stride=SLAB), :] = val[:, s * LANES:(s + 1) * LANES]


def _from_slabs(ref, first_slab, rows):
    return jnp.concatenate(
        [ref[pl.ds(first_slab * SLAB + s, rows, stride=SLAB), :] for s in range(SLAB)], axis=1)


def _post_kernel(hm_ref, hr_ref, gm_ref, gr_ref, x_ref, mod_ref, wbm_ref, wbr_ref, wout_ref, g_ref,
                 wr_hi_ref, wr_lo_ref, br_ref, x1_ref, hf_ref, logit_ref):
    n_sub = 2
    sub = x_ref.shape[0] // n_sub

    def steps(s):
        r = pl.ds(s * sub, sub)
        ym = _dot(hm_ref[r, :], wbm_ref[...])
        yield False
        yr = _dot(hr_ref[r, :], wbr_ref[...])
        yield False
        y = _sigmoid_mul(gm_ref[r, :].astype(F32), ym) + _sigmoid_mul(gr_ref[r, :].astype(F32), yr)
        o = _dot(y.astype(BF16), wout_ref[...])
        yield False
        x1 = x_ref[r, :] + mod_ref[0, 2:3, :] * o
        x1_ref[r, :] = x1
        hf = _rms_mod(x1, g_ref[...], mod_ref[0, 4:5, :], mod_ref[0, 3:4, :])
        yield False
        _to_slabs(hf_ref.at[pl.ds(s * sub * SLAB, sub * SLAB)], hf)
        logit_ref[r, :] = _dot3(hf, wr_hi_ref[...], wr_lo_ref[...]) + br_ref[...]
        yield False

    _round_robin([steps(s) for s in range(n_sub)])


def _post(hm, hr, proj, x2, mod6, wbm, wbr, wout, g, wr_hi, wr_lo, br, seq):
    t = x2.shape[0]
    tm = POST_TM
    per_b = seq // tm
    d = D_MODEL
    const = lambda shape: pl.BlockSpec(shape, lambda i: (0,) * len(shape))
    return pl.pallas_call(
        _post_kernel,
        grid=(t // tm,),
        in_specs=[
            pl.BlockSpec((tm, d), lambda i: (i, 0)),
            pl.BlockSpec((tm, d), lambda i: (i, 0)),
            pl.BlockSpec((tm, d), lambda i: (i, C_GML // d)),
            pl.BlockSpec((tm, d), lambda i: (i, C_GRET // d)),
            pl.BlockSpec((tm, d), lambda i: (i, 0)),
            pl.BlockSpec((1, 6, d), lambda i: (i // per_b, 0, 0)),
            const((d, d)), const((d, d)), const((d, d)),
            const((1, d)),
            const((d, LANES)), const((d, LANES)), const((1, LANES)),
        ],
        out_specs=[
            pl.BlockSpec((tm, d), lambda i: (i, 0)),
            pl.BlockSpec((tm * SLAB, LANES), lambda i: (i, 0)),
            pl.BlockSpec((tm, LANES), lambda i: (i, 0)),
        ],
        out_shape=[
            jax.ShapeDtypeStruct((t, d), F32),
            jax.ShapeDtypeStruct((t * SLAB, LANES), F32),
            jax.ShapeDtypeStruct((t, LANES), F32),
        ],
        compiler_params=pltpu.CompilerParams(
            dimension_semantics=("arbitrary",), vmem_limit_bytes=VMEM_LIMIT),
        name="post",
    )(hm, hr, proj, proj, x2, mod6, wbm, wbr, wout, g, wr_hi, wr_lo, br)


def _route_kernel(logit_ref, dest_ref, w_ref, bend_ref, cnt_st, pad_st, sel_st, pick_st, wts_st):
    ph = pl.program_id(0)
    i = pl.program_id(1)
    tm = logit_ref.shape[0]
    rows = pl.ds(pl.multiple_of(i * tm, tm), tm)
    lane = lax.broadcasted_iota(jnp.int32, (tm, LANES), 1)
    lane_f = lane.astype(F32)

    @pl.when(jnp.logical_and(ph == 0, i == 0))
    def _():
        cnt_st[...] = jnp.zeros_like(cnt_st)

    @pl.when(ph == 0)
    def _():
        l = logit_ref[...]
        picks, vals = [], []
        sel = jnp.zeros((tm, LANES), F32)
        for _ in range(TOP_K):
            m = jnp.max(l, axis=-1, keepdims=True)
            idx = jnp.min(jnp.where(l == m, lane_f, float(LANES)), axis=-1, keepdims=True)
            oh = lane_f == idx
            picks.append(idx)
            vals.append(m)
            sel = jnp.where(oh, 1.0, sel)
            l = jnp.where(oh, -jnp.inf, l)
        ex = [jnp.exp(v - vals[0]) for v in vals]
        den = ex[0] + ex[1] + ex[2] + ex[3]
        pick = jnp.zeros((tm, LANES), F32)
        wts = jnp.zeros((tm, LANES), F32)
        for k in range(TOP_K):
            pick = jnp.where(lane == k, picks[k], pick)
            wts = jnp.where(lane == k, ex[k] / den, wts)
        sel_st[rows, :] = sel.astype(BF16)
        pick_st[rows, :] = pick
        wts_st[rows, :] = wts
        cnt_st[...] = cnt_st[...] + jnp.sum(sel, axis=0, keepdims=True)

    @pl.when(jnp.logical_and(ph == 1, i == 0))
    def _():
        blocks = jnp.floor((cnt_st[...] + (EXPERT_BLOCK - 1)) * (1.0 / EXPERT_BLOCK))
        r = lax.broadcasted_iota(jnp.int32, (LANES, LANES), 0)
        c = lax.broadcasted_iota(jnp.int32, (LANES, LANES), 1)
        upper = jnp.where(r < c, 1.0, 0.0).astype(BF16)
        blocks8 = jnp.broadcast_to(blocks, (SUBLANES, LANES))
        excl = _dot(blocks8.astype(BF16), upper)
        pad_st[...] = excl[0:1, :] * EXPERT_BLOCK
        row = lax.broadcasted_iota(jnp.int32, (SUBLANES, LANES), 0)
        bend_ref[...] = jnp.where(row == 0, excl + blocks8,
                                  jnp.where(row == 1, excl * EXPERT_BLOCK + cnt_st[...], 0.0))
        cnt_st[...] = jnp.zeros_like(cnt_st)

    @pl.when(ph == 1)
    def _():
        carry = cnt_st[...]
        sel = sel_st[rows, :]
        pick = pick_st[rows, :]
        r = lax.broadcasted_iota(jnp.int32, (tm, tm), 0)
        c = lax.broadcasted_iota(jnp.int32, (tm, tm), 1)
        lower = jnp.where(r > c, 1.0, 0.0).astype(BF16)
        base = pad_st[...] + carry + _dot(lower, sel)
        dest = jnp.zeros((tm, LANES), F32)
        for k in range(TOP_K):
            dk = jnp.sum(jnp.where(lane_f == pick[:, k:k + 1], base, 0.0), axis=-1, keepdims=True)
            dest = jnp.where(lane == k, dk, dest)
        dest_ref[...] = dest.astype(jnp.int32)
        w_ref[...] = wts_st[rows, :]
        cnt_st[...] = carry + jnp.sum(sel.astype(F32), axis=0, keepdims=True)


def _route(logits):
    t = logits.shape[0]
    tm = ROUTE_TM
    return pl.pallas_call(
        _route_kernel,
        grid=(2, t // tm),
        in_specs=[pl.BlockSpec((tm, LANES), lambda ph, i: (i * (1 - ph), 0))],
        out_specs=[
            pl.BlockSpec((tm, LANES), lambda ph, i: (i * ph, 0)),
            pl.BlockSpec((tm, LANES), lambda ph, i: (i * ph, 0)),
            pl.BlockSpec((SUBLANES, LANES), lambda ph, i: (0, 0)),
        ],
        out_shape=[
            jax.ShapeDtypeStruct((t, LANES), jnp.int32),
            jax.ShapeDtypeStruct((t, LANES), F32),
            jax.ShapeDtypeStruct((SUBLANES, LANES), F32),
        ],
        scratch_shapes=[
            pltpu.VMEM((1, LANES), F32),
            pltpu.VMEM((1, LANES), F32),
            pltpu.VMEM((t, LANES), BF16),
            pltpu.VMEM((t, LANES), F32),
            pltpu.VMEM((t, LANES), F32),
        ],
        compiler_params=pltpu.CompilerParams(
            dimension_semantics=("arbitrary", "arbitrary"), vmem_limit_bytes=VMEM_LIMIT),
        name="route",
    )(logits)


DMA_UNROLL = 32


def _row_copy(src_ref, src_slab, dst_ref, dst_slab, sem):
    src = pl.multiple_of(src_slab * SLAB, SLAB)
    dst = pl.multiple_of(dst_slab * SLAB, SLAB)
    return pltpu.make_async_copy(src_ref.at[pl.ds(src, SLAB), :], dst_ref.at[pl.ds(dst, SLAB), :], sem)


def _dispatch_kernel(dest_ref, padfirst_ref, hf_hbm, xs_out, tiles, zbuf, lsem, rsem, zsem, tsem):
    i = pl.program_id(0)
    n_steps = pl.num_programs(0)
    n_assign = dest_ref.shape[0]
    tile_rows = tiles.shape[1]
    blk_rows = zbuf.shape[0]
    n_buf = tiles.shape[0]

    def tile_load(step, slot):
        start = pl.multiple_of(step * tile_rows, tile_rows)
        return pltpu.make_async_copy(hf_hbm.at[pl.ds(start, tile_rows), :], tiles.at[slot], lsem.at[slot])

    def wait_rows(slot):
        for _ in range(TOP_K):
            pltpu.make_async_copy(tiles.at[slot], xs_out.at[pl.ds(0, tile_rows), :], rsem.at[slot]).wait()

    @pl.when(i == 0)
    def _():
        tile_load(0, 0).start()
        tile_load(1, 1).start()
        zbuf[...] = jnp.zeros_like(zbuf)
        for e in range(N_EXPERTS):
            start = pl.multiple_of(padfirst_ref[e] * SLAB, SLAB)
            pltpu.make_async_copy(zbuf, xs_out.at[pl.ds(start, blk_rows), :], zsem).start()
        for e in range(N_EXPERTS):
            pltpu.make_async_copy(zbuf, xs_out.at[pl.ds(0, blk_rows), :], zsem).wait()

        n_used = padfirst_ref[N_EXPERTS]
        n_total = xs_out.shape[0] // blk_rows

        def zero_start(b, carry):
            start = pl.multiple_of(b * blk_rows, blk_rows)
            pltpu.make_async_copy(zbuf, xs_out.at[pl.ds(start, blk_rows), :], tsem).start()
            return carry

        lax.fori_loop(n_used, n_total, zero_start, 0)

    slot = i % n_buf
    tile_load(i, slot).wait()
    tile = tiles.at[slot]

    def body(g, carry):
        for u in range(DMA_UNROLL):
            t_local = g * (DMA_UNROLL // TOP_K) + u // TOP_K
            _row_copy(tile, t_local, xs_out, dest_ref[g * DMA_UNROLL + u], rsem.at[slot]).start(priority=u % 2)
        return carry

    lax.fori_loop(0, n_assign // DMA_UNROLL, body, 0)

    @pl.when(i >= 1)
    def _():
        wait_rows((i + n_buf - 1) % n_buf)

    @pl.when(i + 2 < n_steps)
    def _():
        tile_load(i + 2, (i + 2) % n_buf).start()

    @pl.when(i == n_steps - 1)
    def _():
        wait_rows(slot)

        def zero_wait(b, carry):
            pltpu.make_async_copy(zbuf, xs_out.at[pl.ds(0, blk_rows), :], tsem).wait()
            return carry

        lax.fori_loop(padfirst_ref[N_EXPERTS], xs_out.shape[0] // blk_rows, zero_wait, 0)


def _dispatch(dest_flat, pad_first, hf_slab, n_blocks):
    n_assign = dest_flat.shape[0]
    tm = DISPATCH_TM
    blk_rows = EXPERT_BLOCK * SLAB
    return pl.pallas_call(
        _dispatch_kernel,
        grid=(n_assign // (tm * TOP_K),),
        in_specs=[
            pl.BlockSpec((tm * TOP_K,), lambda i: (i,), memory_space=pltpu.SMEM),
            pl.BlockSpec(memory_space=pltpu.SMEM),
            pl.BlockSpec(memory_space=pl.ANY),
        ],
        out_specs=pl.BlockSpec(memory_space=pl.ANY),
        out_shape=jax.ShapeDtypeStruct(((n_blocks + 1) * blk_rows, LANES), F32),
        scratch_shapes=[
            pltpu.VMEM((3, tm * SLAB, LANES), F32),
            pltpu.VMEM((blk_rows, LANES), F32),
            pltpu.SemaphoreType.DMA((3,)),
            pltpu.SemaphoreType.DMA((3,)),
            pltpu.SemaphoreType.DMA,
            pltpu.SemaphoreType.DMA,
        ],
        compiler_params=pltpu.CompilerParams(dimension_semantics=("arbitrary",), vmem_limit_bytes=VMEM_LIMIT),
        name="dispatch",
    )(dest_flat, pad_first, hf_slab)


def _weight_copies(wgu_hbm, wd_hbm, wgu_f32, wd_f32, wsem, expert, slot):
    return (pltpu.make_async_copy(wgu_hbm.at[expert], wgu_f32.at[slot], wsem.at[0, slot]),
            pltpu.make_async_copy(wd_hbm.at[expert], wd_f32.at[slot], wsem.at[1, slot]))


def _expert_kernel(be_ref, nused_ref, first_ref, slot_ref, next_ref,
                   xs_ref, wgu_hbm, bgu_ref, wd_hbm, bd_ref, ys_ref,
                   wgu_f32, wd_f32, wgu_bf, wd_bf, wsem):
    b = pl.program_id(0)
    copies = functools.partial(_weight_copies, wgu_hbm, wd_hbm, wgu_f32, wd_f32, wsem)

    @pl.when(jnp.logical_and(b < nused_ref[0], first_ref[b] == 1))
    def _():
        slot = slot_ref[b]

        @pl.when(b == 0)
        def _():
            for cp in copies(be_ref[0], 0):
                cp.start()

        for cp in copies(be_ref[b], slot):
            cp.wait()

        @pl.when(next_ref[b] >= 0)
        def _():
            for cp in copies(next_ref[b], 1 - slot):
                cp.start()

        two = 2 * LANES
        r = lax.broadcasted_iota(jnp.int32, (two, two), 0)
        c = lax.broadcasted_iota(jnp.int32, (two, two), 1)
        src = jnp.where(c < LANES, 2 * c, 2 * (c - LANES) + 1)
        perm = jnp.where(r == src, 1.0, 0.0).astype(BF16)
        for blk in range(wgu_bf.shape[1] // two):
            wb = wgu_f32[slot, :, blk * two:(blk + 1) * two].astype(BF16)
            wgu_bf[:, blk * two:(blk + 1) * two] = _dot(wb, perm).astype(BF16)
        wd_bf[...] = wd_f32[slot].astype(BF16)

    @pl.when(b < nused_ref[0])
    def _():
        two = 2 * LANES
        x = _from_slabs(xs_ref, 0, EXPERT_BLOCK).astype(BF16)
        gu = _dot(x, wgu_bf[...]) + bgu_ref[0]
        n_blk = gu.shape[1] // two
        x_glu = jnp.concatenate([gu[:, i * two:i * two + LANES] for i in range(n_blk)], axis=1)
        x_lin = jnp.concatenate([gu[:, i * two + LANES:(i + 1) * two] for i in range(n_blk)], axis=1)
        x_glu = jnp.minimum(x_glu, SWIGLU_LIMIT)
        x_lin = jnp.clip(x_lin, -SWIGLU_LIMIT, SWIGLU_LIMIT)
        act = _sigmoid_mul(SWIGLU_ALPHA * x_glu, x_glu * (x_lin + 1.0))
        y = _dot(act.astype(BF16), wd_bf[...]) + bd_ref[0]
        _to_slabs(ys_ref, y)

    @pl.when(b >= nused_ref[0])
    def _():
        ys_ref[...] = jnp.zeros_like(ys_ref)


def _experts(block_expert, n_used, xs, w_gate_up, b_gu_perm, w_down, b_down):
    n_blocks = block_expert.shape[0]
    d = D_MODEL
    f2 = w_gate_up.shape[-1]
    blk_rows = EXPERT_BLOCK * SLAB

    idx = jnp.arange(n_blocks, dtype=jnp.int32)
    used = idx < n_used
    prev = jnp.concatenate([jnp.full((1,), -1, jnp.int32), block_expert[:-1]])
    first = (used & (block_expert != prev)).astype(jnp.int32)
    slot = (jnp.cumsum(first) - 1) % 2
    later = used[None, :] & (block_expert[None, :] > block_expert[:, None])
    nxt = jnp.min(jnp.where(later, block_expert[None, :], N_EXPERTS), axis=1)
    nxt = jnp.where(nxt == N_EXPERTS, -1, nxt)

    grid_spec = pltpu.PrefetchScalarGridSpec(
        num_scalar_prefetch=5,
        grid=(n_blocks,),
        in_specs=[
            pl.BlockSpec((blk_rows, LANES), lambda b, be, nu, *_: (jnp.minimum(b, nu[0] - 1), 0)),
            pl.BlockSpec(memory_space=pl.ANY),
            pl.BlockSpec((1, 1, f2), lambda b, be, *_: (be[b], 0, 0)),
            pl.BlockSpec(memory_space=pl.ANY),
            pl.BlockSpec((1, 1, d), lambda b, be, *_: (be[b], 0, 0)),
        ],
        out_specs=pl.BlockSpec((blk_rows, LANES), lambda b, be, *_: (b, 0)),
        scratch_shapes=[
            pltpu.VMEM((2, d, f2), F32),
            pltpu.VMEM((2, f2 // 2, d), F32),
            pltpu.VMEM((d, f2), BF16),
            pltpu.VMEM((f2 // 2, d), BF16),
            pltpu.SemaphoreType.DMA((2, 2)),
        ],
    )
    return pl.pallas_call(
        _expert_kernel,
        grid_spec=grid_spec,
        out_shape=jax.ShapeDtypeStruct((n_blocks * blk_rows, LANES), F32),
        compiler_params=pltpu.CompilerParams(
            dimension_semantics=("arbitrary",), vmem_limit_bytes=VMEM_LIMIT),
        name="expert",
    )(block_expert, n_used, first, slot.astype(jnp.int32), nxt.astype(jnp.int32),
      xs, w_gate_up, b_gu_perm, w_down, b_down)


def _combine_kernel(dest_ref, dest_next_ref, ys_hbm, w_ref, x1_ref, mod_ref, g_ref, o_ref, buf, sem):
    i = pl.program_id(0)
    n_steps = pl.num_programs(0)
    tm = x1_ref.shape[0]
    n_assign = tm * TOP_K
    slot = i % 2

    def gather(d_ref, to_slot):
        def body(g, carry):
            for u in range(DMA_UNROLL):
                t_local = g * (DMA_UNROLL // TOP_K) + u // TOP_K
                k = u % TOP_K
                _row_copy(ys_hbm, d_ref[g * DMA_UNROLL + u], buf.at[to_slot], k * tm + t_local,
                          sem.at[to_slot]).start(priority=u % 2)
            return carry

        lax.fori_loop(0, n_assign // DMA_UNROLL, body, 0)

    @pl.when(i == 0)
    def _():
        gather(dest_ref, 0)

    @pl.when(i + 1 < n_steps)
    def _():
        gather(dest_next_ref, 1 - slot)

    cur = buf.at[slot]
    pltpu.make_async_copy(ys_hbm.at[pl.ds(0, n_assign * SLAB), :], cur, sem.at[slot]).wait()

    w = w_ref[...]
    y = w[:, 0:1] * _from_slabs(cur, 0, tm)
    for k in range(1, TOP_K):
        y = y + w[:, k:k + 1] * _from_slabs(cur, k * tm, tm)
    x = x1_ref[...] + mod_ref[0, 5:6, :] * y
    ms = jnp.mean(x * x, axis=-1, keepdims=True)
    o_ref[...] = x * lax.rsqrt(ms + EPS) * g_ref[...]


def _combine(dest_flat, ys, wts, x1, mod6, g, seq):
    t = x1.shape[0]
    tm = COMBINE_TM
    per_b = seq // tm
    d = D_MODEL
    last = t // tm - 1
    return pl.pallas_call(
        _combine_kernel,
        grid=(t // tm,),
        in_specs=[
            pl.BlockSpec((tm * TOP_K,), lambda i: (i,), memory_space=pltpu.SMEM),
            pl.BlockSpec((tm * TOP_K,), lambda i: (jnp.minimum(i + 1, last),), memory_space=pltpu.SMEM),
            pl.BlockSpec(memory_space=pl.ANY),
            pl.BlockSpec((tm, LANES), lambda i: (i, 0)),
            pl.BlockSpec((tm, d), lambda i: (i, 0)),
            pl.BlockSpec((1, 6, d), lambda i: (i // per_b, 0, 0)),
            pl.BlockSpec((1, d), lambda i: (0, 0)),
        ],
        out_specs=pl.BlockSpec((tm, d), lambda i: (i, 0)),
        out_shape=jax.ShapeDtypeStruct((t, d), F32),
        scratch_shapes=[pltpu.VMEM((2, tm * TOP_K * SLAB, LANES), F32), pltpu.SemaphoreType.DMA((2,))],
        compiler_params=pltpu.CompilerParams(
            dimension_semantics=("arbitrary",), vmem_limit_bytes=VMEM_LIMIT),
        name="combine",
    )(dest_flat, dest_flat, ys, wts, x1, mod6, g)


def kernel(x, c, positions, w_ada, b_ada, norm_mix_g, w_in, conv_w, conv_b, b_if, ml_norm_g, ret_norm_g,
           w_branch_ml, w_branch_ret, w_out, norm_ffn_g, w_router, b_router, w_gate_up, b_gate_up, w_down,
           b_down, norm_final_g):
    batch, seq, d = x.shape
    t = batch * seq
    x2 = x.reshape(t, d)
    depth = w_ada.shape[0]
    c8 = jnp.concatenate([c, jnp.zeros((SUBLANES - batch, d), c.dtype)], axis=0)

    half = D_QK // 2
    inv = ROPE_BASE ** (-jnp.arange(half, dtype=F32) / half)
    inv2 = jnp.concatenate([inv, inv])[None, :]
    sign2 = jnp.concatenate([-jnp.ones((half,), F32), jnp.ones((half,), F32)])[None, :]
    pos_b = jnp.broadcast_to(positions.astype(F32).reshape(t, 1), (t, LANES))
    L = CHUNK
    log_gamma = jnp.log(1.0 - 2.0 ** (-5.0 - jnp.arange(N_HEADS, dtype=F32)))
    pos = jnp.arange(L, dtype=F32)
    rel = pos[:, None] - pos[None, :]
    decay = jnp.where(rel >= 0, jnp.exp(log_gamma[:, None, None] * jnp.maximum(rel, 0.0)), 0.0)
    kdec = jnp.exp(log_gamma[:, None] * (L - 1 - pos))[:, :, None]
    qdec = jnp.exp(log_gamma[:, None] * (pos + 1.0))[:, :, None]
    gch = jnp.broadcast_to(jnp.exp(log_gamma * L)[:, None, None], (N_HEADS, 1, LANES))

    assert depth == 1, "the final norm is fused after the single layer"
    l = 0
    mod = _ada(c8, w_ada[l], b_ada[l][None, :])
    mod6 = mod[:batch].reshape(batch, 6, d)

    w = w_in[l]
    w_main, w_gate = _wprep(w.T)
    w_if = w_gate.T
    zpad = jnp.zeros((d, LANES - N_HEADS), F32)
    wif = jnp.concatenate([w_if[:, :N_HEADS], zpad, w_if[:, N_HEADS:], zpad], axis=1)
    wif_hi, wif_lo = _split(wif)
    zb = jnp.zeros((LANES - N_HEADS,), F32)
    b_if2 = jnp.concatenate([b_if[l][:N_HEADS], zb, b_if[l][N_HEADS:], zb])[None, :]

    proj, gates, cos2, sin2 = _inproj(x2, mod6, norm_mix_g[l][None, :], w_main, wif_hi, wif_lo,
                                      pos_b, inv2, sign2, seq)
    hm, hr = _mix(proj, gates, cos2, sin2, conv_w[l], conv_b[l][None, :], b_if2,
                  ml_norm_g[l][None, :], ret_norm_g[l][None, :], decay, kdec, qdec, gch, batch, seq)

    wr = jnp.concatenate([w_router[l], jnp.zeros((d, LANES - N_EXPERTS), F32)], axis=1)
    wr_hi, wr_lo = _split(wr)
    br = jnp.concatenate([b_router[l], jnp.full((LANES - N_EXPERTS,), NEG_BIG, F32)])[None, :]
    x1, hf, logits = _post(hm, hr, proj, x2, mod6, w_branch_ml[l].astype(BF16), w_branch_ret[l].astype(BF16),
                           w_out[l].astype(BF16), norm_ffn_g[l][None, :], wr_hi, wr_lo, br, seq)

    n_assign = t * TOP_K
    n_blocks = -(-n_assign // EXPERT_BLOCK) + N_EXPERTS
    dest, wts, tables = _route(logits)
    dest_flat = dest[:, :TOP_K].reshape(n_assign)
    bend = tables[0, :N_EXPERTS].astype(jnp.int32)
    pad_first = tables[1, :N_EXPERTS].astype(jnp.int32)
    n_used = bend[N_EXPERTS - 1:]
    blk = jnp.minimum(jnp.arange(n_blocks, dtype=jnp.int32), n_used - 1)
    block_expert = jnp.minimum(
        jnp.sum((bend[None, :] <= blk[:, None]).astype(jnp.int32), axis=1), N_EXPERTS - 1)
    xs = _dispatch(dest_flat, jnp.concatenate([pad_first, n_used]), hf, n_blocks)
    f2 = w_gate_up.shape[-1]
    bgu = b_gate_up[l].reshape(N_EXPERTS, f2 // (2 * LANES), LANES, 2)
    bgu = jnp.swapaxes(bgu, -1, -2).reshape(N_EXPERTS, 1, f2)
    ys = _experts(block_expert, n_used, xs, w_gate_up[l], bgu, w_down[l], b_down[l][:, None, :])
    out = _combine(dest_flat, ys, wts, x1, mod6, norm_final_g[None, :], seq)
    return out.reshape(batch, seq, d)
```

```python
import functools

import jax
import jax.numpy as jnp
from jax import lax
from jax.experimental import pallas as pl
from jax.experimental.pallas import tpu as pltpu

D_MODEL = 1024
N_HEADS = 4
D_QK = 128
D_V = 256
CONV_K = 4
ROPE_BASE = 10000.0
CHUNK = 128
N_EXPERTS = 32
TOP_K = 4
SWIGLU_LIMIT = 7.0
SWIGLU_ALPHA = 1.702
EXPERT_BLOCK = 512
EPS = 1e-5

QK_W = N_HEADS * D_QK
V_W = N_HEADS * D_V
LANES = 128
SUBLANES = 8
GATE_W = 2 * LANES
V_EXT = D_V + LANES
MIX_GROUP = 4
SLAB = D_MODEL // LANES

C_MLQ, C_MLK, C_MLV, C_MLO = 0, 512, 1024, 2048
C_RQ, C_RK, C_RV, C_RG = 3072, 3584, 4096, 5120
C_GML, C_GRET = 6144, 7168
MIX_W = 6144
PROJ_W = 8192

BF16 = jnp.bfloat16
F32 = jnp.float32
NEG_BIG = -1e30

ADA_TN = 1024
WPREP_TN = 1024
INPROJ_TM, INPROJ_TN = 1024, 2048
POST_TM = 1024
ROUTE_TM = 1024
DISPATCH_TM = 512
COMBINE_TM = 256
VMEM_LIMIT = 56 * 1024 * 1024


def _dot(a, b):
    return jnp.dot(a, b, preferred_element_type=F32)


def _dot_nt(a, b):
    return lax.dot_general(a, b, (((1,), (1,)), ((), ())), preferred_element_type=F32)


def _split(a):
    hi = a.astype(BF16)
    lo = (a - hi.astype(F32)).astype(BF16)
    return hi, lo


def _dot3(a, b_hi, b_lo):
    a_hi, a_lo = _split(a)
    return _dot(a_hi, b_hi) + (_dot(a_lo, b_hi) + _dot(a_hi, b_lo))


def _sigmoid_mul(x, y):
    hy = 0.5 * y
    return hy * jnp.tanh(0.5 * x) + hy


def _silu(x):
    hx = 0.5 * x
    return hx * jnp.tanh(hx) + hx


def _round_robin(stage_generators):
    pending = list(stage_generators)
    while pending:
        for gen in list(pending):
            if next(gen, True):
                pending.remove(gen)


def _rms_mod(x, g, scale, shift):
    ms = jnp.mean(x * x, axis=-1, keepdims=True)
    return (x * lax.rsqrt(ms + EPS) * g) * (1.0 + scale) + shift


def _ada_kernel(c_ref, w_ref, b_ref, o_ref):
    w_hi, w_lo = _split(w_ref[...])
    o_ref[...] = _dot3(_silu(c_ref[...]), w_hi, w_lo) + b_ref[...]


def _ada(c8, w, b):
    n = w.shape[1]
    tn = ADA_TN
    return pl.pallas_call(
        _ada_kernel,
        grid=(n // tn,),
        in_specs=[
            pl.BlockSpec((SUBLANES, D_MODEL), lambda j: (0, 0)),
            pl.BlockSpec((D_MODEL, tn), lambda j: (0, j)),
            pl.BlockSpec((1, tn), lambda j: (0, j)),
        ],
        out_specs=pl.BlockSpec((SUBLANES, tn), lambda j: (0, j)),
        out_shape=jax.ShapeDtypeStruct((SUBLANES, n), F32),
        name="ada",
    )(c8, w, b)


def _inproj_kernel(x_ref, xn_ref, mod_ref, modn_ref, g_ref, w_ref, wif_hi_ref, wif_lo_ref, pos_ref, inv_ref,
                   sign_ref, proj_ref, gates_ref, cos_ref, sin_ref, hi_a, lo_a, hi_b, lo_b):
    i = pl.program_id(0)
    j = pl.program_id(1)

    def normalise(x, mod):
        return _split(_rms_mod(x, g_ref[...], mod[0, 1:2, :], mod[0, 0:1, :]))

    @pl.when(jnp.logical_and(i == 0, j == 0))
    def _():
        hi_a[...], lo_a[...] = normalise(x_ref[...], mod_ref)

    n = x_ref.shape[0] // pl.num_programs(1)
    rows = pl.ds(pl.multiple_of(j * n, n), n)
    n_split = 4
    tc = w_ref.shape[1] // n_split

    def step(cur_hi, cur_lo, nxt_hi, nxt_lo):
        def matmul_steps():
            for c in range(n_split):
                proj_ref[:, c * tc:(c + 1) * tc] = _dot(cur_hi[...], w_ref[:, c * tc:(c + 1) * tc]).astype(BF16)
                yield False

        def side_steps():
            h_hi = cur_hi[rows, :]
            h_lo = cur_lo[rows, :]
            gates_ref[rows, :] = (_dot(h_hi, wif_hi_ref[...])
                                  + (_dot(h_lo, wif_hi_ref[...]) + _dot(h_hi, wif_lo_ref[...])))
            yield False
            ang = pos_ref[rows, :] * inv_ref[...]
            cos_ref[rows, :] = jnp.cos(ang)
            yield False
            sin_ref[rows, :] = jnp.sin(ang) * sign_ref[...]
            yield False
            nxt_hi[rows, :], nxt_lo[rows, :] = normalise(xn_ref[rows, :], modn_ref)
            yield False

        _round_robin([matmul_steps(), side_steps()])

    @pl.when(i % 2 == 0)
    def _():
        step(hi_a, lo_a, hi_b, lo_b)

    @pl.when(i % 2 == 1)
    def _():
        step(hi_b, lo_b, hi_a, lo_a)


GATE_COL = 3072


def _wprep_kernel(a_ref, b_ref, o_ref, gate_ref):
    c = pl.program_id(0)
    tn = a_ref.shape[0]

    @pl.when(c * tn == GATE_COL)
    def _():
        gate_ref[...] = a_ref[0:2 * N_HEADS, :]

    @pl.when((c + 1) * tn <= GATE_COL)
    def _():
        o_ref[...] = a_ref[...].T.astype(BF16)

    @pl.when((c + 1) * tn > GATE_COL)
    def _():
        shifted = jnp.concatenate([a_ref[2 * N_HEADS:, :], b_ref[...]], axis=0)
        o_ref[...] = shifted.T.astype(BF16)


def _wprep(w_t):
    d = w_t.shape[1]
    tn = WPREP_TN
    gate = 2 * N_HEADS
    assert GATE_COL % tn == 0 and w_t.shape[0] == PROJ_W + gate
    return pl.pallas_call(
        _wprep_kernel,
        grid=(PROJ_W // tn,),
        in_specs=[
            pl.BlockSpec((tn, d), lambda c: (c, 0)),
            pl.BlockSpec((gate, d), lambda c: ((c + 1) * (tn // gate), 0)),
        ],
        out_specs=[pl.BlockSpec((d, tn), lambda c: (0, c)), pl.BlockSpec((gate, d), lambda c: (0, 0))],
        out_shape=[jax.ShapeDtypeStruct((d, PROJ_W), BF16), jax.ShapeDtypeStruct((gate, d), F32)],
        compiler_params=pltpu.CompilerParams(
            dimension_semantics=("arbitrary",), vmem_limit_bytes=VMEM_LIMIT),
        name="wprep",
    )(w_t, w_t)


def _inproj(x2, mod6, g, w_main, wif_hi, wif_lo, pos_b, inv2, sign2, seq):
    t = x2.shape[0]
    tm, tn = INPROJ_TM, INPROJ_TN
    per_b = seq // tm
    last = t // tm - 1
    nxt = lambda i: jnp.minimum(i + 1, last)
    return pl.pallas_call(
        _inproj_kernel,
        grid=(t // tm, PROJ_W // tn),
        in_specs=[
            pl.BlockSpec((tm, D_MODEL), lambda i, j: (i, 0)),
            pl.BlockSpec((tm, D_MODEL), lambda i, j: (nxt(i), 0)),
            pl.BlockSpec((1, 6, D_MODEL), lambda i, j: (i // per_b, 0, 0)),
            pl.BlockSpec((1, 6, D_MODEL), lambda i, j: (nxt(i) // per_b, 0, 0)),
            pl.BlockSpec((1, D_MODEL), lambda i, j: (0, 0)),
            pl.BlockSpec((D_MODEL, tn), lambda i, j: (0, j)),
            pl.BlockSpec((D_MODEL, GATE_W), lambda i, j: (0, 0)),
            pl.BlockSpec((D_MODEL, GATE_W), lambda i, j: (0, 0)),
            pl.BlockSpec((tm, LANES), lambda i, j: (i, 0)),
            pl.BlockSpec((1, LANES), lambda i, j: (0, 0)),
            pl.BlockSpec((1, LANES), lambda i, j: (0, 0)),
        ],
        out_specs=[
            pl.BlockSpec((tm, tn), lambda i, j: (i, j)),
            pl.BlockSpec((tm, GATE_W), lambda i, j: (i, 0)),
            pl.BlockSpec((tm, LANES), lambda i, j: (i, 0)),
            pl.BlockSpec((tm, LANES), lambda i, j: (i, 0)),
        ],
        out_shape=[
            jax.ShapeDtypeStruct((t, PROJ_W), BF16),
            jax.ShapeDtypeStruct((t, GATE_W), F32),
            jax.ShapeDtypeStruct((t, LANES), F32),
            jax.ShapeDtypeStruct((t, LANES), F32),
        ],
        scratch_shapes=[pltpu.VMEM((tm, D_MODEL), BF16)] * 4,
        compiler_params=pltpu.CompilerParams(
            dimension_semantics=("arbitrary", "arbitrary"), vmem_limit_bytes=VMEM_LIMIT),
        name="inproj",
    )(x2, x2, mod6, mod6, g, w_main, wif_hi, wif_lo, pos_b, inv2, sign2)


def _row_mean(a):
    inv_n = jnp.full((a.shape[1], LANES), 1.0 / a.shape[1], BF16)
    m = _dot(a.astype(BF16), inv_n)
    return jnp.concatenate([m] * (a.shape[1] // LANES), axis=1)


def _head_norm(h, g):
    d = h - _row_mean(h)
    return d * lax.rsqrt(_row_mean(d * d) + EPS) * g


def _mix_kernel(proj_ref, gates_ref, cos_ref, sin_ref, convw_ref, convb_ref, bif_ref, mlg_ref, retg_ref,
                decay_ref, kdec_ref, qdec_ref, gch_ref,
                hm_ref, hr_ref,
                prev_ref, c_st, m_st, r_st):
    group = proj_ref.shape[0]

    @pl.when(pl.program_id(1) == 0)
    def _():
        def zero_state(i, carry):
            c_st[i] = jnp.zeros(c_st.shape[1:], F32)
            r_st[i] = jnp.zeros(r_st.shape[1:], F32)
            return carry

        lax.fori_loop(0, group * N_HEADS, zero_state, 0)
        prev_ref[...] = jnp.zeros_like(prev_ref)
        m_st[...] = jnp.zeros_like(m_st)

    ml_steps, ret_steps = [], []
    for gb in range(group):
        states = slice(gb * N_HEADS, (gb + 1) * N_HEADS)
        ml_steps.append(_mlstm_steps(proj_ref.at[gb], gates_ref.at[gb], convw_ref, convb_ref, bif_ref, mlg_ref,
                                     hm_ref.at[gb], prev_ref.at[gb], c_st.at[states], m_st.at[gb]))
        ret_steps.append(_retention_steps(proj_ref.at[gb], cos_ref.at[gb], sin_ref.at[gb], retg_ref, decay_ref,
                                          kdec_ref, qdec_ref, gch_ref, hr_ref.at[gb], r_st.at[states]))
    def delayed(gen, rounds):
        for _ in range(rounds):
            yield False
        yield from gen

    _round_robin(ml_steps + [delayed(g, 4) for g in ret_steps])


def _mlstm_steps(proj_ref, gates_ref, convw_ref, convb_ref, bif_ref, mlg_ref, hm_ref, prev_ref, c_st, m_st):
    L = CHUNK
    rows = lax.broadcasted_iota(jnp.int32, (L, L), 0)
    cols = lax.broadcasted_iota(jnp.int32, (L, L), 1)
    causal = rows >= cols
    tril = jnp.where(causal, 1.0, 0.0).astype(BF16)

    cur = proj_ref[:, C_MLQ:C_MLQ + 2 * QK_W]
    xx = jnp.concatenate([prev_ref[...], cur], axis=0)
    r2 = lax.broadcasted_iota(jnp.int32, (L, 2 * L), 0)
    c2 = lax.broadcasted_iota(jnp.int32, (L, 2 * L), 1)
    acc = convb_ref[...] + cur.astype(F32) * convw_ref[CONV_K - 1:CONV_K, :]
    for d in range(1, CONV_K):
        shift = jnp.where(c2 == r2 + (L - d), 1.0, 0.0).astype(BF16)
        acc = acc + _dot(shift, xx) * convw_ref[CONV_K - 1 - d:CONV_K - d, :]
        yield False
    prev_ref[...] = cur
    qk = _silu(acc)
    yield False

    g = gates_ref[...] + bif_ref[...]
    gi = g[:, :LANES]
    gf = g[:, LANES:]
    lf = jnp.minimum(gf, 0.0) - jnp.log(1.0 + jnp.exp(-jnp.abs(gf)))
    lf_hi, lf_lo = _split(lf)
    a_all = _dot(tril, lf_hi) + _dot(tril, lf_lo)
    yield False
    a_last = a_all[L - 1:L, :]
    bm = gi - a_all
    bm_t = bm.T
    w_state = a_last + bm
    m_loc = jnp.max(w_state, axis=0, keepdims=True)
    m_prev = m_st[...]
    inter_log = a_all + m_prev
    m_new = jnp.maximum(a_last + m_prev, m_loc)
    s_prev = jnp.exp(a_last + m_prev - m_new)
    s_loc = jnp.exp(m_loc - m_new)
    ws_all = jnp.exp(w_state - m_loc) * s_loc
    m_st[...] = m_new
    yield False

    ones_blk = jnp.ones((L, LANES), BF16)
    q_scale = D_QK ** -0.5

    for h in range(N_HEADS):
        q = (qk[:, h * D_QK:(h + 1) * D_QK] * q_scale).astype(BF16)
        k_f = qk[:, QK_W + h * D_QK:QK_W + (h + 1) * D_QK]
        k = k_f.astype(BF16)
        v_ext = jnp.concatenate([proj_ref[:, C_MLV + h * D_V:C_MLV + (h + 1) * D_V], ones_blk], axis=1)
        dlog = jnp.where(causal, a_all[:, h:h + 1] + bm_t[h:h + 1, :], -jnp.inf)
        m_intra = jnp.max(dlog, axis=-1, keepdims=True)
        s = _dot_nt(q, k)
        qc = _dot(q, c_st[h].astype(BF16))
        yield False
        il = inter_log[:, h:h + 1]
        m_t = jnp.maximum(il, m_intra)
        p = jnp.exp(dlog - m_t) * s
        isc = jnp.exp(il - m_t)
        tot = _dot(p.astype(BF16), v_ext) + isc * qc
        yield False
        den = tot[:, D_V:D_V + 1]
        hout = tot[:, :D_V] / jnp.maximum(jnp.abs(den), jnp.exp(-m_t))
        kw_t = (k_f * ws_all[:, h:h + 1]).T.astype(BF16)
        c_st[h] = s_prev[:, h:h + 1] * c_st[h] + _dot(kw_t, v_ext)
        yield False
        y = _head_norm(hout, mlg_ref[:, h * D_V:(h + 1) * D_V])
        o = proj_ref[:, C_MLO + h * D_V:C_MLO + (h + 1) * D_V].astype(F32)
        hm_ref[:, h * D_V:(h + 1) * D_V] = _sigmoid_mul(o, y).astype(BF16)
        yield False


def _retention_steps(proj_ref, cos_ref, sin_ref, retg_ref, decay_ref, kdec_ref, qdec_ref, gch_ref, hr_ref, r_st):
    cos2 = cos_ref[...]
    sin2 = sin_ref[...]
    k_scale = D_QK ** -0.5
    for h in range(N_HEADS):
        q_raw = proj_ref[:, C_RQ + h * D_QK:C_RQ + (h + 1) * D_QK].astype(F32)
        k_raw = proj_ref[:, C_RK + h * D_QK:C_RK + (h + 1) * D_QK].astype(F32)
        q = (q_raw * cos2 + pltpu.roll(q_raw, D_QK // 2, 1) * sin2).astype(BF16)
        k_f = (k_raw * cos2 + pltpu.roll(k_raw, D_QK // 2, 1) * sin2) * k_scale
        v = proj_ref[:, C_RV + h * D_V:C_RV + (h + 1) * D_V]
        yield False
        sc = _dot_nt(q, k_f.astype(BF16)) * decay_ref[h]
        hret = _dot(sc.astype(BF16), v) + _dot(q, r_st[h].astype(BF16)) * qdec_ref[h]
        yield False
        kd_t = (k_f * kdec_ref[h]).T.astype(BF16)
        r_st[h] = gch_ref[h][:, 0:1] * r_st[h] + _dot(kd_t, v)
        yield False
        y = _head_norm(hret, retg_ref[:, h * D_V:(h + 1) * D_V])
        gt = proj_ref[:, C_RG + h * D_V:C_RG + (h + 1) * D_V].astype(F32)
        hr_ref[:, h * D_V:(h + 1) * D_V] = (_silu(gt) * y).astype(BF16)
        yield False


def _mix(proj, gates, cos2, sin2, conv_w, conv_b, b_if2, ml_g, ret_g, decay, kdec, qdec, gch, batch, seq):
    t = proj.shape[0]
    nc = seq // CHUNK
    L = CHUNK
    G = MIX_GROUP
    proj = proj.reshape(batch, seq, PROJ_W)
    gates = gates.reshape(batch, seq, GATE_W)
    cos2 = cos2.reshape(batch, seq, LANES)
    sin2 = sin2.reshape(batch, seq, LANES)
    full = lambda shape: pl.BlockSpec(shape, lambda b, c: (0,) * len(shape))
    hm, hr = pl.pallas_call(
        _mix_kernel,
        grid=(batch // G, nc),
        in_specs=[
            pl.BlockSpec((G, L, MIX_W), lambda b, c: (b, c, 0)),
            pl.BlockSpec((G, L, GATE_W), lambda b, c: (b, c, 0)),
            pl.BlockSpec((G, L, LANES), lambda b, c: (b, c, 0)),
            pl.BlockSpec((G, L, LANES), lambda b, c: (b, c, 0)),
            full((CONV_K, 2 * QK_W)),
            full((1, 2 * QK_W)),
            full((1, GATE_W)),
            full((1, V_W)),
            full((1, V_W)),
            full((N_HEADS, L, L)),
            full((N_HEADS, L, 1)),
            full((N_HEADS, L, 1)),
            full((N_HEADS, 1, LANES)),
        ],
        out_specs=[
            pl.BlockSpec((G, L, V_W), lambda b, c: (b, c, 0)),
            pl.BlockSpec((G, L, V_W), lambda b, c: (b, c, 0)),
        ],
        out_shape=[jax.ShapeDtypeStruct((batch, seq, V_W), BF16), jax.ShapeDtypeStruct((batch, seq, V_W), BF16)],
        scratch_shapes=[
            pltpu.VMEM((G, L, 2 * QK_W), BF16),
            pltpu.VMEM((G * N_HEADS, D_QK, V_EXT), F32),
            pltpu.VMEM((G, 1, LANES), F32),
            pltpu.VMEM((G * N_HEADS, D_QK, D_V), F32),
        ],
        compiler_params=pltpu.CompilerParams(
            dimension_semantics=("arbitrary", "arbitrary"), vmem_limit_bytes=VMEM_LIMIT),
        name="mix",
    )(proj, gates, cos2, sin2, conv_w, conv_b, b_if2, ml_g, ret_g, decay, kdec, qdec, gch)
    return hm.reshape(t, V_W), hr.reshape(t, V_W)


def _to_slabs(ref, val):
    rows = val.shape[0]
    for s in range(SLAB):
        ref[pl.ds(s, rows, stride=SLAB), :] = val[:, s * LANES:(s + 1) * LANES]


def _from_slabs(ref, first_slab, rows):
    return jnp.concatenate(
        [ref[pl.ds(first_slab * SLAB + s, rows, stride=SLAB), :] for s in range(SLAB)], axis=1)


def _post_kernel(hm_ref, hr_ref, gm_ref, gr_ref, x_ref, mod_ref, wbm_ref, wbr_ref, wout_ref, g_ref,
                 wr_hi_ref, wr_lo_ref, br_ref, x1_ref, hf_ref, logit_ref):
    n_sub = 2
    sub = x_ref.shape[0] // n_sub

    def steps(s):
        r = pl.ds(s * sub, sub)
        ym = _dot(hm_ref[r, :], wbm_ref[...])
        yield False
        yr = _dot(hr_ref[r, :], wbr_ref[...])
        yield False
        y = _sigmoid_mul(gm_ref[r, :].astype(F32), ym) + _sigmoid_mul(gr_ref[r, :].astype(F32), yr)
        o = _dot(y.astype(BF16), wout_ref[...])
        yield False
        x1 = x_ref[r, :] + mod_ref[0, 2:3, :] * o
        x1_ref[r, :] = x1
        hf = _rms_mod(x1, g_ref[...], mod_ref[0, 4:5, :], mod_ref[0, 3:4, :])
        yield False
        _to_slabs(hf_ref.at[pl.ds(s * sub * SLAB, sub * SLAB)], hf)
        logit_ref[r, :] = _dot3(hf, wr_hi_ref[...], wr_lo_ref[...]) + br_ref[...]
        yield False

    _round_robin([steps(s) for s in range(n_sub)])


def _post(hm, hr, proj, x2, mod6, wbm, wbr, wout, g, wr_hi, wr_lo, br, seq):
    t = x2.shape[0]
    tm = POST_TM
    per_b = seq // tm
    d = D_MODEL
    const = lambda shape: pl.BlockSpec(shape, lambda i: (0,) * len(shape))
    return pl.pallas_call(
        _post_kernel,
        grid=(t // tm,),
        in_specs=[
            pl.BlockSpec((tm, d), lambda i: (i, 0)),
            pl.BlockSpec((tm, d), lambda i: (i, 0)),
            pl.BlockSpec((tm, d), lambda i: (i, C_GML // d)),
            pl.BlockSpec((tm, d), lambda i: (i, C_GRET // d)),
            pl.BlockSpec((tm, d), lambda i: (i, 0)),
            pl.BlockSpec((1, 6, d), lambda i: (i // per_b, 0, 0)),
            const((d, d)), const((d, d)), const((d, d)),
            const((1, d)),
            const((d, LANES)), const((d, LANES)), const((1, LANES)),
        ],
        out_specs=[
            pl.BlockSpec((tm, d), lambda i: (i, 0)),
            pl.BlockSpec((tm * SLAB, LANES), lambda i: (i, 0)),
            pl.BlockSpec((tm, LANES), lambda i: (i, 0)),
        ],
        out_shape=[
            jax.ShapeDtypeStruct((t, d), F32),
            jax.ShapeDtypeStruct((t * SLAB, LANES), F32),
            jax.ShapeDtypeStruct((t, LANES), F32),
        ],
        compiler_params=pltpu.CompilerParams(
            dimension_semantics=("arbitrary",), vmem_limit_bytes=VMEM_LIMIT),
        name="post",
    )(hm, hr, proj, proj, x2, mod6, wbm, wbr, wout, g, wr_hi, wr_lo, br)


def _route_kernel(logit_ref, dest_ref, w_ref, bend_ref, cnt_st, pad_st, sel_st, pick_st, wts_st):
    ph = pl.program_id(0)
    i = pl.program_id(1)
    tm = logit_ref.shape[0]
    rows = pl.ds(pl.multiple_of(i * tm, tm), tm)
    lane = lax.broadcasted_iota(jnp.int32, (tm, LANES), 1)
    lane_f = lane.astype(F32)

    @pl.when(jnp.logical_and(ph == 0, i == 0))
    def _():
        cnt_st[...] = jnp.zeros_like(cnt_st)

    @pl.when(ph == 0)
    def _():
        l = logit_ref[...]
        picks, vals = [], []
        sel = jnp.zeros((tm, LANES), F32)
        for _ in range(TOP_K):
            m = jnp.max(l, axis=-1, keepdims=True)
            idx = jnp.min(jnp.where(l == m, lane_f, float(LANES)), axis=-1, keepdims=True)
            oh = lane_f == idx
            picks.append(idx)
            vals.append(m)
            sel = jnp.where(oh, 1.0, sel)
            l = jnp.where(oh, -jnp.inf, l)
        ex = [jnp.exp(v - vals[0]) for v in vals]
        den = ex[0] + ex[1] + ex[2] + ex[3]
        pick = jnp.zeros((tm, LANES), F32)
        wts = jnp.zeros((tm, LANES), F32)
        for k in range(TOP_K):
            pick = jnp.where(lane == k, picks[k], pick)
            wts = jnp.where(lane == k, ex[k] / den, wts)
        sel_st[rows, :] = sel.astype(BF16)
        pick_st[rows, :] = pick
        wts_st[rows, :] = wts
        cnt_st[...] = cnt_st[...] + jnp.sum(sel, axis=0, keepdims=True)

    @pl.when(jnp.logical_and(ph == 1, i == 0))
    def _():
        blocks = jnp.floor((cnt_st[...] + (EXPERT_BLOCK - 1)) * (1.0 / EXPERT_BLOCK))
        r = lax.broadcasted_iota(jnp.int32, (LANES, LANES), 0)
        c = lax.broadcasted_iota(jnp.int32, (LANES, LANES), 1)
        upper = jnp.where(r < c, 1.0, 0.0).astype(BF16)
        blocks8 = jnp.broadcast_to(blocks, (SUBLANES, LANES))
        excl = _dot(blocks8.astype(BF16), upper)
        pad_st[...] = excl[0:1, :] * EXPERT_BLOCK
        row = lax.broadcasted_iota(jnp.int32, (SUBLANES, LANES), 0)
        bend_ref[...] = jnp.where(row == 0, excl + blocks8,
                                  jnp.where(row == 1, excl * EXPERT_BLOCK + cnt_st[...], 0.0))
        cnt_st[...] = jnp.zeros_like(cnt_st)

    @pl.when(ph == 1)
    def _():
        carry = cnt_st[...]
        sel = sel_st[rows, :]
        pick = pick_st[rows, :]
        r = lax.broadcasted_iota(jnp.int32, (tm, tm), 0)
        c = lax.broadcasted_iota(jnp.int32, (tm, tm), 1)
        lower = jnp.where(r > c, 1.0, 0.0).astype(BF16)
        base = pad_st[...] + carry + _dot(lower, sel)
        dest = jnp.zeros((tm, LANES), F32)
        for k in range(TOP_K):
            dk = jnp.sum(jnp.where(lane_f == pick[:, k:k + 1], base, 0.0), axis=-1, keepdims=True)
            dest = jnp.where(lane == k, dk, dest)
        dest_ref[...] = dest.astype(jnp.int32)
        w_ref[...] = wts_st[rows, :]
        cnt_st[...] = carry + jnp.sum(sel.astype(F32), axis=0, keepdims=True)


def _route(logits):
    t = logits.shape[0]
    tm = ROUTE_TM
    return pl.pallas_call(
        _route_kernel,
        grid=(2, t // tm),
        in_specs=[pl.BlockSpec((tm, LANES), lambda ph, i: (i * (1 - ph), 0))],
        out_specs=[
            pl.BlockSpec((tm, LANES), lambda ph, i: (i * ph, 0)),
            pl.BlockSpec((tm, LANES), lambda ph, i: (i * ph, 0)),
            pl.BlockSpec((SUBLANES, LANES), lambda ph, i: (0, 0)),
        ],
        out_shape=[
            jax.ShapeDtypeStruct((t, LANES), jnp.int32),
            jax.ShapeDtypeStruct((t, LANES), F32),
            jax.ShapeDtypeStruct((SUBLANES, LANES), F32),
        ],
        scratch_shapes=[
            pltpu.VMEM((1, LANES), F32),
            pltpu.VMEM((1, LANES), F32),
            pltpu.VMEM((t, LANES), BF16),
            pltpu.VMEM((t, LANES), F32),
            pltpu.VMEM((t, LANES), F32),
        ],
        compiler_params=pltpu.CompilerParams(
            dimension_semantics=("arbitrary", "arbitrary"), vmem_limit_bytes=VMEM_LIMIT),
        name="route",
    )(logits)


DMA_UNROLL = 32


def _row_copy(src_ref, src_slab, dst_ref, dst_slab, sem):
    src = pl.multiple_of(src_slab * SLAB, SLAB)
    dst = pl.multiple_of(dst_slab * SLAB, SLAB)
    return pltpu.make_async_copy(src_ref.at[pl.ds(src, SLAB), :], dst_ref.at[pl.ds(dst, SLAB), :], sem)


def _dispatch_kernel(dest_ref, padfirst_ref, hf_hbm, xs_out, tiles, zbuf, lsem, rsem, zsem, tsem):
    i = pl.program_id(0)
    n_steps = pl.num_programs(0)
    n_assign = dest_ref.shape[0]
    tile_rows = tiles.shape[1]
    blk_rows = zbuf.shape[0]
    n_buf = tiles.shape[0]

    def tile_load(step, slot):
        start = pl.multiple_of(step * tile_rows, tile_rows)
        return pltpu.make_async_copy(hf_hbm.at[pl.ds(start, tile_rows), :], tiles.at[slot], lsem.at[slot])

    def wait_rows(slot):
        for _ in range(TOP_K):
            pltpu.make_async_copy(tiles.at[slot], xs_out.at[pl.ds(0, tile_rows), :], rsem.at[slot]).wait()

    @pl.when(i == 0)
    def _():
        tile_load(0, 0).start()
        tile_load(1, 1).start()
        zbuf[...] = jnp.zeros_like(zbuf)
        for e in range(N_EXPERTS):
            start = pl.multiple_of(padfirst_ref[e] * SLAB, SLAB)
            pltpu.make_async_copy(zbuf, xs_out.at[pl.ds(start, blk_rows), :], zsem).start()
        for e in range(N_EXPERTS):
            pltpu.make_async_copy(zbuf, xs_out.at[pl.ds(0, blk_rows), :], zsem).wait()

        n_used = padfirst_ref[N_EXPERTS]
        n_total = xs_out.shape[0] // blk_rows

        def zero_start(b, carry):
            start = pl.multiple_of(b * blk_rows, blk_rows)
            pltpu.make_async_copy(zbuf, xs_out.at[pl.ds(start, blk_rows), :], tsem).start()
            return carry

        lax.fori_loop(n_used, n_total, zero_start, 0)

    slot = i % n_buf
    tile_load(i, slot).wait()
    tile = tiles.at[slot]

    def body(g, carry):
        for u in range(DMA_UNROLL):
            t_local = g * (DMA_UNROLL // TOP_K) + u // TOP_K
            _row_copy(tile, t_local, xs_out, dest_ref[g * DMA_UNROLL + u], rsem.at[slot]).start(priority=u % 2)
        return carry

    lax.fori_loop(0, n_assign // DMA_UNROLL, body, 0)

    @pl.when(i >= 1)
    def _():
        wait_rows((i + n_buf - 1) % n_buf)

    @pl.when(i + 2 < n_steps)
    def _():
        tile_load(i + 2, (i + 2) % n_buf).start()

    @pl.when(i == n_steps - 1)
    def _():
        wait_rows(slot)

        def zero_wait(b, carry):
            pltpu.make_async_copy(zbuf, xs_out.at[pl.ds(0, blk_rows), :], tsem).wait()
            return carry

        lax.fori_loop(padfirst_ref[N_EXPERTS], xs_out.shape[0] // blk_rows, zero_wait, 0)


def _dispatch(dest_flat, pad_first, hf_slab, n_blocks):
    n_assign = dest_flat.shape[0]
    tm = DISPATCH_TM
    blk_rows = EXPERT_BLOCK * SLAB
    return pl.pallas_call(
        _dispatch_kernel,
        grid=(n_assign // (tm * TOP_K),),
        in_specs=[
            pl.BlockSpec((tm * TOP_K,), lambda i: (i,), memory_space=pltpu.SMEM),
            pl.BlockSpec(memory_space=pltpu.SMEM),
            pl.BlockSpec(memory_space=pl.ANY),
        ],
        out_specs=pl.BlockSpec(memory_space=pl.ANY),
        out_shape=jax.ShapeDtypeStruct(((n_blocks + 1) * blk_rows, LANES), F32),
        scratch_shapes=[
            pltpu.VMEM((3, tm * SLAB, LANES), F32),
            pltpu.VMEM((blk_rows, LANES), F32),
            pltpu.SemaphoreType.DMA((3,)),
            pltpu.SemaphoreType.DMA((3,)),
            pltpu.SemaphoreType.DMA,
            pltpu.SemaphoreType.DMA,
        ],
        compiler_params=pltpu.CompilerParams(dimension_semantics=("arbitrary",), vmem_limit_bytes=VMEM_LIMIT),
        name="dispatch",
    )(dest_flat, pad_first, hf_slab)


def _weight_copies(wgu_hbm, wd_hbm, wgu_f32, wd_f32, wsem, expert, slot):
    return (pltpu.make_async_copy(wgu_hbm.at[expert], wgu_f32.at[slot], wsem.at[0, slot]),
            pltpu.make_async_copy(wd_hbm.at[expert], wd_f32.at[slot], wsem.at[1, slot]))


def _expert_kernel(be_ref, nused_ref, first_ref, slot_ref, next_ref,
                   xs_ref, wgu_hbm, bgu_ref, wd_hbm, bd_ref, ys_ref,
                   wgu_f32, wd_f32, wgu_bf, wd_bf, wsem):
    b = pl.program_id(0)
    copies = functools.partial(_weight_copies, wgu_hbm, wd_hbm, wgu_f32, wd_f32, wsem)

    @pl.when(jnp.logical_and(b < nused_ref[0], first_ref[b] == 1))
    def _():
        slot = slot_ref[b]

        @pl.when(b == 0)
        def _():
            for cp in copies(be_ref[0], 0):
                cp.start()

        for cp in copies(be_ref[b], slot):
            cp.wait()

        @pl.when(next_ref[b] >= 0)
        def _():
            for cp in copies(next_ref[b], 1 - slot):
                cp.start()

        two = 2 * LANES
        r = lax.broadcasted_iota(jnp.int32, (two, two), 0)
        c = lax.broadcasted_iota(jnp.int32, (two, two), 1)
        src = jnp.where(c < LANES, 2 * c, 2 * (c - LANES) + 1)
        perm = jnp.where(r == src, 1.0, 0.0).astype(BF16)
        for blk in range(wgu_bf.shape[1] // two):
            wb = wgu_f32[slot, :, blk * two:(blk + 1) * two].astype(BF16)
            wgu_bf[:, blk * two:(blk + 1) * two] = _dot(wb, perm).astype(BF16)
        wd_bf[...] = wd_f32[slot].astype(BF16)

    @pl.when(b < nused_ref[0])
    def _():
        two = 2 * LANES
        x = _from_slabs(xs_ref, 0, EXPERT_BLOCK).astype(BF16)
        gu = _dot(x, wgu_bf[...]) + bgu_ref[0]
        n_blk = gu.shape[1] // two
        x_glu = jnp.concatenate([gu[:, i * two:i * two + LANES] for i in range(n_blk)], axis=1)
        x_lin = jnp.concatenate([gu[:, i * two + LANES:(i + 1) * two] for i in range(n_blk)], axis=1)
        x_glu = jnp.minimum(x_glu, SWIGLU_LIMIT)
        x_lin = jnp.clip(x_lin, -SWIGLU_LIMIT, SWIGLU_LIMIT)
        act = _sigmoid_mul(SWIGLU_ALPHA * x_glu, x_glu * (x_lin + 1.0))
        y = _dot(act.astype(BF16), wd_bf[...]) + bd_ref[0]
        _to_slabs(ys_ref, y)

    @pl.when(b >= nused_ref[0])
    def _():
        ys_ref[...] = jnp.zeros_like(ys_ref)


def _experts(block_expert, n_used, xs, w_gate_up, b_gu_perm, w_down, b_down):
    n_blocks = block_expert.shape[0]
    d = D_MODEL
    f2 = w_gate_up.shape[-1]
    blk_rows = EXPERT_BLOCK * SLAB

    idx = jnp.arange(n_blocks, dtype=jnp.int32)
    used = idx < n_used
    prev = jnp.concatenate([jnp.full((1,), -1, jnp.int32), block_expert[:-1]])
    first = (used & (block_expert != prev)).astype(jnp.int32)
    slot = (jnp.cumsum(first) - 1) % 2
    later = used[None, :] & (block_expert[None, :] > block_expert[:, None])
    nxt = jnp.min(jnp.where(later, block_expert[None, :], N_EXPERTS), axis=1)
    nxt = jnp.where(nxt == N_EXPERTS, -1, nxt)

    grid_spec = pltpu.PrefetchScalarGridSpec(
        num_scalar_prefetch=5,
        grid=(n_blocks,),
        in_specs=[
            pl.BlockSpec((blk_rows, LANES), lambda b, be, nu, *_: (jnp.minimum(b, nu[0] - 1), 0)),
            pl.BlockSpec(memory_space=pl.ANY),
            pl.BlockSpec((1, 1, f2), lambda b, be, *_: (be[b], 0, 0)),
            pl.BlockSpec(memory_space=pl.ANY),
            pl.BlockSpec((1, 1, d), lambda b, be, *_: (be[b], 0, 0)),
        ],
        out_specs=pl.BlockSpec((blk_rows, LANES), lambda b, be, *_: (b, 0)),
        scratch_shapes=[
            pltpu.VMEM((2, d, f2), F32),
            pltpu.VMEM((2, f2 // 2, d), F32),
            pltpu.VMEM((d, f2), BF16),
            pltpu.VMEM((f2 // 2, d), BF16),
            pltpu.SemaphoreType.DMA((2, 2)),
        ],
    )
    return pl.pallas_call(
        _expert_kernel,
        grid_spec=grid_spec,
        out_shape=jax.ShapeDtypeStruct((n_blocks * blk_rows, LANES), F32),
        compiler_params=pltpu.CompilerParams(
            dimension_semantics=("arbitrary",), vmem_limit_bytes=VMEM_LIMIT),
        name="expert",
    )(block_expert, n_used, first, slot.astype(jnp.int32), nxt.astype(jnp.int32),
      xs, w_gate_up, b_gu_perm, w_down, b_down)


def _combine_kernel(dest_ref, dest_next_ref, ys_hbm, w_ref, x1_ref, mod_ref, g_ref, o_ref, buf, sem):
    i = pl.program_id(0)
    n_steps = pl.num_programs(0)
    tm = x1_ref.shape[0]
    n_assign = tm * TOP_K
    slot = i % 2

    def gather(d_ref, to_slot):
        def body(g, carry):
            for u in range(DMA_UNROLL):
                t_local = g * (DMA_UNROLL // TOP_K) + u // TOP_K
                k = u % TOP_K
                _row_copy(ys_hbm, d_ref[g * DMA_UNROLL + u], buf.at[to_slot], k * tm + t_local,
                          sem.at[to_slot]).start(priority=u % 2)
            return carry

        lax.fori_loop(0, n_assign // DMA_UNROLL, body, 0)

    @pl.when(i == 0)
    def _():
        gather(dest_ref, 0)

    @pl.when(i + 1 < n_steps)
    def _():
        gather(dest_next_ref, 1 - slot)

    cur = buf.at[slot]
    pltpu.make_async_copy(ys_hbm.at[pl.ds(0, n_assign * SLAB), :], cur, sem.at[slot]).wait()

    w = w_ref[...]
    y = w[:, 0:1] * _from_slabs(cur, 0, tm)
    for k in range(1, TOP_K):
        y = y + w[:, k:k + 1] * _from_slabs(cur, k * tm, tm)
    x = x1_ref[...] + mod_ref[0, 5:6, :] * y
    ms = jnp.mean(x * x, axis=-1, keepdims=True)
    o_ref[...] = x * lax.rsqrt(ms + EPS) * g_ref[...]


def _combine(dest_flat, ys, wts, x1, mod6, g, seq):
    t = x1.shape[0]
    tm = COMBINE_TM
    per_b = seq // tm
    d = D_MODEL
    last = t // tm - 1
    return pl.pallas_call(
        _combine_kernel,
        grid=(t // tm,),
        in_specs=[
            pl.BlockSpec((tm * TOP_K,), lambda i: (i,), memory_space=pltpu.SMEM),
            pl.BlockSpec((tm * TOP_K,), lambda i: (jnp.minimum(i + 1, last),), memory_space=pltpu.SMEM),
            pl.BlockSpec(memory_space=pl.ANY),
            pl.BlockSpec((tm, LANES), lambda i: (i, 0)),
            pl.BlockSpec((tm, d), lambda i: (i, 0)),
            pl.BlockSpec((1, 6, d), lambda i: (i // per_b, 0, 0)),
            pl.BlockSpec((1, d), lambda i: (0, 0)),
        ],
        out_specs=pl.BlockSpec((tm, d), lambda i: (i, 0)),
        out_shape=jax.ShapeDtypeStruct((t, d), F32),
        scratch_shapes=[pltpu.VMEM((2, tm * TOP_K * SLAB, LANES), F32), pltpu.SemaphoreType.DMA((2,))],
        compiler_params=pltpu.CompilerParams(
            dimension_semantics=("arbitrary",), vmem_limit_bytes=VMEM_LIMIT),
        name="combine",
    )(dest_flat, dest_flat, ys, wts, x1, mod6, g)


def kernel(x, c, positions, w_ada, b_ada, norm_mix_g, w_in, conv_w, conv_b, b_if, ml_norm_g, ret_norm_g,
           w_branch_ml, w_branch_ret, w_out, norm_ffn_g, w_router, b_router, w_gate_up, b_gate_up, w_down,
           b_down, norm_final_g):
    batch, seq, d = x.shape
    t = batch * seq
    x2 = x.reshape(t, d)
    depth = w_ada.shape[0]
    c8 = jnp.concatenate([c, jnp.zeros((SUBLANES - batch, d), c.dtype)], axis=0)

    half = D_QK // 2
    inv = ROPE_BASE ** (-jnp.arange(half, dtype=F32) / half)
    inv2 = jnp.concatenate([inv, inv])[None, :]
    sign2 = jnp.concatenate([-jnp.ones((half,), F32), jnp.ones((half,), F32)])[None, :]
    pos_b = jnp.broadcast_to(positions.astype(F32).reshape(t, 1), (t, LANES))
    L = CHUNK
    log_gamma = jnp.log(1.0 - 2.0 ** (-5.0 - jnp.arange(N_HEADS, dtype=F32)))
    pos = jnp.arange(L, dtype=F32)
    rel = pos[:, None] - pos[None, :]
    decay = jnp.where(rel >= 0, jnp.exp(log_gamma[:, None, None] * jnp.maximum(rel, 0.0)), 0.0)
    kdec = jnp.exp(log_gamma[:, None] * (L - 1 - pos))[:, :, None]
    qdec = jnp.exp(log_gamma[:, None] * (pos + 1.0))[:, :, None]
    gch = jnp.broadcast_to(jnp.exp(log_gamma * L)[:, None, None], (N_HEADS, 1, LANES))

    assert depth == 1, "the final norm is fused after the single layer"
    l = 0
    mod = _ada(c8, w_ada[l], b_ada[l][None, :])
    mod6 = mod[:batch].reshape(batch, 6, d)

    w = w_in[l]
    w_main, w_gate = _wprep(w.T)
    w_if = w_gate.T
    zpad = jnp.zeros((d, LANES - N_HEADS), F32)
    wif = jnp.concatenate([w_if[:, :N_HEADS], zpad, w_if[:, N_HEADS:], zpad], axis=1)
    wif_hi, wif_lo = _split(wif)
    zb = jnp.zeros((LANES - N_HEADS,), F32)
    b_if2 = jnp.concatenate([b_if[l][:N_HEADS], zb, b_if[l][N_HEADS:], zb])[None, :]

    proj, gates, cos2, sin2 = _inproj(x2, mod6, norm_mix_g[l][None, :], w_main, wif_hi, wif_lo,
                                      pos_b, inv2, sign2, seq)
    hm, hr = _mix(proj, gates, cos2, sin2, conv_w[l], conv_b[l][None, :], b_if2,
                  ml_norm_g[l][None, :], ret_norm_g[l][None, :], decay, kdec, qdec, gch, batch, seq)

    wr = jnp.concatenate([w_router[l], jnp.zeros((d, LANES - N_EXPERTS), F32)], axis=1)
    wr_hi, wr_lo = _split(wr)
    br = jnp.concatenate([b_router[l], jnp.full((LANES - N_EXPERTS,), NEG_BIG, F32)])[None, :]
    x1, hf, logits = _post(hm, hr, proj, x2, mod6, w_branch_ml[l].astype(BF16), w_branch_ret[l].astype(BF16),
                           w_out[l].astype(BF16), norm_ffn_g[l][None, :], wr_hi, wr_lo, br, seq)

    n_assign = t * TOP_K
    n_blocks = -(-n_assign // EXPERT_BLOCK) + N_EXPERTS
    dest, wts, tables = _route(logits)
    dest_flat = dest[:, :TOP_K].reshape(n_assign)
    bend = tables[0, :N_EXPERTS].astype(jnp.int32)
    pad_first = tables[1, :N_EXPERTS].astype(jnp.int32)
    n_used = bend[N_EXPERTS - 1:]
    blk = jnp.minimum(jnp.arange(n_blocks, dtype=jnp.int32), n_used - 1)
    block_expert = jnp.minimum(
        jnp.sum((bend[None, :] <= blk[:, None]).astype(jnp.int32), axis=1), N_EXPERTS - 1)
    xs = _dispatch(dest_flat, jnp.concatenate([pad_first, n_used]), hf, n_blocks)
    f2 = w_gate_up.shape[-1]
    bgu = b_gate_up[l].reshape(N_EXPERTS, f2 // (2 * LANES), LANES, 2)
    bgu = jnp.swapaxes(bgu, -1, -2).reshape(N_EXPERTS, 1, f2)
    ys = _experts(block_expert, n_used, xs, w_gate_up[l], bgu, w_down[l], b_down[l][:, None, :])
    out = _combine(dest_flat, ys, wts, x1, mod6, norm_final_g[None, :], seq)
    return out.reshape(batch, seq, d)
```
